```python
import math
import jax, jax.numpy as jnp
from jax import lax
import numpy as np

D_MODEL = 1024
BATCH = 8
SEQ = 4096
DEPTH = 2

N_MIXERS = 2
N_ATTN_LAYERS = (DEPTH + 1) // 2
N_SSM_LAYERS = DEPTH // 2
DIL_GROUPS = ((128, 1), (512, 4), (2048, 16))
N_DIL_GROUPS = len(DIL_GROUPS)
SUB_WIN = 128
ATTN_HEAD_DIM = 128
ATTN_HEADS_PER_GROUP = D_MODEL // ATTN_HEAD_DIM
ATTN_PROJ = N_DIL_GROUPS * 3 * ATTN_HEADS_PER_GROUP * ATTN_HEAD_DIM
ROT_DIM = ATTN_HEAD_DIM // 4
ROPE_THETA = 500000.0
SSM_D_INNER = 2 * D_MODEL
SSM_HEAD_DIM = 64
SSM_HEADS = SSM_D_INNER // SSM_HEAD_DIM
SSM_GROUPS = 4
SSM_HEADS_PER_GROUP = SSM_HEADS // SSM_GROUPS
SSM_D_STATE = 128
SSM_CONV = 4
SSM_CHUNK = 128
SSM_CONV_DIM = SSM_D_INNER + 2 * SSM_GROUPS * SSM_D_STATE
SSM_PROJ = 2 * SSM_D_INNER + 2 * SSM_GROUPS * SSM_D_STATE + SSM_HEADS
N_EXPERTS = 32
TOP_K = 4
D_FF = D_MODEL
SWIGLU_LIMIT = 7.0
SWIGLU_ALPHA = 1.702
MOE_BLOCK = 256
DEEPNORM_ALPHA = (2 * DEPTH) ** 0.25
DEEPNORM_BETA = (8 * DEPTH) ** -0.25
LN_EPS = 1e-5
RMS_EPS = 1e-5

kernel_name = "hybrid_dilated_attn_mamba2_moe_deepnorm"


def layer_norm(x, g, b):
    xf = x.astype(jnp.float32)
    mu = jnp.mean(xf, axis=-1, keepdims=True)
    var = jnp.mean(jnp.square(xf - mu), axis=-1, keepdims=True)
    return ((xf - mu) * lax.rsqrt(var + LN_EPS)).astype(x.dtype) * g + b


def rope_tables(positions, dtype):
    inv_freq = ROPE_THETA ** (-jnp.arange(0, ROT_DIM, 2, dtype=jnp.float32) / ROT_DIM)
    ang = positions.astype(jnp.float32)[..., None] * inv_freq
    return jnp.cos(ang)[:, :, None, :].astype(dtype), jnp.sin(ang)[:, :, None, :].astype(dtype)


def rope_partial(t, cos, sin):
    half = ROT_DIM // 2
    r1, r2, rest = t[..., :half], t[..., half:ROT_DIM], t[..., ROT_DIM:]
    return jnp.concatenate([r1 * cos - r2 * sin, r2 * cos + r1 * sin, rest], axis=-1)


def dilated_group_attn(q, k, v, dil):
    bsz, seq, nh, dh = q.shape
    span = dil * SUB_WIN
    total = -(-seq // span) * span
    padw = ((0, 0), (0, total - seq), (0, 0), (0, 0))
    nb = total // span
    split = lambda a: jnp.pad(a, padw).reshape(bsz, nb, SUB_WIN, dil, nh, dh)
    qb, kb, vb = split(q), split(k), split(v)
    band = lambda a: jnp.concatenate(
        [jnp.concatenate([jnp.zeros_like(a[:, :1]), a[:, :-1]], axis=1), a], axis=2)
    k_band, v_band = band(kb), band(vb)
    s = jnp.einsum('bnqrhd,bnkrhd->bnrhqk', qb, k_band,
                   preferred_element_type=jnp.float32) * (1.0 / math.sqrt(dh))
    blk = jnp.arange(nb)[:, None, None]
    qi = jnp.arange(SUB_WIN)[None, :, None]
    kj = jnp.arange(2 * SUB_WIN)[None, None, :]
    valid = (kj >= qi) & (kj <= qi + SUB_WIN) & ((kj >= SUB_WIN) | (blk > 0))
    s = jnp.where(valid[None, :, None, None, :, :], s, -jnp.inf)
    m = jnp.max(s, axis=-1, keepdims=True)
    p = jnp.exp(s - m)
    l = jnp.sum(p, axis=-1)
    o = jnp.einsum('bnrhqk,bnkrhd->bnqrhd', p, v_band.astype(jnp.float32))
    l_t = jnp.transpose(l, (0, 1, 4, 2, 3))
    o = o / l_t[..., None]
    lse = jnp.transpose(m[..., 0], (0, 1, 4, 2, 3)) + jnp.log(l_t)
    o = o.reshape(bsz, total, nh, dh)[:, :seq]
    lse = lse.reshape(bsz, total, nh)[:, :seq]
    return o, lse


def dilated_attention(x, cos, sin, w_in, w_out):
    bsz, seq, _ = x.shape
    proj = jnp.matmul(x, w_in).reshape(bsz, seq, N_DIL_GROUPS, 3, ATTN_HEADS_PER_GROUP, ATTN_HEAD_DIM)
    outs, lses = [], []
    for g, (win, dil) in enumerate(DIL_GROUPS):
        q = rope_partial(proj[:, :, g, 0], cos, sin)
        k = rope_partial(proj[:, :, g, 1], cos, sin)
        o, lse = dilated_group_attn(q, k, proj[:, :, g, 2], dil)
        outs.append(o)
        lses.append(lse)
    w = jax.nn.softmax(jnp.stack(lses, axis=0), axis=0)
    o = jnp.sum(w[..., None] * jnp.stack(outs, axis=0), axis=0).astype(x.dtype)
    return jnp.matmul(o.reshape(bsz, seq, ATTN_HEADS_PER_GROUP * ATTN_HEAD_DIM), w_out)


def ssd_chunked(xh, dt, a, bm, cm, d_skip):
    bsz, seq = xh.shape[:2]
    nc, Q, G, R = seq // SSM_CHUNK, SSM_CHUNK, SSM_GROUPS, SSM_HEADS_PER_GROUP
    f32 = jnp.float32
    x = xh.astype(f32).reshape(bsz, nc, Q, G, R, SSM_HEAD_DIM)
    dt = dt.reshape(bsz, nc, Q, G, R)
    bm = bm.astype(f32).reshape(bsz, nc, Q, G, SSM_D_STATE)
    cm = cm.astype(f32).reshape(bsz, nc, Q, G, SSM_D_STATE)
    a_cum = jnp.cumsum(dt * a.reshape(G, R), axis=2)
    xdt = x * dt[..., None]
    causal = (jnp.arange(Q)[:, None] >= jnp.arange(Q)[None, :])[None, None, :, :, None, None]
    diff = a_cum[:, :, :, None] - a_cum[:, :, None, :]
    lmat = jnp.exp(jnp.where(causal, diff, -jnp.inf))
    cb = jnp.einsum('bclgn,bcsgn->bclsg', cm, bm)
    y_diag = jnp.einsum('bclsgr,bcsgrp->bclgrp', cb[..., None] * lmat, xdt)
    decay_states = jnp.exp(a_cum[:, :, -1:] - a_cum)
    states = jnp.einsum('bcsgn,bcsgrp->bcgrpn', bm, decay_states[..., None] * xdt)
    chunk_decay = jnp.exp(a_cum[:, :, -1])

    def step(h, inp):
        st, dec = inp
        return h * dec[..., None, None] + st, h

    h0 = jnp.zeros((bsz, G, R, SSM_HEAD_DIM, SSM_D_STATE), f32)
    _, prev = lax.scan(step, h0, (jnp.moveaxis(states, 1, 0), jnp.moveaxis(chunk_decay, 1, 0)))
    prev = jnp.moveaxis(prev, 0, 1)
    y_off = jnp.einsum('bclgn,bcgrpn->bclgrp', cm, prev) * jnp.exp(a_cum)[..., None]
    y = y_diag + y_off + x * d_skip.astype(f32).reshape(G, R)[:, :, None]
    return y.reshape(bsz, seq, SSM_D_INNER)


def mamba2_mixer(x, w_in, conv_w, conv_b, dt_bias, a_log, d_skip, norm_w, w_out):
    bsz, seq, _ = x.shape
    zxbcdt = jnp.matmul(x, w_in)
    z = zxbcdt[..., :SSM_D_INNER]
    xbc = zxbcdt[..., SSM_D_INNER:SSM_D_INNER + SSM_CONV_DIM]
    dt = zxbcdt[..., SSM_D_INNER + SSM_CONV_DIM:]
    xbc = lax.conv_general_dilated(xbc, conv_w[:, None, :].astype(xbc.dtype), window_strides=(1,),
                                   padding=[(SSM_CONV - 1, 0)], dimension_numbers=('NWC', 'WIO', 'NWC'),
                                   feature_group_count=SSM_CONV_DIM)
    xbc = jax.nn.silu(xbc + conv_b)
    gn = SSM_GROUPS * SSM_D_STATE
    xs = xbc[..., :SSM_D_INNER].reshape(bsz, seq, SSM_HEADS, SSM_HEAD_DIM)
    bm = xbc[..., SSM_D_INNER:SSM_D_INNER + gn].reshape(bsz, seq, SSM_GROUPS, SSM_D_STATE)
    cm = xbc[..., SSM_D_INNER + gn:].reshape(bsz, seq, SSM_GROUPS, SSM_D_STATE)
    dt = jax.nn.softplus((dt + dt_bias).astype(jnp.float32))
    a = -jnp.exp(a_log.astype(jnp.float32))
    y = ssd_chunked(xs, dt, a, bm, cm, d_skip)
    yg = (y * jax.nn.silu(z.astype(jnp.float32))).reshape(bsz, seq, SSM_GROUPS, SSM_D_INNER // SSM_GROUPS)
    yg = yg * lax.rsqrt(jnp.mean(jnp.square(yg), axis=-1, keepdims=True) + RMS_EPS)
    yg = yg.reshape(bsz, seq, SSM_D_INNER).astype(x.dtype) * norm_w
    return jnp.matmul(yg, w_out)


def moe_ffn(x, w_router, b_router, w_up, b_up, w_down, b_down):
    bsz, seq, d = x.shape
    xt = x.reshape(-1, d)
    n_tok = xt.shape[0]
    n_assign = n_tok * TOP_K
    logits = jnp.matmul(xt, w_router, preferred_element_type=jnp.float32) + b_router.astype(jnp.float32)
    top_logits, top_idx = lax.top_k(logits, TOP_K)
    gates = jax.nn.softmax(top_logits, axis=-1).astype(x.dtype)
    flat_e = top_idx.reshape(-1)
    order = jnp.argsort(flat_e, stable=True)
    sorted_e = flat_e[order]
    counts = jnp.bincount(flat_e, length=N_EXPERTS)
    starts = jnp.cumsum(counts) - counts
    rank = jnp.arange(n_assign) - starts[sorted_e]
    padded = (counts + MOE_BLOCK - 1) // MOE_BLOCK * MOE_BLOCK
    pad_ends = jnp.cumsum(padded)
    pad_starts = pad_ends - padded
    dest = jnp.zeros((n_assign,), jnp.int32).at[order].set((pad_starts[sorted_e] + rank).astype(jnp.int32))
    n_blocks = -(-n_assign // MOE_BLOCK) + N_EXPERTS
    x_rows = jnp.zeros((n_blocks * MOE_BLOCK, d), x.dtype).at[dest].set(jnp.repeat(xt, TOP_K, axis=0))
    block_e = jnp.minimum(jnp.searchsorted(pad_ends, jnp.arange(n_blocks) * MOE_BLOCK, side='right'),
                          N_EXPERTS - 1)

    def expert_block(args):
        xb, e = args
        h = jnp.matmul(xb, w_up[e]) + b_up[e]
        gate = jnp.minimum(h[:, :D_FF], SWIGLU_LIMIT)
        up = jnp.clip(h[:, D_FF:], -SWIGLU_LIMIT, SWIGLU_LIMIT)
        glu = gate * jax.nn.sigmoid(SWIGLU_ALPHA * gate)
        return jnp.matmul((up + 1.0) * glu, w_down[e]) + b_down[e]

    y_rows = lax.map(expert_block, (x_rows.reshape(n_blocks, MOE_BLOCK, d), block_e))
    y = y_rows.reshape(-1, d)[dest].reshape(n_tok, TOP_K, d)
    return jnp.einsum('tk,tkd->td', gates, y).reshape(bsz, seq, d)


def setup_inputs(seed: int = 0) -> dict:
    key = jax.random.key(seed)
    ks = jax.random.split(key, 24)
    nrm = lambda k, shape, scale: jax.random.normal(k, shape, jnp.float32) * scale
    x = jax.random.normal(ks[0], (BATCH, SEQ, D_MODEL), jnp.float32)
    offs = jax.random.randint(ks[1], (BATCH, 1), 0, 1024, dtype=jnp.int32)
    positions = jnp.arange(SEQ, dtype=jnp.int32)[None, :] + offs
    attn_w_in = nrm(ks[2], (N_ATTN_LAYERS, D_MODEL, ATTN_PROJ), D_MODEL ** -0.5)
    attn_w_out = nrm(ks[3], (N_ATTN_LAYERS, ATTN_HEADS_PER_GROUP * ATTN_HEAD_DIM, D_MODEL),
                     DEEPNORM_BETA * (ATTN_HEADS_PER_GROUP * ATTN_HEAD_DIM) ** -0.5)
    ssm_w_in = nrm(ks[4], (N_SSM_LAYERS, D_MODEL, SSM_PROJ), D_MODEL ** -0.5)
    ssm_conv_w = nrm(ks[5], (N_SSM_LAYERS, SSM_CONV, SSM_CONV_DIM), SSM_CONV ** -0.5)
    ssm_conv_b = nrm(ks[6], (N_SSM_LAYERS, SSM_CONV_DIM), 0.02)
    dt0 = jnp.exp(jax.random.uniform(ks[7], (N_SSM_LAYERS, SSM_HEADS), jnp.float32,
                                     math.log(1e-3), math.log(1e-1)))
    ssm_dt_bias = dt0 + jnp.log(-jnp.expm1(-dt0))
    ssm_a_log = jnp.log(jax.random.uniform(ks[8], (N_SSM_LAYERS, SSM_HEADS), jnp.float32, 1.0, 16.0))
    ssm_d = 1.0 + nrm(ks[9], (N_SSM_LAYERS, SSM_HEADS), 0.1)
    ssm_norm_w = 1.0 + nrm(ks[10], (N_SSM_LAYERS, SSM_D_INNER), 0.1)
    ssm_w_out = nrm(ks[11], (N_SSM_LAYERS, SSM_D_INNER, D_MODEL), DEEPNORM_BETA * SSM_D_INNER ** -0.5)
    moe_w_router = nrm(ks[12], (DEPTH, D_MODEL, N_EXPERTS), D_MODEL ** -0.5)
    moe_b_router = nrm(ks[13], (DEPTH, N_EXPERTS), 0.01)
    moe_w_up = nrm(ks[14], (DEPTH, N_EXPERTS, D_MODEL, 2 * D_FF), D_MODEL ** -0.5)
    moe_b_up = nrm(ks[15], (DEPTH, N_EXPERTS, 2 * D_FF), 0.02)
    moe_w_down = nrm(ks[16], (DEPTH, N_EXPERTS, D_FF, D_MODEL), DEEPNORM_BETA * D_FF ** -0.5)
    moe_b_down = nrm(ks[17], (DEPTH, N_EXPERTS, D_MODEL), 0.02)
    ln_mix_g = 1.0 + nrm(ks[18], (DEPTH, D_MODEL), 0.1)
    ln_mix_b = nrm(ks[19], (DEPTH, D_MODEL), 0.02)
    ln_ffn_g = 1.0 + nrm(ks[20], (DEPTH, D_MODEL), 0.1)
    ln_ffn_b = nrm(ks[21], (DEPTH, D_MODEL), 0.02)
    return {"x": x, "positions": positions, "attn_w_in": attn_w_in, "attn_w_out": attn_w_out,
            "ssm_w_in": ssm_w_in, "ssm_conv_w": ssm_conv_w, "ssm_conv_b": ssm_conv_b,
            "ssm_dt_bias": ssm_dt_bias, "ssm_a_log": ssm_a_log, "ssm_d": ssm_d,
            "ssm_norm_w": ssm_norm_w, "ssm_w_out": ssm_w_out,
            "moe_w_router": moe_w_router, "moe_b_router": moe_b_router, "moe_w_up": moe_w_up,
            "moe_b_up": moe_b_up, "moe_w_down": moe_w_down, "moe_b_down": moe_b_down,
            "ln_mix_g": ln_mix_g, "ln_mix_b": ln_mix_b, "ln_ffn_g": ln_ffn_g, "ln_ffn_b": ln_ffn_b}


def reference(x, positions, attn_w_in, attn_w_out, ssm_w_in, ssm_conv_w, ssm_conv_b, ssm_dt_bias,
              ssm_a_log, ssm_d, ssm_norm_w, ssm_w_out, moe_w_router, moe_b_router, moe_w_up,
              moe_b_up, moe_w_down, moe_b_down, ln_mix_g, ln_mix_b, ln_ffn_g, ln_ffn_b):
    cos, sin = rope_tables(positions, x.dtype)
    for i in range(DEPTH):
        j = i // N_MIXERS
        if i % N_MIXERS == 0:
            mix = dilated_attention(x, cos, sin, attn_w_in[j], attn_w_out[j])
        else:
            mix = mamba2_mixer(x, ssm_w_in[j], ssm_conv_w[j], ssm_conv_b[j], ssm_dt_bias[j],
                               ssm_a_log[j], ssm_d[j], ssm_norm_w[j], ssm_w_out[j])
        x = layer_norm(DEEPNORM_ALPHA * x + mix, ln_mix_g[i], ln_mix_b[i])
        ffn = moe_ffn(x, moe_w_router[i], moe_b_router[i], moe_w_up[i], moe_b_up[i],
                      moe_w_down[i], moe_b_down[i])
        x = layer_norm(DEEPNORM_ALPHA * x + ffn, ln_ffn_g[i], ln_ffn_b[i])
    return x
```

```python
import functools
import math

import jax
import jax.numpy as jnp
from jax import lax
from jax.experimental import pallas as pl
from jax.experimental.pallas import tpu as pltpu

D_MODEL = 1024
DEPTH = 2
DIL = (1, 4, 16)
N_GROUPS = 3
SUB_WIN = 128
HEAD_DIM = 128
HEADS = 8
ROT_DIM = 32
ROT_HALF = 16
ROPE_THETA = 500000.0
ATTN_PROJ = N_GROUPS * 3 * D_MODEL
SSM_INNER = 2048
SSM_P = 64
SSM_HEADS = 32
SSM_GROUPS = 4
SSM_GHEADS = 8
SSM_N = 128
SSM_CONV = 4
CHUNK = 128
SSM_BC = SSM_GROUPS * SSM_N
SSM_XBC = SSM_INNER + 2 * SSM_BC
N_EXPERTS = 32
TOP_K = 4
D_FF = 1024
SWIGLU_LIMIT = 7.0
SWIGLU_ALPHA = 1.702
DEEPNORM_ALPHA = (2 * DEPTH) ** 0.25
LN_EPS = 1e-5
RMS_EPS = 1e-5

LANES = 128
SUBLANES = 8
VMEM_LIMIT_BYTES = 56 * 1024 * 1024

NEG_BIG = -1e30

_F32 = jnp.float32
_BF16 = jnp.bfloat16


def _cparams(*sem):
    return pltpu.CompilerParams(dimension_semantics=sem, vmem_limit_bytes=VMEM_LIMIT_BYTES)


def _rope_table_kernel(pos_ref, invf_ref, c_ref, sa_ref, sb_ref):
    pos = pos_ref[...].astype(_F32)
    ang = pos * invf_ref[...]
    lane = lax.broadcasted_iota(jnp.int32, ang.shape, 1)
    cos = jnp.cos(ang)
    sin = jnp.sin(ang)
    c_ref[...] = jnp.where(lane < ROT_DIM, cos, 1.0)
    sa_ref[...] = jnp.where((lane >= ROT_HALF) & (lane < ROT_DIM), sin, 0.0)
    sb_ref[...] = jnp.where(lane < ROT_HALF, -sin, 0.0)


def rope_tables(positions):
    t = positions.size
    tm = 1024
    pos = positions.reshape(t, 1)
    j = jnp.arange(LANES)
    invf = jnp.where(j < ROT_DIM,
                     ROPE_THETA ** (-(2.0 * (j % ROT_HALF)).astype(_F32) / ROT_DIM), 0.0)
    invf = invf.astype(_F32).reshape(1, LANES)
    out = jax.ShapeDtypeStruct((t, LANES), _F32)
    return pl.pallas_call(
        _rope_table_kernel,
        grid=(t // tm,),
        in_specs=[pl.BlockSpec((tm, 1), lambda i: (i, 0)),
                  pl.BlockSpec((1, LANES), lambda i: (0, 0))],
        out_specs=[pl.BlockSpec((tm, LANES), lambda i: (i, 0))] * 3,
        out_shape=[out, out, out],
        compiler_params=_cparams("parallel"),
        name="rope_tables",
    )(pos, invf)


def _proj_kernel(x_ref, w_ref, o_ref, xb_ref):
    @pl.when(pl.program_id(1) == 0)
    def _():
        xb_ref[...] = x_ref[...].astype(_BF16)

    o_ref[...] = jnp.dot(xb_ref[...], w_ref[...], preferred_element_type=_F32).astype(o_ref.dtype)


def _proj_rope_kernel(x_ref, w_ref, c_ref, sa_ref, sb_ref, o_ref, xb_ref):
    j = pl.program_id(1)

    @pl.when(j == 0)
    def _():
        xb_ref[...] = x_ref[...].astype(_BF16)

    acc = jnp.dot(xb_ref[...], w_ref[...], preferred_element_type=_F32)

    @pl.when(j % 3 == 2)
    def _():
        o_ref[...] = acc.astype(o_ref.dtype)

    @pl.when(j % 3 != 2)
    def _():
        c = c_ref[...]
        sa = sa_ref[...]
        sb = sb_ref[...]
        for h in range(acc.shape[1] // HEAD_DIM):
            t = acc[:, h * HEAD_DIM:(h + 1) * HEAD_DIM]
            r = t * c + pltpu.roll(t, ROT_HALF, 1) * sa + pltpu.roll(t, HEAD_DIM - ROT_HALF, 1) * sb
            o_ref[:, h * HEAD_DIM:(h + 1) * HEAD_DIM] = r.astype(o_ref.dtype)


def dense_proj(x, w_bf16, *, tn, out_dtype, tm=512, rope=None):
    t, k = x.shape
    n = w_bf16.shape[1]
    grid = (t // tm, n // tn)
    x_spec = pl.BlockSpec((tm, k), lambda i, j: (i, 0))
    w_spec = pl.BlockSpec((k, tn), lambda i, j: (0, j))
    o_spec = pl.BlockSpec((tm, tn), lambda i, j: (i, j))
    scratch = [pltpu.VMEM((tm, k), _BF16)]
    if rope is None:
        return pl.pallas_call(
            _proj_kernel, grid=grid, in_specs=[x_spec, w_spec], out_specs=o_spec,
            out_shape=jax.ShapeDtypeStruct((t, n), out_dtype), scratch_shapes=scratch,
            compiler_params=_cparams("parallel", "arbitrary"), name="dense_proj",
        )(x, w_bf16)
    tab_spec = pl.BlockSpec((tm, LANES), lambda i, j: (i, 0))
    return pl.pallas_call(
        _proj_rope_kernel, grid=grid, in_specs=[x_spec, w_spec, tab_spec, tab_spec, tab_spec],
        out_specs=o_spec, out_shape=jax.ShapeDtypeStruct((t, n), out_dtype), scratch_shapes=scratch,
        compiler_params=_cparams("parallel", "arbitrary"), name="qkv_proj_rope",
    )(x, w_bf16, *rope)


def _dil_attn_kernel(q_ref, kc_ref, kp_ref, vc_ref, vp_ref, o_ref, lse_ref):
    n = pl.program_id(2)
    row = lax.broadcasted_iota(jnp.int32, (SUB_WIN, SUB_WIN), 0)
    col = lax.broadcasted_iota(jnp.int32, (SUB_WIN, SUB_WIN), 1)
    cur_ok = col <= row
    prev_ok = (col >= row) & (n > 0)
    lane = lax.broadcasted_iota(jnp.int32, (SUB_WIN, LANES), 1)
    scale = 1.0 / math.sqrt(HEAD_DIM)
    nt = (((1,), (1,)), ((), ()))
    lse_all = jnp.zeros((SUB_WIN, LANES), _F32)
    for h in range(HEADS):
        sl = slice(h * HEAD_DIM, (h + 1) * HEAD_DIM)
        q = q_ref[:, sl]
        sc = lax.dot_general(q, kc_ref[:, sl], nt, preferred_element_type=_F32) * scale
        sp = lax.dot_general(q, kp_ref[:, sl], nt, preferred_element_type=_F32) * scale
        sc = jnp.where(cur_ok, sc, NEG_BIG)
        sp = jnp.where(prev_ok, sp, NEG_BIG)
        m = jnp.maximum(jnp.max(sc, axis=1, keepdims=True), jnp.max(sp, axis=1, keepdims=True))
        pc = jnp.exp(sc - m)
        pp = jnp.exp(sp - m)
        l = jnp.sum(pc, axis=1, keepdims=True) + jnp.sum(pp, axis=1, keepdims=True)
        acc = jnp.dot(pc.astype(_BF16), vc_ref[:, sl], preferred_element_type=_F32)
        acc += jnp.dot(pp.astype(_BF16), vp_ref[:, sl], preferred_element_type=_F32)
        o_ref[:, sl] = (acc / l).astype(o_ref.dtype)
        lse_all = jnp.where(lane == h, m + jnp.log(l), lse_all)
    lse_ref[...] = lse_all


def dilated_group_attention(qkv, g, bsz, seq):
    d = DIL[g]
    nb = seq // (SUB_WIN * d)
    ncol = ATTN_PROJ // D_MODEL
    qkv4 = qkv.reshape(bsz, nb, SUB_WIN, d * ATTN_PROJ)
    blk = (None, None, SUB_WIN, D_MODEL)

    def spec(comp, prev):
        if prev:
            return pl.BlockSpec(blk, lambda b, r, n: (b, jnp.maximum(n - 1, 0), 0, r * ncol + g * 3 + comp))
        return pl.BlockSpec(blk, lambda b, r, n: (b, n, 0, r * ncol + g * 3 + comp))

    o, lse = pl.pallas_call(
        _dil_attn_kernel,
        grid=(bsz, d, nb),
        in_specs=[spec(0, False), spec(1, False), spec(1, True), spec(2, False), spec(2, True)],
        out_specs=[pl.BlockSpec(blk, lambda b, r, n: (b, n, 0, r)),
                   pl.BlockSpec((None, None, SUB_WIN, LANES), lambda b, r, n: (b, n, 0, r))],
        out_shape=[jax.ShapeDtypeStruct((bsz, nb, SUB_WIN, d * D_MODEL), _F32),
                   jax.ShapeDtypeStruct((bsz, nb, SUB_WIN, d * LANES), _F32)],
        compiler_params=_cparams("parallel", "parallel", "arbitrary"),
        name=f"dilated_attn_g{g}",
    )(qkv4, qkv4, qkv4, qkv4, qkv4)
    return o.reshape(bsz * seq, D_MODEL), lse.reshape(bsz * seq, LANES)


HALO = SUBLANES
ZX_COLS = SSM_INNER + SSM_XBC


def _softplus(x):
    return jnp.maximum(x, 0.0) + jnp.log1p(jnp.exp(-jnp.abs(x)))


def _ssd_kernel(zx_ref, halo_ref, dt_ref, cw_ref, cb_ref, dtb_ref, a_ref, dsk_ref, nw_ref,
                o_ref, state_ref, ext_ref, y_ref):
    c = pl.program_id(1)
    q = CHUNK

    @pl.when(c == 0)
    def _():
        state_ref[...] = jnp.zeros_like(state_ref)
        ext_ref[0:HALO, :] = jnp.zeros((HALO, SSM_XBC), _F32)

    @pl.when(c > 0)
    def _():
        ext_ref[0:HALO, :] = halo_ref[:, SSM_INNER:]

    ext_ref[HALO:, :] = zx_ref[:, SSM_INNER:]
    conv = cb_ref[...]
    for j in range(SSM_CONV):
        off = HALO - (SSM_CONV - 1) + j
        conv = conv + cw_ref[j:j + 1, :] * ext_ref[off:off + q, :]
    xc = conv * jax.nn.sigmoid(conv)
    bm = xc[:, SSM_INNER:SSM_INNER + SSM_BC].astype(_BF16)
    cm = xc[:, SSM_INNER + SSM_BC:].astype(_BF16)

    dt = _softplus(dt_ref[...] + dtb_ref[...])
    da = dt * a_ref[...]
    r_io = lax.broadcasted_iota(jnp.int32, (q, q), 0)
    c_io = lax.broadcasted_iota(jnp.int32, (q, q), 1)
    causal = r_io >= c_io
    tri = causal.astype(_F32)
    a_cum = jnp.dot(tri, da, preferred_element_type=_F32, precision=lax.Precision.HIGHEST)
    a_cum_t = a_cum.T
    dt_t = dt.T
    a_last = a_cum[q - 1:q, :]
    w_state = jnp.exp(a_last - a_cum) * dt
    e_acum = jnp.exp(a_cum)
    chunk_decay = jnp.exp(a_last)
    lane = lax.broadcasted_iota(jnp.int32, (q, LANES), 1)
    low = lane < SSM_P
    lane1 = lax.broadcasted_iota(jnp.int32, (1, LANES), 1)
    low1 = lane1 < SSM_P
    tn = (((0,), (0,)), ((), ()))
    nt = (((1,), (1,)), ((), ()))

    for g in range(SSM_GROUPS):
        bg = bm[:, g * SSM_N:(g + 1) * SSM_N]
        cg = cm[:, g * SSM_N:(g + 1) * SSM_N]
        cbm = lax.dot_general(cg, bg, nt, preferred_element_type=_F32)
        st = state_ref[g]
        y_off = jnp.dot(cg, st.astype(_BF16), preferred_element_type=_F32)
        xw_parts, dec_parts = [], []
        for j in range(SSM_GHEADS // 2):
            h0 = g * SSM_GHEADS + 2 * j
            col = (g * SSM_GHEADS // 2 + j) * LANES
            xp = xc[:, col:col + LANES]
            y_diag = jnp.zeros((q, LANES), _F32)
            for half, hh in enumerate((h0, h0 + 1)):
                diff = a_cum[:, hh:hh + 1] - a_cum_t[hh:hh + 1, :]
                lmat = jnp.exp(jnp.where(causal, diff, NEG_BIG))
                mm = (cbm * lmat * dt_t[hh:hh + 1, :]).astype(_BF16)
                keep = low if half == 0 else jnp.logical_not(low)
                xh = jnp.where(keep, xp, 0.0).astype(_BF16)
                y_diag = y_diag + jnp.dot(mm, xh, preferred_element_type=_F32)
            e_pair = jnp.where(low, e_acum[:, h0:h0 + 1], e_acum[:, h0 + 1:h0 + 2])
            w_pair = jnp.where(low, w_state[:, h0:h0 + 1], w_state[:, h0 + 1:h0 + 2])
            y_ref[:, col:col + LANES] = (y_diag + y_off[:, j * LANES:(j + 1) * LANES] * e_pair
                                         + xp * dsk_ref[:, col:col + LANES])
            xw_parts.append((xp * w_pair).astype(_BF16))
            dec_parts.append(jnp.where(low1, chunk_decay[:, h0:h0 + 1], chunk_decay[:, h0 + 1:h0 + 2]))
        xw = jnp.concatenate(xw_parts, axis=1)
        dec = jnp.concatenate(dec_parts, axis=1)
        state_ref[g] = st * dec + lax.dot_general(bg, xw, tn, preferred_element_type=_F32)

    z = zx_ref[:, :SSM_INNER]
    yg = y_ref[...] * (z * jax.nn.sigmoid(z))
    gw = SSM_INNER // SSM_GROUPS
    for g in range(SSM_GROUPS):
        v = yg[:, g * gw:(g + 1) * gw]
        ms = jnp.mean(v * v, axis=-1, keepdims=True)
        o_ref[:, g * gw:(g + 1) * gw] = (v * lax.rsqrt(ms + RMS_EPS)
                                         * nw_ref[:, g * gw:(g + 1) * gw]).astype(o_ref.dtype)


def ssd_mixer(zx, dt_raw, conv_w, conv_b, dt_bias, a_log, d_skip, norm_w, bsz, seq):
    nc = seq // CHUNK
    per8 = CHUNK // HALO
    pad = LANES - SSM_HEADS
    dtb = jnp.pad(dt_bias, (0, pad)).reshape(1, LANES)
    a = jnp.pad(-jnp.exp(a_log.astype(_F32)), (0, pad)).reshape(1, LANES)
    dsk = jnp.repeat(d_skip.astype(_F32), SSM_P).reshape(1, SSM_INNER)
    cst = lambda shape: pl.BlockSpec(shape, lambda b, c: (0,) * len(shape))
    return pl.pallas_call(
        _ssd_kernel,
        grid=(bsz, nc),
        in_specs=[pl.BlockSpec((CHUNK, ZX_COLS), lambda b, c: (b * nc + c, 0)),
                  pl.BlockSpec((HALO, ZX_COLS), lambda b, c: (jnp.maximum((b * nc + c) * per8 - 1, 0), 0)),
                  pl.BlockSpec((CHUNK, LANES), lambda b, c: (b * nc + c, 0)),
                  cst((SSM_CONV, SSM_XBC)), cst((1, SSM_XBC)), cst((1, LANES)), cst((1, LANES)),
                  cst((1, SSM_INNER)), cst((1, SSM_INNER))],
        out_specs=pl.BlockSpec((CHUNK, SSM_INNER), lambda b, c: (b * nc + c, 0)),
        out_shape=jax.ShapeDtypeStruct((bsz * seq, SSM_INNER), _BF16),
        scratch_shapes=[pltpu.VMEM((SSM_GROUPS, SSM_N, SSM_GHEADS * SSM_P), _F32),
                        pltpu.VMEM((HALO + CHUNK, SSM_XBC), _F32),
                        pltpu.VMEM((CHUNK, SSM_INNER), _F32)],
        compiler_params=_cparams("parallel", "arbitrary"), name="ssd_mixer",
    )(zx, zx, dt_raw, conv_w, conv_b.reshape(1, -1), dtb, a, dsk, norm_w.reshape(1, -1))


ROUTE_TM = 256


def _layer_norm(y, g, b):
    mu = jnp.mean(y, axis=-1, keepdims=True)
    yc = y - mu
    var = jnp.mean(yc * yc, axis=-1, keepdims=True)
    return yc * lax.rsqrt(var + LN_EPS) * g + b


def _route(h1, wrt_ref, br_ref, eidx_ref, gate_ref, rank_ref, cnt_ref, run_ref):
    m_rows = h1.shape[0]
    nt = (((1,), (1,)), ((), ()))
    logits = lax.dot_general(wrt_ref[...], h1, nt, preferred_element_type=_F32,
                             precision=lax.Precision.HIGHEST) + br_ref[...]
    eio = lax.broadcasted_iota(jnp.int32, logits.shape, 0)
    vals = logits
    tops, ids, hots = [], [], []
    for _ in range(TOP_K):
        mk = jnp.max(vals, axis=0, keepdims=True)
        ik = jnp.min(jnp.where(vals == mk, eio, N_EXPERTS), axis=0, keepdims=True)
        hot = eio == ik
        vals = jnp.where(hot, -jnp.inf, vals)
        tops.append(mk)
        ids.append(ik)
        hots.append(hot)
    exps = [jnp.exp(t - tops[0]) for t in tops]
    den = exps[0] + exps[1] + exps[2] + exps[3]
    hot_all = (hots[0] | hots[1] | hots[2] | hots[3])
    r_io = lax.broadcasted_iota(jnp.int32, (m_rows, m_rows), 0)
    c_io = lax.broadcasted_iota(jnp.int32, (m_rows, m_rows), 1)
    upper = (r_io < c_io).astype(_BF16)
    prefix = jnp.dot(hot_all.astype(_BF16), upper, preferred_element_type=_F32)
    base = prefix + run_ref[...]
    gates = [e / den for e in exps]
    for k in range(TOP_K):
        eidx_ref[k:k + 1, :] = ids[k]
        rank_ref[k:k + 1, :] = jnp.sum(jnp.where(hots[k], base, 0.0), axis=0,
                                       keepdims=True).astype(jnp.int32)
    g8 = jnp.concatenate(gates + [jnp.zeros((SUBLANES - TOP_K, m_rows), _F32)], axis=0)
    gate_ref[...] = g8.T
    run_new = run_ref[...] + jnp.sum(hot_all.astype(_F32), axis=1, keepdims=True)
    run_ref[...] = run_new
    cnt_ref[...] = jnp.broadcast_to(run_new, cnt_ref.shape)


def _attn_out_kernel(o0_ref, o1_ref, o2_ref, l0_ref, l1_ref, l2_ref, x_ref, wo_ref, g_ref, b_ref,
                     wrt_ref, br_ref, h_ref, eidx_ref, gate_ref, rank_ref, cnt_ref, run_ref, ob_ref):
    @pl.when(pl.program_id(0) == 0)
    def _():
        run_ref[...] = jnp.zeros_like(run_ref)

    l0, l1, l2 = l0_ref[...], l1_ref[...], l2_ref[...]
    m = jnp.maximum(jnp.maximum(l0, l1), l2)
    e0, e1, e2 = jnp.exp(l0 - m), jnp.exp(l1 - m), jnp.exp(l2 - m)
    inv = 1.0 / (e0 + e1 + e2)
    w0, w1, w2 = e0 * inv, e1 * inv, e2 * inv
    for h in range(HEADS):
        sl = slice(h * HEAD_DIM, (h + 1) * HEAD_DIM)
        o = (w0[:, h:h + 1] * o0_ref[:, sl] + w1[:, h:h + 1] * o1_ref[:, sl]
             + w2[:, h:h + 1] * o2_ref[:, sl])
        ob_ref[:, sl] = o.astype(_BF16)
    mix = jnp.dot(ob_ref[...], wo_ref[...], preferred_element_type=_F32)
    h1 = _layer_norm(DEEPNORM_ALPHA * x_ref[...] + mix, g_ref[...], b_ref[...])
    h_ref[...] = h1
    _route(h1, wrt_ref, br_ref, eidx_ref, gate_ref, rank_ref, cnt_ref, run_ref)


def _ssm_out_kernel(y_ref, x_ref, wo_ref, g_ref, b_ref, wrt_ref, br_ref,
                    h_ref, eidx_ref, gate_ref, rank_ref, cnt_ref, run_ref):
    @pl.when(pl.program_id(0) == 0)
    def _():
        run_ref[...] = jnp.zeros_like(run_ref)

    mix = jnp.dot(y_ref[...], wo_ref[...], preferred_element_type=_F32)
    h1 = _layer_norm(DEEPNORM_ALPHA * x_ref[...] + mix, g_ref[...], b_ref[...])
    h_ref[...] = h1
    _route(h1, wrt_ref, br_ref, eidx_ref, gate_ref, rank_ref, cnt_ref, run_ref)


def _route_out_specs(t):
    tm = ROUTE_TM
    specs = [pl.BlockSpec((tm, D_MODEL), lambda i: (i, 0)),
             pl.BlockSpec((TOP_K, tm), lambda i: (0, i)),
             pl.BlockSpec((tm, SUBLANES), lambda i: (i, 0)),
             pl.BlockSpec((TOP_K, tm), lambda i: (0, i)),
             pl.BlockSpec((N_EXPERTS, LANES), lambda i: (0, 0))]
    shapes = [jax.ShapeDtypeStruct((t, D_MODEL), _F32),
              jax.ShapeDtypeStruct((TOP_K, t), jnp.int32),
              jax.ShapeDtypeStruct((t, SUBLANES), _F32),
              jax.ShapeDtypeStruct((TOP_K, t), jnp.int32),
              jax.ShapeDtypeStruct((N_EXPERTS, LANES), _F32)]
    return specs, shapes


def _const_spec(shape):
    return pl.BlockSpec(shape, lambda i: (0,) * len(shape))


def attn_out_norm_route(os_, lses, x, wo_bf16, ln_g, ln_b, w_router, b_router):
    t = x.shape[0]
    tm = ROUTE_TM
    row = lambda w: pl.BlockSpec((tm, w), lambda i: (i, 0))
    out_specs, out_shapes = _route_out_specs(t)
    return pl.pallas_call(
        _attn_out_kernel,
        grid=(t // tm,),
        in_specs=[row(D_MODEL)] * 3 + [row(LANES)] * 3 + [row(D_MODEL),
                  _const_spec((D_MODEL, D_MODEL)), _const_spec((1, D_MODEL)), _const_spec((1, D_MODEL)),
                  _const_spec((N_EXPERTS, D_MODEL)), _const_spec((N_EXPERTS, 1))],
        out_specs=out_specs, out_shape=out_shapes,
        scratch_shapes=[pltpu.VMEM((N_EXPERTS, 1), _F32), pltpu.VMEM((tm, D_MODEL), _BF16)],
        compiler_params=_cparams("arbitrary"), name="attn_out_norm_route",
    )(*os_, *lses, x, wo_bf16, ln_g.reshape(1, -1), ln_b.reshape(1, -1),
      w_router.T, b_router.reshape(-1, 1))


def ssm_out_norm_route(yn, x, wo_bf16, ln_g, ln_b, w_router, b_router):
    t = x.shape[0]
    tm = ROUTE_TM
    row = lambda w: pl.BlockSpec((tm, w), lambda i: (i, 0))
    out_specs, out_shapes = _route_out_specs(t)
    return pl.pallas_call(
        _ssm_out_kernel,
        grid=(t // tm,),
        in_specs=[row(SSM_INNER), row(D_MODEL),
                  _const_spec((SSM_INNER, D_MODEL)), _const_spec((1, D_MODEL)), _const_spec((1, D_MODEL)),
                  _const_spec((N_EXPERTS, D_MODEL)), _const_spec((N_EXPERTS, 1))],
        out_specs=out_specs, out_shape=out_shapes,
        scratch_shapes=[pltpu.VMEM((N_EXPERTS, 1), _F32)],
        compiler_params=_cparams("arbitrary"), name="ssm_out_norm_route",
    )(yn, x, wo_bf16, ln_g.reshape(1, -1), ln_b.reshape(1, -1), w_router.T, b_router.reshape(-1, 1))


MOE_TM = 256
MOE_BM = 256


def _dest_kernel(eidx_ref, rank_ref, offs_ref, dest_ref):
    eidx = eidx_ref[...]
    acc = rank_ref[...]
    for e in range(N_EXPERTS):
        acc = acc + jnp.where(eidx == e, offs_ref[e], 0)
    dest_ref[...] = acc


def moe_dest(eidx, rank, offs):
    t = eidx.shape[1]
    tn = min(4096, t)
    blk = pl.BlockSpec((TOP_K, tn), lambda i: (0, i))
    return pl.pallas_call(
        _dest_kernel, grid=(t // tn,),
        in_specs=[blk, blk, pl.BlockSpec(memory_space=pltpu.SMEM)],
        out_specs=blk, out_shape=jax.ShapeDtypeStruct((TOP_K, t), jnp.int32),
        compiler_params=_cparams("parallel"), name="moe_dest",
    )(eidx, rank, offs)


def _tile_major(dest, tm):
    t = dest.shape[1]
    return dest.reshape(TOP_K, t // tm, tm).transpose(1, 0, 2)


def _dispatch_kernel(dest_ref, h_ref, xs_ref, sem):
    tm = h_ref.shape[0]

    def issue(t, carry):
        for k in range(TOP_K):
            pltpu.make_async_copy(h_ref.at[pl.ds(t, 1)], xs_ref.at[pl.ds(dest_ref[k, t], 1)], sem).start()
        return carry

    lax.fori_loop(0, tm, issue, 0, unroll=8)
    for k in range(TOP_K):
        pltpu.make_async_copy(h_ref, xs_ref.at[pl.ds(0, tm)], sem).wait()


def moe_dispatch(h, dest):
    t = h.shape[0]
    tm = MOE_TM
    return pl.pallas_call(
        _dispatch_kernel, grid=(t // tm,),
        in_specs=[pl.BlockSpec((None, TOP_K, tm), lambda i: (i, 0, 0), memory_space=pltpu.SMEM),
                  pl.BlockSpec((tm, D_MODEL), lambda i: (i, 0))],
        out_specs=pl.BlockSpec(memory_space=pl.ANY),
        out_shape=jax.ShapeDtypeStruct((t * TOP_K, D_MODEL), _F32),
        scratch_shapes=[pltpu.SemaphoreType.DMA(())],
        compiler_params=_cparams("arbitrary"), name="moe_dispatch",
    )(_tile_major(dest, tm), h)


def _expert_kernel(blk_ref, e_ref, lo_ref, hi_ref, xs_ref, wu_ref, bu_ref, wd_ref, bd_ref,
                   ys_ref, wub_ref, wdb_ref):
    i = pl.program_id(0)
    e = e_ref[i]
    e_prev = e_ref[jnp.maximum(i - 1, 0)]

    @pl.when((i == 0) | (e != e_prev))
    def _():
        wub_ref[...] = wu_ref[...].astype(_BF16)
        wdb_ref[...] = wd_ref[...].astype(_BF16)

    lo = lo_ref[i]
    hi = hi_ref[i]

    @pl.when(hi > lo)
    def _():
        x = xs_ref[...].astype(_BF16)
        h = jnp.dot(x, wub_ref[...], preferred_element_type=_F32) + bu_ref[...]
        gate = jnp.minimum(h[:, :D_FF], SWIGLU_LIMIT)
        up = jnp.clip(h[:, D_FF:], -SWIGLU_LIMIT, SWIGLU_LIMIT)
        glu = gate * jax.nn.sigmoid(SWIGLU_ALPHA * gate)
        act = ((up + 1.0) * glu).astype(_BF16)
        y = jnp.dot(act, wdb_ref[...], preferred_element_type=_F32) + bd_ref[...]
        row = lax.broadcasted_iota(jnp.int32, y.shape, 0)
        y = jnp.where((row >= lo) & (row < hi), y, 0.0)

        @pl.when(lo == 0)
        def _():
            ys_ref[...] = y

        @pl.when(lo > 0)
        def _():
            ys_ref[...] += y


def _expert_schedule(offs, n_rows, bm):
    nblk = n_rows // bm
    cuts = jnp.sort(jnp.concatenate([jnp.arange(nblk, dtype=jnp.int32) * bm, offs[1:]]))
    ends = jnp.concatenate([cuts[1:], jnp.array([n_rows], jnp.int32)])
    blk = jnp.minimum(cuts // bm, nblk - 1)
    e = jnp.minimum(jnp.searchsorted(offs[1:], cuts, side="right"), N_EXPERTS - 1).astype(jnp.int32)
    lo = cuts - blk * bm
    hi = ends - blk * bm
    return blk.astype(jnp.int32), e, lo.astype(jnp.int32), hi.astype(jnp.int32)


def moe_experts(xs, offs, w_up, b_up, w_down, b_down):
    n_rows = xs.shape[0]
    bm = MOE_BM
    blk, e, lo, hi = _expert_schedule(offs, n_rows, bm)
    n_items = n_rows // bm + N_EXPERTS - 1
    grid_spec = pltpu.PrefetchScalarGridSpec(
        num_scalar_prefetch=4, grid=(n_items,),
        in_specs=[pl.BlockSpec((bm, D_MODEL), lambda i, b, e, lo, hi: (b[i], 0)),
                  pl.BlockSpec((None, D_MODEL, 2 * D_FF), lambda i, b, e, lo, hi: (e[i], 0, 0)),
                  pl.BlockSpec((None, 1, 2 * D_FF), lambda i, b, e, lo, hi: (e[i], 0, 0)),
                  pl.BlockSpec((None, D_FF, D_MODEL), lambda i, b, e, lo, hi: (e[i], 0, 0)),
                  pl.BlockSpec((None, 1, D_MODEL), lambda i, b, e, lo, hi: (e[i], 0, 0))],
        out_specs=pl.BlockSpec((bm, D_MODEL), lambda i, b, e, lo, hi: (b[i], 0)),
        scratch_shapes=[pltpu.VMEM((D_MODEL, 2 * D_FF), _BF16), pltpu.VMEM((D_FF, D_MODEL), _BF16)])
    return pl.pallas_call(
        _expert_kernel, grid_spec=grid_spec,
        out_shape=jax.ShapeDtypeStruct((n_rows, D_MODEL), _F32),
        compiler_params=_cparams("arbitrary"), name="moe_experts",
    )(blk, e, lo, hi, xs, w_up, b_up.reshape(N_EXPERTS, 1, -1), w_down, b_down.reshape(N_EXPERTS, 1, -1))


def _combine_kernel(dest_ref, ys_ref, h_ref, gate_ref, g_ref, b_ref, o_ref, buf_ref, sem):
    tm = h_ref.shape[0]

    def issue(t, carry):
        for k in range(TOP_K):
            pltpu.make_async_copy(ys_ref.at[pl.ds(dest_ref[k, t], 1)], buf_ref.at[k, pl.ds(t, 1)], sem).start()
        return carry

    lax.fori_loop(0, tm, issue, 0, unroll=8)
    for k in range(TOP_K):
        pltpu.make_async_copy(ys_ref.at[pl.ds(0, tm)], buf_ref.at[k], sem).wait()
    gates = gate_ref[...]
    y = DEEPNORM_ALPHA * h_ref[...]
    for k in range(TOP_K):
        y = y + gates[:, k:k + 1] * buf_ref[k]
    o_ref[...] = _layer_norm(y, g_ref[...], b_ref[...])


def moe_combine(ys, dest, h, gates, ln_g, ln_b):
    t = h.shape[0]
    tm = MOE_TM
    row = pl.BlockSpec((tm, D_MODEL), lambda i: (i, 0))
    return pl.pallas_call(
        _combine_kernel, grid=(t // tm,),
        in_specs=[pl.BlockSpec((None, TOP_K, tm), lambda i: (i, 0, 0), memory_space=pltpu.SMEM),
                  pl.BlockSpec(memory_space=pl.ANY), row,
                  pl.BlockSpec((tm, SUBLANES), lambda i: (i, 0)),
                  _const_spec((1, D_MODEL)), _const_spec((1, D_MODEL))],
        out_specs=row, out_shape=jax.ShapeDtypeStruct((t, D_MODEL), _F32),
        scratch_shapes=[pltpu.VMEM((TOP_K, tm, D_MODEL), _F32), pltpu.SemaphoreType.DMA(())],
        compiler_params=_cparams("arbitrary"), name="moe_combine",
    )(_tile_major(dest, tm), ys, h, gates, ln_g.reshape(1, -1), ln_b.reshape(1, -1))


def moe_layer(h, eidx, gates, rank, counts, w_up, b_up, w_down, b_down, ln_g, ln_b):
    cnt = counts[:, 0].astype(jnp.int32)
    offs = jnp.cumsum(cnt) - cnt
    dest = moe_dest(eidx, rank, offs)
    xs = moe_dispatch(h, dest)
    ys = moe_experts(xs, offs, w_up, b_up, w_down, b_down)
    return moe_combine(ys, dest, h, gates, ln_g, ln_b)


def kernel(x, positions, attn_w_in, attn_w_out, ssm_w_in, ssm_conv_w, ssm_conv_b, ssm_dt_bias,
           ssm_a_log, ssm_d, ssm_norm_w, ssm_w_out, moe_w_router, moe_b_router, moe_w_up,
           moe_b_up, moe_w_down, moe_b_down, ln_mix_g, ln_mix_b, ln_ffn_g, ln_ffn_b):
    bsz, seq, _ = x.shape
    xf = x.reshape(bsz * seq, D_MODEL)

    tabs = rope_tables(positions)
    qkv = dense_proj(xf, attn_w_in[0].astype(_BF16), tn=D_MODEL, out_dtype=_BF16, rope=tabs)
    outs = [dilated_group_attention(qkv, g, bsz, seq) for g in range(N_GROUPS)]
    h, eidx, gates, rank, counts = attn_out_norm_route(
        [o for o, _ in outs], [l for _, l in outs], xf, attn_w_out[0].astype(_BF16),
        ln_mix_g[0], ln_mix_b[0], moe_w_router[0], moe_b_router[0])
    xf = moe_layer(h, eidx, gates, rank, counts, moe_w_up[0], moe_b_up[0], moe_w_down[0],
                   moe_b_down[0], ln_ffn_g[0], ln_ffn_b[0])

    w_in = ssm_w_in[0]
    zx = dense_proj(xf, w_in[:, :ZX_COLS].astype(_BF16), tn=D_MODEL, out_dtype=_F32)
    w_dt = jnp.pad(w_in[:, ZX_COLS:], ((0, 0), (0, LANES - SSM_HEADS))).astype(_BF16)
    dt_raw = dense_proj(xf, w_dt, tn=LANES, out_dtype=_F32)
    yn = ssd_mixer(zx, dt_raw, ssm_conv_w[0], ssm_conv_b[0], ssm_dt_bias[0], ssm_a_log[0],
                   ssm_d[0], ssm_norm_w[0], bsz, seq)
    h, eidx, gates, rank, counts = ssm_out_norm_route(
        yn, xf, ssm_w_out[0].astype(_BF16), ln_mix_g[1], ln_mix_b[1], moe_w_router[1], moe_b_router[1])
    xf = moe_layer(h, eidx, gates, rank, counts, moe_w_up[1], moe_b_up[1], moe_w_down[1],
                   moe_b_down[1], ln_ffn_g[1], ln_ffn_b[1])
    return xf.reshape(bsz, seq, D_MODEL)
```

```python
import functools
import math

import jax
import jax.numpy as jnp
from jax import lax
from jax.experimental import pallas as pl
from jax.experimental.pallas import tpu as pltpu

D_MODEL = 1024
DEPTH = 2
DIL = (1, 4, 16)
N_GROUPS = 3
SUB_WIN = 128
HEAD_DIM = 128
HEADS = 8
ROT_DIM = 32
ROT_HALF = 16
ROPE_THETA = 500000.0
ATTN_PROJ = N_GROUPS * 3 * D_MODEL
SSM_INNER = 2048
SSM_P = 64
SSM_HEADS = 32
SSM_GROUPS = 4
SSM_GHEADS = 8
SSM_N = 128
SSM_CONV = 4
CHUNK = 128
SSM_BC = SSM_GROUPS * SSM_N
SSM_XBC = SSM_INNER + 2 * SSM_BC
N_EXPERTS = 32
TOP_K = 4
D_FF = 1024
SWIGLU_LIMIT = 7.0
SWIGLU_ALPHA = 1.702
DEEPNORM_ALPHA = (2 * DEPTH) ** 0.25
LN_EPS = 1e-5
RMS_EPS = 1e-5

LANES = 128
SUBLANES = 8
VMEM_LIMIT_BYTES = 56 * 1024 * 1024

NEG_BIG = -1e30

_F32 = jnp.float32
_BF16 = jnp.bfloat16


def _cparams(*sem):
    return pltpu.CompilerParams(dimension_semantics=sem, vmem_limit_bytes=VMEM_LIMIT_BYTES)


def _rope_table_kernel(pos_ref, invf_ref, c_ref, sa_ref, sb_ref):
    pos = pos_ref[...].astype(_F32)
    ang = pos * invf_ref[...]
    lane = lax.broadcasted_iota(jnp.int32, ang.shape, 1)
    cos = jnp.cos(ang)
    sin = jnp.sin(ang)
    c_ref[...] = jnp.where(lane < ROT_DIM, cos, 1.0)
    sa_ref[...] = jnp.where((lane >= ROT_HALF) & (lane < ROT_DIM), sin, 0.0)
    sb_ref[...] = jnp.where(lane < ROT_HALF, -sin, 0.0)


def rope_tables(positions):
    t = positions.size
    tm = 1024
    pos = positions.reshape(t, 1)
    j = jnp.arange(LANES)
    invf = jnp.where(j < ROT_DIM,
                     ROPE_THETA ** (-(2.0 * (j % ROT_HALF)).astype(_F32) / ROT_DIM), 0.0)
    invf = invf.astype(_F32).reshape(1, LANES)
    out = jax.ShapeDtypeStruct((t, LANES), _F32)
    return pl.pallas_call(
        _rope_table_kernel,
        grid=(t // tm,),
        in_specs=[pl.BlockSpec((tm, 1), lambda i: (i, 0)),
                  pl.BlockSpec((1, LANES), lambda i: (0, 0))],
        out_specs=[pl.BlockSpec((tm, LANES), lambda i: (i, 0))] * 3,
        out_shape=[out, out, out],
        compiler_params=_cparams("parallel"),
        name="rope_tables",
    )(pos, invf)


def _proj_kernel(x_ref, w_ref, o_ref, xb_ref):
    @pl.when(pl.program_id(1) == 0)
    def _():
        xb_ref[...] = x_ref[...].astype(_BF16)

    o_ref[...] = jnp.dot(xb_ref[...], w_ref[...], preferred_element_type=_F32).astype(o_ref.dtype)


def dense_proj(x, w_bf16, *, tn, out_dtype, tm=512):
    t, k = x.shape
    n = w_bf16.shape[1]
    return pl.pallas_call(
        _proj_kernel, grid=(t // tm, n // tn),
        in_specs=[pl.BlockSpec((tm, k), lambda i, j: (i, 0)), pl.BlockSpec((k, tn), lambda i, j: (0, j))],
        out_specs=pl.BlockSpec((tm, tn), lambda i, j: (i, j)),
        out_shape=jax.ShapeDtypeStruct((t, n), out_dtype), scratch_shapes=[pltpu.VMEM((tm, k), _BF16)],
        compiler_params=_cparams("parallel", "arbitrary"), name="dense_proj",
    )(x, w_bf16)


QKV_TM = 512
ATTN_ROWS = 2048
HEADS_PER_STEP = 2


def _qkv_proj_kernel(x_ref, w_ref, c_ref, sa_ref, sb_ref, o_ref, xb_ref, slab_ref, *, d, ns, tt):
    j = pl.program_id(1)

    @pl.when(j == 0)
    def _():
        xb_ref[...] = x_ref[...].astype(_BF16)

    acc = jnp.dot(xb_ref[...], w_ref[...], preferred_element_type=_F32)

    @pl.when(j == 2)
    def _():
        for h in range(HEADS):
            slab_ref[h] = acc[:, h * HEAD_DIM:(h + 1) * HEAD_DIM]

    @pl.when(j < 2)
    def _():
        c = c_ref[...]
        sa = sa_ref[...]
        sb = sb_ref[...]
        for h in range(HEADS):
            t = acc[:, h * HEAD_DIM:(h + 1) * HEAD_DIM]
            slab_ref[h] = (t * c + pltpu.roll(t, ROT_HALF, 1) * sa
                           + pltpu.roll(t, HEAD_DIM - ROT_HALF, 1) * sb)

    for s in range(ns):
        for r in range(d):
            rows = pl.ds(s * tt * d + r, tt, stride=d) if d > 1 else pl.ds(s * tt, tt)
            for h in range(HEADS):
                o_ref[s, r, :, h * HEAD_DIM:(h + 1) * HEAD_DIM] = slab_ref[h, rows, :].astype(o_ref.dtype)


def qkv_proj_group(x, w_bf16, rope, g):
    t, k = x.shape
    d = DIL[g]
    span = SUB_WIN * d
    tm = QKV_TM
    ns = max(1, tm // span)
    tps = max(1, span // tm)
    tt = tm // (ns * d)
    tab_spec = pl.BlockSpec((tm, LANES), lambda i, j: (i, 0))
    return pl.pallas_call(
        functools.partial(_qkv_proj_kernel, d=d, ns=ns, tt=tt),
        grid=(t // tm, 3),
        in_specs=[pl.BlockSpec((tm, k), lambda i, j: (i, 0)),
                  pl.BlockSpec((k, D_MODEL), lambda i, j: (0, g * 3 + j)),
                  tab_spec, tab_spec, tab_spec],
        out_specs=pl.BlockSpec((ns, d, tt, D_MODEL), lambda i, j: (i // tps, 0, i % tps, j)),
        out_shape=jax.ShapeDtypeStruct((t // span, d, SUB_WIN, 3 * D_MODEL), _BF16),
        scratch_shapes=[pltpu.VMEM((tm, k), _BF16), pltpu.VMEM((HEADS, tm, HEAD_DIM), _F32)],
        compiler_params=_cparams("parallel", "arbitrary"), name=f"qkv_proj_g{g}",
    )(x, w_bf16, *rope)


def _dil_attn_kernel(q_ref, k_ref, v_ref, kp_ref, vp_ref, o_ref, lse_ref, *, d, ns):
    j = pl.program_id(1)
    hp = pl.program_id(2)
    row = lax.broadcasted_iota(jnp.int32, (SUB_WIN, SUB_WIN), 0)
    col = lax.broadcasted_iota(jnp.int32, (SUB_WIN, SUB_WIN), 1)
    cur_ok = col <= row
    prev_tri = col >= row
    prev_first = prev_tri & (j > 0)
    lane = lax.broadcasted_iota(jnp.int32, (SUB_WIN, LANES), 1)
    scale = 1.0 / math.sqrt(HEAD_DIM)
    nt = (((1,), (1,)), ((), ()))

    @pl.when(hp == 0)
    def _():
        lse_ref[...] = jnp.zeros_like(lse_ref)

    for s in range(ns):
        for r in range(d):
            start = s * SUB_WIN * d + r
            rows = pl.ds(start, SUB_WIN, stride=d) if d > 1 else pl.ds(start, SUB_WIN)
            lse_rows = lse_ref[rows, :]
            for hh in range(HEADS_PER_STEP):
                sl = slice(hh * HEAD_DIM, (hh + 1) * HEAD_DIM)
                q = q_ref[s, r, :, sl]
                if s > 0:
                    kp, vp, prev_ok = k_ref[s - 1, r, :, sl], v_ref[s - 1, r, :, sl], prev_tri
                else:
                    kp, vp, prev_ok = kp_ref[0, r, :, sl], vp_ref[0, r, :, sl], prev_first
                sc = lax.dot_general(q, k_ref[s, r, :, sl], nt, preferred_element_type=_F32) * scale
                sp = lax.dot_general(q, kp, nt, preferred_element_type=_F32) * scale
                sc = jnp.where(cur_ok, sc, NEG_BIG)
                sp = jnp.where(prev_ok, sp, NEG_BIG)
                m = jnp.maximum(jnp.max(sc, axis=1, keepdims=True), jnp.max(sp, axis=1, keepdims=True))
                pc = jnp.exp(sc - m)
                pp = jnp.exp(sp - m)
                l = jnp.sum(pc, axis=1, keepdims=True) + jnp.sum(pp, axis=1, keepdims=True)
                acc = jnp.dot(pc.astype(_BF16), v_ref[s, r, :, sl], preferred_element_type=_F32)
                acc += jnp.dot(pp.astype(_BF16), vp, preferred_element_type=_F32)
                o_ref[hh, rows, :] = acc / l
                lse_rows = jnp.where(lane == hp * HEADS_PER_STEP + hh, m + jnp.log(l), lse_rows)
            lse_ref[rows, :] = lse_rows


def dilated_group_attention(qkv, g, bsz, seq):
    d = DIL[g]
    ns = ATTN_ROWS // (SUB_WIN * d)
    nsteps = seq // ATTN_ROWS
    hw = HEADS_PER_STEP * HEAD_DIM
    ncol = D_MODEL // hw
    cur = lambda comp: pl.BlockSpec((ns, d, SUB_WIN, hw), lambda b, j, hp: (b * nsteps + j, 0, 0, comp * ncol + hp))
    prev = lambda comp: pl.BlockSpec(
        (1, d, SUB_WIN, hw), lambda b, j, hp: (jnp.maximum((b * nsteps + j) * ns - 1, 0), 0, 0, comp * ncol + hp))
    return pl.pallas_call(
        functools.partial(_dil_attn_kernel, d=d, ns=ns),
        grid=(bsz, nsteps, HEADS // HEADS_PER_STEP),
        in_specs=[cur(0), cur(1), cur(2), prev(1), prev(2)],
        out_specs=[pl.BlockSpec((HEADS_PER_STEP, ATTN_ROWS, HEAD_DIM), lambda b, j, hp: (hp, b * nsteps + j, 0)),
                   pl.BlockSpec((ATTN_ROWS, LANES), lambda b, j, hp: (b * nsteps + j, 0))],
        out_shape=[jax.ShapeDtypeStruct((HEADS, bsz * seq, HEAD_DIM), _F32),
                   jax.ShapeDtypeStruct((bsz * seq, LANES), _F32)],
        compiler_params=_cparams("parallel", "parallel", "arbitrary"),
        name=f"dilated_attn_g{g}",
    )(qkv, qkv, qkv, qkv, qkv)


HALO = SUBLANES
ZX_COLS = SSM_INNER + SSM_XBC


def _softplus(x):
    return jnp.maximum(x, 0.0) + jnp.log1p(jnp.exp(-jnp.abs(x)))


def _ssd_kernel(zx_ref, halo_ref, dt_ref, cw_ref, cb_ref, dtb_ref, a_ref, dsk_ref, nw_ref,
                o_ref, state_ref, ext_ref, y_ref):
    c = pl.program_id(1)
    q = CHUNK

    @pl.when(c == 0)
    def _():
        state_ref[...] = jnp.zeros_like(state_ref)
        ext_ref[0:HALO, :] = jnp.zeros((HALO, SSM_XBC), _F32)

    @pl.when(c > 0)
    def _():
        ext_ref[0:HALO, :] = halo_ref[:, SSM_INNER:]

    ext_ref[HALO:, :] = zx_ref[:, SSM_INNER:]
    conv = cb_ref[...]
    for j in range(SSM_CONV):
        off = HALO - (SSM_CONV - 1) + j
        conv = conv + cw_ref[j:j + 1, :] * ext_ref[off:off + q, :]
    xc = conv * jax.nn.sigmoid(conv)
    bm = xc[:, SSM_INNER:SSM_INNER + SSM_BC].astype(_BF16)
    cm = xc[:, SSM_INNER + SSM_BC:].astype(_BF16)

    dt = _softplus(dt_ref[...] + dtb_ref[...])
    da = dt * a_ref[...]
    r_io = lax.broadcasted_iota(jnp.int32, (q, q), 0)
    c_io = lax.broadcasted_iota(jnp.int32, (q, q), 1)
    causal = r_io >= c_io
    tri = causal.astype(_F32)
    a_cum = jnp.dot(tri, da, preferred_element_type=_F32, precision=lax.Precision.HIGHEST)
    a_cum_t = a_cum.T
    dt_t = dt.T
    a_last = a_cum[q - 1:q, :]
    w_state = jnp.exp(a_last - a_cum) * dt
    e_acum = jnp.exp(a_cum)
    chunk_decay = jnp.exp(a_last)
    lane = lax.broadcasted_iota(jnp.int32, (q, LANES), 1)
    low = lane < SSM_P
    lane1 = lax.broadcasted_iota(jnp.int32, (1, LANES), 1)
    low1 = lane1 < SSM_P
    tn = (((0,), (0,)), ((), ()))
    nt = (((1,), (1,)), ((), ()))

    for g in range(SSM_GROUPS):
        bg = bm[:, g * SSM_N:(g + 1) * SSM_N]
        cg = cm[:, g * SSM_N:(g + 1) * SSM_N]
        cbm = lax.dot_general(cg, bg, nt, preferred_element_type=_F32)
        st = state_ref[g]
        y_off = jnp.dot(cg, st.astype(_BF16), preferred_element_type=_F32)
        xw_parts, dec_parts = [], []
        for j in range(SSM_GHEADS // 2):
            h0 = g * SSM_GHEADS + 2 * j
            col = (g * SSM_GHEADS // 2 + j) * LANES
            xp = xc[:, col:col + LANES]
            y_diag = jnp.zeros((q, LANES), _F32)
            for half, hh in enumerate((h0, h0 + 1)):
                diff = a_cum[:, hh:hh + 1] - a_cum_t[hh:hh + 1, :]
                lmat = jnp.exp(jnp.where(causal, diff, NEG_BIG))
                mm = (cbm * lmat * dt_t[hh:hh + 1, :]).astype(_BF16)
                keep = low if half == 0 else jnp.logical_not(low)
                xh = jnp.where(keep, xp, 0.0).astype(_BF16)
                y_diag = y_diag + jnp.dot(mm, xh, preferred_element_type=_F32)
            e_pair = jnp.where(low, e_acum[:, h0:h0 + 1], e_acum[:, h0 + 1:h0 + 2])
            w_pair = jnp.where(low, w_state[:, h0:h0 + 1], w_state[:, h0 + 1:h0 + 2])
            y_ref[:, col:col + LANES] = (y_diag + y_off[:, j * LANES:(j + 1) * LANES] * e_pair
                                         + xp * dsk_ref[:, col:col + LANES])
            xw_parts.append((xp * w_pair).astype(_BF16))
            dec_parts.append(jnp.where(low1, chunk_decay[:, h0:h0 + 1], chunk_decay[:, h0 + 1:h0 + 2]))
        xw = jnp.concatenate(xw_parts, axis=1)
        dec = jnp.concatenate(dec_parts, axis=1)
        state_ref[g] = st * dec + lax.dot_general(bg, xw, tn, preferred_element_type=_F32)

    z = zx_ref[:, :SSM_INNER]
    yg = y_ref[...] * (z * jax.nn.sigmoid(z))
    gw = SSM_INNER // SSM_GROUPS
    for g in range(SSM_GROUPS):
        v = yg[:, g * gw:(g + 1) * gw]
        ms = jnp.mean(v * v, axis=-1, keepdims=True)
        o_ref[:, g * gw:(g + 1) * gw] = (v * lax.rsqrt(ms + RMS_EPS)
                                         * nw_ref[:, g * gw:(g + 1) * gw]).astype(o_ref.dtype)


def ssd_mixer(zx, dt_raw, conv_w, conv_b, dt_bias, a_log, d_skip, norm_w, bsz, seq):
    nc = seq // CHUNK
    per8 = CHUNK // HALO
    pad = LANES - SSM_HEADS
    dtb = jnp.pad(dt_bias, (0, pad)).reshape(1, LANES)
    a = jnp.pad(-jnp.exp(a_log.astype(_F32)), (0, pad)).reshape(1, LANES)
    dsk = jnp.repeat(d_skip.astype(_F32), SSM_P).reshape(1, SSM_INNER)
    cst = lambda shape: pl.BlockSpec(shape, lambda b, c: (0,) * len(shape))
    return pl.pallas_call(
        _ssd_kernel,
        grid=(bsz, nc),
        in_specs=[pl.BlockSpec((CHUNK, ZX_COLS), lambda b, c: (b * nc + c, 0)),
                  pl.BlockSpec((HALO, ZX_COLS), lambda b, c: (jnp.maximum((b * nc + c) * per8 - 1, 0), 0)),
                  pl.BlockSpec((CHUNK, LANES), lambda b, c: (b * nc + c, 0)),
                  cst((SSM_CONV, SSM_XBC)), cst((1, SSM_XBC)), cst((1, LANES)), cst((1, LANES)),
                  cst((1, SSM_INNER)), cst((1, SSM_INNER))],
        out_specs=pl.BlockSpec((CHUNK, SSM_INNER), lambda b, c: (b * nc + c, 0)),
        out_shape=jax.ShapeDtypeStruct((bsz * seq, SSM_INNER), _BF16),
        scratch_shapes=[pltpu.VMEM((SSM_GROUPS, SSM_N, SSM_GHEADS * SSM_P), _F32),
                        pltpu.VMEM((HALO + CHUNK, SSM_XBC), _F32),
                        pltpu.VMEM((CHUNK, SSM_INNER), _F32)],
        compiler_params=_cparams("parallel", "arbitrary"), name="ssd_mixer",
    )(zx, zx, dt_raw, conv_w, conv_b.reshape(1, -1), dtb, a, dsk, norm_w.reshape(1, -1))


ROUTE_TM = 256


def _layer_norm(y, g, b):
    mu = jnp.mean(y, axis=-1, keepdims=True)
    yc = y - mu
    var = jnp.mean(yc * yc, axis=-1, keepdims=True)
    return yc * lax.rsqrt(var + LN_EPS) * g + b


def _route(h1, wrt_ref, br_ref, eidx_ref, gate_ref, rank_ref, cnt_ref, run_ref):
    m_rows = h1.shape[0]
    nt = (((1,), (1,)), ((), ()))
    logits = lax.dot_general(wrt_ref[...], h1, nt, preferred_element_type=_F32,
                             precision=lax.Precision.HIGHEST) + br_ref[...]
    eio = lax.broadcasted_iota(jnp.int32, logits.shape, 0)
    vals = logits
    tops, ids, hots = [], [], []
    for _ in range(TOP_K):
        mk = jnp.max(vals, axis=0, keepdims=True)
        ik = jnp.min(jnp.where(vals == mk, eio, N_EXPERTS), axis=0, keepdims=True)
        hot = eio == ik
        vals = jnp.where(hot, -jnp.inf, vals)
        tops.append(mk)
        ids.append(ik)
        hots.append(hot)
    exps = [jnp.exp(t - tops[0]) for t in tops]
    den = exps[0] + exps[1] + exps[2] + exps[3]
    hot_all = (hots[0] | hots[1] | hots[2] | hots[3])
    r_io = lax.broadcasted_iota(jnp.int32, (m_rows, m_rows), 0)
    c_io = lax.broadcasted_iota(jnp.int32, (m_rows, m_rows), 1)
    upper = (r_io < c_io).astype(_BF16)
    prefix = jnp.dot(hot_all.astype(_BF16), upper, preferred_element_type=_F32)
    base = prefix + run_ref[...]
    gates = [e / den for e in exps]
    for k in range(TOP_K):
        eidx_ref[k:k + 1, :] = ids[k]
        rank_ref[k:k + 1, :] = jnp.sum(jnp.where(hots[k], base, 0.0), axis=0,
                                       keepdims=True).astype(jnp.int32)
    g8 = jnp.concatenate(gates + [jnp.zeros((SUBLANES - TOP_K, m_rows), _F32)], axis=0)
    gate_ref[...] = g8.T
    run_new = run_ref[...] + jnp.sum(hot_all.astype(_F32), axis=1, keepdims=True)
    run_ref[...] = run_new
    cnt_ref[...] = jnp.broadcast_to(run_new, cnt_ref.shape)


def _attn_out_kernel(o0_ref, o1_ref, o2_ref, l0_ref, l1_ref, l2_ref, x_ref, wo_ref, g_ref, b_ref,
                     wrt_ref, br_ref, h_ref, eidx_ref, gate_ref, rank_ref, cnt_ref, run_ref, ob_ref):
    @pl.when(pl.program_id(0) == 0)
    def _():
        run_ref[...] = jnp.zeros_like(run_ref)

    l0, l1, l2 = l0_ref[...], l1_ref[...], l2_ref[...]
    m = jnp.maximum(jnp.maximum(l0, l1), l2)
    e0, e1, e2 = jnp.exp(l0 - m), jnp.exp(l1 - m), jnp.exp(l2 - m)
    inv = 1.0 / (e0 + e1 + e2)
    w0, w1, w2 = e0 * inv, e1 * inv, e2 * inv
    for h in range(HEADS):
        o = w0[:, h:h + 1] * o0_ref[h] + w1[:, h:h + 1] * o1_ref[h] + w2[:, h:h + 1] * o2_ref[h]
        ob_ref[:, h * HEAD_DIM:(h + 1) * HEAD_DIM] = o.astype(_BF16)
    mix = jnp.dot(ob_ref[...], wo_ref[...], preferred_element_type=_F32)
    h1 = _layer_norm(DEEPNORM_ALPHA * x_ref[...] + mix, g_ref[...], b_ref[...])
    h_ref[...] = h1
    _route(h1, wrt_ref, br_ref, eidx_ref, gate_ref, rank_ref, cnt_ref, run_ref)


def _ssm_out_kernel(y_ref, x_ref, wo_ref, g_ref, b_ref, wrt_ref, br_ref,
                    h_ref, eidx_ref, gate_ref, rank_ref, cnt_ref, run_ref):
    @pl.when(pl.program_id(0) == 0)
    def _():
        run_ref[...] = jnp.zeros_like(run_ref)

    mix = jnp.dot(y_ref[...], wo_ref[...], preferred_element_type=_F32)
    h1 = _layer_norm(DEEPNORM_ALPHA * x_ref[...] + mix, g_ref[...], b_ref[...])
    h_ref[...] = h1
    _route(h1, wrt_ref, br_ref, eidx_ref, gate_ref, rank_ref, cnt_ref, run_ref)


def _route_out_specs(t):
    tm = ROUTE_TM
    specs = [pl.BlockSpec((tm, D_MODEL), lambda i: (i, 0)),
             pl.BlockSpec((TOP_K, tm), lambda i: (0, i)),
             pl.BlockSpec((tm, SUBLANES), lambda i: (i, 0)),
             pl.BlockSpec((TOP_K, tm), lambda i: (0, i)),
             pl.BlockSpec((N_EXPERTS, LANES), lambda i: (0, 0))]
    shapes = [jax.ShapeDtypeStruct((t, D_MODEL), _F32),
              jax.ShapeDtypeStruct((TOP_K, t), jnp.int32),
              jax.ShapeDtypeStruct((t, SUBLANES), _F32),
              jax.ShapeDtypeStruct((TOP_K, t), jnp.int32),
              jax.ShapeDtypeStruct((N_EXPERTS, LANES), _F32)]
    return specs, shapes


def _const_spec(shape):
    return pl.BlockSpec(shape, lambda i: (0,) * len(shape))


def attn_out_norm_route(os_, lses, x, wo_bf16, ln_g, ln_b, w_router, b_router):
    t = x.shape[0]
    tm = ROUTE_TM
    row = lambda w: pl.BlockSpec((tm, w), lambda i: (i, 0))
    out_specs, out_shapes = _route_out_specs(t)
    return pl.pallas_call(
        _attn_out_kernel,
        grid=(t // tm,),
        in_specs=[pl.BlockSpec((HEADS, tm, HEAD_DIM), lambda i: (0, i, 0))] * 3 + [row(LANES)] * 3 + [row(D_MODEL),
                  _const_spec((D_MODEL, D_MODEL)), _const_spec((1, D_MODEL)), _const_spec((1, D_MODEL)),
                  _const_spec((N_EXPERTS, D_MODEL)), _const_spec((N_EXPERTS, 1))],
        out_specs=out_specs, out_shape=out_shapes,
        scratch_shapes=[pltpu.VMEM((N_EXPERTS, 1), _F32), pltpu.VMEM((tm, D_MODEL), _BF16)],
        compiler_params=_cparams("arbitrary"), name="attn_out_norm_route",
    )(*os_, *lses, x, wo_bf16, ln_g.reshape(1, -1), ln_b.reshape(1, -1),
      w_router.T, b_router.reshape(-1, 1))


def ssm_out_norm_route(yn, x, wo_bf16, ln_g, ln_b, w_router, b_router):
    t = x.shape[0]
    tm = ROUTE_TM
    row = lambda w: pl.BlockSpec((tm, w), lambda i: (i, 0))
    out_specs, out_shapes = _route_out_specs(t)
    return pl.pallas_call(
        _ssm_out_kernel,
        grid=(t // tm,),
        in_specs=[row(SSM_INNER), row(D_MODEL),
                  _const_spec((SSM_INNER, D_MODEL)), _const_spec((1, D_MODEL)), _const_spec((1, D_MODEL)),
                  _const_spec((N_EXPERTS, D_MODEL)), _const_spec((N_EXPERTS, 1))],
        out_specs=out_specs, out_shape=out_shapes,
        scratch_shapes=[pltpu.VMEM((N_EXPERTS, 1), _F32)],
        compiler_params=_cparams("arbitrary"), name="ssm_out_norm_route",
    )(yn, x, wo_bf16, ln_g.reshape(1, -1), ln_b.reshape(1, -1), w_router.T, b_router.reshape(-1, 1))


MOE_TM = 256
MOE_BM = 256


def _dest_kernel(eidx_ref, rank_ref, offs_ref, dest_ref):
    eidx = eidx_ref[...]
    acc = rank_ref[...]
    for e in range(N_EXPERTS):
        acc = acc + jnp.where(eidx == e, offs_ref[e], 0)
    dest_ref[...] = acc


def moe_dest(eidx, rank, offs):
    t = eidx.shape[1]
    tn = min(4096, t)
    blk = pl.BlockSpec((TOP_K, tn), lambda i: (0, i))
    return pl.pallas_call(
        _dest_kernel, grid=(t // tn,),
        in_specs=[blk, blk, pl.BlockSpec(memory_space=pltpu.SMEM)],
        out_specs=blk, out_shape=jax.ShapeDtypeStruct((TOP_K, t), jnp.int32),
        compiler_params=_cparams("parallel"), name="moe_dest",
    )(eidx, rank, offs)


def _tile_major(dest, tm):
    t = dest.shape[1]
    return dest.reshape(TOP_K, t // tm, tm).transpose(1, 0, 2)


def _dispatch_kernel(dest_ref, h_ref, xs_ref, sem):
    tm = h_ref.shape[0]

    def issue(t, carry):
        for k in range(TOP_K):
            pltpu.make_async_copy(h_ref.at[pl.ds(t, 1)], xs_ref.at[pl.ds(dest_ref[k, t], 1)], sem).start()
        return carry

    lax.fori_loop(0, tm, issue, 0, unroll=8)
    for k in range(TOP_K):
        pltpu.make_async_copy(h_ref, xs_ref.at[pl.ds(0, tm)], sem).wait()


def moe_dispatch(h, dest):
    t = h.shape[0]
    tm = MOE_TM
    return pl.pallas_call(
        _dispatch_kernel, grid=(t // tm,),
        in_specs=[pl.BlockSpec((None, TOP_K, tm), lambda i: (i, 0, 0), memory_space=pltpu.SMEM),
                  pl.BlockSpec((tm, D_MODEL), lambda i: (i, 0))],
        out_specs=pl.BlockSpec(memory_space=pl.ANY),
        out_shape=jax.ShapeDtypeStruct((t * TOP_K, D_MODEL), _F32),
        scratch_shapes=[pltpu.SemaphoreType.DMA(())],
        compiler_params=_cparams("arbitrary"), name="moe_dispatch",
    )(_tile_major(dest, tm), h)


def _expert_kernel(blk_ref, e_ref, lo_ref, hi_ref, xs_ref, wu_ref, bu_ref, wd_ref, bd_ref,
                   ys_ref, wub_ref, wdb_ref):
    i = pl.program_id(0)
    e = e_ref[i]
    e_prev = e_ref[jnp.maximum(i - 1, 0)]

    @pl.when((i == 0) | (e != e_prev))
    def _():
        wub_ref[...] = wu_ref[...].astype(_BF16)
        wdb_ref[...] = wd_ref[...].astype(_BF16)

    lo = lo_ref[i]
    hi = hi_ref[i]

    @pl.when(hi > lo)
    def _():
        x = xs_ref[...].astype(_BF16)
        h = jnp.dot(x, wub_ref[...], preferred_element_type=_F32) + bu_ref[...]
        gate = jnp.minimum(h[:, :D_FF], SWIGLU_LIMIT)
        up = jnp.clip(h[:, D_FF:], -SWIGLU_LIMIT, SWIGLU_LIMIT)
        glu = gate * jax.nn.sigmoid(SWIGLU_ALPHA * gate)
        act = ((up + 1.0) * glu).astype(_BF16)
        y = jnp.dot(act, wdb_ref[...], preferred_element_type=_F32) + bd_ref[...]
        row = lax.broadcasted_iota(jnp.int32, y.shape, 0)
        y = jnp.where((row >= lo) & (row < hi), y, 0.0)

        @pl.when(lo == 0)
        def _():
            ys_ref[...] = y

        @pl.when(lo > 0)
        def _():
            ys_ref[...] += y


def _expert_schedule(offs, n_rows, bm):
    nblk = n_rows // bm
    n_items = nblk + N_EXPERTS - 1
    bnd = offs[1:]
    pos = jnp.arange(N_EXPERTS - 1, dtype=jnp.int32) + jnp.minimum(bnd // bm + 1, nblk)
    i = jnp.arange(n_items, dtype=jnp.int32)[:, None]
    at = pos[None, :] == i
    is_bnd = jnp.any(at, axis=1)
    bnd_val = jnp.sum(jnp.where(at, bnd[None, :], 0), axis=1)
    n_before = jnp.sum((pos[None, :] < i).astype(jnp.int32), axis=1)
    cuts = jnp.where(is_bnd, bnd_val, (i[:, 0] - n_before) * bm)
    ends = jnp.concatenate([cuts[1:], jnp.array([n_rows], jnp.int32)])
    blk = jnp.minimum(cuts // bm, nblk - 1)
    e = jnp.minimum(jnp.sum((bnd[None, :] <= cuts[:, None]).astype(jnp.int32), axis=1), N_EXPERTS - 1)
    return blk, e, cuts - blk * bm, ends - blk * bm


def moe_experts(xs, offs, layer, w_up, b_up, w_down, b_down):
    n_rows = xs.shape[0]
    bm = MOE_BM
    blk, e, lo, hi = _expert_schedule(offs, n_rows, bm)
    n_items = n_rows // bm + N_EXPERTS - 1
    wmap = lambda i, b, e, lo, hi: (layer, e[i], 0, 0)
    grid_spec = pltpu.PrefetchScalarGridSpec(
        num_scalar_prefetch=4, grid=(n_items,),
        in_specs=[pl.BlockSpec((bm, D_MODEL), lambda i, b, e, lo, hi: (b[i], 0)),
                  pl.BlockSpec((None, None, D_MODEL, 2 * D_FF), wmap),
                  pl.BlockSpec((None, None, 1, 2 * D_FF), wmap),
                  pl.BlockSpec((None, None, D_FF, D_MODEL), wmap),
                  pl.BlockSpec((None, None, 1, D_MODEL), wmap)],
        out_specs=pl.BlockSpec((bm, D_MODEL), lambda i, b, e, lo, hi: (b[i], 0)),
        scratch_shapes=[pltpu.VMEM((D_MODEL, 2 * D_FF), _BF16), pltpu.VMEM((D_FF, D_MODEL), _BF16)])
    nl = w_up.shape[0]
    return pl.pallas_call(
        _expert_kernel, grid_spec=grid_spec,
        out_shape=jax.ShapeDtypeStruct((n_rows, D_MODEL), _F32),
        compiler_params=_cparams("arbitrary"), name="moe_experts",
    )(blk, e, lo, hi, xs, w_up, b_up.reshape(nl, N_EXPERTS, 1, -1), w_down, b_down.reshape(nl, N_EXPERTS, 1, -1))


def _combine_kernel(dest_ref, ys_ref, h_ref, gate_ref, g_ref, b_ref, o_ref, buf_ref, sem):
    tm = h_ref.shape[0]

    def issue(t, carry):
        for k in range(TOP_K):
            pltpu.make_async_copy(ys_ref.at[pl.ds(dest_ref[k, t], 1)], buf_ref.at[k, pl.ds(t, 1)], sem).start()
        return carry

    lax.fori_loop(0, tm, issue, 0, unroll=8)
    for k in range(TOP_K):
        pltpu.make_async_copy(ys_ref.at[pl.ds(0, tm)], buf_ref.at[k], sem).wait()
    gates = gate_ref[...]
    y = DEEPNORM_ALPHA * h_ref[...]
    for k in range(TOP_K):
        y = y + gates[:, k:k + 1] * buf_ref[k]
    o_ref[...] = _layer_norm(y, g_ref[...], b_ref[...])


def moe_combine(ys, dest, h, gates, ln_g, ln_b):
    t = h.shape[0]
    tm = MOE_TM
    row = pl.BlockSpec((tm, D_MODEL), lambda i: (i, 0))
    return pl.pallas_call(
        _combine_kernel, grid=(t // tm,),
        in_specs=[pl.BlockSpec((None, TOP_K, tm), lambda i: (i, 0, 0), memory_space=pltpu.SMEM),
                  pl.BlockSpec(memory_space=pl.ANY), row,
                  pl.BlockSpec((tm, SUBLANES), lambda i: (i, 0)),
                  _const_spec((1, D_MODEL)), _const_spec((1, D_MODEL))],
        out_specs=row, out_shape=jax.ShapeDtypeStruct((t, D_MODEL), _F32),
        scratch_shapes=[pltpu.VMEM((TOP_K, tm, D_MODEL), _F32), pltpu.SemaphoreType.DMA(())],
        compiler_params=_cparams("arbitrary"), name="moe_combine",
    )(_tile_major(dest, tm), ys, h, gates, ln_g.reshape(1, -1), ln_b.reshape(1, -1))


def moe_layer(h, eidx, gates, rank, counts, layer, w_up, b_up, w_down, b_down, ln_g, ln_b):
    cnt = counts[:, 0].astype(jnp.int32)
    offs = jnp.cumsum(cnt) - cnt
    dest = moe_dest(eidx, rank, offs)
    xs = moe_dispatch(h, dest)
    ys = moe_experts(xs, offs, layer, w_up, b_up, w_down, b_down)
    return moe_combine(ys, dest, h, gates, ln_g, ln_b)


def kernel(x, positions, attn_w_in, attn_w_out, ssm_w_in, ssm_conv_w, ssm_conv_b, ssm_dt_bias,
           ssm_a_log, ssm_d, ssm_norm_w, ssm_w_out, moe_w_router, moe_b_router, moe_w_up,
           moe_b_up, moe_w_down, moe_b_down, ln_mix_g, ln_mix_b, ln_ffn_g, ln_ffn_b):
    bsz, seq, _ = x.shape
    xf = x.reshape(bsz * seq, D_MODEL)

    tabs = rope_tables(positions)
    w_qkv = attn_w_in[0].astype(_BF16)
    outs = [dilated_group_attention(qkv_proj_group(xf, w_qkv, tabs, g), g, bsz, seq) for g in range(N_GROUPS)]
    h, eidx, gates, rank, counts = attn_out_norm_route(
        [o for o, _ in outs], [l for _, l in outs], xf, attn_w_out[0].astype(_BF16),
        ln_mix_g[0], ln_mix_b[0], moe_w_router[0], moe_b_router[0])
    xf = moe_layer(h, eidx, gates, rank, counts, 0, moe_w_up, moe_b_up, moe_w_down, moe_b_down,
                   ln_ffn_g[0], ln_ffn_b[0])

    w_in = ssm_w_in[0]
    zx = dense_proj(xf, w_in[:, :ZX_COLS].astype(_BF16), tn=D_MODEL, out_dtype=_F32)
    w_dt = jnp.pad(w_in[:, ZX_COLS:], ((0, 0), (0, LANES - SSM_HEADS))).astype(_BF16)
    dt_raw = dense_proj(xf, w_dt, tn=LANES, out_dtype=_F32)
    yn = ssd_mixer(zx, dt_raw, ssm_conv_w[0], ssm_conv_b[0], ssm_dt_bias[0], ssm_a_log[0],
                   ssm_d[0], ssm_norm_w[0], bsz, seq)
    h, eidx, gates, rank, counts = ssm_out_norm_route(
        yn, xf, ssm_w_out[0].astype(_BF16), ln_mix_g[1], ln_mix_b[1], moe_w_router[1], moe_b_router[1])
    xf = moe_layer(h, eidx, gates, rank, counts, 1, moe_w_up, moe_b_up, moe_w_down, moe_b_down,
                   ln_ffn_g[1], ln_ffn_b[1])
    return xf.reshape(bsz, seq, D_MODEL)
```

```python
import functools
import math

import jax
import jax.numpy as jnp
from jax import lax
from jax.experimental import pallas as pl
from jax.experimental.pallas import tpu as pltpu

D_MODEL = 1024
DEPTH = 2
DIL = (1, 4, 16)
N_GROUPS = 3
SUB_WIN = 128
HEAD_DIM = 128
HEADS = 8
ROT_DIM = 32
ROT_HALF = 16
ROPE_THETA = 500000.0
ATTN_PROJ = N_GROUPS * 3 * D_MODEL
SSM_INNER = 2048
SSM_P = 64
SSM_HEADS = 32
SSM_GROUPS = 4
SSM_GHEADS = 8
SSM_N = 128
SSM_CONV = 4
CHUNK = 128
SSM_BC = SSM_GROUPS * SSM_N
SSM_XBC = SSM_INNER + 2 * SSM_BC
N_EXPERTS = 32
TOP_K = 4
D_FF = 1024
SWIGLU_LIMIT = 7.0
SWIGLU_ALPHA = 1.702
DEEPNORM_ALPHA = (2 * DEPTH) ** 0.25
LN_EPS = 1e-5
RMS_EPS = 1e-5

LANES = 128
SUBLANES = 8
VMEM_LIMIT_BYTES = 56 * 1024 * 1024

NEG_BIG = -1e30

_F32 = jnp.float32
_BF16 = jnp.bfloat16


def _cparams(*sem):
    return pltpu.CompilerParams(dimension_semantics=sem, vmem_limit_bytes=VMEM_LIMIT_BYTES)


def _rope_table_kernel(pos_ref, invf_ref, c_ref, sa_ref, sb_ref):
    pos = pos_ref[...].astype(_F32)
    ang = pos * invf_ref[...]
    lane = lax.broadcasted_iota(jnp.int32, ang.shape, 1)
    cos = jnp.cos(ang)
    sin = jnp.sin(ang)
    c_ref[...] = jnp.where(lane < ROT_DIM, cos, 1.0)
    sa_ref[...] = jnp.where((lane >= ROT_HALF) & (lane < ROT_DIM), sin, 0.0)
    sb_ref[...] = jnp.where(lane < ROT_HALF, -sin, 0.0)


def rope_tables(positions):
    t = positions.size
    tm = 1024
    pos = positions.reshape(t, 1)
    j = jnp.arange(LANES)
    invf = jnp.where(j < ROT_DIM,
                     ROPE_THETA ** (-(2.0 * (j % ROT_HALF)).astype(_F32) / ROT_DIM), 0.0)
    invf = invf.astype(_F32).reshape(1, LANES)
    out = jax.ShapeDtypeStruct((t, LANES), _F32)
    return pl.pallas_call(
        _rope_table_kernel,
        grid=(t // tm,),
        in_specs=[pl.BlockSpec((tm, 1), lambda i: (i, 0)),
                  pl.BlockSpec((1, LANES), lambda i: (0, 0))],
        out_specs=[pl.BlockSpec((tm, LANES), lambda i: (i, 0))] * 3,
        out_shape=[out, out, out],
        compiler_params=_cparams("parallel"),
        name="rope_tables",
    )(pos, invf)


def _proj_kernel(x_ref, w_ref, o_ref, xb_ref):
    @pl.when(pl.program_id(1) == 0)
    def _():
        xb_ref[...] = x_ref[...].astype(_BF16)

    o_ref[...] = jnp.dot(xb_ref[...], w_ref[...], preferred_element_type=_F32).astype(o_ref.dtype)


def dense_proj(x, w_bf16, *, tn, out_dtype, tm=512):
    t, k = x.shape
    n = w_bf16.shape[1]
    return pl.pallas_call(
        _proj_kernel, grid=(t // tm, n // tn),
        in_specs=[pl.BlockSpec((tm, k), lambda i, j: (i, 0)), pl.BlockSpec((k, tn), lambda i, j: (0, j))],
        out_specs=pl.BlockSpec((tm, tn), lambda i, j: (i, j)),
        out_shape=jax.ShapeDtypeStruct((t, n), out_dtype), scratch_shapes=[pltpu.VMEM((tm, k), _BF16)],
        compiler_params=_cparams("parallel", "arbitrary"), name="dense_proj",
    )(x, w_bf16)


QKV_TM = 512
ATTN_ROWS = 2048
HEADS_PER_STEP = 2


def _qkv_proj_kernel(x_ref, w_ref, c_ref, sa_ref, sb_ref, o_ref, *, d, ns, tt):
    tm = x_ref.shape[0]
    xb = x_ref[...].astype(_BF16)
    if d > 1:
        dst = lax.broadcasted_iota(jnp.int32, (tm, tm), 0)
        src = lax.broadcasted_iota(jnp.int32, (tm, tm), 1)
        lt, ld = tt.bit_length() - 1, d.bit_length() - 1
        t_ = dst & (tt - 1)
        r_ = lax.shift_right_logical(dst, lt) & (d - 1)
        s_ = lax.shift_right_logical(dst, lt + ld)
        perm = (src == ((s_ * tt + t_) * d + r_)).astype(_BF16)
        xb = jnp.dot(perm, xb, preferred_element_type=_F32).astype(_BF16)
    c = c_ref[...]
    sa = sa_ref[...]
    sb = sb_ref[...]
    for comp in range(3):
        acc = jnp.dot(xb, w_ref[:, comp * D_MODEL:(comp + 1) * D_MODEL], preferred_element_type=_F32)
        for h in range(HEADS):
            sl = slice(h * HEAD_DIM, (h + 1) * HEAD_DIM)
            t = acc[:, sl]
            if comp < 2:
                t = t * c + pltpu.roll(t, ROT_HALF, 1) * sa + pltpu.roll(t, HEAD_DIM - ROT_HALF, 1) * sb
            o_ref[:, :, :, comp * D_MODEL + h * HEAD_DIM:comp * D_MODEL + (h + 1) * HEAD_DIM] = (
                t.reshape(ns, d, tt, HEAD_DIM).astype(o_ref.dtype))


def qkv_proj_group(x, w_bf16, positions, g):
    t, k = x.shape
    d = DIL[g]
    span = SUB_WIN * d
    tm = QKV_TM
    ns = max(1, tm // span)
    tps = max(1, span // tm)
    tt = tm // (ns * d)
    rope = rope_tables(positions.reshape(-1, ns, tt, d).swapaxes(2, 3).reshape(-1))
    tab_spec = pl.BlockSpec((tm, LANES), lambda i: (i, 0))
    return pl.pallas_call(
        functools.partial(_qkv_proj_kernel, d=d, ns=ns, tt=tt),
        grid=(t // tm,),
        in_specs=[pl.BlockSpec((tm, k), lambda i: (i, 0)),
                  pl.BlockSpec((k, 3 * D_MODEL), lambda i: (0, g)),
                  tab_spec, tab_spec, tab_spec],
        out_specs=pl.BlockSpec((ns, d, tt, 3 * D_MODEL), lambda i: (i // tps, 0, i % tps, 0)),
        out_shape=jax.ShapeDtypeStruct((t // span, d, SUB_WIN, 3 * D_MODEL), _BF16),
        compiler_params=_cparams("parallel"), name=f"qkv_proj_g{g}",
    )(x, w_bf16, *rope)


def _dil_attn_kernel(q_ref, k_ref, v_ref, kp_ref, vp_ref, o_ref, lse_ref, *, d, ns):
    j = pl.program_id(1)
    hp = pl.program_id(2)
    nb = ns * d
    row = lax.broadcasted_iota(jnp.int32, (1, SUB_WIN, SUB_WIN), 1)
    col = lax.broadcasted_iota(jnp.int32, (1, SUB_WIN, SUB_WIN), 2)
    cur_ok = col <= row
    blk = lax.broadcasted_iota(jnp.int32, (nb, 1, 1), 0)
    prev_ok = (col >= row) & ((blk >= d) | (j > 0))
    lane = lax.broadcasted_iota(jnp.int32, (SUB_WIN, LANES), 1)
    scale = 1.0 / math.sqrt(HEAD_DIM)

    def block_rows(b):
        s, r = divmod(b, d)
        start = s * SUB_WIN * d + r
        return pl.ds(start, SUB_WIN, stride=d) if d > 1 else pl.ds(start, SUB_WIN)

    @pl.when(hp == 0)
    def _():
        lse_ref[...] = jnp.zeros_like(lse_ref)

    lses = []
    for hh in range(HEADS_PER_STEP):
        sl = slice(hh * HEAD_DIM, (hh + 1) * HEAD_DIM)
        flat = lambda a: a.reshape(nb, SUB_WIN, HEAD_DIM)
        q, kc, vc = flat(q_ref[:, :, :, sl]), flat(k_ref[:, :, :, sl]), flat(v_ref[:, :, :, sl])
        if ns > 1:
            kp = flat(jnp.concatenate([kp_ref[:, :, :, sl], k_ref[:ns - 1, :, :, sl]], axis=0))
            vp = flat(jnp.concatenate([vp_ref[:, :, :, sl], v_ref[:ns - 1, :, :, sl]], axis=0))
        else:
            kp, vp = flat(kp_ref[:, :, :, sl]), flat(vp_ref[:, :, :, sl])
        sc = jnp.einsum('bqd,bkd->bqk', q, kc, preferred_element_type=_F32) * scale
        sp = jnp.einsum('bqd,bkd->bqk', q, kp, preferred_element_type=_F32) * scale
        sc = jnp.where(cur_ok, sc, NEG_BIG)
        sp = jnp.where(prev_ok, sp, NEG_BIG)
        m = jnp.maximum(jnp.max(sc, axis=2, keepdims=True), jnp.max(sp, axis=2, keepdims=True))
        pc = jnp.exp(sc - m)
        pp = jnp.exp(sp - m)
        l = jnp.sum(pc, axis=2, keepdims=True) + jnp.sum(pp, axis=2, keepdims=True)
        acc = jnp.einsum('bqk,bkd->bqd', pc.astype(_BF16), vc, preferred_element_type=_F32)
        acc += jnp.einsum('bqk,bkd->bqd', pp.astype(_BF16), vp, preferred_element_type=_F32)
        o = acc / l
        for b in range(nb):
            o_ref[hh, block_rows(b), :] = o[b]
        lses.append(m + jnp.log(l))
    for b in range(nb):
        cur = lse_ref[block_rows(b), :]
        for hh in range(HEADS_PER_STEP):
            cur = jnp.where(lane == hp * HEADS_PER_STEP + hh, lses[hh][b], cur)
        lse_ref[block_rows(b), :] = cur


def dilated_group_attention(qkv, g, bsz, seq):
    d = DIL[g]
    ns = ATTN_ROWS // (SUB_WIN * d)
    nsteps = seq // ATTN_ROWS
    hw = HEADS_PER_STEP * HEAD_DIM
    ncol = D_MODEL // hw
    cur = lambda comp: pl.BlockSpec((ns, d, SUB_WIN, hw), lambda b, j, hp: (b * nsteps + j, 0, 0, comp * ncol + hp))
    prev = lambda comp: pl.BlockSpec(
        (1, d, SUB_WIN, hw), lambda b, j, hp: (jnp.maximum((b * nsteps + j) * ns - 1, 0), 0, 0, comp * ncol + hp))
    return pl.pallas_call(
        functools.partial(_dil_attn_kernel, d=d, ns=ns),
        grid=(bsz, nsteps, HEADS // HEADS_PER_STEP),
        in_specs=[cur(0), cur(1), cur(2), prev(1), prev(2)],
        out_specs=[pl.BlockSpec((HEADS_PER_STEP, ATTN_ROWS, HEAD_DIM), lambda b, j, hp: (hp, b * nsteps + j, 0)),
                   pl.BlockSpec((ATTN_ROWS, LANES), lambda b, j, hp: (b * nsteps + j, 0))],
        out_shape=[jax.ShapeDtypeStruct((HEADS, bsz * seq, HEAD_DIM), _F32),
                   jax.ShapeDtypeStruct((bsz * seq, LANES), _F32)],
        compiler_params=_cparams("parallel", "parallel", "arbitrary"),
        name=f"dilated_attn_g{g}",
    )(qkv, qkv, qkv, qkv, qkv)


HALO = SUBLANES
ZX_COLS = SSM_INNER + SSM_XBC


def _softplus(x):
    return jnp.maximum(x, 0.0) + jnp.log1p(jnp.exp(-jnp.abs(x)))


def _ssd_kernel(zx_ref, halo_ref, dt_ref, cw_ref, cb_ref, dtb_ref, a_ref, dsk_ref, nw_ref,
                o_ref, state_ref, ext_ref, y_ref):
    c = pl.program_id(1)
    q = CHUNK

    @pl.when(c == 0)
    def _():
        state_ref[...] = jnp.zeros_like(state_ref)
        ext_ref[0:HALO, :] = jnp.zeros((HALO, SSM_XBC), _F32)

    @pl.when(c > 0)
    def _():
        ext_ref[0:HALO, :] = halo_ref[:, SSM_INNER:]

    ext_ref[HALO:, :] = zx_ref[:, SSM_INNER:]
    conv = cb_ref[...]
    for j in range(SSM_CONV):
        off = HALO - (SSM_CONV - 1) + j
        conv = conv + cw_ref[j:j + 1, :] * ext_ref[off:off + q, :]
    xc = conv * jax.nn.sigmoid(conv)
    bm = xc[:, SSM_INNER:SSM_INNER + SSM_BC].astype(_BF16)
    cm = xc[:, SSM_INNER + SSM_BC:].astype(_BF16)

    dt = _softplus(dt_ref[...] + dtb_ref[...])
    da = dt * a_ref[...]
    r_io = lax.broadcasted_iota(jnp.int32, (q, q), 0)
    c_io = lax.broadcasted_iota(jnp.int32, (q, q), 1)
    causal = r_io >= c_io
    tri = causal.astype(_F32)
    a_cum = jnp.dot(tri, da, preferred_element_type=_F32, precision=lax.Precision.HIGHEST)
    a_cum_t = a_cum.T
    dt_t = dt.T
    a_last = a_cum[q - 1:q, :]
    w_state = jnp.exp(a_last - a_cum) * dt
    e_acum = jnp.exp(a_cum)
    chunk_decay = jnp.exp(a_last)
    lane = lax.broadcasted_iota(jnp.int32, (q, LANES), 1)
    low = lane < SSM_P
    lane1 = lax.broadcasted_iota(jnp.int32, (1, LANES), 1)
    low1 = lane1 < SSM_P
    tn = (((0,), (0,)), ((), ()))
    nt = (((1,), (1,)), ((), ()))

    for g in range(SSM_GROUPS):
        bg = bm[:, g * SSM_N:(g + 1) * SSM_N]
        cg = cm[:, g * SSM_N:(g + 1) * SSM_N]
        cbm = lax.dot_general(cg, bg, nt, preferred_element_type=_F32)
        st = state_ref[g]
        y_off = jnp.dot(cg, st.astype(_BF16), preferred_element_type=_F32)
        xw_parts, dec_parts = [], []
        for j in range(SSM_GHEADS // 2):
            h0 = g * SSM_GHEADS + 2 * j
            col = (g * SSM_GHEADS // 2 + j) * LANES
            xp = xc[:, col:col + LANES]
            y_diag = jnp.zeros((q, LANES), _F32)
            for half, hh in enumerate((h0, h0 + 1)):
                diff = a_cum[:, hh:hh + 1] - a_cum_t[hh:hh + 1, :]
                lmat = jnp.exp(jnp.where(causal, diff, NEG_BIG))
                mm = (cbm * lmat * dt_t[hh:hh + 1, :]).astype(_BF16)
                keep = low if half == 0 else jnp.logical_not(low)
                xh = jnp.where(keep, xp, 0.0).astype(_BF16)
                y_diag = y_diag + jnp.dot(mm, xh, preferred_element_type=_F32)
            e_pair = jnp.where(low, e_acum[:, h0:h0 + 1], e_acum[:, h0 + 1:h0 + 2])
            w_pair = jnp.where(low, w_state[:, h0:h0 + 1], w_state[:, h0 + 1:h0 + 2])
            y_ref[:, col:col + LANES] = (y_diag + y_off[:, j * LANES:(j + 1) * LANES] * e_pair
                                         + xp * dsk_ref[:, col:col + LANES])
            xw_parts.append((xp * w_pair).astype(_BF16))
            dec_parts.append(jnp.where(low1, chunk_decay[:, h0:h0 + 1], chunk_decay[:, h0 + 1:h0 + 2]))
        xw = jnp.concatenate(xw_parts, axis=1)
        dec = jnp.concatenate(dec_parts, axis=1)
        state_ref[g] = st * dec + lax.dot_general(bg, xw, tn, preferred_element_type=_F32)

    z = zx_ref[:, :SSM_INNER]
    yg = y_ref[...] * (z * jax.nn.sigmoid(z))
    gw = SSM_INNER // SSM_GROUPS
    for g in range(SSM_GROUPS):
        v = yg[:, g * gw:(g + 1) * gw]
        ms = jnp.mean(v * v, axis=-1, keepdims=True)
        o_ref[:, g * gw:(g + 1) * gw] = (v * lax.rsqrt(ms + RMS_EPS)
                                         * nw_ref[:, g * gw:(g + 1) * gw]).astype(o_ref.dtype)


def ssd_mixer(zx, dt_raw, conv_w, conv_b, dt_bias, a_log, d_skip, norm_w, bsz, seq):
    nc = seq // CHUNK
    per8 = CHUNK // HALO
    pad = LANES - SSM_HEADS
    dtb = jnp.pad(dt_bias, (0, pad)).reshape(1, LANES)
    a = jnp.pad(-jnp.exp(a_log.astype(_F32)), (0, pad)).reshape(1, LANES)
    dsk = jnp.repeat(d_skip.astype(_F32), SSM_P).reshape(1, SSM_INNER)
    cst = lambda shape: pl.BlockSpec(shape, lambda b, c: (0,) * len(shape))
    return pl.pallas_call(
        _ssd_kernel,
        grid=(bsz, nc),
        in_specs=[pl.BlockSpec((CHUNK, ZX_COLS), lambda b, c: (b * nc + c, 0)),
                  pl.BlockSpec((HALO, ZX_COLS), lambda b, c: (jnp.maximum((b * nc + c) * per8 - 1, 0), 0)),
                  pl.BlockSpec((CHUNK, LANES), lambda b, c: (b * nc + c, 0)),
                  cst((SSM_CONV, SSM_XBC)), cst((1, SSM_XBC)), cst((1, LANES)), cst((1, LANES)),
                  cst((1, SSM_INNER)), cst((1, SSM_INNER))],
        out_specs=pl.BlockSpec((CHUNK, SSM_INNER), lambda b, c: (b * nc + c, 0)),
        out_shape=jax.ShapeDtypeStruct((bsz * seq, SSM_INNER), _BF16),
        scratch_shapes=[pltpu.VMEM((SSM_GROUPS, SSM_N, SSM_GHEADS * SSM_P), _F32),
                        pltpu.VMEM((HALO + CHUNK, SSM_XBC), _F32),
                        pltpu.VMEM((CHUNK, SSM_INNER), _F32)],
        compiler_params=_cparams("parallel", "arbitrary"), name="ssd_mixer",
    )(zx, zx, dt_raw, conv_w, conv_b.reshape(1, -1), dtb, a, dsk, norm_w.reshape(1, -1))


ROUTE_TM = 256


def _layer_norm(y, g, b):
    mu = jnp.mean(y, axis=-1, keepdims=True)
    yc = y - mu
    var = jnp.mean(yc * yc, axis=-1, keepdims=True)
    return yc * lax.rsqrt(var + LN_EPS) * g + b


def _route(h1, wrt_ref, br_ref, eidx_ref, gate_ref, rank_ref, cnt_ref, run_ref):
    m_rows = h1.shape[0]
    nt = (((1,), (1,)), ((), ()))
    logits = lax.dot_general(wrt_ref[...], h1, nt, preferred_element_type=_F32,
                             precision=lax.Precision.HIGHEST) + br_ref[...]
    eio = lax.broadcasted_iota(jnp.int32, logits.shape, 0)
    vals = logits
    tops, ids, hots = [], [], []
    for _ in range(TOP_K):
        mk = jnp.max(vals, axis=0, keepdims=True)
        ik = jnp.min(jnp.where(vals == mk, eio, N_EXPERTS), axis=0, keepdims=True)
        hot = eio == ik
        vals = jnp.where(hot, -jnp.inf, vals)
        tops.append(mk)
        ids.append(ik)
        hots.append(hot)
    exps = [jnp.exp(t - tops[0]) for t in tops]
    den = exps[0] + exps[1] + exps[2] + exps[3]
    hot_all = (hots[0] | hots[1] | hots[2] | hots[3])
    r_io = lax.broadcasted_iota(jnp.int32, (m_rows, m_rows), 0)
    c_io = lax.broadcasted_iota(jnp.int32, (m_rows, m_rows), 1)
    upper = (r_io < c_io).astype(_BF16)
    prefix = jnp.dot(hot_all.astype(_BF16), upper, preferred_element_type=_F32)
    base = prefix + run_ref[...]
    gates = [e / den for e in exps]
    for k in range(TOP_K):
        eidx_ref[k:k + 1, :] = ids[k]
        rank_ref[k:k + 1, :] = jnp.sum(jnp.where(hots[k], base, 0.0), axis=0,
                                       keepdims=True).astype(jnp.int32)
    g8 = jnp.concatenate(gates + [jnp.zeros((SUBLANES - TOP_K, m_rows), _F32)], axis=0)
    gate_ref[...] = g8.T
    run_new = run_ref[...] + jnp.sum(hot_all.astype(_F32), axis=1, keepdims=True)
    run_ref[...] = run_new
    cnt_ref[...] = jnp.broadcast_to(run_new, cnt_ref.shape)


def _attn_out_kernel(o0_ref, o1_ref, o2_ref, l0_ref, l1_ref, l2_ref, x_ref, wo_ref, g_ref, b_ref,
                     wrt_ref, br_ref, h_ref, eidx_ref, gate_ref, rank_ref, cnt_ref, run_ref, ob_ref):
    @pl.when(pl.program_id(0) == 0)
    def _():
        run_ref[...] = jnp.zeros_like(run_ref)

    l0, l1, l2 = l0_ref[...], l1_ref[...], l2_ref[...]
    m = jnp.maximum(jnp.maximum(l0, l1), l2)
    e0, e1, e2 = jnp.exp(l0 - m), jnp.exp(l1 - m), jnp.exp(l2 - m)
    inv = 1.0 / (e0 + e1 + e2)
    w0, w1, w2 = e0 * inv, e1 * inv, e2 * inv
    for h in range(HEADS):
        o = w0[:, h:h + 1] * o0_ref[h] + w1[:, h:h + 1] * o1_ref[h] + w2[:, h:h + 1] * o2_ref[h]
        ob_ref[:, h * HEAD_DIM:(h + 1) * HEAD_DIM] = o.astype(_BF16)
    mix = jnp.dot(ob_ref[...], wo_ref[...], preferred_element_type=_F32)
    h1 = _layer_norm(DEEPNORM_ALPHA * x_ref[...] + mix, g_ref[...], b_ref[...])
    h_ref[...] = h1
    _route(h1, wrt_ref, br_ref, eidx_ref, gate_ref, rank_ref, cnt_ref, run_ref)


def _ssm_out_kernel(y_ref, x_ref, wo_ref, g_ref, b_ref, wrt_ref, br_ref,
                    h_ref, eidx_ref, gate_ref, rank_ref, cnt_ref, run_ref):
    @pl.when(pl.program_id(0) == 0)
    def _():
        run_ref[...] = jnp.zeros_like(run_ref)

    mix = jnp.dot(y_ref[...], wo_ref[...], preferred_element_type=_F32)
    h1 = _layer_norm(DEEPNORM_ALPHA * x_ref[...] + mix, g_ref[...], b_ref[...])
    h_ref[...] = h1
    _route(h1, wrt_ref, br_ref, eidx_ref, gate_ref, rank_ref, cnt_ref, run_ref)


def _route_out_specs(t):
    tm = ROUTE_TM
    specs = [pl.BlockSpec((tm, D_MODEL), lambda i: (i, 0)),
             pl.BlockSpec((TOP_K, tm), lambda i: (0, i)),
             pl.BlockSpec((tm, SUBLANES), lambda i: (i, 0)),
             pl.BlockSpec((TOP_K, tm), lambda i: (0, i)),
             pl.BlockSpec((N_EXPERTS, LANES), lambda i: (0, 0))]
    shapes = [jax.ShapeDtypeStruct((t, D_MODEL), _F32),
              jax.ShapeDtypeStruct((TOP_K, t), jnp.int32),
              jax.ShapeDtypeStruct((t, SUBLANES), _F32),
              jax.ShapeDtypeStruct((TOP_K, t), jnp.int32),
              jax.ShapeDtypeStruct((N_EXPERTS, LANES), _F32)]
    return specs, shapes


def _const_spec(shape):
    return pl.BlockSpec(shape, lambda i: (0,) * len(shape))


def attn_out_norm_route(os_, lses, x, wo_bf16, ln_g, ln_b, w_router, b_router):
    t = x.shape[0]
    tm = ROUTE_TM
    row = lambda w: pl.BlockSpec((tm, w), lambda i: (i, 0))
    out_specs, out_shapes = _route_out_specs(t)
    return pl.pallas_call(
        _attn_out_kernel,
        grid=(t // tm,),
        in_specs=[pl.BlockSpec((HEADS, tm, HEAD_DIM), lambda i: (0, i, 0))] * 3 + [row(LANES)] * 3 + [row(D_MODEL),
                  _const_spec((D_MODEL, D_MODEL)), _const_spec((1, D_MODEL)), _const_spec((1, D_MODEL)),
                  _const_spec((N_EXPERTS, D_MODEL)), _const_spec((N_EXPERTS, 1))],
        out_specs=out_specs, out_shape=out_shapes,
        scratch_shapes=[pltpu.VMEM((N_EXPERTS, 1), _F32), pltpu.VMEM((tm, D_MODEL), _BF16)],
        compiler_params=_cparams("arbitrary"), name="attn_out_norm_route",
    )(*os_, *lses, x, wo_bf16, ln_g.reshape(1, -1), ln_b.reshape(1, -1),
      w_router.T, b_router.reshape(-1, 1))


def ssm_out_norm_route(yn, x, wo_bf16, ln_g, ln_b, w_router, b_router):
    t = x.shape[0]
    tm = ROUTE_TM
    row = lambda w: pl.BlockSpec((tm, w), lambda i: (i, 0))
    out_specs, out_shapes = _route_out_specs(t)
    return pl.pallas_call(
        _ssm_out_kernel,
        grid=(t // tm,),
        in_specs=[row(SSM_INNER), row(D_MODEL),
                  _const_spec((SSM_INNER, D_MODEL)), _const_spec((1, D_MODEL)), _const_spec((1, D_MODEL)),
                  _const_spec((N_EXPERTS, D_MODEL)), _const_spec((N_EXPERTS, 1))],
        out_specs=out_specs, out_shape=out_shapes,
        scratch_shapes=[pltpu.VMEM((N_EXPERTS, 1), _F32)],
        compiler_params=_cparams("arbitrary"), name="ssm_out_norm_route",
    )(yn, x, wo_bf16, ln_g.reshape(1, -1), ln_b.reshape(1, -1), w_router.T, b_router.reshape(-1, 1))


MOE_TM = 256
MOE_BM = 256


def _dest_kernel(eidx_ref, rank_ref, offs_ref, dest_ref):
    eidx = eidx_ref[...]
    acc = rank_ref[...]
    for e in range(N_EXPERTS):
        acc = acc + jnp.where(eidx == e, offs_ref[e], 0)
    dest_ref[...] = acc


def moe_dest(eidx, rank, offs):
    t = eidx.shape[1]
    tn = min(4096, t)
    blk = pl.BlockSpec((TOP_K, tn), lambda i: (0, i))
    return pl.pallas_call(
        _dest_kernel, grid=(t // tn,),
        in_specs=[blk, blk, pl.BlockSpec(memory_space=pltpu.SMEM)],
        out_specs=blk, out_shape=jax.ShapeDtypeStruct((TOP_K, t), jnp.int32),
        compiler_params=_cparams("parallel"), name="moe_dest",
    )(eidx, rank, offs)


def _tile_major(dest, tm):
    t = dest.shape[1]
    return dest.reshape(TOP_K, t // tm, tm).transpose(1, 0, 2)


def _dispatch_kernel(dest_ref, h_ref, xs_ref, sem):
    tm = h_ref.shape[0]

    def issue(t, carry):
        for k in range(TOP_K):
            pltpu.make_async_copy(h_ref.at[pl.ds(t, 1)], xs_ref.at[pl.ds(dest_ref[k, t], 1)], sem).start()
        return carry

    lax.fori_loop(0, tm, issue, 0, unroll=8)
    for k in range(TOP_K):
        pltpu.make_async_copy(h_ref, xs_ref.at[pl.ds(0, tm)], sem).wait()


def moe_dispatch(h, dest):
    t = h.shape[0]
    tm = MOE_TM
    return pl.pallas_call(
        _dispatch_kernel, grid=(t // tm,),
        in_specs=[pl.BlockSpec((None, TOP_K, tm), lambda i: (i, 0, 0), memory_space=pltpu.SMEM),
                  pl.BlockSpec((tm, D_MODEL), lambda i: (i, 0))],
        out_specs=pl.BlockSpec(memory_space=pl.ANY),
        out_shape=jax.ShapeDtypeStruct((t * TOP_K, D_MODEL), _F32),
        scratch_shapes=[pltpu.SemaphoreType.DMA(())],
        compiler_params=_cparams("arbitrary"), name="moe_dispatch",
    )(_tile_major(dest, tm), h)


def _expert_kernel(blk_ref, e_ref, lo_ref, hi_ref, xs_ref, wu_ref, bu_ref, wd_ref, bd_ref,
                   ys_ref, wub_ref, wdb_ref):
    i = pl.program_id(0)
    e = e_ref[i]
    e_prev = e_ref[jnp.maximum(i - 1, 0)]

    @pl.when((i == 0) | (e != e_prev))
    def _():
        wub_ref[...] = wu_ref[...].astype(_BF16)
        wdb_ref[...] = wd_ref[...].astype(_BF16)

    lo = lo_ref[i]
    hi = hi_ref[i]

    @pl.when(hi > lo)
    def _():
        x = xs_ref[...].astype(_BF16)
        h = jnp.dot(x, wub_ref[...], preferred_element_type=_F32) + bu_ref[...]
        gate = jnp.minimum(h[:, :D_FF], SWIGLU_LIMIT)
        up = jnp.clip(h[:, D_FF:], -SWIGLU_LIMIT, SWIGLU_LIMIT)
        glu = gate * jax.nn.sigmoid(SWIGLU_ALPHA * gate)
        act = ((up + 1.0) * glu).astype(_BF16)
        y = jnp.dot(act, wdb_ref[...], preferred_element_type=_F32) + bd_ref[...]
        row = lax.broadcasted_iota(jnp.int32, y.shape, 0)
        y = jnp.where((row >= lo) & (row < hi), y, 0.0)

        @pl.when(lo == 0)
        def _():
            ys_ref[...] = y

        @pl.when(lo > 0)
        def _():
            ys_ref[...] += y


def _expert_schedule(offs, n_rows, bm):
    nblk = n_rows // bm
    n_items = nblk + N_EXPERTS - 1
    bnd = offs[1:]
    pos = jnp.arange(N_EXPERTS - 1, dtype=jnp.int32) + jnp.minimum(bnd // bm + 1, nblk)
    i = jnp.arange(n_items, dtype=jnp.int32)[:, None]
    at = pos[None, :] == i
    is_bnd = jnp.any(at, axis=1)
    bnd_val = jnp.sum(jnp.where(at, bnd[None, :], 0), axis=1)
    n_before = jnp.sum((pos[None, :] < i).astype(jnp.int32), axis=1)
    cuts = jnp.where(is_bnd, bnd_val, (i[:, 0] - n_before) * bm)
    ends = jnp.concatenate([cuts[1:], jnp.array([n_rows], jnp.int32)])
    blk = jnp.minimum(cuts // bm, nblk - 1)
    e = jnp.minimum(jnp.sum((bnd[None, :] <= cuts[:, None]).astype(jnp.int32), axis=1), N_EXPERTS - 1)
    return blk, e, cuts - blk * bm, ends - blk * bm


def moe_experts(xs, offs, layer, w_up, b_up, w_down, b_down):
    n_rows = xs.shape[0]
    bm = MOE_BM
    blk, e, lo, hi = _expert_schedule(offs, n_rows, bm)
    n_items = n_rows // bm + N_EXPERTS - 1
    wmap = lambda i, b, e, lo, hi: (layer, e[i], 0, 0)
    grid_spec = pltpu.PrefetchScalarGridSpec(
        num_scalar_prefetch=4, grid=(n_items,),
        in_specs=[pl.BlockSpec((bm, D_MODEL), lambda i, b, e, lo, hi: (b[i], 0)),
                  pl.BlockSpec((None, None, D_MODEL, 2 * D_FF), wmap),
                  pl.BlockSpec((None, None, 1, 2 * D_FF), wmap),
                  pl.BlockSpec((None, None, D_FF, D_MODEL), wmap),
                  pl.BlockSpec((None, None, 1, D_MODEL), wmap)],
        out_specs=pl.BlockSpec((bm, D_MODEL), lambda i, b, e, lo, hi: (b[i], 0)),
        scratch_shapes=[pltpu.VMEM((D_MODEL, 2 * D_FF), _BF16), pltpu.VMEM((D_FF, D_MODEL), _BF16)])
    nl = w_up.shape[0]
    return pl.pallas_call(
        _expert_kernel, grid_spec=grid_spec,
        out_shape=jax.ShapeDtypeStruct((n_rows, D_MODEL), _F32),
        compiler_params=_cparams("arbitrary"), name="moe_experts",
    )(blk, e, lo, hi, xs, w_up, b_up.reshape(nl, N_EXPERTS, 1, -1), w_down, b_down.reshape(nl, N_EXPERTS, 1, -1))


def _combine_kernel(dest_ref, ys_ref, h_ref, gate_ref, g_ref, b_ref, o_ref, buf_ref, sem):
    tm = h_ref.shape[0]

    def issue(t, carry):
        for k in range(TOP_K):
            pltpu.make_async_copy(ys_ref.at[pl.ds(dest_ref[k, t], 1)], buf_ref.at[k, pl.ds(t, 1)], sem).start()
        return carry

    lax.fori_loop(0, tm, issue, 0, unroll=8)
    for k in range(TOP_K):
        pltpu.make_async_copy(ys_ref.at[pl.ds(0, tm)], buf_ref.at[k], sem).wait()
    gates = gate_ref[...]
    y = DEEPNORM_ALPHA * h_ref[...]
    for k in range(TOP_K):
        y = y + gates[:, k:k + 1] * buf_ref[k]
    o_ref[...] = _layer_norm(y, g_ref[...], b_ref[...])


def moe_combine(ys, dest, h, gates, ln_g, ln_b):
    t = h.shape[0]
    tm = MOE_TM
    row = pl.BlockSpec((tm, D_MODEL), lambda i: (i, 0))
    return pl.pallas_call(
        _combine_kernel, grid=(t // tm,),
        in_specs=[pl.BlockSpec((None, TOP_K, tm), lambda i: (i, 0, 0), memory_space=pltpu.SMEM),
                  pl.BlockSpec(memory_space=pl.ANY), row,
                  pl.BlockSpec((tm, SUBLANES), lambda i: (i, 0)),
                  _const_spec((1, D_MODEL)), _const_spec((1, D_MODEL))],
        out_specs=row, out_shape=jax.ShapeDtypeStruct((t, D_MODEL), _F32),
        scratch_shapes=[pltpu.VMEM((TOP_K, tm, D_MODEL), _F32), pltpu.SemaphoreType.DMA(())],
        compiler_params=_cparams("arbitrary"), name="moe_combine",
    )(_tile_major(dest, tm), ys, h, gates, ln_g.reshape(1, -1), ln_b.reshape(1, -1))


def moe_layer(h, eidx, gates, rank, counts, layer, w_up, b_up, w_down, b_down, ln_g, ln_b):
    cnt = counts[:, 0].astype(jnp.int32)
    offs = jnp.cumsum(cnt) - cnt
    dest = moe_dest(eidx, rank, offs)
    xs = moe_dispatch(h, dest)
    ys = moe_experts(xs, offs, layer, w_up, b_up, w_down, b_down)
    return moe_combine(ys, dest, h, gates, ln_g, ln_b)


def kernel(x, positions, attn_w_in, attn_w_out, ssm_w_in, ssm_conv_w, ssm_conv_b, ssm_dt_bias,
           ssm_a_log, ssm_d, ssm_norm_w, ssm_w_out, moe_w_router, moe_b_router, moe_w_up,
           moe_b_up, moe_w_down, moe_b_down, ln_mix_g, ln_mix_b, ln_ffn_g, ln_ffn_b):
    bsz, seq, _ = x.shape
    xf = x.reshape(bsz * seq, D_MODEL)

    w_qkv = attn_w_in[0].astype(_BF16)
    outs = [dilated_group_attention(qkv_proj_group(xf, w_qkv, positions, g), g, bsz, seq)
            for g in range(N_GROUPS)]
    h, eidx, gates, rank, counts = attn_out_norm_route(
        [o for o, _ in outs], [l for _, l in outs], xf, attn_w_out[0].astype(_BF16),
        ln_mix_g[0], ln_mix_b[0], moe_w_router[0], moe_b_router[0])
    xf = moe_layer(h, eidx, gates, rank, counts, 0, moe_w_up, moe_b_up, moe_w_down, moe_b_down,
                   ln_ffn_g[0], ln_ffn_b[0])

    w_in = ssm_w_in[0]
    zx = dense_proj(xf, w_in[:, :ZX_COLS].astype(_BF16), tn=D_MODEL, out_dtype=_F32)
    w_dt = jnp.pad(w_in[:, ZX_COLS:], ((0, 0), (0, LANES - SSM_HEADS))).astype(_BF16)
    dt_raw = dense_proj(xf, w_dt, tn=LANES, out_dtype=_F32)
    yn = ssd_mixer(zx, dt_raw, ssm_conv_w[0], ssm_conv_b[0], ssm_dt_bias[0], ssm_a_log[0],
                   ssm_d[0], ssm_norm_w[0], bsz, seq)
    h, eidx, gates, rank, counts = ssm_out_norm_route(
        yn, xf, ssm_w_out[0].astype(_BF16), ln_mix_g[1], ln_mix_b[1], moe_w_router[1], moe_b_router[1])
    xf = moe_layer(h, eidx, gates, rank, counts, 1, moe_w_up, moe_b_up, moe_w_down, moe_b_down,
                   ln_ffn_g[1], ln_ffn_b[1])
    return xf.reshape(bsz, seq, D_MODEL)
```

```python
import functools
import math

import jax
import jax.numpy as jnp
from jax import lax
from jax.experimental import pallas as pl
from jax.experimental.pallas import tpu as pltpu

D_MODEL = 1024
DEPTH = 2
DIL = (1, 4, 16)
N_GROUPS = 3
SUB_WIN = 128
HEAD_DIM = 128
HEADS = 8
ROT_DIM = 32
ROT_HALF = 16
ROPE_THETA = 500000.0
ATTN_PROJ = N_GROUPS * 3 * D_MODEL
SSM_INNER = 2048
SSM_P = 64
SSM_HEADS = 32
SSM_GROUPS = 4
SSM_GHEADS = 8
SSM_N = 128
SSM_CONV = 4
CHUNK = 128
SSM_BC = SSM_GROUPS * SSM_N
SSM_XBC = SSM_INNER + 2 * SSM_BC
N_EXPERTS = 32
TOP_K = 4
D_FF = 1024
SWIGLU_LIMIT = 7.0
SWIGLU_ALPHA = 1.702
DEEPNORM_ALPHA = (2 * DEPTH) ** 0.25
LN_EPS = 1e-5
RMS_EPS = 1e-5

LANES = 128
SUBLANES = 8
VMEM_LIMIT_BYTES = 56 * 1024 * 1024

NEG_BIG = -1e30

_F32 = jnp.float32
_BF16 = jnp.bfloat16


def _cparams(*sem):
    return pltpu.CompilerParams(dimension_semantics=sem, vmem_limit_bytes=VMEM_LIMIT_BYTES)


def _rope_table_kernel(pos_ref, invf_ref, c_ref, sa_ref, sb_ref):
    pos = pos_ref[...].astype(_F32)
    ang = pos * invf_ref[...]
    lane = lax.broadcasted_iota(jnp.int32, ang.shape, 1)
    cos = jnp.cos(ang)
    sin = jnp.sin(ang)
    c_ref[...] = jnp.where(lane < ROT_DIM, cos, 1.0)
    sa_ref[...] = jnp.where((lane >= ROT_HALF) & (lane < ROT_DIM), sin, 0.0)
    sb_ref[...] = jnp.where(lane < ROT_HALF, -sin, 0.0)


def rope_tables(positions):
    t = positions.size
    tm = 1024
    pos = positions.reshape(t, 1)
    j = jnp.arange(LANES)
    invf = jnp.where(j < ROT_DIM,
                     ROPE_THETA ** (-(2.0 * (j % ROT_HALF)).astype(_F32) / ROT_DIM), 0.0)
    invf = invf.astype(_F32).reshape(1, LANES)
    out = jax.ShapeDtypeStruct((t, LANES), _F32)
    return pl.pallas_call(
        _rope_table_kernel,
        grid=(t // tm,),
        in_specs=[pl.BlockSpec((tm, 1), lambda i: (i, 0)),
                  pl.BlockSpec((1, LANES), lambda i: (0, 0))],
        out_specs=[pl.BlockSpec((tm, LANES), lambda i: (i, 0))] * 3,
        out_shape=[out, out, out],
        compiler_params=_cparams("parallel"),
        name="rope_tables",
    )(pos, invf)


QKV_TM = 512
ATTN_ROWS = 2048
HEADS_PER_STEP = 2


def _qkv_proj_kernel(x_ref, w_ref, c_ref, sa_ref, sb_ref, o_ref, *, d, ns, tt):
    tm = x_ref.shape[0]
    xb = x_ref[...].astype(_BF16)
    if d > 1:
        dst = lax.broadcasted_iota(jnp.int32, (tm, tm), 0)
        src = lax.broadcasted_iota(jnp.int32, (tm, tm), 1)
        lt, ld = tt.bit_length() - 1, d.bit_length() - 1
        t_ = dst & (tt - 1)
        r_ = lax.shift_right_logical(dst, lt) & (d - 1)
        s_ = lax.shift_right_logical(dst, lt + ld)
        perm = (src == ((s_ * tt + t_) * d + r_)).astype(_BF16)
        xb = jnp.dot(perm, xb, preferred_element_type=_F32).astype(_BF16)
    c = c_ref[...]
    sa = sa_ref[...]
    sb = sb_ref[...]
    for comp in range(3):
        acc = jnp.dot(xb, w_ref[:, comp * D_MODEL:(comp + 1) * D_MODEL], preferred_element_type=_F32)
        for h in range(HEADS):
            sl = slice(h * HEAD_DIM, (h + 1) * HEAD_DIM)
            t = acc[:, sl]
            if comp < 2:
                t = t * c + pltpu.roll(t, ROT_HALF, 1) * sa + pltpu.roll(t, HEAD_DIM - ROT_HALF, 1) * sb
            o_ref[:, :, :, comp * D_MODEL + h * HEAD_DIM:comp * D_MODEL + (h + 1) * HEAD_DIM] = (
                t.reshape(ns, d, tt, HEAD_DIM).astype(o_ref.dtype))


def qkv_proj_group(x, w_bf16, positions, g):
    t, k = x.shape
    d = DIL[g]
    span = SUB_WIN * d
    tm = QKV_TM
    ns = max(1, tm // span)
    tps = max(1, span // tm)
    tt = tm // (ns * d)
    rope = rope_tables(positions.reshape(-1, ns, tt, d).swapaxes(2, 3).reshape(-1))
    tab_spec = pl.BlockSpec((tm, LANES), lambda i: (i, 0))
    return pl.pallas_call(
        functools.partial(_qkv_proj_kernel, d=d, ns=ns, tt=tt),
        grid=(t // tm,),
        in_specs=[pl.BlockSpec((tm, k), lambda i: (i, 0)),
                  pl.BlockSpec((k, 3 * D_MODEL), lambda i: (0, g)),
                  tab_spec, tab_spec, tab_spec],
        out_specs=pl.BlockSpec((ns, d, tt, 3 * D_MODEL), lambda i: (i // tps, 0, i % tps, 0)),
        out_shape=jax.ShapeDtypeStruct((t // span, d, SUB_WIN, 3 * D_MODEL), _BF16),
        compiler_params=_cparams("parallel"), name=f"qkv_proj_g{g}",
    )(x, w_bf16, *rope)


def _dil_attn_kernel(q_ref, k_ref, v_ref, kp_ref, vp_ref, o_ref, lse_ref, *, d, ns):
    j = pl.program_id(1)
    hp = pl.program_id(2)
    nb = ns * d
    row = lax.broadcasted_iota(jnp.int32, (1, SUB_WIN, SUB_WIN), 1)
    col = lax.broadcasted_iota(jnp.int32, (1, SUB_WIN, SUB_WIN), 2)
    cur_ok = col <= row
    blk = lax.broadcasted_iota(jnp.int32, (nb, 1, 1), 0)
    prev_ok = (col >= row) & ((blk >= d) | (j > 0))
    lane = lax.broadcasted_iota(jnp.int32, (SUB_WIN, LANES), 1)
    scale = 1.0 / math.sqrt(HEAD_DIM)

    def block_rows(b):
        s, r = divmod(b, d)
        start = s * SUB_WIN * d + r
        return pl.ds(start, SUB_WIN, stride=d) if d > 1 else pl.ds(start, SUB_WIN)

    @pl.when(hp == 0)
    def _():
        lse_ref[...] = jnp.zeros_like(lse_ref)

    lses = []
    for hh in range(HEADS_PER_STEP):
        sl = slice(hh * HEAD_DIM, (hh + 1) * HEAD_DIM)
        flat = lambda a: a.reshape(nb, SUB_WIN, HEAD_DIM)
        q, kc, vc = flat(q_ref[:, :, :, sl]), flat(k_ref[:, :, :, sl]), flat(v_ref[:, :, :, sl])
        if ns > 1:
            kp = flat(jnp.concatenate([kp_ref[:, :, :, sl], k_ref[:ns - 1, :, :, sl]], axis=0))
            vp = flat(jnp.concatenate([vp_ref[:, :, :, sl], v_ref[:ns - 1, :, :, sl]], axis=0))
        else:
            kp, vp = flat(kp_ref[:, :, :, sl]), flat(vp_ref[:, :, :, sl])
        sc = jnp.einsum('bqd,bkd->bqk', q, kc, preferred_element_type=_F32) * scale
        sp = jnp.einsum('bqd,bkd->bqk', q, kp, preferred_element_type=_F32) * scale
        sc = jnp.where(cur_ok, sc, NEG_BIG)
        sp = jnp.where(prev_ok, sp, NEG_BIG)
        m = jnp.maximum(jnp.max(sc, axis=2, keepdims=True), jnp.max(sp, axis=2, keepdims=True))
        pc = jnp.exp(sc - m)
        pp = jnp.exp(sp - m)
        l = jnp.sum(pc, axis=2, keepdims=True) + jnp.sum(pp, axis=2, keepdims=True)
        acc = jnp.einsum('bqk,bkd->bqd', pc.astype(_BF16), vc, preferred_element_type=_F32)
        acc += jnp.einsum('bqk,bkd->bqd', pp.astype(_BF16), vp, preferred_element_type=_F32)
        o = acc / l
        for b in range(nb):
            o_ref[hh, block_rows(b), :] = o[b]
        lses.append(m + jnp.log(l))
    for b in range(nb):
        cur = lse_ref[block_rows(b), :]
        for hh in range(HEADS_PER_STEP):
            cur = jnp.where(lane == hp * HEADS_PER_STEP + hh, lses[hh][b], cur)
        lse_ref[block_rows(b), :] = cur


def dilated_group_attention(qkv, g, bsz, seq):
    d = DIL[g]
    ns = ATTN_ROWS // (SUB_WIN * d)
    nsteps = seq // ATTN_ROWS
    hw = HEADS_PER_STEP * HEAD_DIM
    ncol = D_MODEL // hw
    cur = lambda comp: pl.BlockSpec((ns, d, SUB_WIN, hw), lambda b, j, hp: (b * nsteps + j, 0, 0, comp * ncol + hp))
    prev = lambda comp: pl.BlockSpec(
        (1, d, SUB_WIN, hw), lambda b, j, hp: (jnp.maximum((b * nsteps + j) * ns - 1, 0), 0, 0, comp * ncol + hp))
    return pl.pallas_call(
        functools.partial(_dil_attn_kernel, d=d, ns=ns),
        grid=(bsz, nsteps, HEADS // HEADS_PER_STEP),
        in_specs=[cur(0), cur(1), cur(2), prev(1), prev(2)],
        out_specs=[pl.BlockSpec((HEADS_PER_STEP, ATTN_ROWS, HEAD_DIM), lambda b, j, hp: (hp, b * nsteps + j, 0)),
                   pl.BlockSpec((ATTN_ROWS, LANES), lambda b, j, hp: (b * nsteps + j, 0))],
        out_shape=[jax.ShapeDtypeStruct((HEADS, bsz * seq, HEAD_DIM), _F32),
                   jax.ShapeDtypeStruct((bsz * seq, LANES), _F32)],
        compiler_params=_cparams("parallel", "parallel", "arbitrary"),
        name=f"dilated_attn_g{g}",
    )(qkv, qkv, qkv, qkv, qkv)


SSM_IN_TM = 512
HALO = SUBLANES
ZX_COLS = SSM_INNER + SSM_XBC


def _softplus(x):
    return jnp.maximum(x, 0.0) + jnp.log1p(jnp.exp(-jnp.abs(x)))


def _ssm_in_kernel(x_ref, w_ref, wdt_ref, cw_ref, cb_ref, dtb_ref, gz_ref, xs_ref, bc_ref, dt_ref,
                   carry_ref, *, tiles_per_seq):
    tm = x_ref.shape[0]
    cw = D_MODEL
    xb = x_ref[...].astype(_BF16)

    @pl.when(pl.program_id(0) % tiles_per_seq == 0)
    def _():
        carry_ref[...] = jnp.zeros_like(carry_ref)

    for i in range(SSM_INNER // cw):
        z = jnp.dot(xb, w_ref[:, i * cw:(i + 1) * cw], preferred_element_type=_F32)
        gz_ref[:, i * cw:(i + 1) * cw] = z * jax.nn.sigmoid(z)
    for c in range(SSM_XBC // cw):
        cols = slice(c * cw, (c + 1) * cw)
        acc = jnp.dot(xb, w_ref[:, SSM_INNER + c * cw:SSM_INNER + (c + 1) * cw], preferred_element_type=_F32)
        ext = jnp.concatenate([carry_ref[c], acc], axis=0)
        carry_ref[c] = acc[tm - HALO:, :]
        conv = cb_ref[:, cols] + cw_ref[SSM_CONV - 1:SSM_CONV, cols] * acc
        for k in range(1, SSM_CONV):
            conv = conv + cw_ref[SSM_CONV - 1 - k:SSM_CONV - k, cols] * pltpu.roll(ext, k, 0)[HALO:, :]
        act = conv * jax.nn.sigmoid(conv)
        if (c + 1) * cw <= SSM_INNER:
            xs_ref[:, cols] = act
        else:
            bc_ref[...] = act.astype(_BF16)
    dt_raw = jnp.dot(xb, wdt_ref[...], preferred_element_type=_F32)
    dt_ref[...] = _softplus(dt_raw + dtb_ref[...])


def ssm_in_proj(x, w_in, conv_w, conv_b, dt_bias, seq):
    t, k = x.shape
    tm = SSM_IN_TM
    assert 2 * SSM_BC == D_MODEL
    pad = LANES - SSM_HEADS
    w_main = w_in[:, :ZX_COLS].astype(_BF16)
    w_dt = jnp.pad(w_in[:, ZX_COLS:], ((0, 0), (0, pad))).astype(_BF16)
    dtb = jnp.pad(dt_bias, (0, pad)).reshape(1, LANES)
    cst = lambda shape: pl.BlockSpec(shape, lambda i: (0,) * len(shape))
    once = lambda shape: pl.BlockSpec(shape, lambda i: (0,) * len(shape), pipeline_mode=pl.Buffered(1))
    row = lambda w: pl.BlockSpec((tm, w), lambda i: (i, 0))
    return pl.pallas_call(
        functools.partial(_ssm_in_kernel, tiles_per_seq=seq // tm),
        grid=(t // tm,),
        in_specs=[row(k), once((k, ZX_COLS)), once((k, LANES)),
                  cst((SSM_CONV, SSM_XBC)), cst((1, SSM_XBC)), cst((1, LANES))],
        out_specs=[row(SSM_INNER), row(SSM_INNER), row(2 * SSM_BC), row(LANES)],
        out_shape=[jax.ShapeDtypeStruct((t, SSM_INNER), _F32), jax.ShapeDtypeStruct((t, SSM_INNER), _F32),
                   jax.ShapeDtypeStruct((t, 2 * SSM_BC), _BF16), jax.ShapeDtypeStruct((t, LANES), _F32)],
        scratch_shapes=[pltpu.VMEM((SSM_XBC // D_MODEL, HALO, D_MODEL), _F32)],
        compiler_params=_cparams("arbitrary"), name="ssm_in_proj",
    )(x, w_main, w_dt, conv_w, conv_b.reshape(1, -1), dtb)


def _ssd_kernel(gz_ref, xs_ref, bc_ref, dt_ref, a_ref, dsk_ref, nw_ref, o_ref, state_ref, y_ref):
    c = pl.program_id(1)
    q = CHUNK

    @pl.when(c == 0)
    def _():
        state_ref[...] = jnp.zeros_like(state_ref)

    dt = dt_ref[...]
    da = dt * a_ref[...]
    r_io = lax.broadcasted_iota(jnp.int32, (q, q), 0)
    c_io = lax.broadcasted_iota(jnp.int32, (q, q), 1)
    causal = r_io >= c_io
    tri = causal.astype(_F32)
    a_cum = jnp.dot(tri, da, preferred_element_type=_F32, precision=lax.Precision.HIGHEST)
    a_cum_t = a_cum.T
    dt_t = dt.T
    a_last = a_cum[q - 1:q, :]
    w_state = jnp.exp(a_last - a_cum) * dt
    e_acum = jnp.exp(a_cum)
    chunk_decay = jnp.exp(a_last)
    lane = lax.broadcasted_iota(jnp.int32, (q, LANES), 1)
    low = lane < SSM_P
    lane1 = lax.broadcasted_iota(jnp.int32, (1, LANES), 1)
    low1 = lane1 < SSM_P
    tn = (((0,), (0,)), ((), ()))
    nt = (((1,), (1,)), ((), ()))

    for g in range(SSM_GROUPS):
        bg = bc_ref[:, g * SSM_N:(g + 1) * SSM_N]
        cg = bc_ref[:, SSM_BC + g * SSM_N:SSM_BC + (g + 1) * SSM_N]
        cbm = lax.dot_general(cg, bg, nt, preferred_element_type=_F32)
        st = state_ref[g]
        y_off = jnp.dot(cg, st.astype(_BF16), preferred_element_type=_F32)
        xw_parts, dec_parts = [], []
        for j in range(SSM_GHEADS // 2):
            h0 = g * SSM_GHEADS + 2 * j
            col = (g * SSM_GHEADS // 2 + j) * LANES
            xp = xs_ref[:, col:col + LANES]
            y_diag = jnp.zeros((q, LANES), _F32)
            for half, hh in enumerate((h0, h0 + 1)):
                diff = a_cum[:, hh:hh + 1] - a_cum_t[hh:hh + 1, :]
                lmat = jnp.exp(jnp.where(causal, diff, NEG_BIG))
                mm = (cbm * lmat * dt_t[hh:hh + 1, :]).astype(_BF16)
                keep = low if half == 0 else jnp.logical_not(low)
                xh = jnp.where(keep, xp, 0.0).astype(_BF16)
                y_diag = y_diag + jnp.dot(mm, xh, preferred_element_type=_F32)
            e_pair = jnp.where(low, e_acum[:, h0:h0 + 1], e_acum[:, h0 + 1:h0 + 2])
            w_pair = jnp.where(low, w_state[:, h0:h0 + 1], w_state[:, h0 + 1:h0 + 2])
            y_ref[:, col:col + LANES] = (y_diag + y_off[:, j * LANES:(j + 1) * LANES] * e_pair
                                         + xp * dsk_ref[:, col:col + LANES])
            xw_parts.append((xp * w_pair).astype(_BF16))
            dec_parts.append(jnp.where(low1, chunk_decay[:, h0:h0 + 1], chunk_decay[:, h0 + 1:h0 + 2]))
        xw = jnp.concatenate(xw_parts, axis=1)
        dec = jnp.concatenate(dec_parts, axis=1)
        state_ref[g] = st * dec + lax.dot_general(bg, xw, tn, preferred_element_type=_F32)

    yg = y_ref[...] * gz_ref[...]
    gw = SSM_INNER // SSM_GROUPS
    for g in range(SSM_GROUPS):
        v = yg[:, g * gw:(g + 1) * gw]
        ms = jnp.mean(v * v, axis=-1, keepdims=True)
        o_ref[:, g * gw:(g + 1) * gw] = (v * lax.rsqrt(ms + RMS_EPS)
                                         * nw_ref[:, g * gw:(g + 1) * gw]).astype(o_ref.dtype)


def ssd_mixer(gz, xs, bc, dt, a_log, d_skip, norm_w, bsz, seq):
    nc = seq // CHUNK
    a = jnp.pad(-jnp.exp(a_log.astype(_F32)), (0, LANES - SSM_HEADS)).reshape(1, LANES)
    dsk = jnp.repeat(d_skip.astype(_F32), SSM_P).reshape(1, SSM_INNER)
    cst = lambda shape: pl.BlockSpec(shape, lambda b, c: (0,) * len(shape))
    row = lambda w: pl.BlockSpec((CHUNK, w), lambda b, c: (b * nc + c, 0))
    return pl.pallas_call(
        _ssd_kernel,
        grid=(bsz, nc),
        in_specs=[row(SSM_INNER), row(SSM_INNER), row(2 * SSM_BC), row(LANES),
                  cst((1, LANES)), cst((1, SSM_INNER)), cst((1, SSM_INNER))],
        out_specs=row(SSM_INNER),
        out_shape=jax.ShapeDtypeStruct((bsz * seq, SSM_INNER), _BF16),
        scratch_shapes=[pltpu.VMEM((SSM_GROUPS, SSM_N, SSM_GHEADS * SSM_P), _F32),
                        pltpu.VMEM((CHUNK, SSM_INNER), _F32)],
        compiler_params=_cparams("parallel", "arbitrary"), name="ssd_mixer",
    )(gz, xs, bc, dt, a, dsk, norm_w.reshape(1, -1))


ROUTE_SUB = 256
ROUTE_TM_ATTN = 512
ROUTE_TM_SSM = 1024


def _layer_norm(y, g, b):
    mu = jnp.mean(y, axis=-1, keepdims=True)
    yc = y - mu
    var = jnp.mean(yc * yc, axis=-1, keepdims=True)
    return yc * lax.rsqrt(var + LN_EPS) * g + b


def _route(h1s, wrt_ref, br_ref, eidx_ref, gate_ref, rank_ref, cnt_ref, run_ref):
    sub = ROUTE_SUB
    n = len(h1s)
    nt = (((1,), (1,)), ((), ()))
    wrt = wrt_ref[...]
    vals = [lax.dot_general(wrt, h, nt, preferred_element_type=_F32, precision=lax.Precision.HIGHEST)
            + br_ref[...] for h in h1s]
    eio = lax.broadcasted_iota(jnp.int32, (N_EXPERTS, sub), 0)
    tops, ids, hots = ([[] for _ in range(n)] for _ in range(3))
    for _ in range(TOP_K):
        for a in range(n):
            mk = jnp.max(vals[a], axis=0, keepdims=True)
            ik = jnp.min(jnp.where(vals[a] == mk, eio, N_EXPERTS), axis=0, keepdims=True)
            hot = eio == ik
            vals[a] = jnp.where(hot, -jnp.inf, vals[a])
            tops[a].append(mk)
            ids[a].append(ik)
            hots[a].append(hot)
    r_io = lax.broadcasted_iota(jnp.int32, (sub, sub), 0)
    c_io = lax.broadcasted_iota(jnp.int32, (sub, sub), 1)
    upper = (r_io < c_io).astype(_BF16)
    hot_all = [h[0] | h[1] | h[2] | h[3] for h in hots]
    prefix = [jnp.dot(ha.astype(_BF16), upper, preferred_element_type=_F32) for ha in hot_all]
    counts = [jnp.sum(ha.astype(_F32), axis=1, keepdims=True) for ha in hot_all]
    run = run_ref[...]
    for a in range(n):
        sl = slice(a * sub, (a + 1) * sub)
        base = prefix[a] + run
        exps = [jnp.exp(t - tops[a][0]) for t in tops[a]]
        den = exps[0] + exps[1] + exps[2] + exps[3]
        for k in range(TOP_K):
            eidx_ref[k:k + 1, sl] = ids[a][k]
            rank_ref[k:k + 1, sl] = jnp.sum(jnp.where(hots[a][k], base, 0.0), axis=0,
                                            keepdims=True).astype(jnp.int32)
        g8 = jnp.concatenate([e / den for e in exps] + [jnp.zeros((SUBLANES - TOP_K, sub), _F32)], axis=0)
        gate_ref[sl, :] = g8.T
        run = run + counts[a]
    run_ref[...] = run
    cnt_ref[...] = jnp.broadcast_to(run, cnt_ref.shape)


def _norm_and_route(mixes, x_ref, g_ref, b_ref, wrt_ref, br_ref, h_ref, eidx_ref, gate_ref, rank_ref,
                    cnt_ref, run_ref):
    sub = ROUTE_SUB
    h1s = [_layer_norm(DEEPNORM_ALPHA * x_ref[a * sub:(a + 1) * sub, :] + mix, g_ref[...], b_ref[...])
           for a, mix in enumerate(mixes)]
    for a, h1 in enumerate(h1s):
        h_ref[a * sub:(a + 1) * sub, :] = h1
    _route(h1s, wrt_ref, br_ref, eidx_ref, gate_ref, rank_ref, cnt_ref, run_ref)


def _attn_out_kernel(o0_ref, o1_ref, o2_ref, l0_ref, l1_ref, l2_ref, x_ref, wo_ref, g_ref, b_ref,
                     wrt_ref, br_ref, h_ref, eidx_ref, gate_ref, rank_ref, cnt_ref, run_ref, ob_ref):
    @pl.when(pl.program_id(0) == 0)
    def _():
        run_ref[...] = jnp.zeros_like(run_ref)

    sub = ROUTE_SUB
    l0, l1, l2 = l0_ref[...], l1_ref[...], l2_ref[...]
    m = jnp.maximum(jnp.maximum(l0, l1), l2)
    e0, e1, e2 = jnp.exp(l0 - m), jnp.exp(l1 - m), jnp.exp(l2 - m)
    inv = 1.0 / (e0 + e1 + e2)
    w0, w1, w2 = e0 * inv, e1 * inv, e2 * inv
    for h in range(HEADS):
        o = w0[:, h:h + 1] * o0_ref[h] + w1[:, h:h + 1] * o1_ref[h] + w2[:, h:h + 1] * o2_ref[h]
        ob_ref[:, h * HEAD_DIM:(h + 1) * HEAD_DIM] = o.astype(_BF16)
    mixes = [jnp.dot(ob_ref[a * sub:(a + 1) * sub, :], wo_ref[...], preferred_element_type=_F32)
             for a in range(x_ref.shape[0] // sub)]
    _norm_and_route(mixes, x_ref, g_ref, b_ref, wrt_ref, br_ref, h_ref, eidx_ref, gate_ref, rank_ref,
                    cnt_ref, run_ref)


def _ssm_out_kernel(y_ref, x_ref, wo_ref, g_ref, b_ref, wrt_ref, br_ref,
                    h_ref, eidx_ref, gate_ref, rank_ref, cnt_ref, run_ref):
    @pl.when(pl.program_id(0) == 0)
    def _():
        run_ref[...] = jnp.zeros_like(run_ref)

    sub = ROUTE_SUB
    mixes = [jnp.dot(y_ref[a * sub:(a + 1) * sub, :], wo_ref[...], preferred_element_type=_F32)
             for a in range(x_ref.shape[0] // sub)]
    _norm_and_route(mixes, x_ref, g_ref, b_ref, wrt_ref, br_ref, h_ref, eidx_ref, gate_ref, rank_ref,
                    cnt_ref, run_ref)


def _route_out_specs(t, tm):
    specs = [pl.BlockSpec((tm, D_MODEL), lambda i: (i, 0)),
             pl.BlockSpec((TOP_K, tm), lambda i: (0, i)),
             pl.BlockSpec((tm, SUBLANES), lambda i: (i, 0)),
             pl.BlockSpec((TOP_K, tm), lambda i: (0, i)),
             pl.BlockSpec((N_EXPERTS, LANES), lambda i: (0, 0))]
    shapes = [jax.ShapeDtypeStruct((t, D_MODEL), _F32),
              jax.ShapeDtypeStruct((TOP_K, t), jnp.int32),
              jax.ShapeDtypeStruct((t, SUBLANES), _F32),
              jax.ShapeDtypeStruct((TOP_K, t), jnp.int32),
              jax.ShapeDtypeStruct((N_EXPERTS, LANES), _F32)]
    return specs, shapes


def _const_spec(shape):
    return pl.BlockSpec(shape, lambda i: (0,) * len(shape))


def attn_out_norm_route(os_, lses, x, wo_bf16, ln_g, ln_b, w_router, b_router):
    t = x.shape[0]
    tm = ROUTE_TM_ATTN
    row = lambda w: pl.BlockSpec((tm, w), lambda i: (i, 0))
    out_specs, out_shapes = _route_out_specs(t, tm)
    return pl.pallas_call(
        _attn_out_kernel,
        grid=(t // tm,),
        in_specs=[pl.BlockSpec((HEADS, tm, HEAD_DIM), lambda i: (0, i, 0))] * 3 + [row(LANES)] * 3 + [row(D_MODEL),
                  _const_spec((D_MODEL, D_MODEL)), _const_spec((1, D_MODEL)), _const_spec((1, D_MODEL)),
                  _const_spec((N_EXPERTS, D_MODEL)), _const_spec((N_EXPERTS, 1))],
        out_specs=out_specs, out_shape=out_shapes,
        scratch_shapes=[pltpu.VMEM((N_EXPERTS, 1), _F32), pltpu.VMEM((tm, D_MODEL), _BF16)],
        compiler_params=_cparams("arbitrary"), name="attn_out_norm_route",
    )(*os_, *lses, x, wo_bf16, ln_g.reshape(1, -1), ln_b.reshape(1, -1),
      w_router.T, b_router.reshape(-1, 1))


def ssm_out_norm_route(yn, x, wo_bf16, ln_g, ln_b, w_router, b_router):
    t = x.shape[0]
    tm = ROUTE_TM_SSM
    row = lambda w: pl.BlockSpec((tm, w), lambda i: (i, 0))
    out_specs, out_shapes = _route_out_specs(t, tm)
    return pl.pallas_call(
        _ssm_out_kernel,
        grid=(t // tm,),
        in_specs=[row(SSM_INNER), row(D_MODEL),
                  _const_spec((SSM_INNER, D_MODEL)), _const_spec((1, D_MODEL)), _const_spec((1, D_MODEL)),
                  _const_spec((N_EXPERTS, D_MODEL)), _const_spec((N_EXPERTS, 1))],
        out_specs=out_specs, out_shape=out_shapes,
        scratch_shapes=[pltpu.VMEM((N_EXPERTS, 1), _F32)],
        compiler_params=_cparams("arbitrary"), name="ssm_out_norm_route",
    )(yn, x, wo_bf16, ln_g.reshape(1, -1), ln_b.reshape(1, -1), w_router.T, b_router.reshape(-1, 1))


MOE_TM = 256
MOE_BM = 256


def _dest_kernel(eidx_ref, rank_ref, offs_ref, dest_ref):
    eidx = eidx_ref[...]
    acc = rank_ref[...]
    for e in range(N_EXPERTS):
        acc = acc + jnp.where(eidx == e, offs_ref[e], 0)
    dest_ref[...] = acc


def moe_dest(eidx, rank, offs):
    t = eidx.shape[1]
    tn = min(4096, t)
    blk = pl.BlockSpec((TOP_K, tn), lambda i: (0, i))
    return pl.pallas_call(
        _dest_kernel, grid=(t // tn,),
        in_specs=[blk, blk, pl.BlockSpec(memory_space=pltpu.SMEM)],
        out_specs=blk, out_shape=jax.ShapeDtypeStruct((TOP_K, t), jnp.int32),
        compiler_params=_cparams("parallel"), name="moe_dest",
    )(eidx, rank, offs)


def _tile_major(dest, tm):
    t = dest.shape[1]
    return dest.reshape(TOP_K, t // tm, tm).transpose(1, 0, 2)


def _dispatch_kernel(dest_ref, h_ref, xs_ref, sem):
    tm = h_ref.shape[0]

    def issue(t, carry):
        for k in range(TOP_K):
            pltpu.make_async_copy(h_ref.at[pl.ds(t, 1)], xs_ref.at[pl.ds(dest_ref[k, t], 1)], sem).start()
        return carry

    lax.fori_loop(0, tm, issue, 0, unroll=8)
    for k in range(TOP_K):
        pltpu.make_async_copy(h_ref, xs_ref.at[pl.ds(0, tm)], sem).wait()


def moe_dispatch(h, dest):
    t = h.shape[0]
    tm = MOE_TM
    return pl.pallas_call(
        _dispatch_kernel, grid=(t // tm,),
        in_specs=[pl.BlockSpec((None, TOP_K, tm), lambda i: (i, 0, 0), memory_space=pltpu.SMEM),
                  pl.BlockSpec((tm, D_MODEL), lambda i: (i, 0))],
        out_specs=pl.BlockSpec(memory_space=pl.ANY),
        out_shape=jax.ShapeDtypeStruct((t * TOP_K, D_MODEL), _F32),
        scratch_shapes=[pltpu.SemaphoreType.DMA(())],
        compiler_params=_cparams("arbitrary"), name="moe_dispatch",
    )(_tile_major(dest, tm), h)


def _expert_kernel(blk_ref, e_ref, lo_ref, hi_ref, xs_ref, wu_ref, bu_ref, wd_ref, bd_ref,
                   ys_ref, wub_ref, wdb_ref):
    i = pl.program_id(0)
    e = e_ref[i]
    e_prev = e_ref[jnp.maximum(i - 1, 0)]

    @pl.when((i == 0) | (e != e_prev))
    def _():
        wub_ref[...] = wu_ref[...].astype(_BF16)
        wdb_ref[...] = wd_ref[...].astype(_BF16)

    lo = lo_ref[i]
    hi = hi_ref[i]

    @pl.when(hi > lo)
    def _():
        x = xs_ref[...].astype(_BF16)
        h = jnp.dot(x, wub_ref[...], preferred_element_type=_F32) + bu_ref[...]
        gate = jnp.minimum(h[:, :D_FF], SWIGLU_LIMIT)
        up = jnp.clip(h[:, D_FF:], -SWIGLU_LIMIT, SWIGLU_LIMIT)
        glu = gate * jax.nn.sigmoid(SWIGLU_ALPHA * gate)
        act = ((up + 1.0) * glu).astype(_BF16)
        y = jnp.dot(act, wdb_ref[...], preferred_element_type=_F32) + bd_ref[...]
        row = lax.broadcasted_iota(jnp.int32, y.shape, 0)
        y = jnp.where((row >= lo) & (row < hi), y, 0.0)

        @pl.when(lo == 0)
        def _():
            ys_ref[...] = y

        @pl.when(lo > 0)
        def _():
            ys_ref[...] += y


def _expert_schedule(offs, n_rows, bm):
    nblk = n_rows // bm
    n_items = nblk + N_EXPERTS - 1
    bnd = offs[1:]
    pos = jnp.arange(N_EXPERTS - 1, dtype=jnp.int32) + jnp.minimum(bnd // bm + 1, nblk)
    i = jnp.arange(n_items, dtype=jnp.int32)[:, None]
    at = pos[None, :] == i
    is_bnd = jnp.any(at, axis=1)
    bnd_val = jnp.sum(jnp.where(at, bnd[None, :], 0), axis=1)
    n_before = jnp.sum((pos[None, :] < i).astype(jnp.int32), axis=1)
    cuts = jnp.where(is_bnd, bnd_val, (i[:, 0] - n_before) * bm)
    ends = jnp.concatenate([cuts[1:], jnp.array([n_rows], jnp.int32)])
    blk = jnp.minimum(cuts // bm, nblk - 1)
    e = jnp.minimum(jnp.sum((bnd[None, :] <= cuts[:, None]).astype(jnp.int32), axis=1), N_EXPERTS - 1)
    return blk, e, cuts - blk * bm, ends - blk * bm


def moe_experts(xs, offs, layer, w_up, b_up, w_down, b_down):
    n_rows = xs.shape[0]
    bm = MOE_BM
    blk, e, lo, hi = _expert_schedule(offs, n_rows, bm)
    n_items = n_rows // bm + N_EXPERTS - 1
    wmap = lambda i, b, e, lo, hi: (layer, e[i], 0, 0)
    grid_spec = pltpu.PrefetchScalarGridSpec(
        num_scalar_prefetch=4, grid=(n_items,),
        in_specs=[pl.BlockSpec((bm, D_MODEL), lambda i, b, e, lo, hi: (b[i], 0)),
                  pl.BlockSpec((None, None, D_MODEL, 2 * D_FF), wmap),
                  pl.BlockSpec((None, None, 1, 2 * D_FF), wmap),
                  pl.BlockSpec((None, None, D_FF, D_MODEL), wmap),
                  pl.BlockSpec((None, None, 1, D_MODEL), wmap)],
        out_specs=pl.BlockSpec((bm, D_MODEL), lambda i, b, e, lo, hi: (b[i], 0)),
        scratch_shapes=[pltpu.VMEM((D_MODEL, 2 * D_FF), _BF16), pltpu.VMEM((D_FF, D_MODEL), _BF16)])
    nl = w_up.shape[0]
    return pl.pallas_call(
        _expert_kernel, grid_spec=grid_spec,
        out_shape=jax.ShapeDtypeStruct((n_rows, D_MODEL), _F32),
        compiler_params=_cparams("arbitrary"), name="moe_experts",
    )(blk, e, lo, hi, xs, w_up, b_up.reshape(nl, N_EXPERTS, 1, -1), w_down, b_down.reshape(nl, N_EXPERTS, 1, -1))


def _combine_kernel(dest_ref, ys_ref, h_ref, gate_ref, g_ref, b_ref, o_ref, buf_ref, sem):
    tm = h_ref.shape[0]

    def issue(t, carry):
        for k in range(TOP_K):
            pltpu.make_async_copy(ys_ref.at[pl.ds(dest_ref[k, t], 1)], buf_ref.at[k, pl.ds(t, 1)], sem).start()
        return carry

    lax.fori_loop(0, tm, issue, 0, unroll=8)
    for k in range(TOP_K):
        pltpu.make_async_copy(ys_ref.at[pl.ds(0, tm)], buf_ref.at[k], sem).wait()
    gates = gate_ref[...]
    y = DEEPNORM_ALPHA * h_ref[...]
    for k in range(TOP_K):
        y = y + gates[:, k:k + 1] * buf_ref[k]
    o_ref[...] = _layer_norm(y, g_ref[...], b_ref[...])


def moe_combine(ys, dest, h, gates, ln_g, ln_b):
    t = h.shape[0]
    tm = MOE_TM
    row = pl.BlockSpec((tm, D_MODEL), lambda i: (i, 0))
    return pl.pallas_call(
        _combine_kernel, grid=(t // tm,),
        in_specs=[pl.BlockSpec((None, TOP_K, tm), lambda i: (i, 0, 0), memory_space=pltpu.SMEM),
                  pl.BlockSpec(memory_space=pl.ANY), row,
                  pl.BlockSpec((tm, SUBLANES), lambda i: (i, 0)),
                  _const_spec((1, D_MODEL)), _const_spec((1, D_MODEL))],
        out_specs=row, out_shape=jax.ShapeDtypeStruct((t, D_MODEL), _F32),
        scratch_shapes=[pltpu.VMEM((TOP_K, tm, D_MODEL), _F32), pltpu.SemaphoreType.DMA(())],
        compiler_params=_cparams("arbitrary"), name="moe_combine",
    )(_tile_major(dest, tm), ys, h, gates, ln_g.reshape(1, -1), ln_b.reshape(1, -1))


def moe_layer(h, eidx, gates, rank, counts, layer, w_up, b_up, w_down, b_down, ln_g, ln_b):
    cnt = counts[:, 0].astype(jnp.int32)
    offs = jnp.cumsum(cnt) - cnt
    dest = moe_dest(eidx, rank, offs)
    xs = moe_dispatch(h, dest)
    ys = moe_experts(xs, offs, layer, w_up, b_up, w_down, b_down)
    return moe_combine(ys, dest, h, gates, ln_g, ln_b)


def kernel(x, positions, attn_w_in, attn_w_out, ssm_w_in, ssm_conv_w, ssm_conv_b, ssm_dt_bias,
           ssm_a_log, ssm_d, ssm_norm_w, ssm_w_out, moe_w_router, moe_b_router, moe_w_up,
           moe_b_up, moe_w_down, moe_b_down, ln_mix_g, ln_mix_b, ln_ffn_g, ln_ffn_b):
    bsz, seq, _ = x.shape
    xf = x.reshape(bsz * seq, D_MODEL)

    w_qkv = attn_w_in[0].astype(_BF16)
    outs = [dilated_group_attention(qkv_proj_group(xf, w_qkv, positions, g), g, bsz, seq)
            for g in range(N_GROUPS)]
    h, eidx, gates, rank, counts = attn_out_norm_route(
        [o for o, _ in outs], [l for _, l in outs], xf, attn_w_out[0].astype(_BF16),
        ln_mix_g[0], ln_mix_b[0], moe_w_router[0], moe_b_router[0])
    xf = moe_layer(h, eidx, gates, rank, counts, 0, moe_w_up, moe_b_up, moe_w_down, moe_b_down,
                   ln_ffn_g[0], ln_ffn_b[0])

    gz, xs, bc, dt = ssm_in_proj(xf, ssm_w_in[0], ssm_conv_w[0], ssm_conv_b[0], ssm_dt_bias[0], seq)
    yn = ssd_mixer(gz, xs, bc, dt, ssm_a_log[0], ssm_d[0], ssm_norm_w[0], bsz, seq)
    h, eidx, gates, rank, counts = ssm_out_norm_route(
        yn, xf, ssm_w_out[0].astype(_BF16), ln_mix_g[1], ln_mix_b[1], moe_w_router[1], moe_b_router[1])
    xf = moe_layer(h, eidx, gates, rank, counts, 1, moe_w_up, moe_b_up, moe_w_down, moe_b_down,
                   ln_ffn_g[1], ln_ffn_b[1])
    return xf.reshape(bsz, seq, D_MODEL)
```

```python
import functools
import math

import jax
import jax.numpy as jnp
from jax import lax
from jax.experimental import pallas as pl
from jax.experimental.pallas import tpu as pltpu

D_MODEL = 1024
DEPTH = 2
DIL = (1, 4, 16)
N_GROUPS = 3
SUB_WIN = 128
HEAD_DIM = 128
HEADS = 8
ROT_DIM = 32
ROT_HALF = 16
ROPE_THETA = 500000.0
ATTN_PROJ = N_GROUPS * 3 * D_MODEL
SSM_INNER = 2048
SSM_P = 64
SSM_HEADS = 32
SSM_GROUPS = 4
SSM_GHEADS = 8
SSM_N = 128
SSM_CONV = 4
CHUNK = 128
SSM_BC = SSM_GROUPS * SSM_N
SSM_XBC = SSM_INNER + 2 * SSM_BC
N_EXPERTS = 32
TOP_K = 4
D_FF = 1024
SWIGLU_LIMIT = 7.0
SWIGLU_ALPHA = 1.702
DEEPNORM_ALPHA = (2 * DEPTH) ** 0.25
LN_EPS = 1e-5
RMS_EPS = 1e-5

LANES = 128
SUBLANES = 8
VMEM_LIMIT_BYTES = 56 * 1024 * 1024

NEG_BIG = -1e30

_F32 = jnp.float32
_BF16 = jnp.bfloat16


def _cparams(*sem):
    return pltpu.CompilerParams(dimension_semantics=sem, vmem_limit_bytes=VMEM_LIMIT_BYTES)


def _rope_table_kernel(pos_ref, invf_ref, c_ref, sa_ref, sb_ref):
    pos = pos_ref[...].astype(_F32)
    ang = pos * invf_ref[...]
    lane = lax.broadcasted_iota(jnp.int32, ang.shape, 1)
    cos = jnp.cos(ang)
    sin = jnp.sin(ang)
    c_ref[...] = jnp.where(lane < ROT_DIM, cos, 1.0)
    sa_ref[...] = jnp.where((lane >= ROT_HALF) & (lane < ROT_DIM), sin, 0.0)
    sb_ref[...] = jnp.where(lane < ROT_HALF, -sin, 0.0)


def rope_tables(positions):
    t = positions.size
    tm = 1024
    pos = positions.reshape(t, 1)
    j = jnp.arange(LANES)
    invf = jnp.where(j < ROT_DIM,
                     ROPE_THETA ** (-(2.0 * (j % ROT_HALF)).astype(_F32) / ROT_DIM), 0.0)
    invf = invf.astype(_F32).reshape(1, LANES)
    out = jax.ShapeDtypeStruct((t, LANES), _F32)
    return pl.pallas_call(
        _rope_table_kernel,
        grid=(t // tm,),
        in_specs=[pl.BlockSpec((tm, 1), lambda i: (i, 0)),
                  pl.BlockSpec((1, LANES), lambda i: (0, 0))],
        out_specs=[pl.BlockSpec((tm, LANES), lambda i: (i, 0))] * 3,
        out_shape=[out, out, out],
        compiler_params=_cparams("parallel"),
        name="rope_tables",
    )(pos, invf)


QKV_TM = 512
ATTN_ROWS = 2048
HEADS_PER_STEP = 2


def _qkv_proj_kernel(x_ref, w_ref, c_ref, sa_ref, sb_ref, o_ref, *, d, ns, tt):
    tm = x_ref.shape[0]
    xb = x_ref[...].astype(_BF16)
    if d > 1:
        dst = lax.broadcasted_iota(jnp.int32, (tm, tm), 0)
        src = lax.broadcasted_iota(jnp.int32, (tm, tm), 1)
        lt, ld = tt.bit_length() - 1, d.bit_length() - 1
        t_ = dst & (tt - 1)
        r_ = lax.shift_right_logical(dst, lt) & (d - 1)
        s_ = lax.shift_right_logical(dst, lt + ld)
        perm = (src == ((s_ * tt + t_) * d + r_)).astype(_BF16)
        xb = jnp.dot(perm, xb, preferred_element_type=_F32).astype(_BF16)
    c = c_ref[...]
    sa = sa_ref[...]
    sb = sb_ref[...]
    for comp in range(3):
        acc = jnp.dot(xb, w_ref[:, comp * D_MODEL:(comp + 1) * D_MODEL], preferred_element_type=_F32)
        for h in range(HEADS):
            sl = slice(h * HEAD_DIM, (h + 1) * HEAD_DIM)
            t = acc[:, sl]
            if comp < 2:
                t = t * c + pltpu.roll(t, ROT_HALF, 1) * sa + pltpu.roll(t, HEAD_DIM - ROT_HALF, 1) * sb
            o_ref[:, :, :, comp * D_MODEL + h * HEAD_DIM:comp * D_MODEL + (h + 1) * HEAD_DIM] = (
                t.reshape(ns, d, tt, HEAD_DIM).astype(o_ref.dtype))


def qkv_proj_group(x, w_bf16, positions, g):
    t, k = x.shape
    d = DIL[g]
    span = SUB_WIN * d
    tm = QKV_TM
    ns = max(1, tm // span)
    tps = max(1, span // tm)
    tt = tm // (ns * d)
    rope = rope_tables(positions.reshape(-1, ns, tt, d).swapaxes(2, 3).reshape(-1))
    tab_spec = pl.BlockSpec((tm, LANES), lambda i: (i, 0))
    return pl.pallas_call(
        functools.partial(_qkv_proj_kernel, d=d, ns=ns, tt=tt),
        grid=(t // tm,),
        in_specs=[pl.BlockSpec((tm, k), lambda i: (i, 0)),
                  pl.BlockSpec((k, 3 * D_MODEL), lambda i: (0, g)),
                  tab_spec, tab_spec, tab_spec],
        out_specs=pl.BlockSpec((ns, d, tt, 3 * D_MODEL), lambda i: (i // tps, 0, i % tps, 0)),
        out_shape=jax.ShapeDtypeStruct((t // span, d, SUB_WIN, 3 * D_MODEL), _BF16),
        compiler_params=_cparams("parallel"), name=f"qkv_proj_g{g}",
    )(x, w_bf16, *rope)


def _dil_attn_kernel(q_ref, k_ref, v_ref, kp_ref, vp_ref, o_ref, lse_ref, *, d, ns):
    j = pl.program_id(1)
    hp = pl.program_id(2)
    nb = ns * d
    row = lax.broadcasted_iota(jnp.int32, (1, SUB_WIN, SUB_WIN), 1)
    col = lax.broadcasted_iota(jnp.int32, (1, SUB_WIN, SUB_WIN), 2)
    cur_ok = col <= row
    blk = lax.broadcasted_iota(jnp.int32, (nb, 1, 1), 0)
    prev_ok = (col >= row) & ((blk >= d) | (j > 0))
    lane = lax.broadcasted_iota(jnp.int32, (SUB_WIN, LANES), 1)
    scale = 1.0 / math.sqrt(HEAD_DIM)

    def block_rows(b):
        s, r = divmod(b, d)
        start = s * SUB_WIN * d + r
        return pl.ds(start, SUB_WIN, stride=d) if d > 1 else pl.ds(start, SUB_WIN)

    @pl.when(hp == 0)
    def _():
        lse_ref[...] = jnp.zeros_like(lse_ref)

    lses = []
    for hh in range(HEADS_PER_STEP):
        sl = slice(hh * HEAD_DIM, (hh + 1) * HEAD_DIM)
        flat = lambda a: a.reshape(nb, SUB_WIN, HEAD_DIM)
        q, kc, vc = flat(q_ref[:, :, :, sl]), flat(k_ref[:, :, :, sl]), flat(v_ref[:, :, :, sl])
        if ns > 1:
            kp = flat(jnp.concatenate([kp_ref[:, :, :, sl], k_ref[:ns - 1, :, :, sl]], axis=0))
            vp = flat(jnp.concatenate([vp_ref[:, :, :, sl], v_ref[:ns - 1, :, :, sl]], axis=0))
        else:
            kp, vp = flat(kp_ref[:, :, :, sl]), flat(vp_ref[:, :, :, sl])
        sc = jnp.einsum('bqd,bkd->bqk', q, kc, preferred_element_type=_F32) * scale
        sp = jnp.einsum('bqd,bkd->bqk', q, kp, preferred_element_type=_F32) * scale
        sc = jnp.where(cur_ok, sc, NEG_BIG)
        sp = jnp.where(prev_ok, sp, NEG_BIG)
        m = jnp.maximum(jnp.max(sc, axis=2, keepdims=True), jnp.max(sp, axis=2, keepdims=True))
        pc = jnp.exp(sc - m)
        pp = jnp.exp(sp - m)
        l = jnp.sum(pc, axis=2, keepdims=True) + jnp.sum(pp, axis=2, keepdims=True)
        acc = jnp.einsum('bqk,bkd->bqd', pc.astype(_BF16), vc, preferred_element_type=_F32)
        acc += jnp.einsum('bqk,bkd->bqd', pp.astype(_BF16), vp, preferred_element_type=_F32)
        o = acc / l
        for b in range(nb):
            o_ref[hh, block_rows(b), :] = o[b]
        lses.append(m + jnp.log(l))
    for b in range(nb):
        cur = lse_ref[block_rows(b), :]
        for hh in range(HEADS_PER_STEP):
            cur = jnp.where(lane == hp * HEADS_PER_STEP + hh, lses[hh][b], cur)
        lse_ref[block_rows(b), :] = cur


def dilated_group_attention(qkv, g, bsz, seq):
    d = DIL[g]
    ns = ATTN_ROWS // (SUB_WIN * d)
    nsteps = seq // ATTN_ROWS
    hw = HEADS_PER_STEP * HEAD_DIM
    ncol = D_MODEL // hw
    cur = lambda comp: pl.BlockSpec((ns, d, SUB_WIN, hw), lambda b, j, hp: (b * nsteps + j, 0, 0, comp * ncol + hp))
    prev = lambda comp: pl.BlockSpec(
        (1, d, SUB_WIN, hw), lambda b, j, hp: (jnp.maximum((b * nsteps + j) * ns - 1, 0), 0, 0, comp * ncol + hp))
    return pl.pallas_call(
        functools.partial(_dil_attn_kernel, d=d, ns=ns),
        grid=(bsz, nsteps, HEADS // HEADS_PER_STEP),
        in_specs=[cur(0), cur(1), cur(2), prev(1), prev(2)],
        out_specs=[pl.BlockSpec((HEADS_PER_STEP, ATTN_ROWS, HEAD_DIM), lambda b, j, hp: (hp, b * nsteps + j, 0)),
                   pl.BlockSpec((ATTN_ROWS, LANES), lambda b, j, hp: (b * nsteps + j, 0))],
        out_shape=[jax.ShapeDtypeStruct((HEADS, bsz * seq, HEAD_DIM), _F32),
                   jax.ShapeDtypeStruct((bsz * seq, LANES), _F32)],
        compiler_params=_cparams("parallel", "parallel", "arbitrary"),
        name=f"dilated_attn_g{g}",
    )(qkv, qkv, qkv, qkv, qkv)


SSM_IN_TM = 512
HALO = SUBLANES
ZX_COLS = SSM_INNER + SSM_XBC


def _softplus(x):
    return jnp.maximum(x, 0.0) + jnp.log1p(jnp.exp(-jnp.abs(x)))


def _ssm_in_kernel(x_ref, w_ref, wdt_ref, cw_ref, cb_ref, dtb_ref, gz_ref, xs_ref, bc_ref, dt_ref,
                   carry_ref, *, tiles_per_seq):
    tm = x_ref.shape[0]
    cw = D_MODEL
    xb = x_ref[...].astype(_BF16)

    @pl.when(pl.program_id(0) % tiles_per_seq == 0)
    def _():
        carry_ref[...] = jnp.zeros_like(carry_ref)

    for i in range(SSM_INNER // cw):
        z = jnp.dot(xb, w_ref[:, i * cw:(i + 1) * cw], preferred_element_type=_F32)
        gz_ref[:, i * cw:(i + 1) * cw] = z * jax.nn.sigmoid(z)
    for c in range(SSM_XBC // cw):
        cols = slice(c * cw, (c + 1) * cw)
        acc = jnp.dot(xb, w_ref[:, SSM_INNER + c * cw:SSM_INNER + (c + 1) * cw], preferred_element_type=_F32)
        ext = jnp.concatenate([carry_ref[c], acc], axis=0)
        carry_ref[c] = acc[tm - HALO:, :]
        conv = cb_ref[:, cols] + cw_ref[SSM_CONV - 1:SSM_CONV, cols] * acc
        for k in range(1, SSM_CONV):
            conv = conv + cw_ref[SSM_CONV - 1 - k:SSM_CONV - k, cols] * pltpu.roll(ext, k, 0)[HALO:, :]
        act = conv * jax.nn.sigmoid(conv)
        if (c + 1) * cw <= SSM_INNER:
            xs_ref[:, cols] = act
        else:
            bc_ref[...] = act.astype(_BF16)
    dt_raw = jnp.dot(xb, wdt_ref[...], preferred_element_type=_F32)
    dt_ref[...] = _softplus(dt_raw + dtb_ref[...])


def ssm_in_proj(x, w_in, conv_w, conv_b, dt_bias, seq):
    t, k = x.shape
    tm = SSM_IN_TM
    assert 2 * SSM_BC == D_MODEL
    pad = LANES - SSM_HEADS
    w_main = w_in[:, :ZX_COLS].astype(_BF16)
    w_dt = jnp.pad(w_in[:, ZX_COLS:], ((0, 0), (0, pad))).astype(_BF16)
    dtb = jnp.pad(dt_bias, (0, pad)).reshape(1, LANES)
    cst = lambda shape: pl.BlockSpec(shape, lambda i: (0,) * len(shape))
    once = lambda shape: pl.BlockSpec(shape, lambda i: (0,) * len(shape), pipeline_mode=pl.Buffered(1))
    row = lambda w: pl.BlockSpec((tm, w), lambda i: (i, 0))
    return pl.pallas_call(
        functools.partial(_ssm_in_kernel, tiles_per_seq=seq // tm),
        grid=(t // tm,),
        in_specs=[row(k), once((k, ZX_COLS)), once((k, LANES)),
                  cst((SSM_CONV, SSM_XBC)), cst((1, SSM_XBC)), cst((1, LANES))],
        out_specs=[row(SSM_INNER), row(SSM_INNER), row(2 * SSM_BC), row(LANES)],
        out_shape=[jax.ShapeDtypeStruct((t, SSM_INNER), _F32), jax.ShapeDtypeStruct((t, SSM_INNER), _F32),
                   jax.ShapeDtypeStruct((t, 2 * SSM_BC), _BF16), jax.ShapeDtypeStruct((t, LANES), _F32)],
        scratch_shapes=[pltpu.VMEM((SSM_XBC // D_MODEL, HALO, D_MODEL), _F32)],
        compiler_params=_cparams("arbitrary"), name="ssm_in_proj",
    )(x, w_main, w_dt, conv_w, conv_b.reshape(1, -1), dtb)


def _ssd_kernel(gz_ref, xs_ref, bc_ref, dt_ref, a_ref, dsk_ref, nw_ref, o_ref, state_ref, y_ref):
    c = pl.program_id(1)
    q = CHUNK

    @pl.when(c == 0)
    def _():
        state_ref[...] = jnp.zeros_like(state_ref)

    dt = dt_ref[...]
    da = dt * a_ref[...]
    r_io = lax.broadcasted_iota(jnp.int32, (q, q), 0)
    c_io = lax.broadcasted_iota(jnp.int32, (q, q), 1)
    causal = r_io >= c_io
    tri = causal.astype(_F32)
    a_cum = jnp.dot(tri, da, preferred_element_type=_F32, precision=lax.Precision.HIGHEST)
    a_cum_t = a_cum.T
    dt_t = dt.T
    a_last = a_cum[q - 1:q, :]
    w_state = jnp.exp(a_last - a_cum) * dt
    e_acum = jnp.exp(a_cum)
    chunk_decay = jnp.exp(a_last)
    lane = lax.broadcasted_iota(jnp.int32, (q, LANES), 1)
    low = lane < SSM_P
    lane1 = lax.broadcasted_iota(jnp.int32, (1, LANES), 1)
    low1 = lane1 < SSM_P
    tn = (((0,), (0,)), ((), ()))
    nt = (((1,), (1,)), ((), ()))

    for g in range(SSM_GROUPS):
        bg = bc_ref[:, g * SSM_N:(g + 1) * SSM_N]
        cg = bc_ref[:, SSM_BC + g * SSM_N:SSM_BC + (g + 1) * SSM_N]
        cbm = lax.dot_general(cg, bg, nt, preferred_element_type=_F32)
        st = state_ref[g]
        y_off = jnp.dot(cg, st.astype(_BF16), preferred_element_type=_F32)
        xw_parts, dec_parts = [], []
        for j in range(SSM_GHEADS // 2):
            h0 = g * SSM_GHEADS + 2 * j
            col = (g * SSM_GHEADS // 2 + j) * LANES
            xp = xs_ref[:, col:col + LANES]
            y_diag = jnp.zeros((q, LANES), _F32)
            for half, hh in enumerate((h0, h0 + 1)):
                diff = a_cum[:, hh:hh + 1] - a_cum_t[hh:hh + 1, :]
                lmat = jnp.exp(jnp.where(causal, diff, NEG_BIG))
                mm = (cbm * lmat * dt_t[hh:hh + 1, :]).astype(_BF16)
                keep = low if half == 0 else jnp.logical_not(low)
                xh = jnp.where(keep, xp, 0.0).astype(_BF16)
                y_diag = y_diag + jnp.dot(mm, xh, preferred_element_type=_F32)
            e_pair = jnp.where(low, e_acum[:, h0:h0 + 1], e_acum[:, h0 + 1:h0 + 2])
            w_pair = jnp.where(low, w_state[:, h0:h0 + 1], w_state[:, h0 + 1:h0 + 2])
            y_ref[:, col:col + LANES] = (y_diag + y_off[:, j * LANES:(j + 1) * LANES] * e_pair
                                         + xp * dsk_ref[:, col:col + LANES])
            xw_parts.append((xp * w_pair).astype(_BF16))
            dec_parts.append(jnp.where(low1, chunk_decay[:, h0:h0 + 1], chunk_decay[:, h0 + 1:h0 + 2]))
        xw = jnp.concatenate(xw_parts, axis=1)
        dec = jnp.concatenate(dec_parts, axis=1)
        state_ref[g] = st * dec + lax.dot_general(bg, xw, tn, preferred_element_type=_F32)

    yg = y_ref[...] * gz_ref[...]
    gw = SSM_INNER // SSM_GROUPS
    for g in range(SSM_GROUPS):
        v = yg[:, g * gw:(g + 1) * gw]
        ms = jnp.mean(v * v, axis=-1, keepdims=True)
        o_ref[:, g * gw:(g + 1) * gw] = (v * lax.rsqrt(ms + RMS_EPS)
                                         * nw_ref[:, g * gw:(g + 1) * gw]).astype(o_ref.dtype)


def ssd_mixer(gz, xs, bc, dt, a_log, d_skip, norm_w, bsz, seq):
    nc = seq // CHUNK
    a = jnp.pad(-jnp.exp(a_log.astype(_F32)), (0, LANES - SSM_HEADS)).reshape(1, LANES)
    dsk = jnp.repeat(d_skip.astype(_F32), SSM_P).reshape(1, SSM_INNER)
    cst = lambda shape: pl.BlockSpec(shape, lambda b, c: (0,) * len(shape))
    row = lambda w: pl.BlockSpec((CHUNK, w), lambda b, c: (b * nc + c, 0))
    return pl.pallas_call(
        _ssd_kernel,
        grid=(bsz, nc),
        in_specs=[row(SSM_INNER), row(SSM_INNER), row(2 * SSM_BC), row(LANES),
                  cst((1, LANES)), cst((1, SSM_INNER)), cst((1, SSM_INNER))],
        out_specs=row(SSM_INNER),
        out_shape=jax.ShapeDtypeStruct((bsz * seq, SSM_INNER), _BF16),
        scratch_shapes=[pltpu.VMEM((SSM_GROUPS, SSM_N, SSM_GHEADS * SSM_P), _F32),
                        pltpu.VMEM((CHUNK, SSM_INNER), _F32)],
        compiler_params=_cparams("parallel", "arbitrary"), name="ssd_mixer",
    )(gz, xs, bc, dt, a, dsk, norm_w.reshape(1, -1))


ROUTE_SUB = 256
ROUTE_TM_ATTN = 512
ROUTE_TM_SSM = 1024


def _layer_norm(y, g, b):
    mu = jnp.mean(y, axis=-1, keepdims=True)
    yc = y - mu
    var = jnp.mean(yc * yc, axis=-1, keepdims=True)
    return yc * lax.rsqrt(var + LN_EPS) * g + b


def _route(h1s, wrt_ref, br_ref, eidx_ref, gate_ref, tcnt_ref):
    sub = ROUTE_SUB
    n = len(h1s)
    nt = (((1,), (1,)), ((), ()))
    wrt = wrt_ref[...]
    vals = [lax.dot_general(wrt, h, nt, preferred_element_type=_F32, precision=lax.Precision.HIGHEST)
            + br_ref[...] for h in h1s]
    eio = lax.broadcasted_iota(jnp.int32, (N_EXPERTS, sub), 0)
    tops, ids, hots = ([[] for _ in range(n)] for _ in range(3))
    for _ in range(TOP_K):
        for a in range(n):
            mk = jnp.max(vals[a], axis=0, keepdims=True)
            ik = jnp.min(jnp.where(vals[a] == mk, eio, N_EXPERTS), axis=0, keepdims=True)
            hot = eio == ik
            vals[a] = jnp.where(hot, -jnp.inf, vals[a])
            tops[a].append(mk)
            ids[a].append(ik)
            hots[a].append(hot)
    col = lax.broadcasted_iota(jnp.int32, tcnt_ref.shape, 1)
    tcnt = tcnt_ref[...]
    for a in range(n):
        sl = slice(a * sub, (a + 1) * sub)
        exps = [jnp.exp(t - tops[a][0]) for t in tops[a]]
        den = exps[0] + exps[1] + exps[2] + exps[3]
        for k in range(TOP_K):
            eidx_ref[k:k + 1, sl] = ids[a][k]
            gate_ref[k:k + 1, sl] = exps[k] / den
        hot_all = hots[a][0] | hots[a][1] | hots[a][2] | hots[a][3]
        counts = jnp.sum(hot_all.astype(_F32), axis=1, keepdims=True)
        tcnt = jnp.where(col == pl.program_id(0) * n + a, counts, tcnt)
    tcnt_ref[...] = tcnt


def _norm_and_route(mixes, x_ref, g_ref, b_ref, wrt_ref, br_ref, h_ref, eidx_ref, gate_ref, tcnt_ref):
    sub = ROUTE_SUB

    @pl.when(pl.program_id(0) == 0)
    def _():
        tcnt_ref[...] = jnp.zeros_like(tcnt_ref)

    h1s = [_layer_norm(DEEPNORM_ALPHA * x_ref[a * sub:(a + 1) * sub, :] + mix, g_ref[...], b_ref[...])
           for a, mix in enumerate(mixes)]
    for a, h1 in enumerate(h1s):
        h_ref[a * sub:(a + 1) * sub, :] = h1
    _route(h1s, wrt_ref, br_ref, eidx_ref, gate_ref, tcnt_ref)


def _attn_out_kernel(o0_ref, o1_ref, o2_ref, l0_ref, l1_ref, l2_ref, x_ref, wo_ref, g_ref, b_ref,
                     wrt_ref, br_ref, h_ref, eidx_ref, gate_ref, tcnt_ref, ob_ref):
    sub = ROUTE_SUB
    l0, l1, l2 = l0_ref[...], l1_ref[...], l2_ref[...]
    m = jnp.maximum(jnp.maximum(l0, l1), l2)
    e0, e1, e2 = jnp.exp(l0 - m), jnp.exp(l1 - m), jnp.exp(l2 - m)
    inv = 1.0 / (e0 + e1 + e2)
    w0, w1, w2 = e0 * inv, e1 * inv, e2 * inv
    for h in range(HEADS):
        o = w0[:, h:h + 1] * o0_ref[h] + w1[:, h:h + 1] * o1_ref[h] + w2[:, h:h + 1] * o2_ref[h]
        ob_ref[:, h * HEAD_DIM:(h + 1) * HEAD_DIM] = o.astype(_BF16)
    mixes = [jnp.dot(ob_ref[a * sub:(a + 1) * sub, :], wo_ref[...], preferred_element_type=_F32)
             for a in range(x_ref.shape[0] // sub)]
    _norm_and_route(mixes, x_ref, g_ref, b_ref, wrt_ref, br_ref, h_ref, eidx_ref, gate_ref, tcnt_ref)


def _ssm_out_kernel(y_ref, x_ref, wo_ref, g_ref, b_ref, wrt_ref, br_ref, h_ref, eidx_ref, gate_ref, tcnt_ref):
    sub = ROUTE_SUB
    mixes = [jnp.dot(y_ref[a * sub:(a + 1) * sub, :], wo_ref[...], preferred_element_type=_F32)
             for a in range(x_ref.shape[0] // sub)]
    _norm_and_route(mixes, x_ref, g_ref, b_ref, wrt_ref, br_ref, h_ref, eidx_ref, gate_ref, tcnt_ref)


def _route_out_specs(t, tm):
    nsub = t // ROUTE_SUB
    specs = [pl.BlockSpec((tm, D_MODEL), lambda i: (i, 0)),
             pl.BlockSpec((TOP_K, tm), lambda i: (0, i)),
             pl.BlockSpec((TOP_K, tm), lambda i: (0, i)),
             pl.BlockSpec((N_EXPERTS, nsub), lambda i: (0, 0))]
    shapes = [jax.ShapeDtypeStruct((t, D_MODEL), _F32),
              jax.ShapeDtypeStruct((TOP_K, t), jnp.int32),
              jax.ShapeDtypeStruct((TOP_K, t), _F32),
              jax.ShapeDtypeStruct((N_EXPERTS, nsub), _F32)]
    return specs, shapes


def _const_spec(shape):
    return pl.BlockSpec(shape, lambda i: (0,) * len(shape))


def attn_out_norm_route(os_, lses, x, wo_bf16, ln_g, ln_b, w_router, b_router):
    t = x.shape[0]
    tm = ROUTE_TM_ATTN
    row = lambda w: pl.BlockSpec((tm, w), lambda i: (i, 0))
    out_specs, out_shapes = _route_out_specs(t, tm)
    return pl.pallas_call(
        _attn_out_kernel,
        grid=(t // tm,),
        in_specs=[pl.BlockSpec((HEADS, tm, HEAD_DIM), lambda i: (0, i, 0))] * 3 + [row(LANES)] * 3 + [row(D_MODEL),
                  _const_spec((D_MODEL, D_MODEL)), _const_spec((1, D_MODEL)), _const_spec((1, D_MODEL)),
                  _const_spec((N_EXPERTS, D_MODEL)), _const_spec((N_EXPERTS, 1))],
        out_specs=out_specs, out_shape=out_shapes,
        scratch_shapes=[pltpu.VMEM((tm, D_MODEL), _BF16)],
        compiler_params=_cparams("arbitrary"), name="attn_out_norm_route",
    )(*os_, *lses, x, wo_bf16, ln_g.reshape(1, -1), ln_b.reshape(1, -1),
      w_router.T, b_router.reshape(-1, 1))


def ssm_out_norm_route(yn, x, wo_bf16, ln_g, ln_b, w_router, b_router):
    t = x.shape[0]
    tm = ROUTE_TM_SSM
    row = lambda w: pl.BlockSpec((tm, w), lambda i: (i, 0))
    out_specs, out_shapes = _route_out_specs(t, tm)
    return pl.pallas_call(
        _ssm_out_kernel,
        grid=(t // tm,),
        in_specs=[row(SSM_INNER), row(D_MODEL),
                  _const_spec((SSM_INNER, D_MODEL)), _const_spec((1, D_MODEL)), _const_spec((1, D_MODEL)),
                  _const_spec((N_EXPERTS, D_MODEL)), _const_spec((N_EXPERTS, 1))],
        out_specs=out_specs, out_shape=out_shapes,
        compiler_params=_cparams("arbitrary"), name="ssm_out_norm_route",
    )(yn, x, wo_bf16, ln_g.reshape(1, -1), ln_b.reshape(1, -1), w_router.T, b_router.reshape(-1, 1))


MOE_BM = 256
MOE_SUB = ROUTE_SUB
SEG_ALIGN = SUBLANES
LOCAL_ROWS = MOE_SUB * TOP_K + N_EXPERTS * SUBLANES
SEG_SIZES = (256, 128, 64, 32, 16, 8)


def _local_positions(eidx):
    sub = eidx.shape[1]
    eio = lax.broadcasted_iota(jnp.int32, (N_EXPERTS, sub), 0)
    hots = [eio == eidx[k:k + 1, :] for k in range(TOP_K)]
    hot_all = (hots[0] | hots[1] | hots[2] | hots[3]).astype(_BF16)
    r_io = lax.broadcasted_iota(jnp.int32, (sub, sub), 0)
    c_io = lax.broadcasted_iota(jnp.int32, (sub, sub), 1)
    upper = (r_io < c_io).astype(_BF16)
    prefix = jnp.dot(hot_all, upper, preferred_element_type=_F32)
    counts = jnp.sum(hot_all.astype(_F32), axis=1, keepdims=True).astype(jnp.int32)
    padded = ((counts + (SEG_ALIGN - 1)) & (-SEG_ALIGN)).astype(_F32)
    e_r = lax.broadcasted_iota(jnp.int32, (N_EXPERTS, N_EXPERTS), 0)
    e_c = lax.broadcasted_iota(jnp.int32, (N_EXPERTS, N_EXPERTS), 1)
    below = (e_c < e_r).astype(_BF16)
    seg_start = jnp.dot(below, jnp.broadcast_to(padded, (N_EXPERTS, sub)).astype(_BF16),
                        preferred_element_type=_F32)
    base = seg_start + prefix
    return [jnp.sum(jnp.where(h, base, 0.0), axis=0, keepdims=True) for h in hots]


def _segment_copies(cnt_ref, lo_ref, go_ref, i, copy_fn):
    for e in range(N_EXPERTS):
        n = cnt_ref[i * N_EXPERTS + e]
        lo = lo_ref[i * N_EXPERTS + e]
        go = go_ref[i * N_EXPERTS + e]
        done = jnp.int32(0)
        for sz in SEG_SIZES:
            @pl.when((n & sz) != 0)
            def _(done=done, sz=sz):
                copy_fn(pl.multiple_of(lo + done, SEG_ALIGN), pl.multiple_of(go + done, SEG_ALIGN), sz)
            done = done + (n & sz)


def _dispatch_kernel(cnt_ref, lo_ref, go_ref, units_ref, tail_ref, eidx_ref, h_ref, xs_ref, buf_ref, sem):
    i = pl.program_id(0)
    lpos = _local_positions(eidx_ref[...])
    p_io = lax.broadcasted_iota(jnp.int32, (LOCAL_ROWS, MOE_SUB), 0).astype(_F32)
    sel = (p_io == lpos[0]) | (p_io == lpos[1]) | (p_io == lpos[2]) | (p_io == lpos[3])
    buf_ref[...] = jnp.dot(sel.astype(_BF16), h_ref[...].astype(_BF16), preferred_element_type=_F32)

    def copy(lo, go, sz):
        pltpu.make_async_copy(buf_ref.at[pl.ds(lo, sz)], xs_ref.at[pl.ds(go, sz)], sem).start()

    _segment_copies(cnt_ref, lo_ref, go_ref, i, copy)

    def wait_unit(j, c):
        pltpu.make_async_copy(buf_ref.at[pl.ds(0, SEG_ALIGN)], xs_ref.at[pl.ds(0, SEG_ALIGN)], sem).wait()
        return c

    lax.fori_loop(0, units_ref[i], wait_unit, 0)

    @pl.when(i == pl.num_programs(0) - 1)
    def _():
        buf_ref[0:MOE_BM, :] = jnp.zeros((MOE_BM, D_MODEL), _F32)
        start, n, n_blocks = tail_ref[0], tail_ref[1], tail_ref[2]
        done = jnp.int32(0)
        for sz in SEG_SIZES[1:]:
            @pl.when((n & sz) != 0)
            def _(done=done, sz=sz):
                pltpu.make_async_copy(buf_ref.at[pl.ds(0, sz)],
                                      xs_ref.at[pl.ds(pl.multiple_of(start + done, SEG_ALIGN), sz)], sem).start()
            done = done + (n & sz)

        def zero_block(j, c):
            row0 = pl.multiple_of(start + n + j * MOE_BM, MOE_BM)
            pltpu.make_async_copy(buf_ref.at[pl.ds(0, MOE_BM)], xs_ref.at[pl.ds(row0, MOE_BM)], sem).start()
            return c

        lax.fori_loop(0, n_blocks, zero_block, 0)
        lax.fori_loop(0, n // SEG_ALIGN + n_blocks * (MOE_BM // SEG_ALIGN), wait_unit, 0)


def _seg_tables(tile_counts):
    cnt = tile_counts.T.astype(jnp.int32)
    pc = (cnt + (SEG_ALIGN - 1)) & (-SEG_ALIGN)
    lo = jnp.cumsum(pc, axis=1) - pc
    tot = jnp.sum(pc, axis=0)
    offs = jnp.concatenate([jnp.zeros((1,), jnp.int32), jnp.cumsum(tot)])
    go = offs[None, :-1] + jnp.cumsum(pc, axis=0) - pc
    units = jnp.sum(pc, axis=1) // SEG_ALIGN
    return pc.reshape(-1), lo.reshape(-1), go.reshape(-1), units.astype(jnp.int32), offs.astype(jnp.int32)


def _max_rows(t):
    n = t * TOP_K + (t // MOE_SUB) * N_EXPERTS * (SEG_ALIGN - 1)
    return -(-n // MOE_BM) * MOE_BM


def moe_dispatch(h, eidx, tables):
    t = h.shape[0]
    pc, lo, go, units, offs = tables
    used = offs[-1]
    partial = (-used) % MOE_BM
    tail = jnp.stack([used, partial, (_max_rows(t) - used - partial) // MOE_BM]).astype(jnp.int32)
    grid_spec = pltpu.PrefetchScalarGridSpec(
        num_scalar_prefetch=5, grid=(t // MOE_SUB,),
        in_specs=[pl.BlockSpec((TOP_K, MOE_SUB), lambda i, *_: (0, i)),
                  pl.BlockSpec((MOE_SUB, D_MODEL), lambda i, *_: (i, 0))],
        out_specs=pl.BlockSpec(memory_space=pl.ANY),
        scratch_shapes=[pltpu.VMEM((LOCAL_ROWS, D_MODEL), _F32), pltpu.SemaphoreType.DMA(())])
    return pl.pallas_call(
        _dispatch_kernel, grid_spec=grid_spec,
        out_shape=jax.ShapeDtypeStruct((_max_rows(t), D_MODEL), _F32),
        compiler_params=_cparams("arbitrary"), name="moe_dispatch",
    )(pc, lo, go, units, tail, eidx, h)


def _combine_kernel(cnt_ref, lo_ref, go_ref, units_ref, eidx_ref, gate_ref, ys_ref, h_ref, g_ref, b_ref,
                    o_ref, buf_ref, sem):
    i = pl.program_id(0)
    buf_ref[MOE_SUB * TOP_K:, :] = jnp.zeros((LOCAL_ROWS - MOE_SUB * TOP_K, D_MODEL), _F32)

    def copy(lo, go, sz):
        pltpu.make_async_copy(ys_ref.at[pl.ds(go, sz)], buf_ref.at[pl.ds(lo, sz)], sem).start()

    _segment_copies(cnt_ref, lo_ref, go_ref, i, copy)

    lpos = _local_positions(eidx_ref[...])
    gates = gate_ref[...]
    p_io = lax.broadcasted_iota(jnp.int32, (LOCAL_ROWS, MOE_SUB), 0).astype(_F32)
    w = jnp.zeros((LOCAL_ROWS, MOE_SUB), _F32)
    for k in range(TOP_K):
        w = w + jnp.where(p_io == lpos[k], gates[k:k + 1, :], 0.0)
    w_hi = w.astype(_BF16)
    w_lo = (w - w_hi.astype(_F32)).astype(_BF16)

    def wait_unit(j, c):
        pltpu.make_async_copy(ys_ref.at[pl.ds(0, SEG_ALIGN)], buf_ref.at[pl.ds(0, SEG_ALIGN)], sem).wait()
        return c

    lax.fori_loop(0, units_ref[i], wait_unit, 0)
    yb = buf_ref[...].astype(_BF16)
    tn = (((0,), (0,)), ((), ()))
    ffn = (lax.dot_general(w_hi, yb, tn, preferred_element_type=_F32)
           + lax.dot_general(w_lo, yb, tn, preferred_element_type=_F32))
    o_ref[...] = _layer_norm(DEEPNORM_ALPHA * h_ref[...] + ffn, g_ref[...], b_ref[...])


def moe_combine(ys, eidx, gates, h, tables, ln_g, ln_b):
    t = h.shape[0]
    pc, lo, go, units, _ = tables
    row = pl.BlockSpec((MOE_SUB, D_MODEL), lambda i, *_: (i, 0))
    kt = pl.BlockSpec((TOP_K, MOE_SUB), lambda i, *_: (0, i))
    cst = lambda shape: pl.BlockSpec(shape, lambda i, *_: (0,) * len(shape))
    grid_spec = pltpu.PrefetchScalarGridSpec(
        num_scalar_prefetch=4, grid=(t // MOE_SUB,),
        in_specs=[kt, kt, pl.BlockSpec(memory_space=pl.ANY), row, cst((1, D_MODEL)), cst((1, D_MODEL))],
        out_specs=row,
        scratch_shapes=[pltpu.VMEM((LOCAL_ROWS, D_MODEL), _F32), pltpu.SemaphoreType.DMA(())])
    return pl.pallas_call(
        _combine_kernel, grid_spec=grid_spec,
        out_shape=jax.ShapeDtypeStruct((t, D_MODEL), _F32),
        compiler_params=_cparams("arbitrary"), name="moe_combine",
    )(pc, lo, go, units, eidx, gates, ys, h, ln_g.reshape(1, -1), ln_b.reshape(1, -1))


def _expert_kernel(blk_ref, e_ref, lo_ref, hi_ref, xs_ref, wu_ref, bu_ref, wd_ref, bd_ref,
                   ys_ref, wub_ref, wdb_ref):
    i = pl.program_id(0)
    e = e_ref[i]
    e_prev = e_ref[jnp.maximum(i - 1, 0)]

    @pl.when((i == 0) | (e != e_prev))
    def _():
        wub_ref[...] = wu_ref[...].astype(_BF16)
        wdb_ref[...] = wd_ref[...].astype(_BF16)

    lo = lo_ref[i]
    hi = hi_ref[i]

    @pl.when(hi < 0)
    def _():
        ys_ref[...] = jnp.zeros_like(ys_ref)

    @pl.when(hi > lo)
    def _():
        x = xs_ref[...].astype(_BF16)
        h = jnp.dot(x, wub_ref[...], preferred_element_type=_F32) + bu_ref[...]
        gate = jnp.minimum(h[:, :D_FF], SWIGLU_LIMIT)
        up = jnp.clip(h[:, D_FF:], -SWIGLU_LIMIT, SWIGLU_LIMIT)
        glu = gate * jax.nn.sigmoid(SWIGLU_ALPHA * gate)
        act = ((up + 1.0) * glu).astype(_BF16)
        y = jnp.dot(act, wdb_ref[...], preferred_element_type=_F32) + bd_ref[...]
        row = lax.broadcasted_iota(jnp.int32, y.shape, 0)
        y = jnp.where((row >= lo) & (row < hi), y, 0.0)

        @pl.when(lo == 0)
        def _():
            ys_ref[...] = y

        @pl.when(lo > 0)
        def _():
            ys_ref[...] += y


def _expert_schedule(offs, n_rows, bm):
    nblk = n_rows // bm
    n_items = nblk + N_EXPERTS - 1
    used = offs[-1]
    bnd = offs[1:-1]
    pos = jnp.arange(N_EXPERTS - 1, dtype=jnp.int32) + jnp.minimum(bnd // bm + 1, nblk)
    i = jnp.arange(n_items, dtype=jnp.int32)[:, None]
    at = pos[None, :] == i
    is_bnd = jnp.any(at, axis=1)
    bnd_val = jnp.sum(jnp.where(at, bnd[None, :], 0), axis=1)
    n_before = jnp.sum((pos[None, :] < i).astype(jnp.int32), axis=1)
    cuts = jnp.where(is_bnd, bnd_val, (i[:, 0] - n_before) * bm)
    ends = jnp.concatenate([cuts[1:], jnp.array([n_rows], jnp.int32)])
    last_blk = jnp.maximum(used - 1, 0) // bm
    past = jnp.logical_not(is_bnd) & (cuts // bm > last_blk)
    raw_blk = cuts // bm
    cuts = jnp.minimum(cuts, used)
    ends = jnp.minimum(ends, used)
    blk = jnp.minimum(cuts // bm, last_blk)
    e = jnp.minimum(jnp.sum((bnd[None, :] <= cuts[:, None]).astype(jnp.int32), axis=1), N_EXPERTS - 1)
    lo = jnp.where(past, 0, cuts - blk * bm)
    hi = jnp.where(past, -1, jnp.minimum(ends - blk * bm, bm))
    return lax.cummax(jnp.where(past, raw_blk, blk)), e, lo, hi


def moe_experts(xs, offs, layer, w_up, b_up, w_down, b_down):
    n_rows = xs.shape[0]
    bm = MOE_BM
    blk, e, lo, hi = _expert_schedule(offs, n_rows, bm)
    n_items = n_rows // bm + N_EXPERTS - 1
    wmap = lambda i, b, e, lo, hi: (layer, e[i], 0, 0)
    grid_spec = pltpu.PrefetchScalarGridSpec(
        num_scalar_prefetch=4, grid=(n_items,),
        in_specs=[pl.BlockSpec((bm, D_MODEL), lambda i, b, e, lo, hi: (b[i], 0)),
                  pl.BlockSpec((None, None, D_MODEL, 2 * D_FF), wmap),
                  pl.BlockSpec((None, None, 1, 2 * D_FF), wmap),
                  pl.BlockSpec((None, None, D_FF, D_MODEL), wmap),
                  pl.BlockSpec((None, None, 1, D_MODEL), wmap)],
        out_specs=pl.BlockSpec((bm, D_MODEL), lambda i, b, e, lo, hi: (b[i], 0)),
        scratch_shapes=[pltpu.VMEM((D_MODEL, 2 * D_FF), _BF16), pltpu.VMEM((D_FF, D_MODEL), _BF16)])
    nl = w_up.shape[0]
    return pl.pallas_call(
        _expert_kernel, grid_spec=grid_spec,
        out_shape=jax.ShapeDtypeStruct((n_rows, D_MODEL), _F32),
        compiler_params=_cparams("arbitrary"), name="moe_experts",
    )(blk, e, lo, hi, xs, w_up, b_up.reshape(nl, N_EXPERTS, 1, -1), w_down, b_down.reshape(nl, N_EXPERTS, 1, -1))


def moe_layer(h, eidx, gates, tile_counts, layer, w_up, b_up, w_down, b_down, ln_g, ln_b):
    tables = _seg_tables(tile_counts)
    xs = moe_dispatch(h, eidx, tables)
    ys = moe_experts(xs, tables[4], layer, w_up, b_up, w_down, b_down)
    return moe_combine(ys, eidx, gates, h, tables, ln_g, ln_b)


def kernel(x, positions, attn_w_in, attn_w_out, ssm_w_in, ssm_conv_w, ssm_conv_b, ssm_dt_bias,
           ssm_a_log, ssm_d, ssm_norm_w, ssm_w_out, moe_w_router, moe_b_router, moe_w_up,
           moe_b_up, moe_w_down, moe_b_down, ln_mix_g, ln_mix_b, ln_ffn_g, ln_ffn_b):
    bsz, seq, _ = x.shape
    xf = x.reshape(bsz * seq, D_MODEL)

    w_qkv = attn_w_in[0].astype(_BF16)
    outs = [dilated_group_attention(qkv_proj_group(xf, w_qkv, positions, g), g, bsz, seq)
            for g in range(N_GROUPS)]
    h, eidx, gates, tile_counts = attn_out_norm_route(
        [o for o, _ in outs], [l for _, l in outs], xf, attn_w_out[0].astype(_BF16),
        ln_mix_g[0], ln_mix_b[0], moe_w_router[0], moe_b_router[0])
    xf = moe_layer(h, eidx, gates, tile_counts, 0, moe_w_up, moe_b_up, moe_w_down, moe_b_down,
                   ln_ffn_g[0], ln_ffn_b[0])

    gz, xs, bc, dt = ssm_in_proj(xf, ssm_w_in[0], ssm_conv_w[0], ssm_conv_b[0], ssm_dt_bias[0], seq)
    yn = ssd_mixer(gz, xs, bc, dt, ssm_a_log[0], ssm_d[0], ssm_norm_w[0], bsz, seq)
    h, eidx, gates, tile_counts = ssm_out_norm_route(
        yn, xf, ssm_w_out[0].astype(_BF16), ln_mix_g[1], ln_mix_b[1], moe_w_router[1], moe_b_router[1])
    xf = moe_layer(h, eidx, gates, tile_counts, 1, moe_w_up, moe_b_up, moe_w_down, moe_b_down,
                   ln_ffn_g[1], ln_ffn_b[1])
    return xf.reshape(bsz, seq, D_MODEL)
```

```python
import functools
import math

import jax
import jax.numpy as jnp
from jax import lax
from jax.experimental import pallas as pl
from jax.experimental.pallas import tpu as pltpu

D_MODEL = 1024
DEPTH = 2
DIL = (1, 4, 16)
N_GROUPS = 3
SUB_WIN = 128
HEAD_DIM = 128
HEADS = 8
ROT_DIM = 32
ROT_HALF = 16
ROPE_THETA = 500000.0
ATTN_PROJ = N_GROUPS * 3 * D_MODEL
SSM_INNER = 2048
SSM_P = 64
SSM_HEADS = 32
SSM_GROUPS = 4
SSM_GHEADS = 8
SSM_N = 128
SSM_CONV = 4
CHUNK = 128
SSM_BC = SSM_GROUPS * SSM_N
SSM_XBC = SSM_INNER + 2 * SSM_BC
N_EXPERTS = 32
TOP_K = 4
D_FF = 1024
SWIGLU_LIMIT = 7.0
SWIGLU_ALPHA = 1.702
DEEPNORM_ALPHA = (2 * DEPTH) ** 0.25
LN_EPS = 1e-5
RMS_EPS = 1e-5

LANES = 128
SUBLANES = 8
VMEM_LIMIT_BYTES = 56 * 1024 * 1024

NEG_BIG = -1e30

_F32 = jnp.float32
_BF16 = jnp.bfloat16


def _cparams(*sem):
    return pltpu.CompilerParams(dimension_semantics=sem, vmem_limit_bytes=VMEM_LIMIT_BYTES)


def _rope_table_kernel(pos_ref, invf_ref, c_ref, sa_ref, sb_ref):
    pos = pos_ref[...].astype(_F32)
    ang = pos * invf_ref[...]
    lane = lax.broadcasted_iota(jnp.int32, ang.shape, 1)
    cos = jnp.cos(ang)
    sin = jnp.sin(ang)
    c_ref[...] = jnp.where(lane < ROT_DIM, cos, 1.0)
    sa_ref[...] = jnp.where((lane >= ROT_HALF) & (lane < ROT_DIM), sin, 0.0)
    sb_ref[...] = jnp.where(lane < ROT_HALF, -sin, 0.0)


def rope_tables(positions):
    t = positions.size
    tm = 1024
    pos = positions.reshape(t, 1)
    j = jnp.arange(LANES)
    invf = jnp.where(j < ROT_DIM,
                     ROPE_THETA ** (-(2.0 * (j % ROT_HALF)).astype(_F32) / ROT_DIM), 0.0)
    invf = invf.astype(_F32).reshape(1, LANES)
    out = jax.ShapeDtypeStruct((t, LANES), _F32)
    return pl.pallas_call(
        _rope_table_kernel,
        grid=(t // tm,),
        in_specs=[pl.BlockSpec((tm, 1), lambda i: (i, 0)),
                  pl.BlockSpec((1, LANES), lambda i: (0, 0))],
        out_specs=[pl.BlockSpec((tm, LANES), lambda i: (i, 0))] * 3,
        out_shape=[out, out, out],
        compiler_params=_cparams("parallel"),
        name="rope_tables",
    )(pos, invf)


QKV_TM = 512
ATTN_ROWS = 2048
HEADS_PER_STEP = 2


def _qkv_proj_kernel(x_ref, w_ref, c_ref, sa_ref, sb_ref, o_ref, *, d, ns, tt):
    tm = x_ref.shape[0]
    xb = x_ref[...].astype(_BF16)
    if d > 1:
        dst = lax.broadcasted_iota(jnp.int32, (tm, tm), 0)
        src = lax.broadcasted_iota(jnp.int32, (tm, tm), 1)
        lt, ld = tt.bit_length() - 1, d.bit_length() - 1
        t_ = dst & (tt - 1)
        r_ = lax.shift_right_logical(dst, lt) & (d - 1)
        s_ = lax.shift_right_logical(dst, lt + ld)
        perm = (src == ((s_ * tt + t_) * d + r_)).astype(_BF16)
        xb = jnp.dot(perm, xb, preferred_element_type=_F32).astype(_BF16)
    c = c_ref[...]
    sa = sa_ref[...]
    sb = sb_ref[...]
    for comp in range(3):
        acc = jnp.dot(xb, w_ref[:, comp * D_MODEL:(comp + 1) * D_MODEL], preferred_element_type=_F32)
        for h in range(HEADS):
            sl = slice(h * HEAD_DIM, (h + 1) * HEAD_DIM)
            t = acc[:, sl]
            if comp < 2:
                t = t * c + pltpu.roll(t, ROT_HALF, 1) * sa + pltpu.roll(t, HEAD_DIM - ROT_HALF, 1) * sb
            o_ref[:, :, :, comp * D_MODEL + h * HEAD_DIM:comp * D_MODEL + (h + 1) * HEAD_DIM] = (
                t.reshape(ns, d, tt, HEAD_DIM).astype(o_ref.dtype))


def qkv_proj_group(x, w_bf16, positions, g):
    t, k = x.shape
    d = DIL[g]
    span = SUB_WIN * d
    tm = QKV_TM
    ns = max(1, tm // span)
    tps = max(1, span // tm)
    tt = tm // (ns * d)
    rope = rope_tables(positions.reshape(-1, ns, tt, d).swapaxes(2, 3).reshape(-1))
    tab_spec = pl.BlockSpec((tm, LANES), lambda i: (i, 0))
    return pl.pallas_call(
        functools.partial(_qkv_proj_kernel, d=d, ns=ns, tt=tt),
        grid=(t // tm,),
        in_specs=[pl.BlockSpec((tm, k), lambda i: (i, 0)),
                  pl.BlockSpec((k, 3 * D_MODEL), lambda i: (0, g)),
                  tab_spec, tab_spec, tab_spec],
        out_specs=pl.BlockSpec((ns, d, tt, 3 * D_MODEL), lambda i: (i // tps, 0, i % tps, 0)),
        out_shape=jax.ShapeDtypeStruct((t // span, d, SUB_WIN, 3 * D_MODEL), _BF16),
        compiler_params=_cparams("parallel"), name=f"qkv_proj_g{g}",
    )(x, w_bf16, *rope)


def _dil_attn_kernel(q_ref, k_ref, v_ref, kp_ref, vp_ref, o_ref, lse_ref, *, d, ns):
    j = pl.program_id(1)
    hp = pl.program_id(2)
    nb = ns * d
    row = lax.broadcasted_iota(jnp.int32, (1, SUB_WIN, SUB_WIN), 1)
    col = lax.broadcasted_iota(jnp.int32, (1, SUB_WIN, SUB_WIN), 2)
    cur_ok = col <= row
    blk = lax.broadcasted_iota(jnp.int32, (nb, 1, 1), 0)
    prev_ok = (col >= row) & ((blk >= d) | (j > 0))
    lane = lax.broadcasted_iota(jnp.int32, (SUB_WIN, LANES), 1)
    scale = 1.0 / math.sqrt(HEAD_DIM)

    def block_rows(b):
        s, r = divmod(b, d)
        start = s * SUB_WIN * d + r
        return pl.ds(start, SUB_WIN, stride=d) if d > 1 else pl.ds(start, SUB_WIN)

    @pl.when(hp == 0)
    def _():
        lse_ref[...] = jnp.zeros_like(lse_ref)

    lses = []
    for hh in range(HEADS_PER_STEP):
        sl = slice(hh * HEAD_DIM, (hh + 1) * HEAD_DIM)
        flat = lambda a: a.reshape(nb, SUB_WIN, HEAD_DIM)
        q, kc, vc = flat(q_ref[:, :, :, sl]), flat(k_ref[:, :, :, sl]), flat(v_ref[:, :, :, sl])
        if ns > 1:
            kp = flat(jnp.concatenate([kp_ref[:, :, :, sl], k_ref[:ns - 1, :, :, sl]], axis=0))
            vp = flat(jnp.concatenate([vp_ref[:, :, :, sl], v_ref[:ns - 1, :, :, sl]], axis=0))
        else:
            kp, vp = flat(kp_ref[:, :, :, sl]), flat(vp_ref[:, :, :, sl])
        sc = jnp.einsum('bqd,bkd->bqk', q, kc, preferred_element_type=_F32) * scale
        sp = jnp.einsum('bqd,bkd->bqk', q, kp, preferred_element_type=_F32) * scale
        sc = jnp.where(cur_ok, sc, NEG_BIG)
        sp = jnp.where(prev_ok, sp, NEG_BIG)
        m = jnp.maximum(jnp.max(sc, axis=2, keepdims=True), jnp.max(sp, axis=2, keepdims=True))
        pc = jnp.exp(sc - m)
        pp = jnp.exp(sp - m)
        l = jnp.sum(pc, axis=2, keepdims=True) + jnp.sum(pp, axis=2, keepdims=True)
        acc = jnp.einsum('bqk,bkd->bqd', pc.astype(_BF16), vc, preferred_element_type=_F32)
        acc += jnp.einsum('bqk,bkd->bqd', pp.astype(_BF16), vp, preferred_element_type=_F32)
        o = acc / l
        for b in range(nb):
            o_ref[hh, block_rows(b), :] = o[b]
        lses.append(m + jnp.log(l))
    for b in range(nb):
        cur = lse_ref[block_rows(b), :]
        for hh in range(HEADS_PER_STEP):
            cur = jnp.where(lane == hp * HEADS_PER_STEP + hh, lses[hh][b], cur)
        lse_ref[block_rows(b), :] = cur


def dilated_group_attention(qkv, g, bsz, seq):
    d = DIL[g]
    ns = ATTN_ROWS // (SUB_WIN * d)
    nsteps = seq // ATTN_ROWS
    hw = HEADS_PER_STEP * HEAD_DIM
    ncol = D_MODEL // hw
    cur = lambda comp: pl.BlockSpec((ns, d, SUB_WIN, hw), lambda b, j, hp: (b * nsteps + j, 0, 0, comp * ncol + hp))
    prev = lambda comp: pl.BlockSpec(
        (1, d, SUB_WIN, hw), lambda b, j, hp: (jnp.maximum((b * nsteps + j) * ns - 1, 0), 0, 0, comp * ncol + hp))
    return pl.pallas_call(
        functools.partial(_dil_attn_kernel, d=d, ns=ns),
        grid=(bsz, nsteps, HEADS // HEADS_PER_STEP),
        in_specs=[cur(0), cur(1), cur(2), prev(1), prev(2)],
        out_specs=[pl.BlockSpec((HEADS_PER_STEP, ATTN_ROWS, HEAD_DIM), lambda b, j, hp: (hp, b * nsteps + j, 0)),
                   pl.BlockSpec((ATTN_ROWS, LANES), lambda b, j, hp: (b * nsteps + j, 0))],
        out_shape=[jax.ShapeDtypeStruct((HEADS, bsz * seq, HEAD_DIM), _F32),
                   jax.ShapeDtypeStruct((bsz * seq, LANES), _F32)],
        compiler_params=_cparams("parallel", "parallel", "arbitrary"),
        name=f"dilated_attn_g{g}",
    )(qkv, qkv, qkv, qkv, qkv)


SSM_IN_TM = 512
HALO = SUBLANES
ZX_COLS = SSM_INNER + SSM_XBC


def _softplus(x):
    return jnp.maximum(x, 0.0) + jnp.log1p(jnp.exp(-jnp.abs(x)))


def _ssm_in_kernel(x_ref, w_ref, wdt_ref, cw_ref, cb_ref, dtb_ref, gz_ref, xs_ref, bc_ref, dt_ref,
                   carry_ref, *, tiles_per_seq):
    tm = x_ref.shape[0]
    cw = D_MODEL
    xb = x_ref[...].astype(_BF16)

    @pl.when(pl.program_id(0) % tiles_per_seq == 0)
    def _():
        carry_ref[...] = jnp.zeros_like(carry_ref)

    for i in range(SSM_INNER // cw):
        z = jnp.dot(xb, w_ref[:, i * cw:(i + 1) * cw], preferred_element_type=_F32)
        gz_ref[:, i * cw:(i + 1) * cw] = z * jax.nn.sigmoid(z)
    for c in range(SSM_XBC // cw):
        cols = slice(c * cw, (c + 1) * cw)
        acc = jnp.dot(xb, w_ref[:, SSM_INNER + c * cw:SSM_INNER + (c + 1) * cw], preferred_element_type=_F32)
        ext = jnp.concatenate([carry_ref[c], acc], axis=0)
        carry_ref[c] = acc[tm - HALO:, :]
        conv = cb_ref[:, cols] + cw_ref[SSM_CONV - 1:SSM_CONV, cols] * acc
        for k in range(1, SSM_CONV):
            conv = conv + cw_ref[SSM_CONV - 1 - k:SSM_CONV - k, cols] * pltpu.roll(ext, k, 0)[HALO:, :]
        act = conv * jax.nn.sigmoid(conv)
        if (c + 1) * cw <= SSM_INNER:
            xs_ref[:, cols] = act
        else:
            bc_ref[...] = act.astype(_BF16)
    dt_raw = jnp.dot(xb, wdt_ref[...], preferred_element_type=_F32)
    dt_ref[...] = _softplus(dt_raw + dtb_ref[...])


def ssm_in_proj(x, w_in, conv_w, conv_b, dt_bias, seq):
    t, k = x.shape
    tm = SSM_IN_TM
    assert 2 * SSM_BC == D_MODEL
    pad = LANES - SSM_HEADS
    w_main = w_in[:, :ZX_COLS].astype(_BF16)
    w_dt = jnp.pad(w_in[:, ZX_COLS:], ((0, 0), (0, pad))).astype(_BF16)
    dtb = jnp.pad(dt_bias, (0, pad)).reshape(1, LANES)
    cst = lambda shape: pl.BlockSpec(shape, lambda i: (0,) * len(shape))
    once = lambda shape: pl.BlockSpec(shape, lambda i: (0,) * len(shape), pipeline_mode=pl.Buffered(1))
    row = lambda w: pl.BlockSpec((tm, w), lambda i: (i, 0))
    return pl.pallas_call(
        functools.partial(_ssm_in_kernel, tiles_per_seq=seq // tm),
        grid=(t // tm,),
        in_specs=[row(k), once((k, ZX_COLS)), once((k, LANES)),
                  cst((SSM_CONV, SSM_XBC)), cst((1, SSM_XBC)), cst((1, LANES))],
        out_specs=[row(SSM_INNER), row(SSM_INNER), row(2 * SSM_BC), row(LANES)],
        out_shape=[jax.ShapeDtypeStruct((t, SSM_INNER), _F32), jax.ShapeDtypeStruct((t, SSM_INNER), _F32),
                   jax.ShapeDtypeStruct((t, 2 * SSM_BC), _BF16), jax.ShapeDtypeStruct((t, LANES), _F32)],
        scratch_shapes=[pltpu.VMEM((SSM_XBC // D_MODEL, HALO, D_MODEL), _F32)],
        compiler_params=_cparams("arbitrary"), name="ssm_in_proj",
    )(x, w_main, w_dt, conv_w, conv_b.reshape(1, -1), dtb)


def _ssd_kernel(gz_ref, xs_ref, bc_ref, dt_ref, a_ref, dsk_ref, nw_ref, o_ref, state_ref, y_ref):
    c = pl.program_id(1)
    q = CHUNK

    @pl.when(c == 0)
    def _():
        state_ref[...] = jnp.zeros_like(state_ref)

    dt = dt_ref[...]
    da = dt * a_ref[...]
    r_io = lax.broadcasted_iota(jnp.int32, (q, q), 0)
    c_io = lax.broadcasted_iota(jnp.int32, (q, q), 1)
    causal = r_io >= c_io
    tri = causal.astype(_F32)
    a_cum = jnp.dot(tri, da, preferred_element_type=_F32, precision=lax.Precision.HIGHEST)
    a_cum_t = a_cum.T
    dt_t = dt.T
    a_last = a_cum[q - 1:q, :]
    w_state = jnp.exp(a_last - a_cum) * dt
    e_acum = jnp.exp(a_cum)
    chunk_decay = jnp.exp(a_last)
    lane = lax.broadcasted_iota(jnp.int32, (q, LANES), 1)
    low = lane < SSM_P
    lane1 = lax.broadcasted_iota(jnp.int32, (1, LANES), 1)
    low1 = lane1 < SSM_P
    tn = (((0,), (0,)), ((), ()))
    nt = (((1,), (1,)), ((), ()))

    for g in range(SSM_GROUPS):
        bg = bc_ref[:, g * SSM_N:(g + 1) * SSM_N]
        cg = bc_ref[:, SSM_BC + g * SSM_N:SSM_BC + (g + 1) * SSM_N]
        cbm = lax.dot_general(cg, bg, nt, preferred_element_type=_F32)
        st = state_ref[g]
        y_off = jnp.dot(cg, st.astype(_BF16), preferred_element_type=_F32)
        xw_parts, dec_parts = [], []
        for j in range(SSM_GHEADS // 2):
            h0 = g * SSM_GHEADS + 2 * j
            col = (g * SSM_GHEADS // 2 + j) * LANES
            xp = xs_ref[:, col:col + LANES]
            y_diag = jnp.zeros((q, LANES), _F32)
            for half, hh in enumerate((h0, h0 + 1)):
                diff = a_cum[:, hh:hh + 1] - a_cum_t[hh:hh + 1, :]
                lmat = jnp.exp(jnp.where(causal, diff, NEG_BIG))
                mm = (cbm * lmat * dt_t[hh:hh + 1, :]).astype(_BF16)
                keep = low if half == 0 else jnp.logical_not(low)
                xh = jnp.where(keep, xp, 0.0).astype(_BF16)
                y_diag = y_diag + jnp.dot(mm, xh, preferred_element_type=_F32)
            e_pair = jnp.where(low, e_acum[:, h0:h0 + 1], e_acum[:, h0 + 1:h0 + 2])
            w_pair = jnp.where(low, w_state[:, h0:h0 + 1], w_state[:, h0 + 1:h0 + 2])
            y_ref[:, col:col + LANES] = (y_diag + y_off[:, j * LANES:(j + 1) * LANES] * e_pair
                                         + xp * dsk_ref[:, col:col + LANES])
            xw_parts.append((xp * w_pair).astype(_BF16))
            dec_parts.append(jnp.where(low1, chunk_decay[:, h0:h0 + 1], chunk_decay[:, h0 + 1:h0 + 2]))
        xw = jnp.concatenate(xw_parts, axis=1)
        dec = jnp.concatenate(dec_parts, axis=1)
        state_ref[g] = st * dec + lax.dot_general(bg, xw, tn, preferred_element_type=_F32)

    yg = y_ref[...] * gz_ref[...]
    gw = SSM_INNER // SSM_GROUPS
    for g in range(SSM_GROUPS):
        v = yg[:, g * gw:(g + 1) * gw]
        ms = jnp.mean(v * v, axis=-1, keepdims=True)
        o_ref[:, g * gw:(g + 1) * gw] = (v * lax.rsqrt(ms + RMS_EPS)
                                         * nw_ref[:, g * gw:(g + 1) * gw]).astype(o_ref.dtype)


def ssd_mixer(gz, xs, bc, dt, a_log, d_skip, norm_w, bsz, seq):
    nc = seq // CHUNK
    a = jnp.pad(-jnp.exp(a_log.astype(_F32)), (0, LANES - SSM_HEADS)).reshape(1, LANES)
    dsk = jnp.repeat(d_skip.astype(_F32), SSM_P).reshape(1, SSM_INNER)
    cst = lambda shape: pl.BlockSpec(shape, lambda b, c: (0,) * len(shape))
    row = lambda w: pl.BlockSpec((CHUNK, w), lambda b, c: (b * nc + c, 0))
    return pl.pallas_call(
        _ssd_kernel,
        grid=(bsz, nc),
        in_specs=[row(SSM_INNER), row(SSM_INNER), row(2 * SSM_BC), row(LANES),
                  cst((1, LANES)), cst((1, SSM_INNER)), cst((1, SSM_INNER))],
        out_specs=row(SSM_INNER),
        out_shape=jax.ShapeDtypeStruct((bsz * seq, SSM_INNER), _BF16),
        scratch_shapes=[pltpu.VMEM((SSM_GROUPS, SSM_N, SSM_GHEADS * SSM_P), _F32),
                        pltpu.VMEM((CHUNK, SSM_INNER), _F32)],
        compiler_params=_cparams("parallel", "arbitrary"), name="ssd_mixer",
    )(gz, xs, bc, dt, a, dsk, norm_w.reshape(1, -1))


ROUTE_SUB = 256
ROUTE_TM_ATTN = 512
ROUTE_TM_SSM = 1024


def _layer_norm(y, g, b):
    mu = jnp.mean(y, axis=-1, keepdims=True)
    yc = y - mu
    var = jnp.mean(yc * yc, axis=-1, keepdims=True)
    return yc * lax.rsqrt(var + LN_EPS) * g + b


def _route(h1s, wrt_ref, br_ref, eidx_ref, gate_ref, tcnt_ref):
    sub = ROUTE_SUB
    n = len(h1s)
    nt = (((1,), (1,)), ((), ()))
    wrt = wrt_ref[...]
    vals = [lax.dot_general(wrt, h, nt, preferred_element_type=_F32, precision=lax.Precision.HIGHEST)
            + br_ref[...] for h in h1s]
    eio = lax.broadcasted_iota(jnp.int32, (N_EXPERTS, sub), 0)
    tops, ids, hots = ([[] for _ in range(n)] for _ in range(3))
    for _ in range(TOP_K):
        for a in range(n):
            mk = jnp.max(vals[a], axis=0, keepdims=True)
            ik = jnp.min(jnp.where(vals[a] == mk, eio, N_EXPERTS), axis=0, keepdims=True)
            hot = eio == ik
            vals[a] = jnp.where(hot, -jnp.inf, vals[a])
            tops[a].append(mk)
            ids[a].append(ik)
            hots[a].append(hot)
    col = lax.broadcasted_iota(jnp.int32, tcnt_ref.shape, 1)
    tcnt = tcnt_ref[...]
    for a in range(n):
        sl = slice(a * sub, (a + 1) * sub)
        exps = [jnp.exp(t - tops[a][0]) for t in tops[a]]
        den = exps[0] + exps[1] + exps[2] + exps[3]
        for k in range(TOP_K):
            eidx_ref[k:k + 1, sl] = ids[a][k]
            gate_ref[k:k + 1, sl] = exps[k] / den
        hot_all = hots[a][0] | hots[a][1] | hots[a][2] | hots[a][3]
        counts = jnp.sum(hot_all.astype(_F32), axis=1, keepdims=True)
        tcnt = jnp.where(col == pl.program_id(0) * n + a, counts, tcnt)
    tcnt_ref[...] = tcnt


def _norm_and_route(mixes, x_ref, g_ref, b_ref, wrt_ref, br_ref, h_ref, eidx_ref, gate_ref, tcnt_ref):
    sub = ROUTE_SUB

    @pl.when(pl.program_id(0) == 0)
    def _():
        tcnt_ref[...] = jnp.zeros_like(tcnt_ref)

    h1s = [_layer_norm(DEEPNORM_ALPHA * x_ref[a * sub:(a + 1) * sub, :] + mix, g_ref[...], b_ref[...])
           for a, mix in enumerate(mixes)]
    for a, h1 in enumerate(h1s):
        h_ref[a * sub:(a + 1) * sub, :] = h1
    _route(h1s, wrt_ref, br_ref, eidx_ref, gate_ref, tcnt_ref)


def _attn_out_kernel(o0_ref, o1_ref, o2_ref, l0_ref, l1_ref, l2_ref, x_ref, wo_ref, g_ref, b_ref,
                     wrt_ref, br_ref, h_ref, eidx_ref, gate_ref, tcnt_ref, ob_ref):
    sub = ROUTE_SUB
    l0, l1, l2 = l0_ref[...], l1_ref[...], l2_ref[...]
    m = jnp.maximum(jnp.maximum(l0, l1), l2)
    e0, e1, e2 = jnp.exp(l0 - m), jnp.exp(l1 - m), jnp.exp(l2 - m)
    inv = 1.0 / (e0 + e1 + e2)
    w0, w1, w2 = e0 * inv, e1 * inv, e2 * inv
    for h in range(HEADS):
        o = w0[:, h:h + 1] * o0_ref[h] + w1[:, h:h + 1] * o1_ref[h] + w2[:, h:h + 1] * o2_ref[h]
        ob_ref[:, h * HEAD_DIM:(h + 1) * HEAD_DIM] = o.astype(_BF16)
    mixes = [jnp.dot(ob_ref[a * sub:(a + 1) * sub, :], wo_ref[...], preferred_element_type=_F32)
             for a in range(x_ref.shape[0] // sub)]
    _norm_and_route(mixes, x_ref, g_ref, b_ref, wrt_ref, br_ref, h_ref, eidx_ref, gate_ref, tcnt_ref)


def _ssm_out_kernel(y_ref, x_ref, wo_ref, g_ref, b_ref, wrt_ref, br_ref, h_ref, eidx_ref, gate_ref, tcnt_ref):
    sub = ROUTE_SUB
    mixes = [jnp.dot(y_ref[a * sub:(a + 1) * sub, :], wo_ref[...], preferred_element_type=_F32)
             for a in range(x_ref.shape[0] // sub)]
    _norm_and_route(mixes, x_ref, g_ref, b_ref, wrt_ref, br_ref, h_ref, eidx_ref, gate_ref, tcnt_ref)


def _route_out_specs(t, tm):
    nsub = t // ROUTE_SUB
    specs = [pl.BlockSpec((tm, D_MODEL), lambda i: (i, 0)),
             pl.BlockSpec((TOP_K, tm), lambda i: (0, i)),
             pl.BlockSpec((TOP_K, tm), lambda i: (0, i)),
             pl.BlockSpec((N_EXPERTS, nsub), lambda i: (0, 0))]
    shapes = [jax.ShapeDtypeStruct((t, D_MODEL), _F32),
              jax.ShapeDtypeStruct((TOP_K, t), jnp.int32),
              jax.ShapeDtypeStruct((TOP_K, t), _F32),
              jax.ShapeDtypeStruct((N_EXPERTS, nsub), _F32)]
    return specs, shapes


def _const_spec(shape):
    return pl.BlockSpec(shape, lambda i: (0,) * len(shape))


def attn_out_norm_route(os_, lses, x, wo_bf16, ln_g, ln_b, w_router, b_router):
    t = x.shape[0]
    tm = ROUTE_TM_ATTN
    row = lambda w: pl.BlockSpec((tm, w), lambda i: (i, 0))
    out_specs, out_shapes = _route_out_specs(t, tm)
    return pl.pallas_call(
        _attn_out_kernel,
        grid=(t // tm,),
        in_specs=[pl.BlockSpec((HEADS, tm, HEAD_DIM), lambda i: (0, i, 0))] * 3 + [row(LANES)] * 3 + [row(D_MODEL),
                  _const_spec((D_MODEL, D_MODEL)), _const_spec((1, D_MODEL)), _const_spec((1, D_MODEL)),
                  _const_spec((N_EXPERTS, D_MODEL)), _const_spec((N_EXPERTS, 1))],
        out_specs=out_specs, out_shape=out_shapes,
        scratch_shapes=[pltpu.VMEM((tm, D_MODEL), _BF16)],
        compiler_params=_cparams("arbitrary"), name="attn_out_norm_route",
    )(*os_, *lses, x, wo_bf16, ln_g.reshape(1, -1), ln_b.reshape(1, -1),
      w_router.T, b_router.reshape(-1, 1))


def ssm_out_norm_route(yn, x, wo_bf16, ln_g, ln_b, w_router, b_router):
    t = x.shape[0]
    tm = ROUTE_TM_SSM
    row = lambda w: pl.BlockSpec((tm, w), lambda i: (i, 0))
    out_specs, out_shapes = _route_out_specs(t, tm)
    return pl.pallas_call(
        _ssm_out_kernel,
        grid=(t // tm,),
        in_specs=[row(SSM_INNER), row(D_MODEL),
                  _const_spec((SSM_INNER, D_MODEL)), _const_spec((1, D_MODEL)), _const_spec((1, D_MODEL)),
                  _const_spec((N_EXPERTS, D_MODEL)), _const_spec((N_EXPERTS, 1))],
        out_specs=out_specs, out_shape=out_shapes,
        compiler_params=_cparams("arbitrary"), name="ssm_out_norm_route",
    )(yn, x, wo_bf16, ln_g.reshape(1, -1), ln_b.reshape(1, -1), w_router.T, b_router.reshape(-1, 1))


MOE_BM = 256
MOE_SUB = ROUTE_SUB
SEG_ALIGN = SUBLANES
LOCAL_ROWS = MOE_SUB * TOP_K + N_EXPERTS * SUBLANES
SEG_SIZES = (256, 128, 64, 32, 16, 8)


def _local_positions(eidx):
    sub = eidx.shape[1]
    eio = lax.broadcasted_iota(jnp.int32, (N_EXPERTS, sub), 0)
    hots = [eio == eidx[k:k + 1, :] for k in range(TOP_K)]
    hot_all = (hots[0] | hots[1] | hots[2] | hots[3]).astype(_BF16)
    r_io = lax.broadcasted_iota(jnp.int32, (sub, sub), 0)
    c_io = lax.broadcasted_iota(jnp.int32, (sub, sub), 1)
    upper = (r_io < c_io).astype(_BF16)
    prefix = jnp.dot(hot_all, upper, preferred_element_type=_F32)
    counts = jnp.sum(hot_all.astype(_F32), axis=1, keepdims=True).astype(jnp.int32)
    padded = ((counts + (SEG_ALIGN - 1)) & (-SEG_ALIGN)).astype(_F32)
    e_r = lax.broadcasted_iota(jnp.int32, (N_EXPERTS, N_EXPERTS), 0)
    e_c = lax.broadcasted_iota(jnp.int32, (N_EXPERTS, N_EXPERTS), 1)
    below = (e_c < e_r).astype(_BF16)
    seg_start = jnp.dot(below, jnp.broadcast_to(padded, (N_EXPERTS, sub)).astype(_BF16),
                        preferred_element_type=_F32)
    base = seg_start + prefix
    return [jnp.sum(jnp.where(h, base, 0.0), axis=0, keepdims=True) for h in hots]


def _segment_copies(cnt_ref, lo_ref, go_ref, i, copy_fn):
    for e in range(N_EXPERTS):
        n = cnt_ref[i * N_EXPERTS + e]
        lo = lo_ref[i * N_EXPERTS + e]
        go = go_ref[i * N_EXPERTS + e]
        done = jnp.int32(0)
        for sz in SEG_SIZES:
            @pl.when((n & sz) != 0)
            def _(done=done, sz=sz):
                copy_fn(pl.multiple_of(lo + done, SEG_ALIGN), pl.multiple_of(go + done, SEG_ALIGN), sz)
            done = done + (n & sz)


def _dispatch_kernel(cnt_ref, lo_ref, go_ref, units_ref, tail_ref, eidx_ref, h_ref, xs_ref, buf_ref, sem):
    i = pl.program_id(0)
    slot = i % 2
    mine, other = buf_ref.at[slot], buf_ref.at[1 - slot]
    lpos = _local_positions(eidx_ref[...])
    p_io = lax.broadcasted_iota(jnp.int32, (LOCAL_ROWS, MOE_SUB), 0).astype(_F32)
    sel = (p_io == lpos[0]) | (p_io == lpos[1]) | (p_io == lpos[2]) | (p_io == lpos[3])
    mine[...] = jnp.dot(sel.astype(_BF16), h_ref[...].astype(_BF16), preferred_element_type=_F32)

    def copy(lo, go, sz):
        pltpu.make_async_copy(mine.at[pl.ds(lo, sz)], xs_ref.at[pl.ds(go, sz)], sem.at[slot]).start()

    _segment_copies(cnt_ref, lo_ref, go_ref, i, copy)

    def wait_units(buf, s, n_units):
        def body(j, c):
            pltpu.make_async_copy(buf.at[pl.ds(0, SEG_ALIGN)], xs_ref.at[pl.ds(0, SEG_ALIGN)], sem.at[s]).wait()
            return c
        lax.fori_loop(0, n_units, body, 0)

    @pl.when(i > 0)
    def _():
        wait_units(other, 1 - slot, units_ref[jnp.maximum(i - 1, 0)])

    @pl.when(i == pl.num_programs(0) - 1)
    def _():
        wait_units(mine, slot, units_ref[i])
        mine[0:MOE_BM, :] = jnp.zeros((MOE_BM, D_MODEL), _F32)
        start, n, n_blocks = tail_ref[0], tail_ref[1], tail_ref[2]
        done = jnp.int32(0)
        for sz in SEG_SIZES[1:]:
            @pl.when((n & sz) != 0)
            def _(done=done, sz=sz):
                pltpu.make_async_copy(mine.at[pl.ds(0, sz)],
                                      xs_ref.at[pl.ds(pl.multiple_of(start + done, SEG_ALIGN), sz)],
                                      sem.at[slot]).start()
            done = done + (n & sz)

        def zero_block(j, c):
            row0 = pl.multiple_of(start + n + j * MOE_BM, MOE_BM)
            pltpu.make_async_copy(mine.at[pl.ds(0, MOE_BM)], xs_ref.at[pl.ds(row0, MOE_BM)], sem.at[slot]).start()
            return c

        lax.fori_loop(0, n_blocks, zero_block, 0)
        wait_units(mine, slot, n // SEG_ALIGN + n_blocks * (MOE_BM // SEG_ALIGN))


def _seg_tables(tile_counts):
    cnt = tile_counts.T.astype(jnp.int32)
    pc = (cnt + (SEG_ALIGN - 1)) & (-SEG_ALIGN)
    lo = jnp.cumsum(pc, axis=1) - pc
    tot = jnp.sum(pc, axis=0)
    offs = jnp.concatenate([jnp.zeros((1,), jnp.int32), jnp.cumsum(tot)])
    go = offs[None, :-1] + jnp.cumsum(pc, axis=0) - pc
    units = jnp.sum(pc, axis=1) // SEG_ALIGN
    return pc.reshape(-1), lo.reshape(-1), go.reshape(-1), units.astype(jnp.int32), offs.astype(jnp.int32)


def _max_rows(t):
    n = t * TOP_K + (t // MOE_SUB) * N_EXPERTS * (SEG_ALIGN - 1)
    return -(-n // MOE_BM) * MOE_BM


def moe_dispatch(h, eidx, tables):
    t = h.shape[0]
    pc, lo, go, units, offs = tables
    used = offs[-1]
    partial = (-used) % MOE_BM
    tail = jnp.stack([used, partial, (_max_rows(t) - used - partial) // MOE_BM]).astype(jnp.int32)
    grid_spec = pltpu.PrefetchScalarGridSpec(
        num_scalar_prefetch=5, grid=(t // MOE_SUB,),
        in_specs=[pl.BlockSpec((TOP_K, MOE_SUB), lambda i, *_: (0, i)),
                  pl.BlockSpec((MOE_SUB, D_MODEL), lambda i, *_: (i, 0))],
        out_specs=pl.BlockSpec(memory_space=pl.ANY),
        scratch_shapes=[pltpu.VMEM((2, LOCAL_ROWS, D_MODEL), _F32), pltpu.SemaphoreType.DMA((2,))])
    return pl.pallas_call(
        _dispatch_kernel, grid_spec=grid_spec,
        out_shape=jax.ShapeDtypeStruct((_max_rows(t), D_MODEL), _F32),
        compiler_params=_cparams("arbitrary"), name="moe_dispatch",
    )(pc, lo, go, units, tail, eidx, h)


def _combine_kernel(cnt_ref, lo_ref, go_ref, units_ref, eidx_ref, gate_ref, ys_ref, h_ref, g_ref, b_ref,
                    o_ref, buf_ref, sem):
    i = pl.program_id(0)
    slot = i % 2

    def fetch(step, s):
        dst = buf_ref.at[s]
        dst[MOE_SUB * TOP_K:, :] = jnp.zeros((LOCAL_ROWS - MOE_SUB * TOP_K, D_MODEL), _F32)

        def copy(lo, go, sz):
            pltpu.make_async_copy(ys_ref.at[pl.ds(go, sz)], dst.at[pl.ds(lo, sz)], sem.at[s]).start()

        _segment_copies(cnt_ref, lo_ref, go_ref, step, copy)

    @pl.when(i == 0)
    def _():
        fetch(i, slot)

    @pl.when(i + 1 < pl.num_programs(0))
    def _():
        fetch(i + 1, 1 - slot)

    mine = buf_ref.at[slot]
    lpos = _local_positions(eidx_ref[...])
    gates = gate_ref[...]
    p_io = lax.broadcasted_iota(jnp.int32, (LOCAL_ROWS, MOE_SUB), 0).astype(_F32)
    w = jnp.zeros((LOCAL_ROWS, MOE_SUB), _F32)
    for k in range(TOP_K):
        w = w + jnp.where(p_io == lpos[k], gates[k:k + 1, :], 0.0)
    w_hi = w.astype(_BF16)
    w_lo = (w - w_hi.astype(_F32)).astype(_BF16)

    def wait_unit(j, c):
        pltpu.make_async_copy(ys_ref.at[pl.ds(0, SEG_ALIGN)], mine.at[pl.ds(0, SEG_ALIGN)], sem.at[slot]).wait()
        return c

    lax.fori_loop(0, units_ref[i], wait_unit, 0)
    yb = mine[...].astype(_BF16)
    tn = (((0,), (0,)), ((), ()))
    ffn = (lax.dot_general(w_hi, yb, tn, preferred_element_type=_F32)
           + lax.dot_general(w_lo, yb, tn, preferred_element_type=_F32))
    o_ref[...] = _layer_norm(DEEPNORM_ALPHA * h_ref[...] + ffn, g_ref[...], b_ref[...])


def moe_combine(ys, eidx, gates, h, tables, ln_g, ln_b):
    t = h.shape[0]
    pc, lo, go, units, _ = tables
    row = pl.BlockSpec((MOE_SUB, D_MODEL), lambda i, *_: (i, 0))
    kt = pl.BlockSpec((TOP_K, MOE_SUB), lambda i, *_: (0, i))
    cst = lambda shape: pl.BlockSpec(shape, lambda i, *_: (0,) * len(shape))
    grid_spec = pltpu.PrefetchScalarGridSpec(
        num_scalar_prefetch=4, grid=(t // MOE_SUB,),
        in_specs=[kt, kt, pl.BlockSpec(memory_space=pl.ANY), row, cst((1, D_MODEL)), cst((1, D_MODEL))],
        out_specs=row,
        scratch_shapes=[pltpu.VMEM((2, LOCAL_ROWS, D_MODEL), _F32), pltpu.SemaphoreType.DMA((2,))])
    return pl.pallas_call(
        _combine_kernel, grid_spec=grid_spec,
        out_shape=jax.ShapeDtypeStruct((t, D_MODEL), _F32),
        compiler_params=_cparams("arbitrary"), name="moe_combine",
    )(pc, lo, go, units, eidx, gates, ys, h, ln_g.reshape(1, -1), ln_b.reshape(1, -1))


def _expert_kernel(blk_ref, e_ref, lo_ref, hi_ref, xs_ref, wu_ref, bu_ref, wd_ref, bd_ref,
                   ys_ref, wub_ref, wdb_ref):
    i = pl.program_id(0)
    e = e_ref[i]
    e_prev = e_ref[jnp.maximum(i - 1, 0)]

    @pl.when((i == 0) | (e != e_prev))
    def _():
        wub_ref[...] = wu_ref[...].astype(_BF16)
        wdb_ref[...] = wd_ref[...].astype(_BF16)

    lo = lo_ref[i]
    hi = hi_ref[i]

    @pl.when(hi < 0)
    def _():
        ys_ref[...] = jnp.zeros_like(ys_ref)

    @pl.when(hi > lo)
    def _():
        x = xs_ref[...].astype(_BF16)
        h = jnp.dot(x, wub_ref[...], preferred_element_type=_F32) + bu_ref[...]
        gate = jnp.minimum(h[:, :D_FF], SWIGLU_LIMIT)
        up = jnp.clip(h[:, D_FF:], -SWIGLU_LIMIT, SWIGLU_LIMIT)
        glu = gate * jax.nn.sigmoid(SWIGLU_ALPHA * gate)
        act = ((up + 1.0) * glu).astype(_BF16)
        y = jnp.dot(act, wdb_ref[...], preferred_element_type=_F32) + bd_ref[...]
        row = lax.broadcasted_iota(jnp.int32, y.shape, 0)
        y = jnp.where((row >= lo) & (row < hi), y, 0.0)

        @pl.when(lo == 0)
        def _():
            ys_ref[...] = y

        @pl.when(lo > 0)
        def _():
            ys_ref[...] += y


def _expert_schedule(offs, n_rows, bm):
    nblk = n_rows // bm
    n_items = nblk + N_EXPERTS - 1
    used = offs[-1]
    bnd = offs[1:-1]
    pos = jnp.arange(N_EXPERTS - 1, dtype=jnp.int32) + jnp.minimum(bnd // bm + 1, nblk)
    i = jnp.arange(n_items, dtype=jnp.int32)[:, None]
    at = pos[None, :] == i
    is_bnd = jnp.any(at, axis=1)
    bnd_val = jnp.sum(jnp.where(at, bnd[None, :], 0), axis=1)
    n_before = jnp.sum((pos[None, :] < i).astype(jnp.int32), axis=1)
    cuts = jnp.where(is_bnd, bnd_val, (i[:, 0] - n_before) * bm)
    ends = jnp.concatenate([cuts[1:], jnp.array([n_rows], jnp.int32)])
    last_blk = jnp.maximum(used - 1, 0) // bm
    past = jnp.logical_not(is_bnd) & (cuts // bm > last_blk)
    raw_blk = cuts // bm
    cuts = jnp.minimum(cuts, used)
    ends = jnp.minimum(ends, used)
    blk = jnp.minimum(cuts // bm, last_blk)
    e = jnp.minimum(jnp.sum((bnd[None, :] <= cuts[:, None]).astype(jnp.int32), axis=1), N_EXPERTS - 1)
    lo = jnp.where(past, 0, cuts - blk * bm)
    hi = jnp.where(past, -1, jnp.minimum(ends - blk * bm, bm))
    return lax.cummax(jnp.where(past, raw_blk, blk)), e, lo, hi


def moe_experts(xs, offs, layer, w_up, b_up, w_down, b_down):
    n_rows = xs.shape[0]
    bm = MOE_BM
    blk, e, lo, hi = _expert_schedule(offs, n_rows, bm)
    n_items = n_rows // bm + N_EXPERTS - 1
    wmap = lambda i, b, e, lo, hi: (layer, e[i], 0, 0)
    grid_spec = pltpu.PrefetchScalarGridSpec(
        num_scalar_prefetch=4, grid=(n_items,),
        in_specs=[pl.BlockSpec((bm, D_MODEL), lambda i, b, e, lo, hi: (b[i], 0)),
                  pl.BlockSpec((None, None, D_MODEL, 2 * D_FF), wmap),
                  pl.BlockSpec((None, None, 1, 2 * D_FF), wmap),
                  pl.BlockSpec((None, None, D_FF, D_MODEL), wmap),
                  pl.BlockSpec((None, None, 1, D_MODEL), wmap)],
        out_specs=pl.BlockSpec((bm, D_MODEL), lambda i, b, e, lo, hi: (b[i], 0)),
        scratch_shapes=[pltpu.VMEM((D_MODEL, 2 * D_FF), _BF16), pltpu.VMEM((D_FF, D_MODEL), _BF16)])
    nl = w_up.shape[0]
    return pl.pallas_call(
        _expert_kernel, grid_spec=grid_spec,
        out_shape=jax.ShapeDtypeStruct((n_rows, D_MODEL), _F32),
        compiler_params=_cparams("arbitrary"), name="moe_experts",
    )(blk, e, lo, hi, xs, w_up, b_up.reshape(nl, N_EXPERTS, 1, -1), w_down, b_down.reshape(nl, N_EXPERTS, 1, -1))


def moe_layer(h, eidx, gates, tile_counts, layer, w_up, b_up, w_down, b_down, ln_g, ln_b):
    tables = _seg_tables(tile_counts)
    xs = moe_dispatch(h, eidx, tables)
    ys = moe_experts(xs, tables[4], layer, w_up, b_up, w_down, b_down)
    return moe_combine(ys, eidx, gates, h, tables, ln_g, ln_b)


def kernel(x, positions, attn_w_in, attn_w_out, ssm_w_in, ssm_conv_w, ssm_conv_b, ssm_dt_bias,
           ssm_a_log, ssm_d, ssm_norm_w, ssm_w_out, moe_w_router, moe_b_router, moe_w_up,
           moe_b_up, moe_w_down, moe_b_down, ln_mix_g, ln_mix_b, ln_ffn_g, ln_ffn_b):
    bsz, seq, _ = x.shape
    xf = x.reshape(bsz * seq, D_MODEL)

    w_qkv = attn_w_in[0].astype(_BF16)
    outs = [dilated_group_attention(qkv_proj_group(xf, w_qkv, positions, g), g, bsz, seq)
            for g in range(N_GROUPS)]
    h, eidx, gates, tile_counts = attn_out_norm_route(
        [o for o, _ in outs], [l for _, l in outs], xf, attn_w_out[0].astype(_BF16),
        ln_mix_g[0], ln_mix_b[0], moe_w_router[0], moe_b_router[0])
    xf = moe_layer(h, eidx, gates, tile_counts, 0, moe_w_up, moe_b_up, moe_w_down, moe_b_down,
                   ln_ffn_g[0], ln_ffn_b[0])

    gz, xs, bc, dt = ssm_in_proj(xf, ssm_w_in[0], ssm_conv_w[0], ssm_conv_b[0], ssm_dt_bias[0], seq)
    yn = ssd_mixer(gz, xs, bc, dt, ssm_a_log[0], ssm_d[0], ssm_norm_w[0], bsz, seq)
    h, eidx, gates, tile_counts = ssm_out_norm_route(
        yn, xf, ssm_w_out[0].astype(_BF16), ln_mix_g[1], ln_mix_b[1], moe_w_router[1], moe_b_router[1])
    xf = moe_layer(h, eidx, gates, tile_counts, 1, moe_w_up, moe_b_up, moe_w_down, moe_b_down,
                   ln_ffn_g[1], ln_ffn_b[1])
    return xf.reshape(bsz, seq, D_MODEL)
```

```python
import functools
import math

import jax
import jax.numpy as jnp
from jax import lax
from jax.experimental import pallas as pl
from jax.experimental.pallas import tpu as pltpu

D_MODEL = 1024
DEPTH = 2
DIL = (1, 4, 16)
N_GROUPS = 3
SUB_WIN = 128
HEAD_DIM = 128
HEADS = 8
ROT_DIM = 32
ROT_HALF = 16
ROPE_THETA = 500000.0
ATTN_PROJ = N_GROUPS * 3 * D_MODEL
SSM_INNER = 2048
SSM_P = 64
SSM_HEADS = 32
SSM_GROUPS = 4
SSM_GHEADS = 8
SSM_N = 128
SSM_CONV = 4
CHUNK = 128
SSM_BC = SSM_GROUPS * SSM_N
SSM_XBC = SSM_INNER + 2 * SSM_BC
N_EXPERTS = 32
TOP_K = 4
D_FF = 1024
SWIGLU_LIMIT = 7.0
SWIGLU_ALPHA = 1.702
DEEPNORM_ALPHA = (2 * DEPTH) ** 0.25
LN_EPS = 1e-5
RMS_EPS = 1e-5

LANES = 128
SUBLANES = 8
VMEM_LIMIT_BYTES = 56 * 1024 * 1024

NEG_BIG = -1e30

_F32 = jnp.float32
_BF16 = jnp.bfloat16


def _cparams(*sem):
    return pltpu.CompilerParams(dimension_semantics=sem, vmem_limit_bytes=VMEM_LIMIT_BYTES)


def _rope_table_kernel(pos_ref, invf_ref, c_ref, sa_ref, sb_ref):
    pos = pos_ref[...].astype(_F32)
    ang = pos * invf_ref[...]
    lane = lax.broadcasted_iota(jnp.int32, ang.shape, 1)
    cos = jnp.cos(ang)
    sin = jnp.sin(ang)
    c_ref[...] = jnp.where(lane < ROT_DIM, cos, 1.0)
    sa_ref[...] = jnp.where((lane >= ROT_HALF) & (lane < ROT_DIM), sin, 0.0)
    sb_ref[...] = jnp.where(lane < ROT_HALF, -sin, 0.0)


def rope_tables(positions):
    t = positions.size
    tm = 1024
    pos = positions.reshape(t, 1)
    j = jnp.arange(LANES)
    invf = jnp.where(j < ROT_DIM,
                     ROPE_THETA ** (-(2.0 * (j % ROT_HALF)).astype(_F32) / ROT_DIM), 0.0)
    invf = invf.astype(_F32).reshape(1, LANES)
    out = jax.ShapeDtypeStruct((t, LANES), _F32)
    return pl.pallas_call(
        _rope_table_kernel,
        grid=(t // tm,),
        in_specs=[pl.BlockSpec((tm, 1), lambda i: (i, 0)),
                  pl.BlockSpec((1, LANES), lambda i: (0, 0))],
        out_specs=[pl.BlockSpec((tm, LANES), lambda i: (i, 0))] * 3,
        out_shape=[out, out, out],
        compiler_params=_cparams("parallel"),
        name="rope_tables",
    )(pos, invf)


QKV_TM = 512
ATTN_ROWS = 2048
HEADS_PER_STEP = 2


def _qkv_proj_kernel(x_ref, w_ref, c_ref, sa_ref, sb_ref, o_ref, *, d, ns, tt):
    tm = x_ref.shape[0]
    xb = x_ref[...].astype(_BF16)
    if d > 1:
        dst = lax.broadcasted_iota(jnp.int32, (tm, tm), 0)
        src = lax.broadcasted_iota(jnp.int32, (tm, tm), 1)
        lt, ld = tt.bit_length() - 1, d.bit_length() - 1
        t_ = dst & (tt - 1)
        r_ = lax.shift_right_logical(dst, lt) & (d - 1)
        s_ = lax.shift_right_logical(dst, lt + ld)
        perm = (src == ((s_ * tt + t_) * d + r_)).astype(_BF16)
        xb = jnp.dot(perm, xb, preferred_element_type=_F32).astype(_BF16)
    c = c_ref[...]
    sa = sa_ref[...]
    sb = sb_ref[...]
    for comp in range(3):
        acc = jnp.dot(xb, w_ref[:, comp * D_MODEL:(comp + 1) * D_MODEL], preferred_element_type=_F32)
        for h in range(HEADS):
            sl = slice(h * HEAD_DIM, (h + 1) * HEAD_DIM)
            t = acc[:, sl]
            if comp < 2:
                t = t * c + pltpu.roll(t, ROT_HALF, 1) * sa + pltpu.roll(t, HEAD_DIM - ROT_HALF, 1) * sb
            o_ref[:, :, :, comp * D_MODEL + h * HEAD_DIM:comp * D_MODEL + (h + 1) * HEAD_DIM] = (
                t.reshape(ns, d, tt, HEAD_DIM).astype(o_ref.dtype))


def qkv_proj_group(x, w_bf16, positions, g):
    t, k = x.shape
    d = DIL[g]
    span = SUB_WIN * d
    tm = QKV_TM
    ns = max(1, tm // span)
    tps = max(1, span // tm)
    tt = tm // (ns * d)
    rope = rope_tables(positions.reshape(-1, ns, tt, d).swapaxes(2, 3).reshape(-1))
    tab_spec = pl.BlockSpec((tm, LANES), lambda i: (i, 0))
    return pl.pallas_call(
        functools.partial(_qkv_proj_kernel, d=d, ns=ns, tt=tt),
        grid=(t // tm,),
        in_specs=[pl.BlockSpec((tm, k), lambda i: (i, 0)),
                  pl.BlockSpec((k, 3 * D_MODEL), lambda i: (0, g)),
                  tab_spec, tab_spec, tab_spec],
        out_specs=pl.BlockSpec((ns, d, tt, 3 * D_MODEL), lambda i: (i // tps, 0, i % tps, 0)),
        out_shape=jax.ShapeDtypeStruct((t // span, d, SUB_WIN, 3 * D_MODEL), _BF16),
        compiler_params=_cparams("parallel"), name=f"qkv_proj_g{g}",
    )(x, w_bf16, *rope)


def _dil_attn_kernel(q_ref, k_ref, v_ref, kp_ref, vp_ref, o_ref, lse_ref, *, d, ns):
    j = pl.program_id(1)
    hp = pl.program_id(2)
    nb = ns * d
    row = lax.broadcasted_iota(jnp.int32, (1, SUB_WIN, SUB_WIN), 1)
    col = lax.broadcasted_iota(jnp.int32, (1, SUB_WIN, SUB_WIN), 2)
    cur_ok = col <= row
    blk = lax.broadcasted_iota(jnp.int32, (nb, 1, 1), 0)
    prev_ok = (col >= row) & ((blk >= d) | (j > 0))
    lane = lax.broadcasted_iota(jnp.int32, (SUB_WIN, LANES), 1)
    scale = 1.0 / math.sqrt(HEAD_DIM)

    def block_rows(b):
        s, r = divmod(b, d)
        start = s * SUB_WIN * d + r
        return pl.ds(start, SUB_WIN, stride=d) if d > 1 else pl.ds(start, SUB_WIN)

    @pl.when(hp == 0)
    def _():
        lse_ref[...] = jnp.zeros_like(lse_ref)

    lses = []
    for hh in range(HEADS_PER_STEP):
        sl = slice(hh * HEAD_DIM, (hh + 1) * HEAD_DIM)
        flat = lambda a: a.reshape(nb, SUB_WIN, HEAD_DIM)
        q, kc, vc = flat(q_ref[:, :, :, sl]), flat(k_ref[:, :, :, sl]), flat(v_ref[:, :, :, sl])
        if ns > 1:
            kp = flat(jnp.concatenate([kp_ref[:, :, :, sl], k_ref[:ns - 1, :, :, sl]], axis=0))
            vp = flat(jnp.concatenate([vp_ref[:, :, :, sl], v_ref[:ns - 1, :, :, sl]], axis=0))
        else:
            kp, vp = flat(kp_ref[:, :, :, sl]), flat(vp_ref[:, :, :, sl])
        sc = jnp.einsum('bqd,bkd->bqk', q, kc, preferred_element_type=_F32) * scale
        sp = jnp.einsum('bqd,bkd->bqk', q, kp, preferred_element_type=_F32) * scale
        sc = jnp.where(cur_ok, sc, NEG_BIG)
        sp = jnp.where(prev_ok, sp, NEG_BIG)
        m = jnp.maximum(jnp.max(sc, axis=2, keepdims=True), jnp.max(sp, axis=2, keepdims=True))
        pc = jnp.exp(sc - m)
        pp = jnp.exp(sp - m)
        l = jnp.sum(pc, axis=2, keepdims=True) + jnp.sum(pp, axis=2, keepdims=True)
        acc = jnp.einsum('bqk,bkd->bqd', pc.astype(_BF16), vc, preferred_element_type=_F32)
        acc += jnp.einsum('bqk,bkd->bqd', pp.astype(_BF16), vp, preferred_element_type=_F32)
        o = acc / l
        for b in range(nb):
            o_ref[hh, block_rows(b), :] = o[b]
        lses.append(m + jnp.log(l))
    for b in range(nb):
        cur = lse_ref[block_rows(b), :]
        for hh in range(HEADS_PER_STEP):
            cur = jnp.where(lane == hp * HEADS_PER_STEP + hh, lses[hh][b], cur)
        lse_ref[block_rows(b), :] = cur


def dilated_group_attention(qkv, g, bsz, seq):
    d = DIL[g]
    ns = ATTN_ROWS // (SUB_WIN * d)
    nsteps = seq // ATTN_ROWS
    hw = HEADS_PER_STEP * HEAD_DIM
    ncol = D_MODEL // hw
    cur = lambda comp: pl.BlockSpec((ns, d, SUB_WIN, hw), lambda b, j, hp: (b * nsteps + j, 0, 0, comp * ncol + hp))
    prev = lambda comp: pl.BlockSpec(
        (1, d, SUB_WIN, hw), lambda b, j, hp: (jnp.maximum((b * nsteps + j) * ns - 1, 0), 0, 0, comp * ncol + hp))
    return pl.pallas_call(
        functools.partial(_dil_attn_kernel, d=d, ns=ns),
        grid=(bsz, nsteps, HEADS // HEADS_PER_STEP),
        in_specs=[cur(0), cur(1), cur(2), prev(1), prev(2)],
        out_specs=[pl.BlockSpec((HEADS_PER_STEP, ATTN_ROWS, HEAD_DIM), lambda b, j, hp: (hp, b * nsteps + j, 0)),
                   pl.BlockSpec((ATTN_ROWS, LANES), lambda b, j, hp: (b * nsteps + j, 0))],
        out_shape=[jax.ShapeDtypeStruct((HEADS, bsz * seq, HEAD_DIM), _F32),
                   jax.ShapeDtypeStruct((bsz * seq, LANES), _F32)],
        compiler_params=_cparams("parallel", "parallel", "arbitrary"),
        name=f"dilated_attn_g{g}",
    )(qkv, qkv, qkv, qkv, qkv)


SSM_IN_TM = 512
HALO = SUBLANES
ZX_COLS = SSM_INNER + SSM_XBC


def _softplus(x):
    return jnp.maximum(x, 0.0) + jnp.log1p(jnp.exp(-jnp.abs(x)))


def _ssm_in_kernel(x_ref, w_ref, wdt_ref, cw_ref, cb_ref, dtb_ref, gz_ref, xs_ref, bc_ref, dt_ref,
                   carry_ref, *, tiles_per_seq):
    tm = x_ref.shape[0]
    cw = D_MODEL
    xb = x_ref[...].astype(_BF16)

    @pl.when(pl.program_id(0) % tiles_per_seq == 0)
    def _():
        carry_ref[...] = jnp.zeros_like(carry_ref)

    for i in range(SSM_INNER // cw):
        z = jnp.dot(xb, w_ref[:, i * cw:(i + 1) * cw], preferred_element_type=_F32)
        gz_ref[:, i * cw:(i + 1) * cw] = z * jax.nn.sigmoid(z)
    for c in range(SSM_XBC // cw):
        cols = slice(c * cw, (c + 1) * cw)
        acc = jnp.dot(xb, w_ref[:, SSM_INNER + c * cw:SSM_INNER + (c + 1) * cw], preferred_element_type=_F32)
        ext = jnp.concatenate([carry_ref[c], acc], axis=0)
        carry_ref[c] = acc[tm - HALO:, :]
        conv = cb_ref[:, cols] + cw_ref[SSM_CONV - 1:SSM_CONV, cols] * acc
        for k in range(1, SSM_CONV):
            conv = conv + cw_ref[SSM_CONV - 1 - k:SSM_CONV - k, cols] * pltpu.roll(ext, k, 0)[HALO:, :]
        act = conv * jax.nn.sigmoid(conv)
        if (c + 1) * cw <= SSM_INNER:
            xs_ref[:, cols] = act
        else:
            bc_ref[...] = act.astype(_BF16)
    dt_raw = jnp.dot(xb, wdt_ref[...], preferred_element_type=_F32)
    dt_ref[...] = _softplus(dt_raw + dtb_ref[...])


def ssm_in_proj(x, w_in, conv_w, conv_b, dt_bias, seq):
    t, k = x.shape
    tm = SSM_IN_TM
    assert 2 * SSM_BC == D_MODEL
    pad = LANES - SSM_HEADS
    w_main = w_in[:, :ZX_COLS].astype(_BF16)
    w_dt = jnp.pad(w_in[:, ZX_COLS:], ((0, 0), (0, pad))).astype(_BF16)
    dtb = jnp.pad(dt_bias, (0, pad)).reshape(1, LANES)
    cst = lambda shape: pl.BlockSpec(shape, lambda i: (0,) * len(shape))
    once = lambda shape: pl.BlockSpec(shape, lambda i: (0,) * len(shape), pipeline_mode=pl.Buffered(1))
    row = lambda w: pl.BlockSpec((tm, w), lambda i: (i, 0))
    return pl.pallas_call(
        functools.partial(_ssm_in_kernel, tiles_per_seq=seq // tm),
        grid=(t // tm,),
        in_specs=[row(k), once((k, ZX_COLS)), once((k, LANES)),
                  cst((SSM_CONV, SSM_XBC)), cst((1, SSM_XBC)), cst((1, LANES))],
        out_specs=[row(SSM_INNER), row(SSM_INNER), row(2 * SSM_BC), row(LANES)],
        out_shape=[jax.ShapeDtypeStruct((t, SSM_INNER), _F32), jax.ShapeDtypeStruct((t, SSM_INNER), _F32),
                   jax.ShapeDtypeStruct((t, 2 * SSM_BC), _BF16), jax.ShapeDtypeStruct((t, LANES), _F32)],
        scratch_shapes=[pltpu.VMEM((SSM_XBC // D_MODEL, HALO, D_MODEL), _F32)],
        compiler_params=_cparams("arbitrary"), name="ssm_in_proj",
    )(x, w_main, w_dt, conv_w, conv_b.reshape(1, -1), dtb)


def _ssd_kernel(gz_ref, xs_ref, bc_ref, dt_ref, a_ref, dsk_ref, nw_ref, o_ref, state_ref, y_ref):
    c = pl.program_id(1)
    q = CHUNK

    @pl.when(c == 0)
    def _():
        state_ref[...] = jnp.zeros_like(state_ref)

    dt = dt_ref[...]
    da = dt * a_ref[...]
    r_io = lax.broadcasted_iota(jnp.int32, (q, q), 0)
    c_io = lax.broadcasted_iota(jnp.int32, (q, q), 1)
    causal = r_io >= c_io
    tri = causal.astype(_F32)
    a_cum = jnp.dot(tri, da, preferred_element_type=_F32, precision=lax.Precision.HIGHEST)
    a_cum_t = a_cum.T
    dt_t = dt.T
    a_last = a_cum[q - 1:q, :]
    w_state = jnp.exp(a_last - a_cum) * dt
    e_acum = jnp.exp(a_cum)
    chunk_decay = jnp.exp(a_last)
    lane = lax.broadcasted_iota(jnp.int32, (q, LANES), 1)
    low = lane < SSM_P
    lane1 = lax.broadcasted_iota(jnp.int32, (1, LANES), 1)
    low1 = lane1 < SSM_P
    tn = (((0,), (0,)), ((), ()))
    nt = (((1,), (1,)), ((), ()))

    for g in range(SSM_GROUPS):
        bg = bc_ref[:, g * SSM_N:(g + 1) * SSM_N]
        cg = bc_ref[:, SSM_BC + g * SSM_N:SSM_BC + (g + 1) * SSM_N]
        cbm = lax.dot_general(cg, bg, nt, preferred_element_type=_F32)
        st = state_ref[g]
        y_off = jnp.dot(cg, st.astype(_BF16), preferred_element_type=_F32)
        xw_parts, dec_parts = [], []
        for j in range(SSM_GHEADS // 2):
            h0 = g * SSM_GHEADS + 2 * j
            col = (g * SSM_GHEADS // 2 + j) * LANES
            xp = xs_ref[:, col:col + LANES]
            y_diag = jnp.zeros((q, LANES), _F32)
            for half, hh in enumerate((h0, h0 + 1)):
                diff = a_cum[:, hh:hh + 1] - a_cum_t[hh:hh + 1, :]
                lmat = jnp.exp(jnp.where(causal, diff, NEG_BIG))
                mm = (cbm * lmat * dt_t[hh:hh + 1, :]).astype(_BF16)
                keep = low if half == 0 else jnp.logical_not(low)
                xh = jnp.where(keep, xp, 0.0).astype(_BF16)
                y_diag = y_diag + jnp.dot(mm, xh, preferred_element_type=_F32)
            e_pair = jnp.where(low, e_acum[:, h0:h0 + 1], e_acum[:, h0 + 1:h0 + 2])
            w_pair = jnp.where(low, w_state[:, h0:h0 + 1], w_state[:, h0 + 1:h0 + 2])
            y_ref[:, col:col + LANES] = (y_diag + y_off[:, j * LANES:(j + 1) * LANES] * e_pair
                                         + xp * dsk_ref[:, col:col + LANES])
            xw_parts.append((xp * w_pair).astype(_BF16))
            dec_parts.append(jnp.where(low1, chunk_decay[:, h0:h0 + 1], chunk_decay[:, h0 + 1:h0 + 2]))
        xw = jnp.concatenate(xw_parts, axis=1)
        dec = jnp.concatenate(dec_parts, axis=1)
        state_ref[g] = st * dec + lax.dot_general(bg, xw, tn, preferred_element_type=_F32)

    yg = y_ref[...] * gz_ref[...]
    gw = SSM_INNER // SSM_GROUPS
    for g in range(SSM_GROUPS):
        v = yg[:, g * gw:(g + 1) * gw]
        ms = jnp.mean(v * v, axis=-1, keepdims=True)
        o_ref[:, g * gw:(g + 1) * gw] = (v * lax.rsqrt(ms + RMS_EPS)
                                         * nw_ref[:, g * gw:(g + 1) * gw]).astype(o_ref.dtype)


def ssd_mixer(gz, xs, bc, dt, a_log, d_skip, norm_w, bsz, seq):
    nc = seq // CHUNK
    a = jnp.pad(-jnp.exp(a_log.astype(_F32)), (0, LANES - SSM_HEADS)).reshape(1, LANES)
    dsk = jnp.repeat(d_skip.astype(_F32), SSM_P).reshape(1, SSM_INNER)
    cst = lambda shape: pl.BlockSpec(shape, lambda b, c: (0,) * len(shape))
    row = lambda w: pl.BlockSpec((CHUNK, w), lambda b, c: (b * nc + c, 0))
    return pl.pallas_call(
        _ssd_kernel,
        grid=(bsz, nc),
        in_specs=[row(SSM_INNER), row(SSM_INNER), row(2 * SSM_BC), row(LANES),
                  cst((1, LANES)), cst((1, SSM_INNER)), cst((1, SSM_INNER))],
        out_specs=row(SSM_INNER),
        out_shape=jax.ShapeDtypeStruct((bsz * seq, SSM_INNER), _BF16),
        scratch_shapes=[pltpu.VMEM((SSM_GROUPS, SSM_N, SSM_GHEADS * SSM_P), _F32),
                        pltpu.VMEM((CHUNK, SSM_INNER), _F32)],
        compiler_params=_cparams("parallel", "arbitrary"), name="ssd_mixer",
    )(gz, xs, bc, dt, a, dsk, norm_w.reshape(1, -1))


ROUTE_SUB = 256
ROUTE_TM_ATTN = 512
ROUTE_TM_SSM = 1024


def _layer_norm(y, g, b):
    mu = jnp.mean(y, axis=-1, keepdims=True)
    yc = y - mu
    var = jnp.mean(yc * yc, axis=-1, keepdims=True)
    return yc * lax.rsqrt(var + LN_EPS) * g + b


def _route(h1s, wrt_ref, br_ref, eidx_ref, gate_ref, tcnt_ref):
    sub = ROUTE_SUB
    n = len(h1s)
    nt = (((1,), (1,)), ((), ()))
    wrt = wrt_ref[...]
    vals = [lax.dot_general(wrt, h, nt, preferred_element_type=_F32, precision=lax.Precision.HIGHEST)
            + br_ref[...] for h in h1s]
    eio = lax.broadcasted_iota(jnp.int32, (N_EXPERTS, sub), 0)
    tops, ids, hots = ([[] for _ in range(n)] for _ in range(3))
    for _ in range(TOP_K):
        for a in range(n):
            mk = jnp.max(vals[a], axis=0, keepdims=True)
            ik = jnp.min(jnp.where(vals[a] == mk, eio, N_EXPERTS), axis=0, keepdims=True)
            hot = eio == ik
            vals[a] = jnp.where(hot, -jnp.inf, vals[a])
            tops[a].append(mk)
            ids[a].append(ik)
            hots[a].append(hot)
    col = lax.broadcasted_iota(jnp.int32, tcnt_ref.shape, 1)
    tcnt = tcnt_ref[...]
    for a in range(n):
        sl = slice(a * sub, (a + 1) * sub)
        exps = [jnp.exp(t - tops[a][0]) for t in tops[a]]
        den = exps[0] + exps[1] + exps[2] + exps[3]
        for k in range(TOP_K):
            eidx_ref[k:k + 1, sl] = ids[a][k]
            gate_ref[k:k + 1, sl] = exps[k] / den
        hot_all = hots[a][0] | hots[a][1] | hots[a][2] | hots[a][3]
        counts = jnp.sum(hot_all.astype(_F32), axis=1, keepdims=True)
        tcnt = jnp.where(col == pl.program_id(0) * n + a, counts, tcnt)
    tcnt_ref[...] = tcnt


def _norm_and_route(mixes, x_ref, g_ref, b_ref, wrt_ref, br_ref, h_ref, eidx_ref, gate_ref, tcnt_ref):
    sub = ROUTE_SUB

    @pl.when(pl.program_id(0) == 0)
    def _():
        tcnt_ref[...] = jnp.zeros_like(tcnt_ref)

    h1s = [_layer_norm(DEEPNORM_ALPHA * x_ref[a * sub:(a + 1) * sub, :] + mix, g_ref[...], b_ref[...])
           for a, mix in enumerate(mixes)]
    for a, h1 in enumerate(h1s):
        h_ref[a * sub:(a + 1) * sub, :] = h1
    _route(h1s, wrt_ref, br_ref, eidx_ref, gate_ref, tcnt_ref)


def _attn_out_kernel(o0_ref, o1_ref, o2_ref, l0_ref, l1_ref, l2_ref, x_ref, wo_ref, g_ref, b_ref,
                     wrt_ref, br_ref, h_ref, eidx_ref, gate_ref, tcnt_ref, ob_ref):
    sub = ROUTE_SUB
    l0, l1, l2 = l0_ref[...], l1_ref[...], l2_ref[...]
    m = jnp.maximum(jnp.maximum(l0, l1), l2)
    e0, e1, e2 = jnp.exp(l0 - m), jnp.exp(l1 - m), jnp.exp(l2 - m)
    inv = 1.0 / (e0 + e1 + e2)
    w0, w1, w2 = e0 * inv, e1 * inv, e2 * inv
    for h in range(HEADS):
        o = w0[:, h:h + 1] * o0_ref[h] + w1[:, h:h + 1] * o1_ref[h] + w2[:, h:h + 1] * o2_ref[h]
        ob_ref[:, h * HEAD_DIM:(h + 1) * HEAD_DIM] = o.astype(_BF16)
    mixes = [jnp.dot(ob_ref[a * sub:(a + 1) * sub, :], wo_ref[...], preferred_element_type=_F32)
             for a in range(x_ref.shape[0] // sub)]
    _norm_and_route(mixes, x_ref, g_ref, b_ref, wrt_ref, br_ref, h_ref, eidx_ref, gate_ref, tcnt_ref)


def _ssm_out_kernel(y_ref, x_ref, wo_ref, g_ref, b_ref, wrt_ref, br_ref, h_ref, eidx_ref, gate_ref, tcnt_ref):
    sub = ROUTE_SUB
    mixes = [jnp.dot(y_ref[a * sub:(a + 1) * sub, :], wo_ref[...], preferred_element_type=_F32)
             for a in range(x_ref.shape[0] // sub)]
    _norm_and_route(mixes, x_ref, g_ref, b_ref, wrt_ref, br_ref, h_ref, eidx_ref, gate_ref, tcnt_ref)


def _route_out_specs(t, tm):
    nsub = t // ROUTE_SUB
    specs = [pl.BlockSpec((tm, D_MODEL), lambda i: (i, 0)),
             pl.BlockSpec((TOP_K, tm), lambda i: (0, i)),
             pl.BlockSpec((TOP_K, tm), lambda i: (0, i)),
             pl.BlockSpec((N_EXPERTS, nsub), lambda i: (0, 0))]
    shapes = [jax.ShapeDtypeStruct((t, D_MODEL), _F32),
              jax.ShapeDtypeStruct((TOP_K, t), jnp.int32),
              jax.ShapeDtypeStruct((TOP_K, t), _F32),
              jax.ShapeDtypeStruct((N_EXPERTS, nsub), _F32)]
    return specs, shapes


def _const_spec(shape):
    return pl.BlockSpec(shape, lambda i: (0,) * len(shape))


def attn_out_norm_route(os_, lses, x, wo_bf16, ln_g, ln_b, w_router, b_router):
    t = x.shape[0]
    tm = ROUTE_TM_ATTN
    row = lambda w: pl.BlockSpec((tm, w), lambda i: (i, 0))
    out_specs, out_shapes = _route_out_specs(t, tm)
    return pl.pallas_call(
        _attn_out_kernel,
        grid=(t // tm,),
        in_specs=[pl.BlockSpec((HEADS, tm, HEAD_DIM), lambda i: (0, i, 0))] * 3 + [row(LANES)] * 3 + [row(D_MODEL),
                  _const_spec((D_MODEL, D_MODEL)), _const_spec((1, D_MODEL)), _const_spec((1, D_MODEL)),
                  _const_spec((N_EXPERTS, D_MODEL)), _const_spec((N_EXPERTS, 1))],
        out_specs=out_specs, out_shape=out_shapes,
        scratch_shapes=[pltpu.VMEM((tm, D_MODEL), _BF16)],
        compiler_params=_cparams("arbitrary"), name="attn_out_norm_route",
    )(*os_, *lses, x, wo_bf16, ln_g.reshape(1, -1), ln_b.reshape(1, -1),
      w_router.T, b_router.reshape(-1, 1))


def ssm_out_norm_route(yn, x, wo_bf16, ln_g, ln_b, w_router, b_router):
    t = x.shape[0]
    tm = ROUTE_TM_SSM
    row = lambda w: pl.BlockSpec((tm, w), lambda i: (i, 0))
    out_specs, out_shapes = _route_out_specs(t, tm)
    return pl.pallas_call(
        _ssm_out_kernel,
        grid=(t // tm,),
        in_specs=[row(SSM_INNER), row(D_MODEL),
                  _const_spec((SSM_INNER, D_MODEL)), _const_spec((1, D_MODEL)), _const_spec((1, D_MODEL)),
                  _const_spec((N_EXPERTS, D_MODEL)), _const_spec((N_EXPERTS, 1))],
        out_specs=out_specs, out_shape=out_shapes,
        compiler_params=_cparams("arbitrary"), name="ssm_out_norm_route",
    )(yn, x, wo_bf16, ln_g.reshape(1, -1), ln_b.reshape(1, -1), w_router.T, b_router.reshape(-1, 1))


MOE_BM = 256
MOE_SUB = ROUTE_SUB
SEG_ALIGN = SUBLANES
LOCAL_ROWS = MOE_SUB * TOP_K + N_EXPERTS * SUBLANES
SEG_SIZES = (256, 128, 64, 32, 16, 8)
PAIR_COLS = D_MODEL // 2
_U32 = jnp.uint32


def _pack_pairs(x):
    lo = lax.shift_right_logical(lax.bitcast_convert_type(x[:, :PAIR_COLS], _U32), jnp.uint32(16))
    hi = lax.bitcast_convert_type(x[:, PAIR_COLS:], _U32) & jnp.uint32(0xFFFF0000)
    return hi | lo


def _unpack_pairs(u):
    lo = lax.bitcast_convert_type(lax.shift_left(u, jnp.uint32(16)), _F32).astype(_BF16)
    hi = lax.bitcast_convert_type(u & jnp.uint32(0xFFFF0000), _F32).astype(_BF16)
    return lo, hi


def _local_positions(eidx):
    sub = eidx.shape[1]
    eio = lax.broadcasted_iota(jnp.int32, (N_EXPERTS, sub), 0)
    hots = [eio == eidx[k:k + 1, :] for k in range(TOP_K)]
    hot_all = (hots[0] | hots[1] | hots[2] | hots[3]).astype(_BF16)
    r_io = lax.broadcasted_iota(jnp.int32, (sub, sub), 0)
    c_io = lax.broadcasted_iota(jnp.int32, (sub, sub), 1)
    upper = (r_io < c_io).astype(_BF16)
    prefix = jnp.dot(hot_all, upper, preferred_element_type=_F32)
    counts = jnp.sum(hot_all.astype(_F32), axis=1, keepdims=True).astype(jnp.int32)
    padded = ((counts + (SEG_ALIGN - 1)) & (-SEG_ALIGN)).astype(_F32)
    e_r = lax.broadcasted_iota(jnp.int32, (N_EXPERTS, N_EXPERTS), 0)
    e_c = lax.broadcasted_iota(jnp.int32, (N_EXPERTS, N_EXPERTS), 1)
    below = (e_c < e_r).astype(_BF16)
    seg_start = jnp.dot(below, jnp.broadcast_to(padded, (N_EXPERTS, sub)).astype(_BF16),
                        preferred_element_type=_F32)
    base = seg_start + prefix
    return [jnp.sum(jnp.where(h, base, 0.0), axis=0, keepdims=True) for h in hots]


def _segment_copies(cnt_ref, lo_ref, go_ref, i, copy_fn):
    for e in range(N_EXPERTS):
        n = cnt_ref[i * N_EXPERTS + e]
        lo = lo_ref[i * N_EXPERTS + e]
        go = go_ref[i * N_EXPERTS + e]
        done = jnp.int32(0)
        for sz in SEG_SIZES:
            @pl.when((n & sz) != 0)
            def _(done=done, sz=sz):
                copy_fn(pl.multiple_of(lo + done, SEG_ALIGN), pl.multiple_of(go + done, SEG_ALIGN), sz)
            done = done + (n & sz)


def _dispatch_kernel(cnt_ref, lo_ref, go_ref, units_ref, tail_ref, eidx_ref, h_ref, xs_ref, buf_ref, sem):
    i = pl.program_id(0)
    slot = i % 2
    mine, other = buf_ref.at[slot], buf_ref.at[1 - slot]
    lpos = _local_positions(eidx_ref[...])
    p_io = lax.broadcasted_iota(jnp.int32, (LOCAL_ROWS, MOE_SUB), 0).astype(_F32)
    sel = (p_io == lpos[0]) | (p_io == lpos[1]) | (p_io == lpos[2]) | (p_io == lpos[3])
    mine[...] = _pack_pairs(jnp.dot(sel.astype(_BF16), h_ref[...].astype(_BF16), preferred_element_type=_F32))

    def copy(lo, go, sz):
        pltpu.make_async_copy(mine.at[pl.ds(lo, sz)], xs_ref.at[pl.ds(go, sz)], sem.at[slot]).start()

    _segment_copies(cnt_ref, lo_ref, go_ref, i, copy)

    def wait_units(buf, s, n_units):
        def body(j, c):
            pltpu.make_async_copy(buf.at[pl.ds(0, SEG_ALIGN)], xs_ref.at[pl.ds(0, SEG_ALIGN)], sem.at[s]).wait()
            return c
        lax.fori_loop(0, n_units, body, 0)

    @pl.when(i > 0)
    def _():
        wait_units(other, 1 - slot, units_ref[jnp.maximum(i - 1, 0)])

    @pl.when(i == pl.num_programs(0) - 1)
    def _():
        wait_units(mine, slot, units_ref[i])
        mine[0:MOE_BM, :] = jnp.zeros((MOE_BM, PAIR_COLS), _U32)
        start, n, n_blocks = tail_ref[0], tail_ref[1], tail_ref[2]
        done = jnp.int32(0)
        for sz in SEG_SIZES[1:]:
            @pl.when((n & sz) != 0)
            def _(done=done, sz=sz):
                pltpu.make_async_copy(mine.at[pl.ds(0, sz)],
                                      xs_ref.at[pl.ds(pl.multiple_of(start + done, SEG_ALIGN), sz)],
                                      sem.at[slot]).start()
            done = done + (n & sz)

        def zero_block(j, c):
            row0 = pl.multiple_of(start + n + j * MOE_BM, MOE_BM)
            pltpu.make_async_copy(mine.at[pl.ds(0, MOE_BM)], xs_ref.at[pl.ds(row0, MOE_BM)], sem.at[slot]).start()
            return c

        lax.fori_loop(0, n_blocks, zero_block, 0)
        wait_units(mine, slot, n // SEG_ALIGN + n_blocks * (MOE_BM // SEG_ALIGN))


def _seg_tables(tile_counts):
    cnt = tile_counts.T.astype(jnp.int32)
    pc = (cnt + (SEG_ALIGN - 1)) & (-SEG_ALIGN)
    lo = jnp.cumsum(pc, axis=1) - pc
    tot = jnp.sum(pc, axis=0)
    offs = jnp.concatenate([jnp.zeros((1,), jnp.int32), jnp.cumsum(tot)])
    go = offs[None, :-1] + jnp.cumsum(pc, axis=0) - pc
    units = jnp.sum(pc, axis=1) // SEG_ALIGN
    return pc.reshape(-1), lo.reshape(-1), go.reshape(-1), units.astype(jnp.int32), offs.astype(jnp.int32)


def _max_rows(t):
    n = t * TOP_K + (t // MOE_SUB) * N_EXPERTS * (SEG_ALIGN - 1)
    return -(-n // MOE_BM) * MOE_BM


def moe_dispatch(h, eidx, tables):
    t = h.shape[0]
    pc, lo, go, units, offs = tables
    used = offs[-1]
    partial = (-used) % MOE_BM
    tail = jnp.stack([used, partial, (_max_rows(t) - used - partial) // MOE_BM]).astype(jnp.int32)
    grid_spec = pltpu.PrefetchScalarGridSpec(
        num_scalar_prefetch=5, grid=(t // MOE_SUB,),
        in_specs=[pl.BlockSpec((TOP_K, MOE_SUB), lambda i, *_: (0, i)),
                  pl.BlockSpec((MOE_SUB, D_MODEL), lambda i, *_: (i, 0))],
        out_specs=pl.BlockSpec(memory_space=pl.ANY),
        scratch_shapes=[pltpu.VMEM((2, LOCAL_ROWS, PAIR_COLS), _U32), pltpu.SemaphoreType.DMA((2,))])
    return pl.pallas_call(
        _dispatch_kernel, grid_spec=grid_spec,
        out_shape=jax.ShapeDtypeStruct((_max_rows(t), PAIR_COLS), _U32),
        compiler_params=_cparams("arbitrary"), name="moe_dispatch",
    )(pc, lo, go, units, tail, eidx, h)


def _combine_kernel(cnt_ref, lo_ref, go_ref, units_ref, eidx_ref, gate_ref, ys_ref, h_ref, g_ref, b_ref,
                    o_ref, buf_ref, sem):
    i = pl.program_id(0)
    slot = i % 2

    def fetch(step, s):
        dst = buf_ref.at[s]
        dst[MOE_SUB * TOP_K:, :] = jnp.zeros((LOCAL_ROWS - MOE_SUB * TOP_K, PAIR_COLS), _U32)

        def copy(lo, go, sz):
            pltpu.make_async_copy(ys_ref.at[pl.ds(go, sz)], dst.at[pl.ds(lo, sz)], sem.at[s]).start()

        _segment_copies(cnt_ref, lo_ref, go_ref, step, copy)

    @pl.when(i == 0)
    def _():
        fetch(i, slot)

    @pl.when(i + 1 < pl.num_programs(0))
    def _():
        fetch(i + 1, 1 - slot)

    mine = buf_ref.at[slot]
    lpos = _local_positions(eidx_ref[...])
    gates = gate_ref[...]
    p_io = lax.broadcasted_iota(jnp.int32, (LOCAL_ROWS, MOE_SUB), 0).astype(_F32)
    w = jnp.zeros((LOCAL_ROWS, MOE_SUB), _F32)
    for k in range(TOP_K):
        w = w + jnp.where(p_io == lpos[k], gates[k:k + 1, :], 0.0)
    w_hi = w.astype(_BF16)
    w_lo = (w - w_hi.astype(_F32)).astype(_BF16)

    def wait_unit(j, c):
        pltpu.make_async_copy(ys_ref.at[pl.ds(0, SEG_ALIGN)], mine.at[pl.ds(0, SEG_ALIGN)], sem.at[slot]).wait()
        return c

    lax.fori_loop(0, units_ref[i], wait_unit, 0)
    tn = (((0,), (0,)), ((), ()))
    ffn = jnp.concatenate(
        [lax.dot_general(w_hi, yb, tn, preferred_element_type=_F32)
         + lax.dot_general(w_lo, yb, tn, preferred_element_type=_F32) for yb in _unpack_pairs(mine[...])],
        axis=1)
    o_ref[...] = _layer_norm(DEEPNORM_ALPHA * h_ref[...] + ffn, g_ref[...], b_ref[...])


def moe_combine(ys, eidx, gates, h, tables, ln_g, ln_b):
    t = h.shape[0]
    pc, lo, go, units, _ = tables
    row = pl.BlockSpec((MOE_SUB, D_MODEL), lambda i, *_: (i, 0))
    kt = pl.BlockSpec((TOP_K, MOE_SUB), lambda i, *_: (0, i))
    cst = lambda shape: pl.BlockSpec(shape, lambda i, *_: (0,) * len(shape))
    grid_spec = pltpu.PrefetchScalarGridSpec(
        num_scalar_prefetch=4, grid=(t // MOE_SUB,),
        in_specs=[kt, kt, pl.BlockSpec(memory_space=pl.ANY), row, cst((1, D_MODEL)), cst((1, D_MODEL))],
        out_specs=row,
        scratch_shapes=[pltpu.VMEM((2, LOCAL_ROWS, PAIR_COLS), _U32), pltpu.SemaphoreType.DMA((2,))])
    return pl.pallas_call(
        _combine_kernel, grid_spec=grid_spec,
        out_shape=jax.ShapeDtypeStruct((t, D_MODEL), _F32),
        compiler_params=_cparams("arbitrary"), name="moe_combine",
    )(pc, lo, go, units, eidx, gates, ys, h, ln_g.reshape(1, -1), ln_b.reshape(1, -1))


def _expert_kernel(blk_ref, e_ref, lo_ref, hi_ref, xs_ref, wu_ref, bu_ref, wd_ref, bd_ref,
                   ys_ref, wub_ref, wdb_ref):
    i = pl.program_id(0)
    e = e_ref[i]
    e_prev = e_ref[jnp.maximum(i - 1, 0)]

    @pl.when((i == 0) | (e != e_prev))
    def _():
        wub_ref[...] = wu_ref[...].astype(_BF16)
        wdb_ref[...] = wd_ref[...].astype(_BF16)

    lo = lo_ref[i]
    hi = hi_ref[i]

    @pl.when(hi < 0)
    def _():
        ys_ref[...] = jnp.zeros_like(ys_ref)

    @pl.when(hi > lo)
    def _():
        x_lo, x_hi = _unpack_pairs(xs_ref[...])
        h = (jnp.dot(x_lo, wub_ref[:PAIR_COLS, :], preferred_element_type=_F32)
             + jnp.dot(x_hi, wub_ref[PAIR_COLS:, :], preferred_element_type=_F32) + bu_ref[...])
        gate = jnp.minimum(h[:, :D_FF], SWIGLU_LIMIT)
        up = jnp.clip(h[:, D_FF:], -SWIGLU_LIMIT, SWIGLU_LIMIT)
        glu = gate * jax.nn.sigmoid(SWIGLU_ALPHA * gate)
        act = ((up + 1.0) * glu).astype(_BF16)
        y = jnp.dot(act, wdb_ref[...], preferred_element_type=_F32) + bd_ref[...]
        packed = _pack_pairs(y.astype(_BF16).astype(_F32))
        row = lax.broadcasted_iota(jnp.int32, packed.shape, 0)
        mine = (row >= lo) & (row < hi)

        @pl.when(lo == 0)
        def _():
            ys_ref[...] = jnp.where(mine, packed, jnp.uint32(0))

        @pl.when(lo > 0)
        def _():
            ys_ref[...] = jnp.where(mine, packed, ys_ref[...])


def _expert_schedule(offs, n_rows, bm):
    nblk = n_rows // bm
    n_items = nblk + N_EXPERTS - 1
    used = offs[-1]
    bnd = offs[1:-1]
    pos = jnp.arange(N_EXPERTS - 1, dtype=jnp.int32) + jnp.minimum(bnd // bm + 1, nblk)
    i = jnp.arange(n_items, dtype=jnp.int32)[:, None]
    at = pos[None, :] == i
    is_bnd = jnp.any(at, axis=1)
    bnd_val = jnp.sum(jnp.where(at, bnd[None, :], 0), axis=1)
    n_before = jnp.sum((pos[None, :] < i).astype(jnp.int32), axis=1)
    cuts = jnp.where(is_bnd, bnd_val, (i[:, 0] - n_before) * bm)
    ends = jnp.concatenate([cuts[1:], jnp.array([n_rows], jnp.int32)])
    last_blk = jnp.maximum(used - 1, 0) // bm
    past = jnp.logical_not(is_bnd) & (cuts // bm > last_blk)
    raw_blk = cuts // bm
    cuts = jnp.minimum(cuts, used)
    ends = jnp.minimum(ends, used)
    blk = jnp.minimum(cuts // bm, last_blk)
    e = jnp.minimum(jnp.sum((bnd[None, :] <= cuts[:, None]).astype(jnp.int32), axis=1), N_EXPERTS - 1)
    lo = jnp.where(past, 0, cuts - blk * bm)
    hi = jnp.where(past, -1, jnp.minimum(ends - blk * bm, bm))
    return lax.cummax(jnp.where(past, raw_blk, blk)), e, lo, hi


def moe_experts(xs, offs, layer, w_up, b_up, w_down, b_down):
    n_rows = xs.shape[0]
    bm = MOE_BM
    blk, e, lo, hi = _expert_schedule(offs, n_rows, bm)
    n_items = n_rows // bm + N_EXPERTS - 1
    wmap = lambda i, b, e, lo, hi: (layer, e[i], 0, 0)
    grid_spec = pltpu.PrefetchScalarGridSpec(
        num_scalar_prefetch=4, grid=(n_items,),
        in_specs=[pl.BlockSpec((bm, PAIR_COLS), lambda i, b, e, lo, hi: (b[i], 0)),
                  pl.BlockSpec((None, None, D_MODEL, 2 * D_FF), wmap),
                  pl.BlockSpec((None, None, 1, 2 * D_FF), wmap),
                  pl.BlockSpec((None, None, D_FF, D_MODEL), wmap),
                  pl.BlockSpec((None, None, 1, D_MODEL), wmap)],
        out_specs=pl.BlockSpec((bm, PAIR_COLS), lambda i, b, e, lo, hi: (b[i], 0)),
        scratch_shapes=[pltpu.VMEM((D_MODEL, 2 * D_FF), _BF16), pltpu.VMEM((D_FF, D_MODEL), _BF16)])
    nl = w_up.shape[0]
    return pl.pallas_call(
        _expert_kernel, grid_spec=grid_spec,
        out_shape=jax.ShapeDtypeStruct((n_rows, PAIR_COLS), _U32),
        compiler_params=_cparams("arbitrary"), name="moe_experts",
    )(blk, e, lo, hi, xs, w_up, b_up.reshape(nl, N_EXPERTS, 1, -1), w_down, b_down.reshape(nl, N_EXPERTS, 1, -1))


def moe_layer(h, eidx, gates, tile_counts, layer, w_up, b_up, w_down, b_down, ln_g, ln_b):
    tables = _seg_tables(tile_counts)
    xs = moe_dispatch(h, eidx, tables)
    ys = moe_experts(xs, tables[4], layer, w_up, b_up, w_down, b_down)
    return moe_combine(ys, eidx, gates, h, tables, ln_g, ln_b)


def kernel(x, positions, attn_w_in, attn_w_out, ssm_w_in, ssm_conv_w, ssm_conv_b, ssm_dt_bias,
           ssm_a_log, ssm_d, ssm_norm_w, ssm_w_out, moe_w_router, moe_b_router, moe_w_up,
           moe_b_up, moe_w_down, moe_b_down, ln_mix_g, ln_mix_b, ln_ffn_g, ln_ffn_b):
    bsz, seq, _ = x.shape
    xf = x.reshape(bsz * seq, D_MODEL)

    w_qkv = attn_w_in[0].astype(_BF16)
    outs = [dilated_group_attention(qkv_proj_group(xf, w_qkv, positions, g), g, bsz, seq)
            for g in range(N_GROUPS)]
    h, eidx, gates, tile_counts = attn_out_norm_route(
        [o for o, _ in outs], [l for _, l in outs], xf, attn_w_out[0].astype(_BF16),
        ln_mix_g[0], ln_mix_b[0], moe_w_router[0], moe_b_router[0])
    xf = moe_layer(h, eidx, gates, tile_counts, 0, moe_w_up, moe_b_up, moe_w_down, moe_b_down,
                   ln_ffn_g[0], ln_ffn_b[0])

    gz, xs, bc, dt = ssm_in_proj(xf, ssm_w_in[0], ssm_conv_w[0], ssm_conv_b[0], ssm_dt_bias[0], seq)
    yn = ssd_mixer(gz, xs, bc, dt, ssm_a_log[0], ssm_d[0], ssm_norm_w[0], bsz, seq)
    h, eidx, gates, tile_counts = ssm_out_norm_route(
        yn, xf, ssm_w_out[0].astype(_BF16), ln_mix_g[1], ln_mix_b[1], moe_w_router[1], moe_b_router[1])
    xf = moe_layer(h, eidx, gates, tile_counts, 1, moe_w_up, moe_b_up, moe_w_down, moe_b_down,
                   ln_ffn_g[1], ln_ffn_b[1])
    return xf.reshape(bsz, seq, D_MODEL)
```

```python
import functools
import math

import jax
import jax.numpy as jnp
from jax import lax
from jax.experimental import pallas as pl
from jax.experimental.pallas import tpu as pltpu

D_MODEL = 1024
DEPTH = 2
DIL = (1, 4, 16)
N_GROUPS = 3
SUB_WIN = 128
HEAD_DIM = 128
HEADS = 8
ROT_DIM = 32
ROT_HALF = 16
ROPE_THETA = 500000.0
ATTN_PROJ = N_GROUPS * 3 * D_MODEL
SSM_INNER = 2048
SSM_P = 64
SSM_HEADS = 32
SSM_GROUPS = 4
SSM_GHEADS = 8
SSM_N = 128
SSM_CONV = 4
CHUNK = 128
SSM_BC = SSM_GROUPS * SSM_N
SSM_XBC = SSM_INNER + 2 * SSM_BC
N_EXPERTS = 32
TOP_K = 4
D_FF = 1024
SWIGLU_LIMIT = 7.0
SWIGLU_ALPHA = 1.702
DEEPNORM_ALPHA = (2 * DEPTH) ** 0.25
LN_EPS = 1e-5
RMS_EPS = 1e-5

LANES = 128
SUBLANES = 8
VMEM_LIMIT_BYTES = 56 * 1024 * 1024

NEG_BIG = -1e30

_F32 = jnp.float32
_BF16 = jnp.bfloat16


def _cparams(*sem):
    return pltpu.CompilerParams(dimension_semantics=sem, vmem_limit_bytes=VMEM_LIMIT_BYTES)


def _rope_table_kernel(pos_ref, invf_ref, c_ref, sa_ref, sb_ref):
    pos = pos_ref[...].astype(_F32)
    ang = pos * invf_ref[...]
    lane = lax.broadcasted_iota(jnp.int32, ang.shape, 1)
    cos = jnp.cos(ang)
    sin = jnp.sin(ang)
    c_ref[...] = jnp.where(lane < ROT_DIM, cos, 1.0)
    sa_ref[...] = jnp.where((lane >= ROT_HALF) & (lane < ROT_DIM), sin, 0.0)
    sb_ref[...] = jnp.where(lane < ROT_HALF, -sin, 0.0)


def rope_tables(positions):
    t = positions.size
    tm = 1024
    pos = positions.reshape(t, 1)
    j = jnp.arange(LANES)
    invf = jnp.where(j < ROT_DIM,
                     ROPE_THETA ** (-(2.0 * (j % ROT_HALF)).astype(_F32) / ROT_DIM), 0.0)
    invf = invf.astype(_F32).reshape(1, LANES)
    out = jax.ShapeDtypeStruct((t, LANES), _F32)
    return pl.pallas_call(
        _rope_table_kernel,
        grid=(t // tm,),
        in_specs=[pl.BlockSpec((tm, 1), lambda i: (i, 0)),
                  pl.BlockSpec((1, LANES), lambda i: (0, 0))],
        out_specs=[pl.BlockSpec((tm, LANES), lambda i: (i, 0))] * 3,
        out_shape=[out, out, out],
        compiler_params=_cparams("parallel"),
        name="rope_tables",
    )(pos, invf)


QKV_TM = 512
ATTN_ROWS = 2048
HEADS_PER_STEP = 2


def _qkv_proj_kernel(x_ref, w_ref, c_ref, sa_ref, sb_ref, o_ref, *, d, ns, tt):
    tm = x_ref.shape[0]
    xb = x_ref[...].astype(_BF16)
    if d > 1:
        dst = lax.broadcasted_iota(jnp.int32, (tm, tm), 0)
        src = lax.broadcasted_iota(jnp.int32, (tm, tm), 1)
        lt, ld = tt.bit_length() - 1, d.bit_length() - 1
        t_ = dst & (tt - 1)
        r_ = lax.shift_right_logical(dst, lt) & (d - 1)
        s_ = lax.shift_right_logical(dst, lt + ld)
        perm = (src == ((s_ * tt + t_) * d + r_)).astype(_BF16)
        xb = jnp.dot(perm, xb, preferred_element_type=_F32).astype(_BF16)
    c = c_ref[...]
    sa = sa_ref[...]
    sb = sb_ref[...]
    for comp in range(3):
        acc = jnp.dot(xb, w_ref[:, comp * D_MODEL:(comp + 1) * D_MODEL], preferred_element_type=_F32)
        for h in range(HEADS):
            sl = slice(h * HEAD_DIM, (h + 1) * HEAD_DIM)
            t = acc[:, sl]
            if comp < 2:
                t = t * c + pltpu.roll(t, ROT_HALF, 1) * sa + pltpu.roll(t, HEAD_DIM - ROT_HALF, 1) * sb
            o_ref[:, :, :, comp * D_MODEL + h * HEAD_DIM:comp * D_MODEL + (h + 1) * HEAD_DIM] = (
                t.reshape(ns, d, tt, HEAD_DIM).astype(o_ref.dtype))


def qkv_proj_group(x, w_bf16, positions, g):
    t, k = x.shape
    d = DIL[g]
    span = SUB_WIN * d
    tm = QKV_TM
    ns = max(1, tm // span)
    tps = max(1, span // tm)
    tt = tm // (ns * d)
    rope = rope_tables(positions.reshape(-1, ns, tt, d).swapaxes(2, 3).reshape(-1))
    tab_spec = pl.BlockSpec((tm, LANES), lambda i: (i, 0))
    return pl.pallas_call(
        functools.partial(_qkv_proj_kernel, d=d, ns=ns, tt=tt),
        grid=(t // tm,),
        in_specs=[pl.BlockSpec((tm, k), lambda i: (i, 0)),
                  pl.BlockSpec((k, 3 * D_MODEL), lambda i: (0, g)),
                  tab_spec, tab_spec, tab_spec],
        out_specs=pl.BlockSpec((ns, d, tt, 3 * D_MODEL), lambda i: (i // tps, 0, i % tps, 0)),
        out_shape=jax.ShapeDtypeStruct((t // span, d, SUB_WIN, 3 * D_MODEL), _BF16),
        compiler_params=_cparams("parallel"), name=f"qkv_proj_g{g}",
    )(x, w_bf16, *rope)


def _dil_attn_kernel(q_ref, k_ref, v_ref, kp_ref, vp_ref, o_ref, lse_ref, *, d, ns):
    j = pl.program_id(1)
    hp = pl.program_id(2)
    nb = ns * d
    row = lax.broadcasted_iota(jnp.int32, (1, SUB_WIN, SUB_WIN), 1)
    col = lax.broadcasted_iota(jnp.int32, (1, SUB_WIN, SUB_WIN), 2)
    cur_ok = col <= row
    blk = lax.broadcasted_iota(jnp.int32, (nb, 1, 1), 0)
    prev_ok = (col >= row) & ((blk >= d) | (j > 0))
    lane = lax.broadcasted_iota(jnp.int32, (SUB_WIN, LANES), 1)
    scale = 1.0 / math.sqrt(HEAD_DIM)

    def block_rows(b):
        s, r = divmod(b, d)
        start = s * SUB_WIN * d + r
        return pl.ds(start, SUB_WIN, stride=d) if d > 1 else pl.ds(start, SUB_WIN)

    @pl.when(hp == 0)
    def _():
        lse_ref[...] = jnp.zeros_like(lse_ref)

    lses = []
    for hh in range(HEADS_PER_STEP):
        sl = slice(hh * HEAD_DIM, (hh + 1) * HEAD_DIM)
        flat = lambda a: a.reshape(nb, SUB_WIN, HEAD_DIM)
        q, kc, vc = flat(q_ref[:, :, :, sl]), flat(k_ref[:, :, :, sl]), flat(v_ref[:, :, :, sl])
        if ns > 1:
            kp = flat(jnp.concatenate([kp_ref[:, :, :, sl], k_ref[:ns - 1, :, :, sl]], axis=0))
            vp = flat(jnp.concatenate([vp_ref[:, :, :, sl], v_ref[:ns - 1, :, :, sl]], axis=0))
        else:
            kp, vp = flat(kp_ref[:, :, :, sl]), flat(vp_ref[:, :, :, sl])
        sc = jnp.einsum('bqd,bkd->bqk', q, kc, preferred_element_type=_F32) * scale
        sp = jnp.einsum('bqd,bkd->bqk', q, kp, preferred_element_type=_F32) * scale
        sc = jnp.where(cur_ok, sc, NEG_BIG)
        sp = jnp.where(prev_ok, sp, NEG_BIG)
        m = jnp.maximum(jnp.max(sc, axis=2, keepdims=True), jnp.max(sp, axis=2, keepdims=True))
        pc = jnp.exp(sc - m)
        pp = jnp.exp(sp - m)
        l = jnp.sum(pc, axis=2, keepdims=True) + jnp.sum(pp, axis=2, keepdims=True)
        acc = jnp.einsum('bqk,bkd->bqd', pc.astype(_BF16), vc, preferred_element_type=_F32)
        acc += jnp.einsum('bqk,bkd->bqd', pp.astype(_BF16), vp, preferred_element_type=_F32)
        o = acc / l
        for b in range(nb):
            o_ref[hh, block_rows(b), :] = o[b]
        lses.append(m + jnp.log(l))
    for b in range(nb):
        cur = lse_ref[block_rows(b), :]
        for hh in range(HEADS_PER_STEP):
            cur = jnp.where(lane == hp * HEADS_PER_STEP + hh, lses[hh][b], cur)
        lse_ref[block_rows(b), :] = cur


def dilated_group_attention(qkv, g, bsz, seq):
    d = DIL[g]
    ns = ATTN_ROWS // (SUB_WIN * d)
    nsteps = seq // ATTN_ROWS
    hw = HEADS_PER_STEP * HEAD_DIM
    ncol = D_MODEL // hw
    cur = lambda comp: pl.BlockSpec((ns, d, SUB_WIN, hw), lambda b, j, hp: (b * nsteps + j, 0, 0, comp * ncol + hp))
    prev = lambda comp: pl.BlockSpec(
        (1, d, SUB_WIN, hw), lambda b, j, hp: (jnp.maximum((b * nsteps + j) * ns - 1, 0), 0, 0, comp * ncol + hp))
    return pl.pallas_call(
        functools.partial(_dil_attn_kernel, d=d, ns=ns),
        grid=(bsz, nsteps, HEADS // HEADS_PER_STEP),
        in_specs=[cur(0), cur(1), cur(2), prev(1), prev(2)],
        out_specs=[pl.BlockSpec((HEADS_PER_STEP, ATTN_ROWS, HEAD_DIM), lambda b, j, hp: (hp, b * nsteps + j, 0)),
                   pl.BlockSpec((ATTN_ROWS, LANES), lambda b, j, hp: (b * nsteps + j, 0))],
        out_shape=[jax.ShapeDtypeStruct((HEADS, bsz * seq, HEAD_DIM), _F32),
                   jax.ShapeDtypeStruct((bsz * seq, LANES), _F32)],
        compiler_params=_cparams("parallel", "parallel", "arbitrary"),
        name=f"dilated_attn_g{g}",
    )(qkv, qkv, qkv, qkv, qkv)


SSM_IN_TM = 512
HALO = SUBLANES
ZX_COLS = SSM_INNER + SSM_XBC


def _softplus(x):
    return jnp.maximum(x, 0.0) + jnp.log1p(jnp.exp(-jnp.abs(x)))


def _ssm_in_kernel(x_ref, w_ref, wdt_ref, cw_ref, cb_ref, dtb_ref, gz_ref, xs_ref, bc_ref, dt_ref,
                   carry_ref, *, tiles_per_seq):
    tm = x_ref.shape[0]
    cw = D_MODEL
    xb = x_ref[...].astype(_BF16)

    @pl.when(pl.program_id(0) % tiles_per_seq == 0)
    def _():
        carry_ref[...] = jnp.zeros_like(carry_ref)

    for i in range(SSM_INNER // cw):
        z = jnp.dot(xb, w_ref[:, i * cw:(i + 1) * cw], preferred_element_type=_F32)
        gz_ref[:, i * cw:(i + 1) * cw] = z * jax.nn.sigmoid(z)
    for c in range(SSM_XBC // cw):
        cols = slice(c * cw, (c + 1) * cw)
        acc = jnp.dot(xb, w_ref[:, SSM_INNER + c * cw:SSM_INNER + (c + 1) * cw], preferred_element_type=_F32)
        ext = jnp.concatenate([carry_ref[c], acc], axis=0)
        carry_ref[c] = acc[tm - HALO:, :]
        conv = cb_ref[:, cols] + cw_ref[SSM_CONV - 1:SSM_CONV, cols] * acc
        for k in range(1, SSM_CONV):
            conv = conv + cw_ref[SSM_CONV - 1 - k:SSM_CONV - k, cols] * pltpu.roll(ext, k, 0)[HALO:, :]
        act = conv * jax.nn.sigmoid(conv)
        if (c + 1) * cw <= SSM_INNER:
            xs_ref[:, cols] = act
        else:
            bc_ref[...] = act.astype(_BF16)
    dt_raw = jnp.dot(xb, wdt_ref[...], preferred_element_type=_F32)
    dt_ref[...] = _softplus(dt_raw + dtb_ref[...])


def ssm_in_proj(x, w_in, conv_w, conv_b, dt_bias, seq):
    t, k = x.shape
    tm = SSM_IN_TM
    assert 2 * SSM_BC == D_MODEL
    pad = LANES - SSM_HEADS
    w_main = w_in[:, :ZX_COLS].astype(_BF16)
    w_dt = jnp.pad(w_in[:, ZX_COLS:], ((0, 0), (0, pad))).astype(_BF16)
    dtb = jnp.pad(dt_bias, (0, pad)).reshape(1, LANES)
    cst = lambda shape: pl.BlockSpec(shape, lambda i: (0,) * len(shape))
    once = lambda shape: pl.BlockSpec(shape, lambda i: (0,) * len(shape), pipeline_mode=pl.Buffered(1))
    row = lambda w: pl.BlockSpec((tm, w), lambda i: (i, 0))
    return pl.pallas_call(
        functools.partial(_ssm_in_kernel, tiles_per_seq=seq // tm),
        grid=(t // tm,),
        in_specs=[row(k), once((k, ZX_COLS)), once((k, LANES)),
                  cst((SSM_CONV, SSM_XBC)), cst((1, SSM_XBC)), cst((1, LANES))],
        out_specs=[row(SSM_INNER), row(SSM_INNER), row(2 * SSM_BC), row(LANES)],
        out_shape=[jax.ShapeDtypeStruct((t, SSM_INNER), _F32), jax.ShapeDtypeStruct((t, SSM_INNER), _F32),
                   jax.ShapeDtypeStruct((t, 2 * SSM_BC), _BF16), jax.ShapeDtypeStruct((t, LANES), _F32)],
        scratch_shapes=[pltpu.VMEM((SSM_XBC // D_MODEL, HALO, D_MODEL), _F32)],
        compiler_params=_cparams("arbitrary"), name="ssm_in_proj",
    )(x, w_main, w_dt, conv_w, conv_b.reshape(1, -1), dtb)


def _ssd_kernel(gz_ref, xs_ref, bc_ref, dt_ref, a_ref, dsk_ref, nw_ref, o_ref, state_ref, y_ref):
    c = pl.program_id(1)
    q = CHUNK

    @pl.when(c == 0)
    def _():
        state_ref[...] = jnp.zeros_like(state_ref)

    dt = dt_ref[...]
    da = dt * a_ref[...]
    r_io = lax.broadcasted_iota(jnp.int32, (q, q), 0)
    c_io = lax.broadcasted_iota(jnp.int32, (q, q), 1)
    causal = r_io >= c_io
    tri = causal.astype(_F32)
    a_cum = jnp.dot(tri, da, preferred_element_type=_F32, precision=lax.Precision.HIGHEST)
    a_cum_t = a_cum.T
    dt_t = dt.T
    a_last = a_cum[q - 1:q, :]
    w_state = jnp.exp(a_last - a_cum) * dt
    e_acum = jnp.exp(a_cum)
    chunk_decay = jnp.exp(a_last)
    lane = lax.broadcasted_iota(jnp.int32, (q, LANES), 1)
    low = lane < SSM_P
    lane1 = lax.broadcasted_iota(jnp.int32, (1, LANES), 1)
    low1 = lane1 < SSM_P
    tn = (((0,), (0,)), ((), ()))
    nt = (((1,), (1,)), ((), ()))

    for g in range(SSM_GROUPS):
        bg = bc_ref[:, g * SSM_N:(g + 1) * SSM_N]
        cg = bc_ref[:, SSM_BC + g * SSM_N:SSM_BC + (g + 1) * SSM_N]
        cbm = lax.dot_general(cg, bg, nt, preferred_element_type=_F32)
        st = state_ref[g]
        y_off = jnp.dot(cg, st.astype(_BF16), preferred_element_type=_F32)
        xw_parts, dec_parts = [], []
        for j in range(SSM_GHEADS // 2):
            h0 = g * SSM_GHEADS + 2 * j
            col = (g * SSM_GHEADS // 2 + j) * LANES
            xp = xs_ref[:, col:col + LANES]
            y_diag = jnp.zeros((q, LANES), _F32)
            for half, hh in enumerate((h0, h0 + 1)):
                diff = a_cum[:, hh:hh + 1] - a_cum_t[hh:hh + 1, :]
                lmat = jnp.exp(jnp.where(causal, diff, NEG_BIG))
                mm = (cbm * lmat * dt_t[hh:hh + 1, :]).astype(_BF16)
                keep = low if half == 0 else jnp.logical_not(low)
                xh = jnp.where(keep, xp, 0.0).astype(_BF16)
                y_diag = y_diag + jnp.dot(mm, xh, preferred_element_type=_F32)
            e_pair = jnp.where(low, e_acum[:, h0:h0 + 1], e_acum[:, h0 + 1:h0 + 2])
            w_pair = jnp.where(low, w_state[:, h0:h0 + 1], w_state[:, h0 + 1:h0 + 2])
            y_ref[:, col:col + LANES] = (y_diag + y_off[:, j * LANES:(j + 1) * LANES] * e_pair
                                         + xp * dsk_ref[:, col:col + LANES])
            xw_parts.append((xp * w_pair).astype(_BF16))
            dec_parts.append(jnp.where(low1, chunk_decay[:, h0:h0 + 1], chunk_decay[:, h0 + 1:h0 + 2]))
        xw = jnp.concatenate(xw_parts, axis=1)
        dec = jnp.concatenate(dec_parts, axis=1)
        state_ref[g] = st * dec + lax.dot_general(bg, xw, tn, preferred_element_type=_F32)

    yg = y_ref[...] * gz_ref[...]
    gw = SSM_INNER // SSM_GROUPS
    for g in range(SSM_GROUPS):
        v = yg[:, g * gw:(g + 1) * gw]
        ms = jnp.mean(v * v, axis=-1, keepdims=True)
        o_ref[:, g * gw:(g + 1) * gw] = (v * lax.rsqrt(ms + RMS_EPS)
                                         * nw_ref[:, g * gw:(g + 1) * gw]).astype(o_ref.dtype)


def ssd_mixer(gz, xs, bc, dt, a_log, d_skip, norm_w, bsz, seq):
    nc = seq // CHUNK
    a = jnp.pad(-jnp.exp(a_log.astype(_F32)), (0, LANES - SSM_HEADS)).reshape(1, LANES)
    dsk = jnp.repeat(d_skip.astype(_F32), SSM_P).reshape(1, SSM_INNER)
    cst = lambda shape: pl.BlockSpec(shape, lambda b, c: (0,) * len(shape))
    row = lambda w: pl.BlockSpec((CHUNK, w), lambda b, c: (b * nc + c, 0))
    return pl.pallas_call(
        _ssd_kernel,
        grid=(bsz, nc),
        in_specs=[row(SSM_INNER), row(SSM_INNER), row(2 * SSM_BC), row(LANES),
                  cst((1, LANES)), cst((1, SSM_INNER)), cst((1, SSM_INNER))],
        out_specs=row(SSM_INNER),
        out_shape=jax.ShapeDtypeStruct((bsz * seq, SSM_INNER), _BF16),
        scratch_shapes=[pltpu.VMEM((SSM_GROUPS, SSM_N, SSM_GHEADS * SSM_P), _F32),
                        pltpu.VMEM((CHUNK, SSM_INNER), _F32)],
        compiler_params=_cparams("parallel", "arbitrary"), name="ssd_mixer",
    )(gz, xs, bc, dt, a, dsk, norm_w.reshape(1, -1))


ROUTE_SUB = 256
ROUTE_TM_ATTN = 512
ROUTE_TM_SSM = 1024


def _layer_norm(y, g, b):
    mu = jnp.mean(y, axis=-1, keepdims=True)
    yc = y - mu
    var = jnp.mean(yc * yc, axis=-1, keepdims=True)
    return yc * lax.rsqrt(var + LN_EPS) * g + b


def _route(h1s, wrt_ref, br_ref, eidx_ref, gate_ref, tcnt_ref):
    sub = ROUTE_SUB
    n = len(h1s)
    nt = (((1,), (1,)), ((), ()))
    wrt = wrt_ref[...]
    vals = [lax.dot_general(wrt, h, nt, preferred_element_type=_F32, precision=lax.Precision.HIGHEST)
            + br_ref[...] for h in h1s]
    eio = lax.broadcasted_iota(jnp.int32, (N_EXPERTS, sub), 0)
    tops, ids, hots = ([[] for _ in range(n)] for _ in range(3))
    for _ in range(TOP_K):
        for a in range(n):
            mk = jnp.max(vals[a], axis=0, keepdims=True)
            ik = jnp.min(jnp.where(vals[a] == mk, eio, N_EXPERTS), axis=0, keepdims=True)
            hot = eio == ik
            vals[a] = jnp.where(hot, -jnp.inf, vals[a])
            tops[a].append(mk)
            ids[a].append(ik)
            hots[a].append(hot)
    col = lax.broadcasted_iota(jnp.int32, tcnt_ref.shape, 1)
    tcnt = tcnt_ref[...]
    for a in range(n):
        sl = slice(a * sub, (a + 1) * sub)
        exps = [jnp.exp(t - tops[a][0]) for t in tops[a]]
        den = exps[0] + exps[1] + exps[2] + exps[3]
        for k in range(TOP_K):
            eidx_ref[k:k + 1, sl] = ids[a][k]
            gate_ref[k:k + 1, sl] = exps[k] / den
        hot_all = hots[a][0] | hots[a][1] | hots[a][2] | hots[a][3]
        counts = jnp.sum(hot_all.astype(_F32), axis=1, keepdims=True)
        tcnt = jnp.where(col == pl.program_id(0) * n + a, counts, tcnt)
    tcnt_ref[...] = tcnt


def _norm_and_route(mixes, x_ref, g_ref, b_ref, wrt_ref, br_ref, h_ref, eidx_ref, gate_ref, tcnt_ref):
    sub = ROUTE_SUB

    @pl.when(pl.program_id(0) == 0)
    def _():
        tcnt_ref[...] = jnp.zeros_like(tcnt_ref)

    h1s = [_layer_norm(DEEPNORM_ALPHA * x_ref[a * sub:(a + 1) * sub, :] + mix, g_ref[...], b_ref[...])
           for a, mix in enumerate(mixes)]
    for a, h1 in enumerate(h1s):
        h_ref[a * sub:(a + 1) * sub, :] = h1
    _route(h1s, wrt_ref, br_ref, eidx_ref, gate_ref, tcnt_ref)


def _attn_out_kernel(o0_ref, o1_ref, o2_ref, l0_ref, l1_ref, l2_ref, x_ref, wo_ref, g_ref, b_ref,
                     wrt_ref, br_ref, h_ref, eidx_ref, gate_ref, tcnt_ref, ob_ref):
    sub = ROUTE_SUB
    l0, l1, l2 = l0_ref[...], l1_ref[...], l2_ref[...]
    m = jnp.maximum(jnp.maximum(l0, l1), l2)
    e0, e1, e2 = jnp.exp(l0 - m), jnp.exp(l1 - m), jnp.exp(l2 - m)
    inv = 1.0 / (e0 + e1 + e2)
    w0, w1, w2 = e0 * inv, e1 * inv, e2 * inv
    for h in range(HEADS):
        o = w0[:, h:h + 1] * o0_ref[h] + w1[:, h:h + 1] * o1_ref[h] + w2[:, h:h + 1] * o2_ref[h]
        ob_ref[:, h * HEAD_DIM:(h + 1) * HEAD_DIM] = o.astype(_BF16)
    mixes = [jnp.dot(ob_ref[a * sub:(a + 1) * sub, :], wo_ref[...], preferred_element_type=_F32)
             for a in range(x_ref.shape[0] // sub)]
    _norm_and_route(mixes, x_ref, g_ref, b_ref, wrt_ref, br_ref, h_ref, eidx_ref, gate_ref, tcnt_ref)


def _ssm_out_kernel(y_ref, x_ref, wo_ref, g_ref, b_ref, wrt_ref, br_ref, h_ref, eidx_ref, gate_ref, tcnt_ref):
    sub = ROUTE_SUB
    mixes = [jnp.dot(y_ref[a * sub:(a + 1) * sub, :], wo_ref[...], preferred_element_type=_F32)
             for a in range(x_ref.shape[0] // sub)]
    _norm_and_route(mixes, x_ref, g_ref, b_ref, wrt_ref, br_ref, h_ref, eidx_ref, gate_ref, tcnt_ref)


def _route_out_specs(t, tm):
    nsub = t // ROUTE_SUB
    specs = [pl.BlockSpec((tm, D_MODEL), lambda i: (i, 0)),
             pl.BlockSpec((TOP_K, tm), lambda i: (0, i)),
             pl.BlockSpec((TOP_K, tm), lambda i: (0, i)),
             pl.BlockSpec((N_EXPERTS, nsub), lambda i: (0, 0))]
    shapes = [jax.ShapeDtypeStruct((t, D_MODEL), _F32),
              jax.ShapeDtypeStruct((TOP_K, t), jnp.int32),
              jax.ShapeDtypeStruct((TOP_K, t), _F32),
              jax.ShapeDtypeStruct((N_EXPERTS, nsub), _F32)]
    return specs, shapes


def _const_spec(shape):
    return pl.BlockSpec(shape, lambda i: (0,) * len(shape))


def attn_out_norm_route(os_, lses, x, wo_bf16, ln_g, ln_b, w_router, b_router):
    t = x.shape[0]
    tm = ROUTE_TM_ATTN
    row = lambda w: pl.BlockSpec((tm, w), lambda i: (i, 0))
    out_specs, out_shapes = _route_out_specs(t, tm)
    return pl.pallas_call(
        _attn_out_kernel,
        grid=(t // tm,),
        in_specs=[pl.BlockSpec((HEADS, tm, HEAD_DIM), lambda i: (0, i, 0))] * 3 + [row(LANES)] * 3 + [row(D_MODEL),
                  _const_spec((D_MODEL, D_MODEL)), _const_spec((1, D_MODEL)), _const_spec((1, D_MODEL)),
                  _const_spec((N_EXPERTS, D_MODEL)), _const_spec((N_EXPERTS, 1))],
        out_specs=out_specs, out_shape=out_shapes,
        scratch_shapes=[pltpu.VMEM((tm, D_MODEL), _BF16)],
        compiler_params=_cparams("arbitrary"), name="attn_out_norm_route",
    )(*os_, *lses, x, wo_bf16, ln_g.reshape(1, -1), ln_b.reshape(1, -1),
      w_router.T, b_router.reshape(-1, 1))


def ssm_out_norm_route(yn, x, wo_bf16, ln_g, ln_b, w_router, b_router):
    t = x.shape[0]
    tm = ROUTE_TM_SSM
    row = lambda w: pl.BlockSpec((tm, w), lambda i: (i, 0))
    out_specs, out_shapes = _route_out_specs(t, tm)
    return pl.pallas_call(
        _ssm_out_kernel,
        grid=(t // tm,),
        in_specs=[row(SSM_INNER), row(D_MODEL),
                  _const_spec((SSM_INNER, D_MODEL)), _const_spec((1, D_MODEL)), _const_spec((1, D_MODEL)),
                  _const_spec((N_EXPERTS, D_MODEL)), _const_spec((N_EXPERTS, 1))],
        out_specs=out_specs, out_shape=out_shapes,
        compiler_params=_cparams("arbitrary"), name="ssm_out_norm_route",
    )(yn, x, wo_bf16, ln_g.reshape(1, -1), ln_b.reshape(1, -1), w_router.T, b_router.reshape(-1, 1))


MOE_BM = 256
MOE_SUB = ROUTE_SUB
SEG_ALIGN = SUBLANES
LOCAL_ROWS = MOE_SUB * TOP_K + N_EXPERTS * SUBLANES
SEG_UNIT = 16
SEG_REST = (8,)
SEG_SIZES = (256, 128, 64, 32, 16, 8)
PAIR_COLS = D_MODEL // 2
_U32 = jnp.uint32


def _pack_pairs(x):
    lo = lax.shift_right_logical(lax.bitcast_convert_type(x[:, :PAIR_COLS], _U32), jnp.uint32(16))
    hi = lax.bitcast_convert_type(x[:, PAIR_COLS:], _U32) & jnp.uint32(0xFFFF0000)
    return hi | lo


def _unpack_pairs(u):
    lo = lax.bitcast_convert_type(lax.shift_left(u, jnp.uint32(16)), _F32).astype(_BF16)
    hi = lax.bitcast_convert_type(u & jnp.uint32(0xFFFF0000), _F32).astype(_BF16)
    return lo, hi


def _local_positions(eidx):
    sub = eidx.shape[1]
    eio = lax.broadcasted_iota(jnp.int32, (N_EXPERTS, sub), 0)
    hots = [eio == eidx[k:k + 1, :] for k in range(TOP_K)]
    hot_all = (hots[0] | hots[1] | hots[2] | hots[3]).astype(_BF16)
    r_io = lax.broadcasted_iota(jnp.int32, (sub, sub), 0)
    c_io = lax.broadcasted_iota(jnp.int32, (sub, sub), 1)
    upper = (r_io < c_io).astype(_BF16)
    prefix = jnp.dot(hot_all, upper, preferred_element_type=_F32)
    counts = jnp.sum(hot_all.astype(_F32), axis=1, keepdims=True).astype(jnp.int32)
    padded = ((counts + (SEG_ALIGN - 1)) & (-SEG_ALIGN)).astype(_F32)
    e_r = lax.broadcasted_iota(jnp.int32, (N_EXPERTS, N_EXPERTS), 0)
    e_c = lax.broadcasted_iota(jnp.int32, (N_EXPERTS, N_EXPERTS), 1)
    below = (e_c < e_r).astype(_BF16)
    seg_start = jnp.dot(below, jnp.broadcast_to(padded, (N_EXPERTS, sub)).astype(_BF16),
                        preferred_element_type=_F32)
    base = seg_start + prefix
    return [jnp.sum(jnp.where(h, base, 0.0), axis=0, keepdims=True) for h in hots]


def _segment_copies(cnt_ref, lo_ref, go_ref, i, copy_fn):
    for e in range(N_EXPERTS):
        n = cnt_ref[i * N_EXPERTS + e]
        lo = lo_ref[i * N_EXPERTS + e]
        go = go_ref[i * N_EXPERTS + e]

        def unit(u, c, lo=lo, go=go):
            copy_fn(pl.multiple_of(lo + u * SEG_UNIT, SEG_ALIGN), pl.multiple_of(go + u * SEG_UNIT, SEG_ALIGN),
                    SEG_UNIT)
            return c

        lax.fori_loop(0, n // SEG_UNIT, unit, 0)
        for sz in SEG_REST:
            done = n & -(2 * sz)

            @pl.when((n & sz) != 0)
            def _(done=done, sz=sz, lo=lo, go=go):
                copy_fn(pl.multiple_of(lo + done, SEG_ALIGN), pl.multiple_of(go + done, SEG_ALIGN), sz)


def _dispatch_kernel(cnt_ref, lo_ref, go_ref, units_ref, tail_ref, eidx_ref, h_ref, xs_ref, buf_ref, sem):
    i = pl.program_id(0)
    slot = i % 2
    mine, other = buf_ref.at[slot], buf_ref.at[1 - slot]
    lpos = _local_positions(eidx_ref[...])
    p_io = lax.broadcasted_iota(jnp.int32, (LOCAL_ROWS, MOE_SUB), 0).astype(_F32)
    sel = (p_io == lpos[0]) | (p_io == lpos[1]) | (p_io == lpos[2]) | (p_io == lpos[3])
    mine[...] = _pack_pairs(jnp.dot(sel.astype(_BF16), h_ref[...].astype(_BF16), preferred_element_type=_F32))

    def copy(lo, go, sz):
        pltpu.make_async_copy(mine.at[pl.ds(lo, sz)], xs_ref.at[pl.ds(go, sz)], sem.at[slot]).start()

    _segment_copies(cnt_ref, lo_ref, go_ref, i, copy)

    def wait_units(buf, s, n_units):
        def body(j, c):
            pltpu.make_async_copy(buf.at[pl.ds(0, SEG_ALIGN)], xs_ref.at[pl.ds(0, SEG_ALIGN)], sem.at[s]).wait()
            return c
        lax.fori_loop(0, n_units, body, 0)

    @pl.when(i > 0)
    def _():
        wait_units(other, 1 - slot, units_ref[jnp.maximum(i - 1, 0)])

    @pl.when(i == pl.num_programs(0) - 1)
    def _():
        wait_units(mine, slot, units_ref[i])
        mine[0:MOE_BM, :] = jnp.zeros((MOE_BM, PAIR_COLS), _U32)
        start, n, n_blocks = tail_ref[0], tail_ref[1], tail_ref[2]
        done = jnp.int32(0)
        for sz in SEG_SIZES[1:]:
            @pl.when((n & sz) != 0)
            def _(done=done, sz=sz):
                pltpu.make_async_copy(mine.at[pl.ds(0, sz)],
                                      xs_ref.at[pl.ds(pl.multiple_of(start + done, SEG_ALIGN), sz)],
                                      sem.at[slot]).start()
            done = done + (n & sz)

        def zero_block(j, c):
            row0 = pl.multiple_of(start + n + j * MOE_BM, MOE_BM)
            pltpu.make_async_copy(mine.at[pl.ds(0, MOE_BM)], xs_ref.at[pl.ds(row0, MOE_BM)], sem.at[slot]).start()
            return c

        lax.fori_loop(0, n_blocks, zero_block, 0)
        wait_units(mine, slot, n // SEG_ALIGN + n_blocks * (MOE_BM // SEG_ALIGN))


def _seg_tables(tile_counts):
    cnt = tile_counts.T.astype(jnp.int32)
    pc = (cnt + (SEG_ALIGN - 1)) & (-SEG_ALIGN)
    lo = jnp.cumsum(pc, axis=1) - pc
    tot = jnp.sum(pc, axis=0)
    offs = jnp.concatenate([jnp.zeros((1,), jnp.int32), jnp.cumsum(tot)])
    go = offs[None, :-1] + jnp.cumsum(pc, axis=0) - pc
    units = jnp.sum(pc, axis=1) // SEG_ALIGN
    return pc.reshape(-1), lo.reshape(-1), go.reshape(-1), units.astype(jnp.int32), offs.astype(jnp.int32)


def _max_rows(t):
    n = t * TOP_K + (t // MOE_SUB) * N_EXPERTS * (SEG_ALIGN - 1)
    return -(-n // MOE_BM) * MOE_BM


def moe_dispatch(h, eidx, tables):
    t = h.shape[0]
    pc, lo, go, units, offs = tables
    used = offs[-1]
    partial = (-used) % MOE_BM
    tail = jnp.stack([used, partial, (_max_rows(t) - used - partial) // MOE_BM]).astype(jnp.int32)
    grid_spec = pltpu.PrefetchScalarGridSpec(
        num_scalar_prefetch=5, grid=(t // MOE_SUB,),
        in_specs=[pl.BlockSpec((TOP_K, MOE_SUB), lambda i, *_: (0, i)),
                  pl.BlockSpec((MOE_SUB, D_MODEL), lambda i, *_: (i, 0))],
        out_specs=pl.BlockSpec(memory_space=pl.ANY),
        scratch_shapes=[pltpu.VMEM((2, LOCAL_ROWS, PAIR_COLS), _U32), pltpu.SemaphoreType.DMA((2,))])
    return pl.pallas_call(
        _dispatch_kernel, grid_spec=grid_spec,
        out_shape=jax.ShapeDtypeStruct((_max_rows(t), PAIR_COLS), _U32),
        compiler_params=_cparams("arbitrary"), name="moe_dispatch",
    )(pc, lo, go, units, tail, eidx, h)


def _combine_kernel(cnt_ref, lo_ref, go_ref, units_ref, eidx_ref, gate_ref, ys_ref, h_ref, g_ref, b_ref,
                    o_ref, buf_ref, sem):
    i = pl.program_id(0)
    slot = i % 2

    def fetch(step, s):
        dst = buf_ref.at[s]
        dst[MOE_SUB * TOP_K:, :] = jnp.zeros((LOCAL_ROWS - MOE_SUB * TOP_K, PAIR_COLS), _U32)

        def copy(lo, go, sz):
            pltpu.make_async_copy(ys_ref.at[pl.ds(go, sz)], dst.at[pl.ds(lo, sz)], sem.at[s]).start()

        _segment_copies(cnt_ref, lo_ref, go_ref, step, copy)

    @pl.when(i == 0)
    def _():
        fetch(i, slot)

    @pl.when(i + 1 < pl.num_programs(0))
    def _():
        fetch(i + 1, 1 - slot)

    mine = buf_ref.at[slot]
    lpos = _local_positions(eidx_ref[...])
    gates = gate_ref[...]
    p_io = lax.broadcasted_iota(jnp.int32, (LOCAL_ROWS, MOE_SUB), 0).astype(_F32)
    w = jnp.zeros((LOCAL_ROWS, MOE_SUB), _F32)
    for k in range(TOP_K):
        w = w + jnp.where(p_io == lpos[k], gates[k:k + 1, :], 0.0)
    w_hi = w.astype(_BF16)
    w_lo = (w - w_hi.astype(_F32)).astype(_BF16)

    def wait_unit(j, c):
        pltpu.make_async_copy(ys_ref.at[pl.ds(0, SEG_ALIGN)], mine.at[pl.ds(0, SEG_ALIGN)], sem.at[slot]).wait()
        return c

    lax.fori_loop(0, units_ref[i], wait_unit, 0)
    tn = (((0,), (0,)), ((), ()))
    ffn = jnp.concatenate(
        [lax.dot_general(w_hi, yb, tn, preferred_element_type=_F32)
         + lax.dot_general(w_lo, yb, tn, preferred_element_type=_F32) for yb in _unpack_pairs(mine[...])],
        axis=1)
    o_ref[...] = _layer_norm(DEEPNORM_ALPHA * h_ref[...] + ffn, g_ref[...], b_ref[...])


def moe_combine(ys, eidx, gates, h, tables, ln_g, ln_b):
    t = h.shape[0]
    pc, lo, go, units, _ = tables
    row = pl.BlockSpec((MOE_SUB, D_MODEL), lambda i, *_: (i, 0))
    kt = pl.BlockSpec((TOP_K, MOE_SUB), lambda i, *_: (0, i))
    cst = lambda shape: pl.BlockSpec(shape, lambda i, *_: (0,) * len(shape))
    grid_spec = pltpu.PrefetchScalarGridSpec(
        num_scalar_prefetch=4, grid=(t // MOE_SUB,),
        in_specs=[kt, kt, pl.BlockSpec(memory_space=pl.ANY), row, cst((1, D_MODEL)), cst((1, D_MODEL))],
        out_specs=row,
        scratch_shapes=[pltpu.VMEM((2, LOCAL_ROWS, PAIR_COLS), _U32), pltpu.SemaphoreType.DMA((2,))])
    return pl.pallas_call(
        _combine_kernel, grid_spec=grid_spec,
        out_shape=jax.ShapeDtypeStruct((t, D_MODEL), _F32),
        compiler_params=_cparams("arbitrary"), name="moe_combine",
    )(pc, lo, go, units, eidx, gates, ys, h, ln_g.reshape(1, -1), ln_b.reshape(1, -1))


def _expert_kernel(blk_ref, e_ref, lo_ref, hi_ref, xs_ref, wu_ref, bu_ref, wd_ref, bd_ref,
                   ys_ref, wub_ref, wdb_ref):
    i = pl.program_id(0)
    e = e_ref[i]
    e_prev = e_ref[jnp.maximum(i - 1, 0)]

    @pl.when((i == 0) | (e != e_prev))
    def _():
        wub_ref[...] = wu_ref[...].astype(_BF16)
        wdb_ref[...] = wd_ref[...].astype(_BF16)

    lo = lo_ref[i]
    hi = hi_ref[i]

    @pl.when(hi < 0)
    def _():
        ys_ref[...] = jnp.zeros_like(ys_ref)

    @pl.when(hi > lo)
    def _():
        x_lo, x_hi = _unpack_pairs(xs_ref[...])
        h = (jnp.dot(x_lo, wub_ref[:PAIR_COLS, :], preferred_element_type=_F32)
             + jnp.dot(x_hi, wub_ref[PAIR_COLS:, :], preferred_element_type=_F32) + bu_ref[...])
        gate = jnp.minimum(h[:, :D_FF], SWIGLU_LIMIT)
        up = jnp.clip(h[:, D_FF:], -SWIGLU_LIMIT, SWIGLU_LIMIT)
        glu = gate * jax.nn.sigmoid(SWIGLU_ALPHA * gate)
        act = ((up + 1.0) * glu).astype(_BF16)
        y = jnp.dot(act, wdb_ref[...], preferred_element_type=_F32) + bd_ref[...]
        packed = _pack_pairs(y.astype(_BF16).astype(_F32))
        row = lax.broadcasted_iota(jnp.int32, packed.shape, 0)
        mine = (row >= lo) & (row < hi)

        @pl.when(lo == 0)
        def _():
            ys_ref[...] = jnp.where(mine, packed, jnp.uint32(0))

        @pl.when(lo > 0)
        def _():
            ys_ref[...] = jnp.where(mine, packed, ys_ref[...])


def _expert_schedule(offs, n_rows, bm):
    nblk = n_rows // bm
    n_items = nblk + N_EXPERTS - 1
    used = offs[-1]
    bnd = offs[1:-1]
    pos = jnp.arange(N_EXPERTS - 1, dtype=jnp.int32) + jnp.minimum(bnd // bm + 1, nblk)
    i = jnp.arange(n_items, dtype=jnp.int32)[:, None]
    at = pos[None, :] == i
    is_bnd = jnp.any(at, axis=1)
    bnd_val = jnp.sum(jnp.where(at, bnd[None, :], 0), axis=1)
    n_before = jnp.sum((pos[None, :] < i).astype(jnp.int32), axis=1)
    cuts = jnp.where(is_bnd, bnd_val, (i[:, 0] - n_before) * bm)
    ends = jnp.concatenate([cuts[1:], jnp.array([n_rows], jnp.int32)])
    last_blk = jnp.maximum(used - 1, 0) // bm
    past = jnp.logical_not(is_bnd) & (cuts // bm > last_blk)
    raw_blk = cuts // bm
    cuts = jnp.minimum(cuts, used)
    ends = jnp.minimum(ends, used)
    blk = jnp.minimum(cuts // bm, last_blk)
    e = jnp.minimum(jnp.sum((bnd[None, :] <= cuts[:, None]).astype(jnp.int32), axis=1), N_EXPERTS - 1)
    lo = jnp.where(past, 0, cuts - blk * bm)
    hi = jnp.where(past, -1, jnp.minimum(ends - blk * bm, bm))
    return lax.cummax(jnp.where(past, raw_blk, blk)), e, lo, hi


def moe_experts(xs, offs, layer, w_up, b_up, w_down, b_down):
    n_rows = xs.shape[0]
    bm = MOE_BM
    blk, e, lo, hi = _expert_schedule(offs, n_rows, bm)
    n_items = n_rows // bm + N_EXPERTS - 1
    wmap = lambda i, b, e, lo, hi: (layer, e[i], 0, 0)
    grid_spec = pltpu.PrefetchScalarGridSpec(
        num_scalar_prefetch=4, grid=(n_items,),
        in_specs=[pl.BlockSpec((bm, PAIR_COLS), lambda i, b, e, lo, hi: (b[i], 0)),
                  pl.BlockSpec((None, None, D_MODEL, 2 * D_FF), wmap),
                  pl.BlockSpec((None, None, 1, 2 * D_FF), wmap),
                  pl.BlockSpec((None, None, D_FF, D_MODEL), wmap),
                  pl.BlockSpec((None, None, 1, D_MODEL), wmap)],
        out_specs=pl.BlockSpec((bm, PAIR_COLS), lambda i, b, e, lo, hi: (b[i], 0)),
        scratch_shapes=[pltpu.VMEM((D_MODEL, 2 * D_FF), _BF16), pltpu.VMEM((D_FF, D_MODEL), _BF16)])
    nl = w_up.shape[0]
    return pl.pallas_call(
        _expert_kernel, grid_spec=grid_spec,
        out_shape=jax.ShapeDtypeStruct((n_rows, PAIR_COLS), _U32),
        compiler_params=_cparams("arbitrary"), name="moe_experts",
    )(blk, e, lo, hi, xs, w_up, b_up.reshape(nl, N_EXPERTS, 1, -1), w_down, b_down.reshape(nl, N_EXPERTS, 1, -1))


def moe_layer(h, eidx, gates, tile_counts, layer, w_up, b_up, w_down, b_down, ln_g, ln_b):
    tables = _seg_tables(tile_counts)
    xs = moe_dispatch(h, eidx, tables)
    ys = moe_experts(xs, tables[4], layer, w_up, b_up, w_down, b_down)
    return moe_combine(ys, eidx, gates, h, tables, ln_g, ln_b)


def kernel(x, positions, attn_w_in, attn_w_out, ssm_w_in, ssm_conv_w, ssm_conv_b, ssm_dt_bias,
           ssm_a_log, ssm_d, ssm_norm_w, ssm_w_out, moe_w_router, moe_b_router, moe_w_up,
           moe_b_up, moe_w_down, moe_b_down, ln_mix_g, ln_mix_b, ln_ffn_g, ln_ffn_b):
    bsz, seq, _ = x.shape
    xf = x.reshape(bsz * seq, D_MODEL)

    w_qkv = attn_w_in[0].astype(_BF16)
    outs = [dilated_group_attention(qkv_proj_group(xf, w_qkv, positions, g), g, bsz, seq)
            for g in range(N_GROUPS)]
    h, eidx, gates, tile_counts = attn_out_norm_route(
        [o for o, _ in outs], [l for _, l in outs], xf, attn_w_out[0].astype(_BF16),
        ln_mix_g[0], ln_mix_b[0], moe_w_router[0], moe_b_router[0])
    xf = moe_layer(h, eidx, gates, tile_counts, 0, moe_w_up, moe_b_up, moe_w_down, moe_b_down,
                   ln_ffn_g[0], ln_ffn_b[0])

    gz, xs, bc, dt = ssm_in_proj(xf, ssm_w_in[0], ssm_conv_w[0], ssm_conv_b[0], ssm_dt_bias[0], seq)
    yn = ssd_mixer(gz, xs, bc, dt, ssm_a_log[0], ssm_d[0], ssm_norm_w[0], bsz, seq)
    h, eidx, gates, tile_counts = ssm_out_norm_route(
        yn, xf, ssm_w_out[0].astype(_BF16), ln_mix_g[1], ln_mix_b[1], moe_w_router[1], moe_b_router[1])
    xf = moe_layer(h, eidx, gates, tile_counts, 1, moe_w_up, moe_b_up, moe_w_down, moe_b_down,
                   ln_ffn_g[1], ln_ffn_b[1])
    return xf.reshape(bsz, seq, D_MODEL)
```

```python
import functools
import math

import jax
import jax.numpy as jnp
from jax import lax
from jax.experimental import pallas as pl
from jax.experimental.pallas import tpu as pltpu

D_MODEL = 1024
DEPTH = 2
DIL = (1, 4, 16)
N_GROUPS = 3
SUB_WIN = 128
HEAD_DIM = 128
HEADS = 8
ROT_DIM = 32
ROT_HALF = 16
ROPE_THETA = 500000.0
ATTN_PROJ = N_GROUPS * 3 * D_MODEL
SSM_INNER = 2048
SSM_P = 64
SSM_HEADS = 32
SSM_GROUPS = 4
SSM_GHEADS = 8
SSM_N = 128
SSM_CONV = 4
CHUNK = 128
SSM_BC = SSM_GROUPS * SSM_N
SSM_XBC = SSM_INNER + 2 * SSM_BC
N_EXPERTS = 32
TOP_K = 4
D_FF = 1024
SWIGLU_LIMIT = 7.0
SWIGLU_ALPHA = 1.702
DEEPNORM_ALPHA = (2 * DEPTH) ** 0.25
LN_EPS = 1e-5
RMS_EPS = 1e-5

LANES = 128
SUBLANES = 8
VMEM_LIMIT_BYTES = 56 * 1024 * 1024

NEG_BIG = -1e30

_F32 = jnp.float32
_BF16 = jnp.bfloat16


def _cparams(*sem):
    return pltpu.CompilerParams(dimension_semantics=sem, vmem_limit_bytes=VMEM_LIMIT_BYTES)


ROPE_PACK = LANES // ROT_DIM


def _rope_table_kernel(pos_ref, invf_ref, c_ref, sa_ref, sb_ref):
    pos = pos_ref[...].astype(_F32)
    lane = lax.broadcasted_iota(jnp.int32, (pos.shape[0], LANES), 1)
    p = jnp.zeros((pos.shape[0], LANES), _F32)
    group = lax.shift_right_logical(lane, ROT_DIM.bit_length() - 1)
    for q in range(ROPE_PACK):
        p = jnp.where(group == q, pos[:, q:q + 1], p)
    ang = p * invf_ref[...]
    cos = jnp.cos(ang)
    sin = jnp.sin(ang)
    for q in range(ROPE_PACK):
        shift = (LANES - q * ROT_DIM) % LANES
        cq = pltpu.roll(cos, shift, 1) if shift else cos
        sq = pltpu.roll(sin, shift, 1) if shift else sin
        c_ref[q] = jnp.where(lane < ROT_DIM, cq, 1.0)
        sa_ref[q] = jnp.where((lane >= ROT_HALF) & (lane < ROT_DIM), sq, 0.0)
        sb_ref[q] = jnp.where(lane < ROT_HALF, -sq, 0.0)


def rope_tables(positions):
    t = positions.size
    rows = t // ROPE_PACK
    tm = min(1024, rows)
    pos = positions.reshape(ROPE_PACK, rows).T
    j = jnp.arange(LANES)
    invf = (ROPE_THETA ** (-(2.0 * (j % ROT_HALF)).astype(_F32) / ROT_DIM)).astype(_F32).reshape(1, LANES)
    out = jax.ShapeDtypeStruct((ROPE_PACK, rows, LANES), _F32)
    tabs = pl.pallas_call(
        _rope_table_kernel,
        grid=(rows // tm,),
        in_specs=[pl.BlockSpec((tm, ROPE_PACK), lambda i: (i, 0)),
                  pl.BlockSpec((1, LANES), lambda i: (0, 0))],
        out_specs=[pl.BlockSpec((ROPE_PACK, tm, LANES), lambda i: (0, i, 0))] * 3,
        out_shape=[out, out, out],
        compiler_params=_cparams("parallel"),
        name="rope_tables",
    )(pos, invf)
    return [x.reshape(t, LANES) for x in tabs]


QKV_TM = 512
ATTN_ROWS = 2048
HEADS_PER_STEP = 2


def _qkv_proj_kernel(x_ref, w_ref, c_ref, sa_ref, sb_ref, o_ref, *, d, ns, tt):
    tm = x_ref.shape[0]
    xb = x_ref[...].astype(_BF16)
    if d > 1:
        dst = lax.broadcasted_iota(jnp.int32, (tm, tm), 0)
        src = lax.broadcasted_iota(jnp.int32, (tm, tm), 1)
        lt, ld = tt.bit_length() - 1, d.bit_length() - 1
        t_ = dst & (tt - 1)
        r_ = lax.shift_right_logical(dst, lt) & (d - 1)
        s_ = lax.shift_right_logical(dst, lt + ld)
        perm = (src == ((s_ * tt + t_) * d + r_)).astype(_BF16)
        xb = jnp.dot(perm, xb, preferred_element_type=_F32).astype(_BF16)
    c = c_ref[...]
    sa = sa_ref[...]
    sb = sb_ref[...]
    for comp in range(3):
        acc = jnp.dot(xb, w_ref[:, comp * D_MODEL:(comp + 1) * D_MODEL], preferred_element_type=_F32)
        for h in range(HEADS):
            sl = slice(h * HEAD_DIM, (h + 1) * HEAD_DIM)
            t = acc[:, sl]
            if comp < 2:
                t = t * c + pltpu.roll(t, ROT_HALF, 1) * sa + pltpu.roll(t, HEAD_DIM - ROT_HALF, 1) * sb
            o_ref[:, :, :, comp * D_MODEL + h * HEAD_DIM:comp * D_MODEL + (h + 1) * HEAD_DIM] = (
                t.reshape(ns, d, tt, HEAD_DIM).astype(o_ref.dtype))


def qkv_proj_group(x, w_bf16, positions, g):
    t, k = x.shape
    d = DIL[g]
    span = SUB_WIN * d
    tm = QKV_TM
    ns = max(1, tm // span)
    tps = max(1, span // tm)
    tt = tm // (ns * d)
    rope = rope_tables(positions.reshape(-1, ns, tt, d).swapaxes(2, 3).reshape(-1))
    tab_spec = pl.BlockSpec((tm, LANES), lambda i: (i, 0))
    return pl.pallas_call(
        functools.partial(_qkv_proj_kernel, d=d, ns=ns, tt=tt),
        grid=(t // tm,),
        in_specs=[pl.BlockSpec((tm, k), lambda i: (i, 0)),
                  pl.BlockSpec((k, 3 * D_MODEL), lambda i: (0, g)),
                  tab_spec, tab_spec, tab_spec],
        out_specs=pl.BlockSpec((ns, d, tt, 3 * D_MODEL), lambda i: (i // tps, 0, i % tps, 0)),
        out_shape=jax.ShapeDtypeStruct((t // span, d, SUB_WIN, 3 * D_MODEL), _BF16),
        compiler_params=_cparams("parallel"), name=f"qkv_proj_g{g}",
    )(x, w_bf16, *rope)


def _dil_attn_kernel(q_ref, k_ref, v_ref, kp_ref, vp_ref, o_ref, lse_ref, *, d, ns):
    j = pl.program_id(1)
    hp = pl.program_id(2)
    nb = ns * d
    row = lax.broadcasted_iota(jnp.int32, (1, SUB_WIN, SUB_WIN), 1)
    col = lax.broadcasted_iota(jnp.int32, (1, SUB_WIN, SUB_WIN), 2)
    cur_ok = col <= row
    blk = lax.broadcasted_iota(jnp.int32, (nb, 1, 1), 0)
    prev_ok = (col >= row) & ((blk >= d) | (j > 0))
    lane = lax.broadcasted_iota(jnp.int32, (SUB_WIN, LANES), 1)
    scale = 1.0 / math.sqrt(HEAD_DIM)

    def block_rows(b):
        s, r = divmod(b, d)
        start = s * SUB_WIN * d + r
        return pl.ds(start, SUB_WIN, stride=d) if d > 1 else pl.ds(start, SUB_WIN)

    @pl.when(hp == 0)
    def _():
        lse_ref[...] = jnp.zeros_like(lse_ref)

    lses = []
    for hh in range(HEADS_PER_STEP):
        sl = slice(hh * HEAD_DIM, (hh + 1) * HEAD_DIM)
        flat = lambda a: a.reshape(nb, SUB_WIN, HEAD_DIM)
        q, kc, vc = flat(q_ref[:, :, :, sl]), flat(k_ref[:, :, :, sl]), flat(v_ref[:, :, :, sl])
        if ns > 1:
            kp = flat(jnp.concatenate([kp_ref[:, :, :, sl], k_ref[:ns - 1, :, :, sl]], axis=0))
            vp = flat(jnp.concatenate([vp_ref[:, :, :, sl], v_ref[:ns - 1, :, :, sl]], axis=0))
        else:
            kp, vp = flat(kp_ref[:, :, :, sl]), flat(vp_ref[:, :, :, sl])
        sc = jnp.einsum('bqd,bkd->bqk', q, kc, preferred_element_type=_F32) * scale
        sp = jnp.einsum('bqd,bkd->bqk', q, kp, preferred_element_type=_F32) * scale
        sc = jnp.where(cur_ok, sc, NEG_BIG)
        sp = jnp.where(prev_ok, sp, NEG_BIG)
        m = jnp.maximum(jnp.max(sc, axis=2, keepdims=True), jnp.max(sp, axis=2, keepdims=True))
        pc = jnp.exp(sc - m)
        pp = jnp.exp(sp - m)
        l = jnp.sum(pc, axis=2, keepdims=True) + jnp.sum(pp, axis=2, keepdims=True)
        acc = jnp.einsum('bqk,bkd->bqd', pc.astype(_BF16), vc, preferred_element_type=_F32)
        acc += jnp.einsum('bqk,bkd->bqd', pp.astype(_BF16), vp, preferred_element_type=_F32)
        o = acc / l
        for b in range(nb):
            o_ref[hh, block_rows(b), :] = o[b]
        lses.append(m + jnp.log(l))
    for b in range(nb):
        cur = lse_ref[block_rows(b), :]
        for hh in range(HEADS_PER_STEP):
            cur = jnp.where(lane == hp * HEADS_PER_STEP + hh, lses[hh][b], cur)
        lse_ref[block_rows(b), :] = cur


def dilated_group_attention(qkv, g, bsz, seq):
    d = DIL[g]
    ns = ATTN_ROWS // (SUB_WIN * d)
    nsteps = seq // ATTN_ROWS
    hw = HEADS_PER_STEP * HEAD_DIM
    ncol = D_MODEL // hw
    cur = lambda comp: pl.BlockSpec((ns, d, SUB_WIN, hw), lambda b, j, hp: (b * nsteps + j, 0, 0, comp * ncol + hp))
    prev = lambda comp: pl.BlockSpec(
        (1, d, SUB_WIN, hw), lambda b, j, hp: (jnp.maximum((b * nsteps + j) * ns - 1, 0), 0, 0, comp * ncol + hp))
    return pl.pallas_call(
        functools.partial(_dil_attn_kernel, d=d, ns=ns),
        grid=(bsz, nsteps, HEADS // HEADS_PER_STEP),
        in_specs=[cur(0), cur(1), cur(2), prev(1), prev(2)],
        out_specs=[pl.BlockSpec((HEADS_PER_STEP, ATTN_ROWS, HEAD_DIM), lambda b, j, hp: (hp, b * nsteps + j, 0)),
                   pl.BlockSpec((ATTN_ROWS, LANES), lambda b, j, hp: (b * nsteps + j, 0))],
        out_shape=[jax.ShapeDtypeStruct((HEADS, bsz * seq, HEAD_DIM), _F32),
                   jax.ShapeDtypeStruct((bsz * seq, LANES), _F32)],
        compiler_params=_cparams("parallel", "parallel", "arbitrary"),
        name=f"dilated_attn_g{g}",
    )(qkv, qkv, qkv, qkv, qkv)


SSM_IN_TM = 512
HALO = SUBLANES
ZX_COLS = SSM_INNER + SSM_XBC


def _softplus(x):
    return jnp.maximum(x, 0.0) + jnp.log1p(jnp.exp(-jnp.abs(x)))


def _ssm_in_kernel(x_ref, w_ref, wdt_ref, cw_ref, cb_ref, dtb_ref, gz_ref, xs_ref, bc_ref, dt_ref,
                   carry_ref, *, tiles_per_seq):
    tm = x_ref.shape[0]
    cw = D_MODEL
    xb = x_ref[...].astype(_BF16)

    @pl.when(pl.program_id(0) % tiles_per_seq == 0)
    def _():
        carry_ref[...] = jnp.zeros_like(carry_ref)

    for i in range(SSM_INNER // cw):
        z = jnp.dot(xb, w_ref[:, i * cw:(i + 1) * cw], preferred_element_type=_F32)
        gz_ref[:, i * cw:(i + 1) * cw] = z * jax.nn.sigmoid(z)
    for c in range(SSM_XBC // cw):
        cols = slice(c * cw, (c + 1) * cw)
        acc = jnp.dot(xb, w_ref[:, SSM_INNER + c * cw:SSM_INNER + (c + 1) * cw], preferred_element_type=_F32)
        ext = jnp.concatenate([carry_ref[c], acc], axis=0)
        carry_ref[c] = acc[tm - HALO:, :]
        conv = cb_ref[:, cols] + cw_ref[SSM_CONV - 1:SSM_CONV, cols] * acc
        for k in range(1, SSM_CONV):
            conv = conv + cw_ref[SSM_CONV - 1 - k:SSM_CONV - k, cols] * pltpu.roll(ext, k, 0)[HALO:, :]
        act = conv * jax.nn.sigmoid(conv)
        if (c + 1) * cw <= SSM_INNER:
            xs_ref[:, cols] = act
        else:
            bc_ref[...] = act.astype(_BF16)
    dt_raw = jnp.dot(xb, wdt_ref[...], preferred_element_type=_F32)
    dt_ref[...] = _softplus(dt_raw + dtb_ref[...])


def ssm_in_proj(x, w_in, conv_w, conv_b, dt_bias, seq):
    t, k = x.shape
    tm = SSM_IN_TM
    assert 2 * SSM_BC == D_MODEL
    pad = LANES - SSM_HEADS
    w_main = w_in[:, :ZX_COLS].astype(_BF16)
    w_dt = jnp.pad(w_in[:, ZX_COLS:], ((0, 0), (0, pad))).astype(_BF16)
    dtb = jnp.pad(dt_bias, (0, pad)).reshape(1, LANES)
    cst = lambda shape: pl.BlockSpec(shape, lambda i: (0,) * len(shape))
    once = lambda shape: pl.BlockSpec(shape, lambda i: (0,) * len(shape), pipeline_mode=pl.Buffered(1))
    row = lambda w: pl.BlockSpec((tm, w), lambda i: (i, 0))
    return pl.pallas_call(
        functools.partial(_ssm_in_kernel, tiles_per_seq=seq // tm),
        grid=(t // tm,),
        in_specs=[row(k), once((k, ZX_COLS)), once((k, LANES)),
                  cst((SSM_CONV, SSM_XBC)), cst((1, SSM_XBC)), cst((1, LANES))],
        out_specs=[row(SSM_INNER), row(SSM_INNER), row(2 * SSM_BC), row(LANES)],
        out_shape=[jax.ShapeDtypeStruct((t, SSM_INNER), _F32), jax.ShapeDtypeStruct((t, SSM_INNER), _F32),
                   jax.ShapeDtypeStruct((t, 2 * SSM_BC), _BF16), jax.ShapeDtypeStruct((t, LANES), _F32)],
        scratch_shapes=[pltpu.VMEM((SSM_XBC // D_MODEL, HALO, D_MODEL), _F32)],
        compiler_params=_cparams("arbitrary"), name="ssm_in_proj",
    )(x, w_main, w_dt, conv_w, conv_b.reshape(1, -1), dtb)


def _ssd_kernel(gz_ref, xs_ref, bc_ref, dt_ref, a_ref, dsk_ref, nw_ref, o_ref, state_ref, y_ref):
    c = pl.program_id(1)
    q = CHUNK

    @pl.when(c == 0)
    def _():
        state_ref[...] = jnp.zeros_like(state_ref)

    dt = dt_ref[...]
    da = dt * a_ref[...]
    r_io = lax.broadcasted_iota(jnp.int32, (q, q), 0)
    c_io = lax.broadcasted_iota(jnp.int32, (q, q), 1)
    causal = r_io >= c_io
    tri = causal.astype(_F32)
    a_cum = jnp.dot(tri, da, preferred_element_type=_F32, precision=lax.Precision.HIGHEST)
    a_cum_t = a_cum.T
    dt_t = dt.T
    a_last = a_cum[q - 1:q, :]
    w_state = jnp.exp(a_last - a_cum) * dt
    e_acum = jnp.exp(a_cum)
    chunk_decay = jnp.exp(a_last)
    lane = lax.broadcasted_iota(jnp.int32, (q, LANES), 1)
    low = lane < SSM_P
    lane1 = lax.broadcasted_iota(jnp.int32, (1, LANES), 1)
    low1 = lane1 < SSM_P
    tn = (((0,), (0,)), ((), ()))
    nt = (((1,), (1,)), ((), ()))

    for g in range(SSM_GROUPS):
        bg = bc_ref[:, g * SSM_N:(g + 1) * SSM_N]
        cg = bc_ref[:, SSM_BC + g * SSM_N:SSM_BC + (g + 1) * SSM_N]
        cbm = lax.dot_general(cg, bg, nt, preferred_element_type=_F32)
        st = state_ref[g]
        y_off = jnp.dot(cg, st.astype(_BF16), preferred_element_type=_F32)
        xw_parts, dec_parts = [], []
        for j in range(SSM_GHEADS // 2):
            h0 = g * SSM_GHEADS + 2 * j
            col = (g * SSM_GHEADS // 2 + j) * LANES
            xp = xs_ref[:, col:col + LANES]
            y_diag = jnp.zeros((q, LANES), _F32)
            for half, hh in enumerate((h0, h0 + 1)):
                diff = a_cum[:, hh:hh + 1] - a_cum_t[hh:hh + 1, :]
                lmat = jnp.exp(jnp.where(causal, diff, NEG_BIG))
                mm = (cbm * lmat * dt_t[hh:hh + 1, :]).astype(_BF16)
                keep = low if half == 0 else jnp.logical_not(low)
                xh = jnp.where(keep, xp, 0.0).astype(_BF16)
                y_diag = y_diag + jnp.dot(mm, xh, preferred_element_type=_F32)
            e_pair = jnp.where(low, e_acum[:, h0:h0 + 1], e_acum[:, h0 + 1:h0 + 2])
            w_pair = jnp.where(low, w_state[:, h0:h0 + 1], w_state[:, h0 + 1:h0 + 2])
            y_ref[:, col:col + LANES] = (y_diag + y_off[:, j * LANES:(j + 1) * LANES] * e_pair
                                         + xp * dsk_ref[:, col:col + LANES])
            xw_parts.append((xp * w_pair).astype(_BF16))
            dec_parts.append(jnp.where(low1, chunk_decay[:, h0:h0 + 1], chunk_decay[:, h0 + 1:h0 + 2]))
        xw = jnp.concatenate(xw_parts, axis=1)
        dec = jnp.concatenate(dec_parts, axis=1)
        state_ref[g] = st * dec + lax.dot_general(bg, xw, tn, preferred_element_type=_F32)

    yg = y_ref[...] * gz_ref[...]
    gw = SSM_INNER // SSM_GROUPS
    for g in range(SSM_GROUPS):
        v = yg[:, g * gw:(g + 1) * gw]
        ms = jnp.mean(v * v, axis=-1, keepdims=True)
        o_ref[:, g * gw:(g + 1) * gw] = (v * lax.rsqrt(ms + RMS_EPS)
                                         * nw_ref[:, g * gw:(g + 1) * gw]).astype(o_ref.dtype)


def ssd_mixer(gz, xs, bc, dt, a_log, d_skip, norm_w, bsz, seq):
    nc = seq // CHUNK
    a = jnp.pad(-jnp.exp(a_log.astype(_F32)), (0, LANES - SSM_HEADS)).reshape(1, LANES)
    dsk = jnp.repeat(d_skip.astype(_F32), SSM_P).reshape(1, SSM_INNER)
    cst = lambda shape: pl.BlockSpec(shape, lambda b, c: (0,) * len(shape))
    row = lambda w: pl.BlockSpec((CHUNK, w), lambda b, c: (b * nc + c, 0))
    return pl.pallas_call(
        _ssd_kernel,
        grid=(bsz, nc),
        in_specs=[row(SSM_INNER), row(SSM_INNER), row(2 * SSM_BC), row(LANES),
                  cst((1, LANES)), cst((1, SSM_INNER)), cst((1, SSM_INNER))],
        out_specs=row(SSM_INNER),
        out_shape=jax.ShapeDtypeStruct((bsz * seq, SSM_INNER), _BF16),
        scratch_shapes=[pltpu.VMEM((SSM_GROUPS, SSM_N, SSM_GHEADS * SSM_P), _F32),
                        pltpu.VMEM((CHUNK, SSM_INNER), _F32)],
        compiler_params=_cparams("parallel", "arbitrary"), name="ssd_mixer",
    )(gz, xs, bc, dt, a, dsk, norm_w.reshape(1, -1))


ROUTE_SUB = 256
ROUTE_TM_ATTN = 512
ROUTE_TM_SSM = 1024


def _layer_norm(y, g, b):
    mu = jnp.mean(y, axis=-1, keepdims=True)
    yc = y - mu
    var = jnp.mean(yc * yc, axis=-1, keepdims=True)
    return yc * lax.rsqrt(var + LN_EPS) * g + b


def _route(h1s, wrt_ref, br_ref, eidx_ref, gate_ref, tcnt_ref):
    sub = ROUTE_SUB
    n = len(h1s)
    nt = (((1,), (1,)), ((), ()))
    wrt = wrt_ref[...]
    vals = [lax.dot_general(wrt, h, nt, preferred_element_type=_F32, precision=lax.Precision.HIGHEST)
            + br_ref[...] for h in h1s]
    eio = lax.broadcasted_iota(jnp.int32, (N_EXPERTS, sub), 0)
    tops, ids, hots = ([[] for _ in range(n)] for _ in range(3))
    for _ in range(TOP_K):
        for a in range(n):
            mk = jnp.max(vals[a], axis=0, keepdims=True)
            ik = jnp.min(jnp.where(vals[a] == mk, eio, N_EXPERTS), axis=0, keepdims=True)
            hot = eio == ik
            vals[a] = jnp.where(hot, -jnp.inf, vals[a])
            tops[a].append(mk)
            ids[a].append(ik)
            hots[a].append(hot)
    col = lax.broadcasted_iota(jnp.int32, tcnt_ref.shape, 1)
    tcnt = tcnt_ref[...]
    for a in range(n):
        sl = slice(a * sub, (a + 1) * sub)
        exps = [jnp.exp(t - tops[a][0]) for t in tops[a]]
        den = exps[0] + exps[1] + exps[2] + exps[3]
        for k in range(TOP_K):
            eidx_ref[k:k + 1, sl] = ids[a][k]
            gate_ref[k:k + 1, sl] = exps[k] / den
        hot_all = hots[a][0] | hots[a][1] | hots[a][2] | hots[a][3]
        counts = jnp.sum(hot_all.astype(_F32), axis=1, keepdims=True)
        tcnt = jnp.where(col == pl.program_id(0) * n + a, counts, tcnt)
    tcnt_ref[...] = tcnt


def _norm_and_route(mixes, x_ref, g_ref, b_ref, wrt_ref, br_ref, h_ref, eidx_ref, gate_ref, tcnt_ref):
    sub = ROUTE_SUB

    @pl.when(pl.program_id(0) == 0)
    def _():
        tcnt_ref[...] = jnp.zeros_like(tcnt_ref)

    h1s = [_layer_norm(DEEPNORM_ALPHA * x_ref[a * sub:(a + 1) * sub, :] + mix, g_ref[...], b_ref[...])
           for a, mix in enumerate(mixes)]
    for a, h1 in enumerate(h1s):
        h_ref[a * sub:(a + 1) * sub, :] = h1
    _route(h1s, wrt_ref, br_ref, eidx_ref, gate_ref, tcnt_ref)


def _attn_out_kernel(o0_ref, o1_ref, o2_ref, l0_ref, l1_ref, l2_ref, x_ref, wo_ref, g_ref, b_ref,
                     wrt_ref, br_ref, h_ref, eidx_ref, gate_ref, tcnt_ref, ob_ref):
    sub = ROUTE_SUB
    l0, l1, l2 = l0_ref[...], l1_ref[...], l2_ref[...]
    m = jnp.maximum(jnp.maximum(l0, l1), l2)
    e0, e1, e2 = jnp.exp(l0 - m), jnp.exp(l1 - m), jnp.exp(l2 - m)
    inv = 1.0 / (e0 + e1 + e2)
    w0, w1, w2 = e0 * inv, e1 * inv, e2 * inv
    for h in range(HEADS):
        o = w0[:, h:h + 1] * o0_ref[h] + w1[:, h:h + 1] * o1_ref[h] + w2[:, h:h + 1] * o2_ref[h]
        ob_ref[:, h * HEAD_DIM:(h + 1) * HEAD_DIM] = o.astype(_BF16)
    mixes = [jnp.dot(ob_ref[a * sub:(a + 1) * sub, :], wo_ref[...], preferred_element_type=_F32)
             for a in range(x_ref.shape[0] // sub)]
    _norm_and_route(mixes, x_ref, g_ref, b_ref, wrt_ref, br_ref, h_ref, eidx_ref, gate_ref, tcnt_ref)


def _ssm_out_kernel(y_ref, x_ref, wo_ref, g_ref, b_ref, wrt_ref, br_ref, h_ref, eidx_ref, gate_ref, tcnt_ref):
    sub = ROUTE_SUB
    mixes = [jnp.dot(y_ref[a * sub:(a + 1) * sub, :], wo_ref[...], preferred_element_type=_F32)
             for a in range(x_ref.shape[0] // sub)]
    _norm_and_route(mixes, x_ref, g_ref, b_ref, wrt_ref, br_ref, h_ref, eidx_ref, gate_ref, tcnt_ref)


def _route_out_specs(t, tm):
    nsub = t // ROUTE_SUB
    specs = [pl.BlockSpec((tm, D_MODEL), lambda i: (i, 0)),
             pl.BlockSpec((TOP_K, tm), lambda i: (0, i)),
             pl.BlockSpec((TOP_K, tm), lambda i: (0, i)),
             pl.BlockSpec((N_EXPERTS, nsub), lambda i: (0, 0))]
    shapes = [jax.ShapeDtypeStruct((t, D_MODEL), _F32),
              jax.ShapeDtypeStruct((TOP_K, t), jnp.int32),
              jax.ShapeDtypeStruct((TOP_K, t), _F32),
              jax.ShapeDtypeStruct((N_EXPERTS, nsub), _F32)]
    return specs, shapes


def _const_spec(shape):
    return pl.BlockSpec(shape, lambda i: (0,) * len(shape))


def attn_out_norm_route(os_, lses, x, wo_bf16, ln_g, ln_b, w_router, b_router):
    t = x.shape[0]
    tm = ROUTE_TM_ATTN
    row = lambda w: pl.BlockSpec((tm, w), lambda i: (i, 0))
    out_specs, out_shapes = _route_out_specs(t, tm)
    return pl.pallas_call(
        _attn_out_kernel,
        grid=(t // tm,),
        in_specs=[pl.BlockSpec((HEADS, tm, HEAD_DIM), lambda i: (0, i, 0))] * 3 + [row(LANES)] * 3 + [row(D_MODEL),
                  _const_spec((D_MODEL, D_MODEL)), _const_spec((1, D_MODEL)), _const_spec((1, D_MODEL)),
                  _const_spec((N_EXPERTS, D_MODEL)), _const_spec((N_EXPERTS, 1))],
        out_specs=out_specs, out_shape=out_shapes,
        scratch_shapes=[pltpu.VMEM((tm, D_MODEL), _BF16)],
        compiler_params=_cparams("arbitrary"), name="attn_out_norm_route",
    )(*os_, *lses, x, wo_bf16, ln_g.reshape(1, -1), ln_b.reshape(1, -1),
      w_router.T, b_router.reshape(-1, 1))


def ssm_out_norm_route(yn, x, wo_bf16, ln_g, ln_b, w_router, b_router):
    t = x.shape[0]
    tm = ROUTE_TM_SSM
    row = lambda w: pl.BlockSpec((tm, w), lambda i: (i, 0))
    out_specs, out_shapes = _route_out_specs(t, tm)
    return pl.pallas_call(
        _ssm_out_kernel,
        grid=(t // tm,),
        in_specs=[row(SSM_INNER), row(D_MODEL),
                  _const_spec((SSM_INNER, D_MODEL)), _const_spec((1, D_MODEL)), _const_spec((1, D_MODEL)),
                  _const_spec((N_EXPERTS, D_MODEL)), _const_spec((N_EXPERTS, 1))],
        out_specs=out_specs, out_shape=out_shapes,
        compiler_params=_cparams("arbitrary"), name="ssm_out_norm_route",
    )(yn, x, wo_bf16, ln_g.reshape(1, -1), ln_b.reshape(1, -1), w_router.T, b_router.reshape(-1, 1))


MOE_BM = 512
MOE_SUB = ROUTE_SUB
SEG_ALIGN = SUBLANES
LOCAL_ROWS = MOE_SUB * TOP_K + N_EXPERTS * SUBLANES
SEG_UNIT = 16
SEG_REST = (8,)
SEG_SIZES = (256, 128, 64, 32, 16, 8)
PAIR_COLS = D_MODEL // 2
_U32 = jnp.uint32


def _pack_pairs(x):
    lo = lax.shift_right_logical(lax.bitcast_convert_type(x[:, :PAIR_COLS], _U32), jnp.uint32(16))
    hi = lax.bitcast_convert_type(x[:, PAIR_COLS:], _U32) & jnp.uint32(0xFFFF0000)
    return hi | lo


def _unpack_pairs(u):
    lo = lax.bitcast_convert_type(lax.shift_left(u, jnp.uint32(16)), _F32).astype(_BF16)
    hi = lax.bitcast_convert_type(u & jnp.uint32(0xFFFF0000), _F32).astype(_BF16)
    return lo, hi


def _local_positions(eidx):
    sub = eidx.shape[1]
    eio = lax.broadcasted_iota(jnp.int32, (N_EXPERTS, sub), 0)
    hots = [eio == eidx[k:k + 1, :] for k in range(TOP_K)]
    hot_all = (hots[0] | hots[1] | hots[2] | hots[3]).astype(_BF16)
    r_io = lax.broadcasted_iota(jnp.int32, (sub, sub), 0)
    c_io = lax.broadcasted_iota(jnp.int32, (sub, sub), 1)
    upper = (r_io < c_io).astype(_BF16)
    prefix = jnp.dot(hot_all, upper, preferred_element_type=_F32)
    counts = jnp.sum(hot_all.astype(_F32), axis=1, keepdims=True).astype(jnp.int32)
    padded = ((counts + (SEG_ALIGN - 1)) & (-SEG_ALIGN)).astype(_F32)
    e_r = lax.broadcasted_iota(jnp.int32, (N_EXPERTS, N_EXPERTS), 0)
    e_c = lax.broadcasted_iota(jnp.int32, (N_EXPERTS, N_EXPERTS), 1)
    below = (e_c < e_r).astype(_BF16)
    seg_start = jnp.dot(below, jnp.broadcast_to(padded, (N_EXPERTS, sub)).astype(_BF16),
                        preferred_element_type=_F32)
    base = seg_start + prefix
    return [jnp.sum(jnp.where(h, base, 0.0), axis=0, keepdims=True) for h in hots]


def _segment_copies(cnt_ref, lo_ref, go_ref, i, copy_fn):
    for e in range(N_EXPERTS):
        n = cnt_ref[i * N_EXPERTS + e]
        lo = lo_ref[i * N_EXPERTS + e]
        go = go_ref[i * N_EXPERTS + e]

        def unit(u, c, lo=lo, go=go):
            copy_fn(pl.multiple_of(lo + u * SEG_UNIT, SEG_ALIGN), pl.multiple_of(go + u * SEG_UNIT, SEG_ALIGN),
                    SEG_UNIT)
            return c

        lax.fori_loop(0, n // SEG_UNIT, unit, 0)
        for sz in SEG_REST:
            done = n & -(2 * sz)

            @pl.when((n & sz) != 0)
            def _(done=done, sz=sz, lo=lo, go=go):
                copy_fn(pl.multiple_of(lo + done, SEG_ALIGN), pl.multiple_of(go + done, SEG_ALIGN), sz)


def _dispatch_kernel(cnt_ref, lo_ref, go_ref, units_ref, tail_ref, eidx_ref, h_ref, xs_ref, buf_ref, sem):
    i = pl.program_id(0)
    slot = i % 2
    mine, other = buf_ref.at[slot], buf_ref.at[1 - slot]
    lpos = _local_positions(eidx_ref[...])
    p_io = lax.broadcasted_iota(jnp.int32, (LOCAL_ROWS, MOE_SUB), 0).astype(_F32)
    sel = (p_io == lpos[0]) | (p_io == lpos[1]) | (p_io == lpos[2]) | (p_io == lpos[3])
    mine[...] = _pack_pairs(jnp.dot(sel.astype(_BF16), h_ref[...].astype(_BF16), preferred_element_type=_F32))

    def copy(lo, go, sz):
        pltpu.make_async_copy(mine.at[pl.ds(lo, sz)], xs_ref.at[pl.ds(go, sz)], sem.at[slot]).start()

    _segment_copies(cnt_ref, lo_ref, go_ref, i, copy)

    def wait_units(buf, s, n_units):
        def body(j, c):
            pltpu.make_async_copy(buf.at[pl.ds(0, SEG_ALIGN)], xs_ref.at[pl.ds(0, SEG_ALIGN)], sem.at[s]).wait()
            return c
        lax.fori_loop(0, n_units, body, 0)

    @pl.when(i > 0)
    def _():
        wait_units(other, 1 - slot, units_ref[jnp.maximum(i - 1, 0)])

    @pl.when(i == pl.num_programs(0) - 1)
    def _():
        wait_units(mine, slot, units_ref[i])
        mine[0:MOE_BM, :] = jnp.zeros((MOE_BM, PAIR_COLS), _U32)
        start, n, n_blocks = tail_ref[0], tail_ref[1], tail_ref[2]
        done = jnp.int32(0)
        for sz in (s for s in SEG_SIZES if s < MOE_BM):
            @pl.when((n & sz) != 0)
            def _(done=done, sz=sz):
                pltpu.make_async_copy(mine.at[pl.ds(0, sz)],
                                      xs_ref.at[pl.ds(pl.multiple_of(start + done, SEG_ALIGN), sz)],
                                      sem.at[slot]).start()
            done = done + (n & sz)

        def zero_block(j, c):
            row0 = pl.multiple_of(start + n + j * MOE_BM, MOE_BM)
            pltpu.make_async_copy(mine.at[pl.ds(0, MOE_BM)], xs_ref.at[pl.ds(row0, MOE_BM)], sem.at[slot]).start()
            return c

        lax.fori_loop(0, n_blocks, zero_block, 0)
        wait_units(mine, slot, n // SEG_ALIGN + n_blocks * (MOE_BM // SEG_ALIGN))


def _seg_tables(tile_counts):
    cnt = tile_counts.T.astype(jnp.int32)
    pc = (cnt + (SEG_ALIGN - 1)) & (-SEG_ALIGN)
    lo = jnp.cumsum(pc, axis=1) - pc
    tot = jnp.sum(pc, axis=0)
    offs = jnp.concatenate([jnp.zeros((1,), jnp.int32), jnp.cumsum(tot)])
    go = offs[None, :-1] + jnp.cumsum(pc, axis=0) - pc
    units = jnp.sum(pc, axis=1) // SEG_ALIGN
    return pc.reshape(-1), lo.reshape(-1), go.reshape(-1), units.astype(jnp.int32), offs.astype(jnp.int32)


def _max_rows(t):
    n = t * TOP_K + (t // MOE_SUB) * N_EXPERTS * (SEG_ALIGN - 1)
    return -(-n // MOE_BM) * MOE_BM


def moe_dispatch(h, eidx, tables):
    t = h.shape[0]
    pc, lo, go, units, offs = tables
    used = offs[-1]
    partial = (-used) % MOE_BM
    tail = jnp.stack([used, partial, (_max_rows(t) - used - partial) // MOE_BM]).astype(jnp.int32)
    grid_spec = pltpu.PrefetchScalarGridSpec(
        num_scalar_prefetch=5, grid=(t // MOE_SUB,),
        in_specs=[pl.BlockSpec((TOP_K, MOE_SUB), lambda i, *_: (0, i)),
                  pl.BlockSpec((MOE_SUB, D_MODEL), lambda i, *_: (i, 0))],
        out_specs=pl.BlockSpec(memory_space=pl.ANY),
        scratch_shapes=[pltpu.VMEM((2, LOCAL_ROWS, PAIR_COLS), _U32), pltpu.SemaphoreType.DMA((2,))])
    return pl.pallas_call(
        _dispatch_kernel, grid_spec=grid_spec,
        out_shape=jax.ShapeDtypeStruct((_max_rows(t), PAIR_COLS), _U32),
        compiler_params=_cparams("arbitrary"), name="moe_dispatch",
    )(pc, lo, go, units, tail, eidx, h)


def _combine_kernel(cnt_ref, lo_ref, go_ref, units_ref, eidx_ref, gate_ref, ys_ref, h_ref, g_ref, b_ref,
                    o_ref, buf_ref, sem):
    i = pl.program_id(0)
    slot = i % 2

    def fetch(step, s):
        dst = buf_ref.at[s]
        dst[MOE_SUB * TOP_K:, :] = jnp.zeros((LOCAL_ROWS - MOE_SUB * TOP_K, PAIR_COLS), _U32)

        def copy(lo, go, sz):
            pltpu.make_async_copy(ys_ref.at[pl.ds(go, sz)], dst.at[pl.ds(lo, sz)], sem.at[s]).start()

        _segment_copies(cnt_ref, lo_ref, go_ref, step, copy)

    @pl.when(i == 0)
    def _():
        fetch(i, slot)

    @pl.when(i + 1 < pl.num_programs(0))
    def _():
        fetch(i + 1, 1 - slot)

    mine = buf_ref.at[slot]
    lpos = _local_positions(eidx_ref[...])
    gates = gate_ref[...]
    p_io = lax.broadcasted_iota(jnp.int32, (LOCAL_ROWS, MOE_SUB), 0).astype(_F32)
    w = jnp.zeros((LOCAL_ROWS, MOE_SUB), _F32)
    for k in range(TOP_K):
        w = w + jnp.where(p_io == lpos[k], gates[k:k + 1, :], 0.0)
    w_hi = w.astype(_BF16)
    w_lo = (w - w_hi.astype(_F32)).astype(_BF16)

    def wait_unit(j, c):
        pltpu.make_async_copy(ys_ref.at[pl.ds(0, SEG_ALIGN)], mine.at[pl.ds(0, SEG_ALIGN)], sem.at[slot]).wait()
        return c

    lax.fori_loop(0, units_ref[i], wait_unit, 0)
    tn = (((0,), (0,)), ((), ()))
    ffn = jnp.concatenate(
        [lax.dot_general(w_hi, yb, tn, preferred_element_type=_F32)
         + lax.dot_general(w_lo, yb, tn, preferred_element_type=_F32) for yb in _unpack_pairs(mine[...])],
        axis=1)
    o_ref[...] = _layer_norm(DEEPNORM_ALPHA * h_ref[...] + ffn, g_ref[...], b_ref[...])


def moe_combine(ys, eidx, gates, h, tables, ln_g, ln_b):
    t = h.shape[0]
    pc, lo, go, units, _ = tables
    row = pl.BlockSpec((MOE_SUB, D_MODEL), lambda i, *_: (i, 0))
    kt = pl.BlockSpec((TOP_K, MOE_SUB), lambda i, *_: (0, i))
    cst = lambda shape: pl.BlockSpec(shape, lambda i, *_: (0,) * len(shape))
    grid_spec = pltpu.PrefetchScalarGridSpec(
        num_scalar_prefetch=4, grid=(t // MOE_SUB,),
        in_specs=[kt, kt, pl.BlockSpec(memory_space=pl.ANY), row, cst((1, D_MODEL)), cst((1, D_MODEL))],
        out_specs=row,
        scratch_shapes=[pltpu.VMEM((2, LOCAL_ROWS, PAIR_COLS), _U32), pltpu.SemaphoreType.DMA((2,))])
    return pl.pallas_call(
        _combine_kernel, grid_spec=grid_spec,
        out_shape=jax.ShapeDtypeStruct((t, D_MODEL), _F32),
        compiler_params=_cparams("arbitrary"), name="moe_combine",
    )(pc, lo, go, units, eidx, gates, ys, h, ln_g.reshape(1, -1), ln_b.reshape(1, -1))


def _expert_kernel(blk_ref, e_ref, lo_ref, hi_ref, xs_ref, wu_ref, bu_ref, wd_ref, bd_ref,
                   ys_ref, wub_ref, wdb_ref):
    i = pl.program_id(0)
    e = e_ref[i]
    e_prev = e_ref[jnp.maximum(i - 1, 0)]

    @pl.when((i == 0) | (e != e_prev))
    def _():
        wub_ref[...] = wu_ref[...].astype(_BF16)
        wdb_ref[...] = wd_ref[...].astype(_BF16)

    lo = lo_ref[i]
    hi = hi_ref[i]

    @pl.when(hi < 0)
    def _():
        ys_ref[...] = jnp.zeros_like(ys_ref)

    @pl.when(hi > lo)
    def _():
        x_lo, x_hi = _unpack_pairs(xs_ref[...])
        h = (jnp.dot(x_lo, wub_ref[:PAIR_COLS, :], preferred_element_type=_F32)
             + jnp.dot(x_hi, wub_ref[PAIR_COLS:, :], preferred_element_type=_F32) + bu_ref[...])
        gate = jnp.minimum(h[:, :D_FF], SWIGLU_LIMIT)
        up = jnp.clip(h[:, D_FF:], -SWIGLU_LIMIT, SWIGLU_LIMIT)
        glu = gate * jax.nn.sigmoid(SWIGLU_ALPHA * gate)
        act = ((up + 1.0) * glu).astype(_BF16)
        y = jnp.dot(act, wdb_ref[...], preferred_element_type=_F32) + bd_ref[...]
        packed = _pack_pairs(y.astype(_BF16).astype(_F32))
        row = lax.broadcasted_iota(jnp.int32, packed.shape, 0)
        mine = (row >= lo) & (row < hi)

        @pl.when(lo == 0)
        def _():
            ys_ref[...] = jnp.where(mine, packed, jnp.uint32(0))

        @pl.when(lo > 0)
        def _():
            ys_ref[...] = jnp.where(mine, packed, ys_ref[...])


def _expert_schedule(offs, n_rows, bm):
    nblk = n_rows // bm
    n_items = nblk + N_EXPERTS - 1
    used = offs[-1]
    bnd = offs[1:-1]
    pos = jnp.arange(N_EXPERTS - 1, dtype=jnp.int32) + jnp.minimum(bnd // bm + 1, nblk)
    i = jnp.arange(n_items, dtype=jnp.int32)[:, None]
    at = pos[None, :] == i
    is_bnd = jnp.any(at, axis=1)
    bnd_val = jnp.sum(jnp.where(at, bnd[None, :], 0), axis=1)
    n_before = jnp.sum((pos[None, :] < i).astype(jnp.int32), axis=1)
    cuts = jnp.where(is_bnd, bnd_val, (i[:, 0] - n_before) * bm)
    ends = jnp.concatenate([cuts[1:], jnp.array([n_rows], jnp.int32)])
    last_blk = jnp.maximum(used - 1, 0) // bm
    past = jnp.logical_not(is_bnd) & (cuts // bm > last_blk)
    raw_blk = cuts // bm
    cuts = jnp.minimum(cuts, used)
    ends = jnp.minimum(ends, used)
    blk = jnp.minimum(cuts // bm, last_blk)
    e = jnp.minimum(jnp.sum((bnd[None, :] <= cuts[:, None]).astype(jnp.int32), axis=1), N_EXPERTS - 1)
    lo = jnp.where(past, 0, cuts - blk * bm)
    hi = jnp.where(past, -1, jnp.minimum(ends - blk * bm, bm))
    return lax.cummax(jnp.where(past, raw_blk, blk)), e, lo, hi


def moe_experts(xs, offs, layer, w_up, b_up, w_down, b_down):
    n_rows = xs.shape[0]
    bm = MOE_BM
    blk, e, lo, hi = _expert_schedule(offs, n_rows, bm)
    n_items = n_rows // bm + N_EXPERTS - 1
    wmap = lambda i, b, e, lo, hi: (layer, e[i], 0, 0)
    grid_spec = pltpu.PrefetchScalarGridSpec(
        num_scalar_prefetch=4, grid=(n_items,),
        in_specs=[pl.BlockSpec((bm, PAIR_COLS), lambda i, b, e, lo, hi: (b[i], 0)),
                  pl.BlockSpec((None, None, D_MODEL, 2 * D_FF), wmap),
                  pl.BlockSpec((None, None, 1, 2 * D_FF), wmap),
                  pl.BlockSpec((None, None, D_FF, D_MODEL), wmap),
                  pl.BlockSpec((None, None, 1, D_MODEL), wmap)],
        out_specs=pl.BlockSpec((bm, PAIR_COLS), lambda i, b, e, lo, hi: (b[i], 0)),
        scratch_shapes=[pltpu.VMEM((D_MODEL, 2 * D_FF), _BF16), pltpu.VMEM((D_FF, D_MODEL), _BF16)])
    nl = w_up.shape[0]
    return pl.pallas_call(
        _expert_kernel, grid_spec=grid_spec,
        out_shape=jax.ShapeDtypeStruct((n_rows, PAIR_COLS), _U32),
        compiler_params=_cparams("arbitrary"), name="moe_experts",
    )(blk, e, lo, hi, xs, w_up, b_up.reshape(nl, N_EXPERTS, 1, -1), w_down, b_down.reshape(nl, N_EXPERTS, 1, -1))


def moe_layer(h, eidx, gates, tile_counts, layer, w_up, b_up, w_down, b_down, ln_g, ln_b):
    tables = _seg_tables(tile_counts)
    xs = moe_dispatch(h, eidx, tables)
    ys = moe_experts(xs, tables[4], layer, w_up, b_up, w_down, b_down)
    return moe_combine(ys, eidx, gates, h, tables, ln_g, ln_b)


def kernel(x, positions, attn_w_in, attn_w_out, ssm_w_in, ssm_conv_w, ssm_conv_b, ssm_dt_bias,
           ssm_a_log, ssm_d, ssm_norm_w, ssm_w_out, moe_w_router, moe_b_router, moe_w_up,
           moe_b_up, moe_w_down, moe_b_down, ln_mix_g, ln_mix_b, ln_ffn_g, ln_ffn_b):
    bsz, seq, _ = x.shape
    xf = x.reshape(bsz * seq, D_MODEL)

    w_qkv = attn_w_in[0].astype(_BF16)
    outs = [dilated_group_attention(qkv_proj_group(xf, w_qkv, positions, g), g, bsz, seq)
            for g in range(N_GROUPS)]
    h, eidx, gates, tile_counts = attn_out_norm_route(
        [o for o, _ in outs], [l for _, l in outs], xf, attn_w_out[0].astype(_BF16),
        ln_mix_g[0], ln_mix_b[0], moe_w_router[0], moe_b_router[0])
    xf = moe_layer(h, eidx, gates, tile_counts, 0, moe_w_up, moe_b_up, moe_w_down, moe_b_down,
                   ln_ffn_g[0], ln_ffn_b[0])

    gz, xs, bc, dt = ssm_in_proj(xf, ssm_w_in[0], ssm_conv_w[0], ssm_conv_b[0], ssm_dt_bias[0], seq)
    yn = ssd_mixer(gz, xs, bc, dt, ssm_a_log[0], ssm_d[0], ssm_norm_w[0], bsz, seq)
    h, eidx, gates, tile_counts = ssm_out_norm_route(
        yn, xf, ssm_w_out[0].astype(_BF16), ln_mix_g[1], ln_mix_b[1], moe_w_router[1], moe_b_router[1])
    xf = moe_layer(h, eidx, gates, tile_counts, 1, moe_w_up, moe_b_up, moe_w_down, moe_b_down,
                   ln_ffn_g[1], ln_ffn_b[1])
    return xf.reshape(bsz, seq, D_MODEL)
```

```python
import functools
import math

import jax
import jax.numpy as jnp
from jax import lax
from jax.experimental import pallas as pl
from jax.experimental.pallas import tpu as pltpu

D_MODEL = 1024
DEPTH = 2
DIL = (1, 4, 16)
N_GROUPS = 3
SUB_WIN = 128
HEAD_DIM = 128
HEADS = 8
ROT_DIM = 32
ROT_HALF = 16
ROPE_THETA = 500000.0
ATTN_PROJ = N_GROUPS * 3 * D_MODEL
SSM_INNER = 2048
SSM_P = 64
SSM_HEADS = 32
SSM_GROUPS = 4
SSM_GHEADS = 8
SSM_N = 128
SSM_CONV = 4
CHUNK = 128
SSM_BC = SSM_GROUPS * SSM_N
SSM_XBC = SSM_INNER + 2 * SSM_BC
N_EXPERTS = 32
TOP_K = 4
D_FF = 1024
SWIGLU_LIMIT = 7.0
SWIGLU_ALPHA = 1.702
DEEPNORM_ALPHA = (2 * DEPTH) ** 0.25
LN_EPS = 1e-5
RMS_EPS = 1e-5

LANES = 128
SUBLANES = 8
VMEM_LIMIT_BYTES = 56 * 1024 * 1024

NEG_BIG = -1e30

_F32 = jnp.float32
_BF16 = jnp.bfloat16


def _cparams(*sem):
    return pltpu.CompilerParams(dimension_semantics=sem, vmem_limit_bytes=VMEM_LIMIT_BYTES)


ROPE_PACK = LANES // ROT_DIM


def _rope_table_kernel(pos_ref, invf_ref, c_ref, sa_ref, sb_ref):
    pos = pos_ref[...].astype(_F32)
    lane = lax.broadcasted_iota(jnp.int32, (pos.shape[0], LANES), 1)
    p = jnp.zeros((pos.shape[0], LANES), _F32)
    group = lax.shift_right_logical(lane, ROT_DIM.bit_length() - 1)
    for q in range(ROPE_PACK):
        p = jnp.where(group == q, pos[:, q:q + 1], p)
    ang = p * invf_ref[...]
    cos = jnp.cos(ang)
    sin = jnp.sin(ang)
    for q in range(ROPE_PACK):
        shift = (LANES - q * ROT_DIM) % LANES
        cq = pltpu.roll(cos, shift, 1) if shift else cos
        sq = pltpu.roll(sin, shift, 1) if shift else sin
        c_ref[q] = jnp.where(lane < ROT_DIM, cq, 1.0)
        sa_ref[q] = jnp.where((lane >= ROT_HALF) & (lane < ROT_DIM), sq, 0.0)
        sb_ref[q] = jnp.where(lane < ROT_HALF, -sq, 0.0)


def rope_tables(positions):
    t = positions.size
    rows = t // ROPE_PACK
    tm = min(1024, rows)
    pos = positions.reshape(ROPE_PACK, rows).T
    j = jnp.arange(LANES)
    invf = (ROPE_THETA ** (-(2.0 * (j % ROT_HALF)).astype(_F32) / ROT_DIM)).astype(_F32).reshape(1, LANES)
    out = jax.ShapeDtypeStruct((ROPE_PACK, rows, LANES), _F32)
    tabs = pl.pallas_call(
        _rope_table_kernel,
        grid=(rows // tm,),
        in_specs=[pl.BlockSpec((tm, ROPE_PACK), lambda i: (i, 0)),
                  pl.BlockSpec((1, LANES), lambda i: (0, 0))],
        out_specs=[pl.BlockSpec((ROPE_PACK, tm, LANES), lambda i: (0, i, 0))] * 3,
        out_shape=[out, out, out],
        compiler_params=_cparams("parallel"),
        name="rope_tables",
    )(pos, invf)
    return [x.reshape(t, LANES) for x in tabs]


QKV_TM = 512
ATTN_ROWS = 2048
HEADS_PER_STEP = 2


def _qkv_proj_kernel(x_ref, w_ref, c_ref, sa_ref, sb_ref, o_ref, *, d, ns, tt):
    tm = x_ref.shape[0]
    xb = x_ref[...].astype(_BF16)
    if d > 1:
        dst = lax.broadcasted_iota(jnp.int32, (tm, tm), 0)
        src = lax.broadcasted_iota(jnp.int32, (tm, tm), 1)
        lt, ld = tt.bit_length() - 1, d.bit_length() - 1
        t_ = dst & (tt - 1)
        r_ = lax.shift_right_logical(dst, lt) & (d - 1)
        s_ = lax.shift_right_logical(dst, lt + ld)
        perm = (src == ((s_ * tt + t_) * d + r_)).astype(_BF16)
        xb = jnp.dot(perm, xb, preferred_element_type=_F32).astype(_BF16)
    c = c_ref[...]
    sa = sa_ref[...]
    sb = sb_ref[...]
    for comp in range(3):
        acc = jnp.dot(xb, w_ref[:, comp * D_MODEL:(comp + 1) * D_MODEL], preferred_element_type=_F32)
        for h in range(HEADS):
            sl = slice(h * HEAD_DIM, (h + 1) * HEAD_DIM)
            t = acc[:, sl]
            if comp < 2:
                t = t * c + pltpu.roll(t, ROT_HALF, 1) * sa + pltpu.roll(t, HEAD_DIM - ROT_HALF, 1) * sb
            o_ref[:, :, :, comp * D_MODEL + h * HEAD_DIM:comp * D_MODEL + (h + 1) * HEAD_DIM] = (
                t.reshape(ns, d, tt, HEAD_DIM).astype(o_ref.dtype))


def qkv_proj_group(x, w_bf16, positions, g):
    t, k = x.shape
    d = DIL[g]
    span = SUB_WIN * d
    tm = QKV_TM
    ns = max(1, tm // span)
    tps = max(1, span // tm)
    tt = tm // (ns * d)
    rope = rope_tables(positions.reshape(-1, ns, tt, d).swapaxes(2, 3).reshape(-1))
    tab_spec = pl.BlockSpec((tm, LANES), lambda i: (i, 0))
    return pl.pallas_call(
        functools.partial(_qkv_proj_kernel, d=d, ns=ns, tt=tt),
        grid=(t // tm,),
        in_specs=[pl.BlockSpec((tm, k), lambda i: (i, 0)),
                  pl.BlockSpec((k, 3 * D_MODEL), lambda i: (0, g)),
                  tab_spec, tab_spec, tab_spec],
        out_specs=pl.BlockSpec((ns, d, tt, 3 * D_MODEL), lambda i: (i // tps, 0, i % tps, 0)),
        out_shape=jax.ShapeDtypeStruct((t // span, d, SUB_WIN, 3 * D_MODEL), _BF16),
        compiler_params=_cparams("parallel"), name=f"qkv_proj_g{g}",
    )(x, w_bf16, *rope)


def _dil_attn_kernel(q_ref, k_ref, v_ref, kp_ref, vp_ref, o_ref, lse_ref, *, d, ns):
    j = pl.program_id(1)
    hp = pl.program_id(2)
    nb = ns * d
    row = lax.broadcasted_iota(jnp.int32, (1, SUB_WIN, SUB_WIN), 1)
    col = lax.broadcasted_iota(jnp.int32, (1, SUB_WIN, SUB_WIN), 2)
    cur_ok = col <= row
    blk = lax.broadcasted_iota(jnp.int32, (nb, 1, 1), 0)
    prev_ok = (col >= row) & ((blk >= d) | (j > 0))
    lane = lax.broadcasted_iota(jnp.int32, (SUB_WIN, LANES), 1)
    scale = 1.0 / math.sqrt(HEAD_DIM)

    def block_rows(b):
        s, r = divmod(b, d)
        start = s * SUB_WIN * d + r
        return pl.ds(start, SUB_WIN, stride=d) if d > 1 else pl.ds(start, SUB_WIN)

    @pl.when(hp == 0)
    def _():
        lse_ref[...] = jnp.zeros_like(lse_ref)

    lses = []
    for hh in range(HEADS_PER_STEP):
        sl = slice(hh * HEAD_DIM, (hh + 1) * HEAD_DIM)
        flat = lambda a: a.reshape(nb, SUB_WIN, HEAD_DIM)
        q, kc, vc = flat(q_ref[:, :, :, sl]), flat(k_ref[:, :, :, sl]), flat(v_ref[:, :, :, sl])
        if ns > 1:
            kp = flat(jnp.concatenate([kp_ref[:, :, :, sl], k_ref[:ns - 1, :, :, sl]], axis=0))
            vp = flat(jnp.concatenate([vp_ref[:, :, :, sl], v_ref[:ns - 1, :, :, sl]], axis=0))
        else:
            kp, vp = flat(kp_ref[:, :, :, sl]), flat(vp_ref[:, :, :, sl])
        sc = jnp.einsum('bqd,bkd->bqk', q, kc, preferred_element_type=_F32) * scale
        sp = jnp.einsum('bqd,bkd->bqk', q, kp, preferred_element_type=_F32) * scale
        sc = jnp.where(cur_ok, sc, NEG_BIG)
        sp = jnp.where(prev_ok, sp, NEG_BIG)
        m = jnp.maximum(jnp.max(sc, axis=2, keepdims=True), jnp.max(sp, axis=2, keepdims=True))
        pc = jnp.exp(sc - m)
        pp = jnp.exp(sp - m)
        l = jnp.sum(pc, axis=2, keepdims=True) + jnp.sum(pp, axis=2, keepdims=True)
        acc = jnp.einsum('bqk,bkd->bqd', pc.astype(_BF16), vc, preferred_element_type=_F32)
        acc += jnp.einsum('bqk,bkd->bqd', pp.astype(_BF16), vp, preferred_element_type=_F32)
        o = acc / l
        for b in range(nb):
            o_ref[hh, block_rows(b), :] = o[b]
        lses.append(m + jnp.log(l))
    for b in range(nb):
        cur = lse_ref[block_rows(b), :]
        for hh in range(HEADS_PER_STEP):
            cur = jnp.where(lane == hp * HEADS_PER_STEP + hh, lses[hh][b], cur)
        lse_ref[block_rows(b), :] = cur


def dilated_group_attention(qkv, g, bsz, seq):
    d = DIL[g]
    ns = ATTN_ROWS // (SUB_WIN * d)
    nsteps = seq // ATTN_ROWS
    hw = HEADS_PER_STEP * HEAD_DIM
    ncol = D_MODEL // hw
    cur = lambda comp: pl.BlockSpec((ns, d, SUB_WIN, hw), lambda b, j, hp: (b * nsteps + j, 0, 0, comp * ncol + hp))
    prev = lambda comp: pl.BlockSpec(
        (1, d, SUB_WIN, hw), lambda b, j, hp: (jnp.maximum((b * nsteps + j) * ns - 1, 0), 0, 0, comp * ncol + hp))
    return pl.pallas_call(
        functools.partial(_dil_attn_kernel, d=d, ns=ns),
        grid=(bsz, nsteps, HEADS // HEADS_PER_STEP),
        in_specs=[cur(0), cur(1), cur(2), prev(1), prev(2)],
        out_specs=[pl.BlockSpec((HEADS_PER_STEP, ATTN_ROWS, HEAD_DIM), lambda b, j, hp: (hp, b * nsteps + j, 0)),
                   pl.BlockSpec((ATTN_ROWS, LANES), lambda b, j, hp: (b * nsteps + j, 0))],
        out_shape=[jax.ShapeDtypeStruct((HEADS, bsz * seq, HEAD_DIM), _F32),
                   jax.ShapeDtypeStruct((bsz * seq, LANES), _F32)],
        compiler_params=_cparams("parallel", "parallel", "arbitrary"),
        name=f"dilated_attn_g{g}",
    )(qkv, qkv, qkv, qkv, qkv)


SSM_IN_TM = 512
HALO = SUBLANES
ZX_COLS = SSM_INNER + SSM_XBC


def _softplus(x):
    return jnp.maximum(x, 0.0) + jnp.log1p(jnp.exp(-jnp.abs(x)))


def _ssm_in_kernel(x_ref, w_ref, wdt_ref, cw_ref, cb_ref, dtb_ref, gz_ref, xs_ref, bc_ref, dt_ref,
                   carry_ref, *, tiles_per_seq):
    tm = x_ref.shape[0]
    cw = D_MODEL
    xb = x_ref[...].astype(_BF16)

    @pl.when(pl.program_id(0) % tiles_per_seq == 0)
    def _():
        carry_ref[...] = jnp.zeros_like(carry_ref)

    for i in range(SSM_INNER // cw):
        z = jnp.dot(xb, w_ref[:, i * cw:(i + 1) * cw], preferred_element_type=_F32)
        gz_ref[:, i * cw:(i + 1) * cw] = z * jax.nn.sigmoid(z)
    for c in range(SSM_XBC // cw):
        cols = slice(c * cw, (c + 1) * cw)
        acc = jnp.dot(xb, w_ref[:, SSM_INNER + c * cw:SSM_INNER + (c + 1) * cw], preferred_element_type=_F32)
        ext = jnp.concatenate([carry_ref[c], acc], axis=0)
        carry_ref[c] = acc[tm - HALO:, :]
        conv = cb_ref[:, cols] + cw_ref[SSM_CONV - 1:SSM_CONV, cols] * acc
        for k in range(1, SSM_CONV):
            conv = conv + cw_ref[SSM_CONV - 1 - k:SSM_CONV - k, cols] * pltpu.roll(ext, k, 0)[HALO:, :]
        act = conv * jax.nn.sigmoid(conv)
        if (c + 1) * cw <= SSM_INNER:
            xs_ref[:, cols] = act
        else:
            bc_ref[...] = act.astype(_BF16)
    dt_raw = jnp.dot(xb, wdt_ref[...], preferred_element_type=_F32)
    dt_ref[...] = _softplus(dt_raw + dtb_ref[...])


def ssm_in_proj(x, w_in, conv_w, conv_b, dt_bias, seq):
    t, k = x.shape
    tm = SSM_IN_TM
    assert 2 * SSM_BC == D_MODEL
    pad = LANES - SSM_HEADS
    w_main = w_in[:, :ZX_COLS].astype(_BF16)
    w_dt = jnp.pad(w_in[:, ZX_COLS:], ((0, 0), (0, pad))).astype(_BF16)
    dtb = jnp.pad(dt_bias, (0, pad)).reshape(1, LANES)
    cst = lambda shape: pl.BlockSpec(shape, lambda i: (0,) * len(shape))
    once = lambda shape: pl.BlockSpec(shape, lambda i: (0,) * len(shape), pipeline_mode=pl.Buffered(1))
    row = lambda w: pl.BlockSpec((tm, w), lambda i: (i, 0))
    return pl.pallas_call(
        functools.partial(_ssm_in_kernel, tiles_per_seq=seq // tm),
        grid=(t // tm,),
        in_specs=[row(k), once((k, ZX_COLS)), once((k, LANES)),
                  cst((SSM_CONV, SSM_XBC)), cst((1, SSM_XBC)), cst((1, LANES))],
        out_specs=[row(SSM_INNER), row(SSM_INNER), row(2 * SSM_BC), row(LANES)],
        out_shape=[jax.ShapeDtypeStruct((t, SSM_INNER), _F32), jax.ShapeDtypeStruct((t, SSM_INNER), _F32),
                   jax.ShapeDtypeStruct((t, 2 * SSM_BC), _BF16), jax.ShapeDtypeStruct((t, LANES), _F32)],
        scratch_shapes=[pltpu.VMEM((SSM_XBC // D_MODEL, HALO, D_MODEL), _F32)],
        compiler_params=_cparams("arbitrary"), name="ssm_in_proj",
    )(x, w_main, w_dt, conv_w, conv_b.reshape(1, -1), dtb)


def _ssd_kernel(gz_ref, xs_ref, bc_ref, dt_ref, a_ref, dsk_ref, nw_ref, o_ref, state_ref, y_ref):
    c = pl.program_id(1)
    q = CHUNK

    @pl.when(c == 0)
    def _():
        state_ref[...] = jnp.zeros_like(state_ref)

    dt = dt_ref[...]
    da = dt * a_ref[...]
    r_io = lax.broadcasted_iota(jnp.int32, (q, q), 0)
    c_io = lax.broadcasted_iota(jnp.int32, (q, q), 1)
    causal = r_io >= c_io
    tri = causal.astype(_F32)
    a_cum = jnp.dot(tri, da, preferred_element_type=_F32, precision=lax.Precision.HIGHEST)
    a_cum_t = a_cum.T
    dt_t = dt.T
    a_last = a_cum[q - 1:q, :]
    w_state = jnp.exp(a_last - a_cum) * dt
    e_acum = jnp.exp(a_cum)
    chunk_decay = jnp.exp(a_last)
    lane = lax.broadcasted_iota(jnp.int32, (q, LANES), 1)
    low = lane < SSM_P
    lane1 = lax.broadcasted_iota(jnp.int32, (1, LANES), 1)
    low1 = lane1 < SSM_P
    tn = (((0,), (0,)), ((), ()))
    nt = (((1,), (1,)), ((), ()))

    for g in range(SSM_GROUPS):
        bg = bc_ref[:, g * SSM_N:(g + 1) * SSM_N]
        cg = bc_ref[:, SSM_BC + g * SSM_N:SSM_BC + (g + 1) * SSM_N]
        cbm = lax.dot_general(cg, bg, nt, preferred_element_type=_F32)
        st = state_ref[g]
        y_off = jnp.dot(cg, st.astype(_BF16), preferred_element_type=_F32)
        xw_parts, dec_parts = [], []
        for j in range(SSM_GHEADS // 2):
            h0 = g * SSM_GHEADS + 2 * j
            col = (g * SSM_GHEADS // 2 + j) * LANES
            xp = xs_ref[:, col:col + LANES]
            y_diag = jnp.zeros((q, LANES), _F32)
            for half, hh in enumerate((h0, h0 + 1)):
                diff = a_cum[:, hh:hh + 1] - a_cum_t[hh:hh + 1, :]
                lmat = jnp.exp(jnp.where(causal, diff, NEG_BIG))
                mm = (cbm * lmat * dt_t[hh:hh + 1, :]).astype(_BF16)
                keep = low if half == 0 else jnp.logical_not(low)
                xh = jnp.where(keep, xp, 0.0).astype(_BF16)
                y_diag = y_diag + jnp.dot(mm, xh, preferred_element_type=_F32)
            e_pair = jnp.where(low, e_acum[:, h0:h0 + 1], e_acum[:, h0 + 1:h0 + 2])
            w_pair = jnp.where(low, w_state[:, h0:h0 + 1], w_state[:, h0 + 1:h0 + 2])
            y_ref[:, col:col + LANES] = (y_diag + y_off[:, j * LANES:(j + 1) * LANES] * e_pair
                                         + xp * dsk_ref[:, col:col + LANES])
            xw_parts.append((xp * w_pair).astype(_BF16))
            dec_parts.append(jnp.where(low1, chunk_decay[:, h0:h0 + 1], chunk_decay[:, h0 + 1:h0 + 2]))
        xw = jnp.concatenate(xw_parts, axis=1)
        dec = jnp.concatenate(dec_parts, axis=1)
        state_ref[g] = st * dec + lax.dot_general(bg, xw, tn, preferred_element_type=_F32)

    yg = y_ref[...] * gz_ref[...]
    gw = SSM_INNER // SSM_GROUPS
    for g in range(SSM_GROUPS):
        v = yg[:, g * gw:(g + 1) * gw]
        ms = jnp.mean(v * v, axis=-1, keepdims=True)
        o_ref[:, g * gw:(g + 1) * gw] = (v * lax.rsqrt(ms + RMS_EPS)
                                         * nw_ref[:, g * gw:(g + 1) * gw]).astype(o_ref.dtype)


def ssd_mixer(gz, xs, bc, dt, a_log, d_skip, norm_w, bsz, seq):
    nc = seq // CHUNK
    a = jnp.pad(-jnp.exp(a_log.astype(_F32)), (0, LANES - SSM_HEADS)).reshape(1, LANES)
    dsk = jnp.repeat(d_skip.astype(_F32), SSM_P).reshape(1, SSM_INNER)
    cst = lambda shape: pl.BlockSpec(shape, lambda b, c: (0,) * len(shape))
    row = lambda w: pl.BlockSpec((CHUNK, w), lambda b, c: (b * nc + c, 0))
    return pl.pallas_call(
        _ssd_kernel,
        grid=(bsz, nc),
        in_specs=[row(SSM_INNER), row(SSM_INNER), row(2 * SSM_BC), row(LANES),
                  cst((1, LANES)), cst((1, SSM_INNER)), cst((1, SSM_INNER))],
        out_specs=row(SSM_INNER),
        out_shape=jax.ShapeDtypeStruct((bsz * seq, SSM_INNER), _BF16),
        scratch_shapes=[pltpu.VMEM((SSM_GROUPS, SSM_N, SSM_GHEADS * SSM_P), _F32),
                        pltpu.VMEM((CHUNK, SSM_INNER), _F32)],
        compiler_params=_cparams("parallel", "arbitrary"), name="ssd_mixer",
    )(gz, xs, bc, dt, a, dsk, norm_w.reshape(1, -1))


ROUTE_SUB = 256
ROUTE_TM_ATTN = 512
ROUTE_TM_SSM = 1024


def _layer_norm(y, g, b):
    mu = jnp.mean(y, axis=-1, keepdims=True)
    yc = y - mu
    var = jnp.mean(yc * yc, axis=-1, keepdims=True)
    return yc * lax.rsqrt(var + LN_EPS) * g + b


def _route(h1s, wrt_ref, br_ref, eidx_ref, gate_ref, tcnt_ref):
    sub = ROUTE_SUB
    n = len(h1s)
    nt = (((1,), (1,)), ((), ()))
    wrt = wrt_ref[...]
    vals = [lax.dot_general(wrt, h, nt, preferred_element_type=_F32, precision=lax.Precision.HIGHEST)
            + br_ref[...] for h in h1s]
    eio = lax.broadcasted_iota(jnp.int32, (N_EXPERTS, sub), 0)
    tops, ids, hots = ([[] for _ in range(n)] for _ in range(3))
    for _ in range(TOP_K):
        for a in range(n):
            mk = jnp.max(vals[a], axis=0, keepdims=True)
            ik = jnp.min(jnp.where(vals[a] == mk, eio, N_EXPERTS), axis=0, keepdims=True)
            hot = eio == ik
            vals[a] = jnp.where(hot, -jnp.inf, vals[a])
            tops[a].append(mk)
            ids[a].append(ik)
            hots[a].append(hot)
    col = lax.broadcasted_iota(jnp.int32, tcnt_ref.shape, 1)
    tcnt = tcnt_ref[...]
    for a in range(n):
        sl = slice(a * sub, (a + 1) * sub)
        exps = [jnp.exp(t - tops[a][0]) for t in tops[a]]
        den = exps[0] + exps[1] + exps[2] + exps[3]
        for k in range(TOP_K):
            eidx_ref[k:k + 1, sl] = ids[a][k]
            gate_ref[k:k + 1, sl] = exps[k] / den
        hot_all = hots[a][0] | hots[a][1] | hots[a][2] | hots[a][3]
        counts = jnp.sum(hot_all.astype(_F32), axis=1, keepdims=True)
        tcnt = jnp.where(col == pl.program_id(0) * n + a, counts, tcnt)
    tcnt_ref[...] = tcnt


def _norm_and_route(mixes, x_ref, g_ref, b_ref, wrt_ref, br_ref, h_ref, eidx_ref, gate_ref, tcnt_ref):
    sub = ROUTE_SUB

    @pl.when(pl.program_id(0) == 0)
    def _():
        tcnt_ref[...] = jnp.zeros_like(tcnt_ref)

    h1s = [_layer_norm(DEEPNORM_ALPHA * x_ref[a * sub:(a + 1) * sub, :] + mix, g_ref[...], b_ref[...])
           for a, mix in enumerate(mixes)]
    for a, h1 in enumerate(h1s):
        h_ref[a * sub:(a + 1) * sub, :] = h1
    _route(h1s, wrt_ref, br_ref, eidx_ref, gate_ref, tcnt_ref)


def _attn_out_kernel(o0_ref, o1_ref, o2_ref, l0_ref, l1_ref, l2_ref, x_ref, wo_ref, g_ref, b_ref,
                     wrt_ref, br_ref, h_ref, eidx_ref, gate_ref, tcnt_ref, ob_ref):
    sub = ROUTE_SUB
    l0, l1, l2 = l0_ref[...], l1_ref[...], l2_ref[...]
    m = jnp.maximum(jnp.maximum(l0, l1), l2)
    e0, e1, e2 = jnp.exp(l0 - m), jnp.exp(l1 - m), jnp.exp(l2 - m)
    inv = 1.0 / (e0 + e1 + e2)
    w0, w1, w2 = e0 * inv, e1 * inv, e2 * inv
    for h in range(HEADS):
        o = w0[:, h:h + 1] * o0_ref[h] + w1[:, h:h + 1] * o1_ref[h] + w2[:, h:h + 1] * o2_ref[h]
        ob_ref[:, h * HEAD_DIM:(h + 1) * HEAD_DIM] = o.astype(_BF16)
    mixes = [jnp.dot(ob_ref[a * sub:(a + 1) * sub, :], wo_ref[...], preferred_element_type=_F32)
             for a in range(x_ref.shape[0] // sub)]
    _norm_and_route(mixes, x_ref, g_ref, b_ref, wrt_ref, br_ref, h_ref, eidx_ref, gate_ref, tcnt_ref)


def _ssm_out_kernel(y_ref, x_ref, wo_ref, g_ref, b_ref, wrt_ref, br_ref, h_ref, eidx_ref, gate_ref, tcnt_ref):
    sub = ROUTE_SUB
    mixes = [jnp.dot(y_ref[a * sub:(a + 1) * sub, :], wo_ref[...], preferred_element_type=_F32)
             for a in range(x_ref.shape[0] // sub)]
    _norm_and_route(mixes, x_ref, g_ref, b_ref, wrt_ref, br_ref, h_ref, eidx_ref, gate_ref, tcnt_ref)


def _route_out_specs(t, tm):
    nsub = t // ROUTE_SUB
    specs = [pl.BlockSpec((tm, D_MODEL), lambda i: (i, 0)),
             pl.BlockSpec((TOP_K, tm), lambda i: (0, i)),
             pl.BlockSpec((TOP_K, tm), lambda i: (0, i)),
             pl.BlockSpec((N_EXPERTS, nsub), lambda i: (0, 0))]
    shapes = [jax.ShapeDtypeStruct((t, D_MODEL), _F32),
              jax.ShapeDtypeStruct((TOP_K, t), jnp.int32),
              jax.ShapeDtypeStruct((TOP_K, t), _F32),
              jax.ShapeDtypeStruct((N_EXPERTS, nsub), _F32)]
    return specs, shapes


def _const_spec(shape):
    return pl.BlockSpec(shape, lambda i: (0,) * len(shape))


def attn_out_norm_route(os_, lses, x, wo_bf16, ln_g, ln_b, w_router, b_router):
    t = x.shape[0]
    tm = ROUTE_TM_ATTN
    row = lambda w: pl.BlockSpec((tm, w), lambda i: (i, 0))
    out_specs, out_shapes = _route_out_specs(t, tm)
    return pl.pallas_call(
        _attn_out_kernel,
        grid=(t // tm,),
        in_specs=[pl.BlockSpec((HEADS, tm, HEAD_DIM), lambda i: (0, i, 0))] * 3 + [row(LANES)] * 3 + [row(D_MODEL),
                  _const_spec((D_MODEL, D_MODEL)), _const_spec((1, D_MODEL)), _const_spec((1, D_MODEL)),
                  _const_spec((N_EXPERTS, D_MODEL)), _const_spec((N_EXPERTS, 1))],
        out_specs=out_specs, out_shape=out_shapes,
        scratch_shapes=[pltpu.VMEM((tm, D_MODEL), _BF16)],
        compiler_params=_cparams("arbitrary"), name="attn_out_norm_route",
    )(*os_, *lses, x, wo_bf16, ln_g.reshape(1, -1), ln_b.reshape(1, -1),
      w_router.T, b_router.reshape(-1, 1))


def ssm_out_norm_route(yn, x, wo_bf16, ln_g, ln_b, w_router, b_router):
    t = x.shape[0]
    tm = ROUTE_TM_SSM
    row = lambda w: pl.BlockSpec((tm, w), lambda i: (i, 0))
    out_specs, out_shapes = _route_out_specs(t, tm)
    return pl.pallas_call(
        _ssm_out_kernel,
        grid=(t // tm,),
        in_specs=[row(SSM_INNER), row(D_MODEL),
                  _const_spec((SSM_INNER, D_MODEL)), _const_spec((1, D_MODEL)), _const_spec((1, D_MODEL)),
                  _const_spec((N_EXPERTS, D_MODEL)), _const_spec((N_EXPERTS, 1))],
        out_specs=out_specs, out_shape=out_shapes,
        compiler_params=_cparams("arbitrary"), name="ssm_out_norm_route",
    )(yn, x, wo_bf16, ln_g.reshape(1, -1), ln_b.reshape(1, -1), w_router.T, b_router.reshape(-1, 1))


MOE_BM = 1024
MOE_SUB_BM = 256
MOE_SUB = ROUTE_SUB
SEG_ALIGN = SUBLANES
LOCAL_ROWS = MOE_SUB * TOP_K + N_EXPERTS * SUBLANES
SEG_UNIT = 16
SEG_REST = (8,)
TAIL_SIZES = tuple(SEG_ALIGN << k for k in reversed(range((MOE_BM // SEG_ALIGN).bit_length() - 1)))
PAIR_COLS = D_MODEL // 2
_U32 = jnp.uint32


def _pack_pairs(x):
    lo = lax.shift_right_logical(lax.bitcast_convert_type(x[:, :PAIR_COLS], _U32), jnp.uint32(16))
    hi = lax.bitcast_convert_type(x[:, PAIR_COLS:], _U32) & jnp.uint32(0xFFFF0000)
    return hi | lo


def _unpack_pairs(u):
    lo = lax.bitcast_convert_type(lax.shift_left(u, jnp.uint32(16)), _F32).astype(_BF16)
    hi = lax.bitcast_convert_type(u & jnp.uint32(0xFFFF0000), _F32).astype(_BF16)
    return lo, hi


def _local_positions(eidx):
    sub = eidx.shape[1]
    eio = lax.broadcasted_iota(jnp.int32, (N_EXPERTS, sub), 0)
    hots = [eio == eidx[k:k + 1, :] for k in range(TOP_K)]
    hot_all = (hots[0] | hots[1] | hots[2] | hots[3]).astype(_BF16)
    r_io = lax.broadcasted_iota(jnp.int32, (sub, sub), 0)
    c_io = lax.broadcasted_iota(jnp.int32, (sub, sub), 1)
    upper = (r_io < c_io).astype(_BF16)
    prefix = jnp.dot(hot_all, upper, preferred_element_type=_F32)
    counts = jnp.sum(hot_all.astype(_F32), axis=1, keepdims=True).astype(jnp.int32)
    padded = ((counts + (SEG_ALIGN - 1)) & (-SEG_ALIGN)).astype(_F32)
    e_r = lax.broadcasted_iota(jnp.int32, (N_EXPERTS, N_EXPERTS), 0)
    e_c = lax.broadcasted_iota(jnp.int32, (N_EXPERTS, N_EXPERTS), 1)
    below = (e_c < e_r).astype(_BF16)
    seg_start = jnp.dot(below, jnp.broadcast_to(padded, (N_EXPERTS, sub)).astype(_BF16),
                        preferred_element_type=_F32)
    base = seg_start + prefix
    return [jnp.sum(jnp.where(h, base, 0.0), axis=0, keepdims=True) for h in hots]


def _segment_copies(cnt_ref, lo_ref, go_ref, i, copy_fn):
    for e in range(N_EXPERTS):
        n = cnt_ref[i * N_EXPERTS + e]
        lo = lo_ref[i * N_EXPERTS + e]
        go = go_ref[i * N_EXPERTS + e]

        def unit(u, c, lo=lo, go=go):
            copy_fn(pl.multiple_of(lo + u * SEG_UNIT, SEG_ALIGN), pl.multiple_of(go + u * SEG_UNIT, SEG_ALIGN),
                    SEG_UNIT)
            return c

        lax.fori_loop(0, n // SEG_UNIT, unit, 0)
        for sz in SEG_REST:
            done = n & -(2 * sz)

            @pl.when((n & sz) != 0)
            def _(done=done, sz=sz, lo=lo, go=go):
                copy_fn(pl.multiple_of(lo + done, SEG_ALIGN), pl.multiple_of(go + done, SEG_ALIGN), sz)


def _dispatch_kernel(cnt_ref, lo_ref, go_ref, units_ref, tail_ref, eidx_ref, h_ref, xs_ref, buf_ref, sem):
    i = pl.program_id(0)
    slot = i % 2
    mine, other = buf_ref.at[slot], buf_ref.at[1 - slot]
    lpos = _local_positions(eidx_ref[...])
    p_io = lax.broadcasted_iota(jnp.int32, (LOCAL_ROWS, MOE_SUB), 0).astype(_F32)
    sel = (p_io == lpos[0]) | (p_io == lpos[1]) | (p_io == lpos[2]) | (p_io == lpos[3])
    mine[...] = _pack_pairs(jnp.dot(sel.astype(_BF16), h_ref[...].astype(_BF16), preferred_element_type=_F32))

    def copy(lo, go, sz):
        pltpu.make_async_copy(mine.at[pl.ds(lo, sz)], xs_ref.at[pl.ds(go, sz)], sem.at[slot]).start()

    _segment_copies(cnt_ref, lo_ref, go_ref, i, copy)

    def wait_units(buf, s, n_units):
        def body(j, c):
            pltpu.make_async_copy(buf.at[pl.ds(0, SEG_ALIGN)], xs_ref.at[pl.ds(0, SEG_ALIGN)], sem.at[s]).wait()
            return c
        lax.fori_loop(0, n_units, body, 0)

    @pl.when(i > 0)
    def _():
        wait_units(other, 1 - slot, units_ref[jnp.maximum(i - 1, 0)])

    @pl.when(i == pl.num_programs(0) - 1)
    def _():
        wait_units(mine, slot, units_ref[i])
        mine[0:MOE_BM, :] = jnp.zeros((MOE_BM, PAIR_COLS), _U32)
        start, n, n_blocks = tail_ref[0], tail_ref[1], tail_ref[2]
        done = jnp.int32(0)
        for sz in TAIL_SIZES:
            @pl.when((n & sz) != 0)
            def _(done=done, sz=sz):
                pltpu.make_async_copy(mine.at[pl.ds(0, sz)],
                                      xs_ref.at[pl.ds(pl.multiple_of(start + done, SEG_ALIGN), sz)],
                                      sem.at[slot]).start()
            done = done + (n & sz)

        def zero_block(j, c):
            row0 = pl.multiple_of(start + n + j * MOE_BM, MOE_BM)
            pltpu.make_async_copy(mine.at[pl.ds(0, MOE_BM)], xs_ref.at[pl.ds(row0, MOE_BM)], sem.at[slot]).start()
            return c

        lax.fori_loop(0, n_blocks, zero_block, 0)
        wait_units(mine, slot, n // SEG_ALIGN + n_blocks * (MOE_BM // SEG_ALIGN))


def _seg_tables(tile_counts):
    cnt = tile_counts.T.astype(jnp.int32)
    pc = (cnt + (SEG_ALIGN - 1)) & (-SEG_ALIGN)
    lo = jnp.cumsum(pc, axis=1) - pc
    tot = jnp.sum(pc, axis=0)
    offs = jnp.concatenate([jnp.zeros((1,), jnp.int32), jnp.cumsum(tot)])
    go = offs[None, :-1] + jnp.cumsum(pc, axis=0) - pc
    units = jnp.sum(pc, axis=1) // SEG_ALIGN
    return pc.reshape(-1), lo.reshape(-1), go.reshape(-1), units.astype(jnp.int32), offs.astype(jnp.int32)


def _max_rows(t):
    n = t * TOP_K + (t // MOE_SUB) * N_EXPERTS * (SEG_ALIGN - 1)
    return -(-n // MOE_BM) * MOE_BM


def moe_dispatch(h, eidx, tables):
    t = h.shape[0]
    pc, lo, go, units, offs = tables
    used = offs[-1]
    partial = (-used) % MOE_BM
    tail = jnp.stack([used, partial, (_max_rows(t) - used - partial) // MOE_BM]).astype(jnp.int32)
    grid_spec = pltpu.PrefetchScalarGridSpec(
        num_scalar_prefetch=5, grid=(t // MOE_SUB,),
        in_specs=[pl.BlockSpec((TOP_K, MOE_SUB), lambda i, *_: (0, i)),
                  pl.BlockSpec((MOE_SUB, D_MODEL), lambda i, *_: (i, 0))],
        out_specs=pl.BlockSpec(memory_space=pl.ANY),
        scratch_shapes=[pltpu.VMEM((2, LOCAL_ROWS, PAIR_COLS), _U32), pltpu.SemaphoreType.DMA((2,))])
    return pl.pallas_call(
        _dispatch_kernel, grid_spec=grid_spec,
        out_shape=jax.ShapeDtypeStruct((_max_rows(t), PAIR_COLS), _U32),
        compiler_params=_cparams("arbitrary"), name="moe_dispatch",
    )(pc, lo, go, units, tail, eidx, h)


def _combine_kernel(cnt_ref, lo_ref, go_ref, units_ref, eidx_ref, gate_ref, ys_ref, h_ref, g_ref, b_ref,
                    o_ref, buf_ref, sem):
    i = pl.program_id(0)
    slot = i % 2

    def fetch(step, s):
        dst = buf_ref.at[s]
        dst[MOE_SUB * TOP_K:, :] = jnp.zeros((LOCAL_ROWS - MOE_SUB * TOP_K, PAIR_COLS), _U32)

        def copy(lo, go, sz):
            pltpu.make_async_copy(ys_ref.at[pl.ds(go, sz)], dst.at[pl.ds(lo, sz)], sem.at[s]).start()

        _segment_copies(cnt_ref, lo_ref, go_ref, step, copy)

    @pl.when(i == 0)
    def _():
        fetch(i, slot)

    @pl.when(i + 1 < pl.num_programs(0))
    def _():
        fetch(i + 1, 1 - slot)

    mine = buf_ref.at[slot]
    lpos = _local_positions(eidx_ref[...])
    gates = gate_ref[...]
    p_io = lax.broadcasted_iota(jnp.int32, (LOCAL_ROWS, MOE_SUB), 0).astype(_F32)
    w = jnp.zeros((LOCAL_ROWS, MOE_SUB), _F32)
    for k in range(TOP_K):
        w = w + jnp.where(p_io == lpos[k], gates[k:k + 1, :], 0.0)
    w_hi = w.astype(_BF16)
    w_lo = (w - w_hi.astype(_F32)).astype(_BF16)

    def wait_unit(j, c):
        pltpu.make_async_copy(ys_ref.at[pl.ds(0, SEG_ALIGN)], mine.at[pl.ds(0, SEG_ALIGN)], sem.at[slot]).wait()
        return c

    lax.fori_loop(0, units_ref[i], wait_unit, 0)
    tn = (((0,), (0,)), ((), ()))
    ffn = jnp.concatenate(
        [lax.dot_general(w_hi, yb, tn, preferred_element_type=_F32)
         + lax.dot_general(w_lo, yb, tn, preferred_element_type=_F32) for yb in _unpack_pairs(mine[...])],
        axis=1)
    o_ref[...] = _layer_norm(DEEPNORM_ALPHA * h_ref[...] + ffn, g_ref[...], b_ref[...])


def moe_combine(ys, eidx, gates, h, tables, ln_g, ln_b):
    t = h.shape[0]
    pc, lo, go, units, _ = tables
    row = pl.BlockSpec((MOE_SUB, D_MODEL), lambda i, *_: (i, 0))
    kt = pl.BlockSpec((TOP_K, MOE_SUB), lambda i, *_: (0, i))
    cst = lambda shape: pl.BlockSpec(shape, lambda i, *_: (0,) * len(shape))
    grid_spec = pltpu.PrefetchScalarGridSpec(
        num_scalar_prefetch=4, grid=(t // MOE_SUB,),
        in_specs=[kt, kt, pl.BlockSpec(memory_space=pl.ANY), row, cst((1, D_MODEL)), cst((1, D_MODEL))],
        out_specs=row,
        scratch_shapes=[pltpu.VMEM((2, LOCAL_ROWS, PAIR_COLS), _U32), pltpu.SemaphoreType.DMA((2,))])
    return pl.pallas_call(
        _combine_kernel, grid_spec=grid_spec,
        out_shape=jax.ShapeDtypeStruct((t, D_MODEL), _F32),
        compiler_params=_cparams("arbitrary"), name="moe_combine",
    )(pc, lo, go, units, eidx, gates, ys, h, ln_g.reshape(1, -1), ln_b.reshape(1, -1))


def _expert_kernel(blk_ref, e_ref, lo_ref, hi_ref, xs_ref, wu_ref, bu_ref, wd_ref, bd_ref,
                   ys_ref, wub_ref, wdb_ref):
    i = pl.program_id(0)
    e = e_ref[i]
    e_prev = e_ref[jnp.maximum(i - 1, 0)]

    @pl.when((i == 0) | (e != e_prev))
    def _():
        wub_ref[...] = wu_ref[...].astype(_BF16)
        wdb_ref[...] = wd_ref[...].astype(_BF16)

    lo = lo_ref[i]
    hi = hi_ref[i]

    @pl.when((lo == 0) | (hi < 0))
    def _():
        ys_ref[...] = jnp.zeros_like(ys_ref)

    for sb in range(xs_ref.shape[0] // MOE_SUB_BM):
        r0 = sb * MOE_SUB_BM
        rows = slice(r0, r0 + MOE_SUB_BM)

        @pl.when((hi > lo) & (hi > r0) & (lo < r0 + MOE_SUB_BM))
        def _(r0=r0, rows=rows):
            x_lo, x_hi = _unpack_pairs(xs_ref[rows, :])
            h = (jnp.dot(x_lo, wub_ref[:PAIR_COLS, :], preferred_element_type=_F32)
                 + jnp.dot(x_hi, wub_ref[PAIR_COLS:, :], preferred_element_type=_F32) + bu_ref[...])
            gate = jnp.minimum(h[:, :D_FF], SWIGLU_LIMIT)
            up = jnp.clip(h[:, D_FF:], -SWIGLU_LIMIT, SWIGLU_LIMIT)
            glu = gate * jax.nn.sigmoid(SWIGLU_ALPHA * gate)
            act = ((up + 1.0) * glu).astype(_BF16)
            y = jnp.dot(act, wdb_ref[...], preferred_element_type=_F32) + bd_ref[...]
            packed = _pack_pairs(y.astype(_BF16).astype(_F32))
            row = r0 + lax.broadcasted_iota(jnp.int32, packed.shape, 0)
            ys_ref[rows, :] = jnp.where((row >= lo) & (row < hi), packed, ys_ref[rows, :])


def _expert_schedule(offs, n_rows, bm):
    nblk = n_rows // bm
    n_items = nblk + N_EXPERTS - 1
    used = offs[-1]
    bnd = offs[1:-1]
    pos = jnp.arange(N_EXPERTS - 1, dtype=jnp.int32) + jnp.minimum(bnd // bm + 1, nblk)
    i = jnp.arange(n_items, dtype=jnp.int32)[:, None]
    at = pos[None, :] == i
    is_bnd = jnp.any(at, axis=1)
    bnd_val = jnp.sum(jnp.where(at, bnd[None, :], 0), axis=1)
    n_before = jnp.sum((pos[None, :] < i).astype(jnp.int32), axis=1)
    cuts = jnp.where(is_bnd, bnd_val, (i[:, 0] - n_before) * bm)
    ends = jnp.concatenate([cuts[1:], jnp.array([n_rows], jnp.int32)])
    last_blk = jnp.maximum(used - 1, 0) // bm
    past = jnp.logical_not(is_bnd) & (cuts // bm > last_blk)
    raw_blk = cuts // bm
    cuts = jnp.minimum(cuts, used)
    ends = jnp.minimum(ends, used)
    blk = jnp.minimum(cuts // bm, last_blk)
    e = jnp.minimum(jnp.sum((bnd[None, :] <= cuts[:, None]).astype(jnp.int32), axis=1), N_EXPERTS - 1)
    lo = jnp.where(past, 0, cuts - blk * bm)
    hi = jnp.where(past, -1, jnp.minimum(ends - blk * bm, bm))
    return lax.cummax(jnp.where(past, raw_blk, blk)), e, lo, hi


def moe_experts(xs, offs, layer, w_up, b_up, w_down, b_down):
    n_rows = xs.shape[0]
    bm = MOE_BM
    blk, e, lo, hi = _expert_schedule(offs, n_rows, bm)
    n_items = n_rows // bm + N_EXPERTS - 1
    wmap = lambda i, b, e, lo, hi: (layer, e[i], 0, 0)
    grid_spec = pltpu.PrefetchScalarGridSpec(
        num_scalar_prefetch=4, grid=(n_items,),
        in_specs=[pl.BlockSpec((bm, PAIR_COLS), lambda i, b, e, lo, hi: (b[i], 0)),
                  pl.BlockSpec((None, None, D_MODEL, 2 * D_FF), wmap),
                  pl.BlockSpec((None, None, 1, 2 * D_FF), wmap),
                  pl.BlockSpec((None, None, D_FF, D_MODEL), wmap),
                  pl.BlockSpec((None, None, 1, D_MODEL), wmap)],
        out_specs=pl.BlockSpec((bm, PAIR_COLS), lambda i, b, e, lo, hi: (b[i], 0)),
        scratch_shapes=[pltpu.VMEM((D_MODEL, 2 * D_FF), _BF16), pltpu.VMEM((D_FF, D_MODEL), _BF16)])
    nl = w_up.shape[0]
    return pl.pallas_call(
        _expert_kernel, grid_spec=grid_spec,
        out_shape=jax.ShapeDtypeStruct((n_rows, PAIR_COLS), _U32),
        compiler_params=_cparams("arbitrary"), name="moe_experts",
    )(blk, e, lo, hi, xs, w_up, b_up.reshape(nl, N_EXPERTS, 1, -1), w_down, b_down.reshape(nl, N_EXPERTS, 1, -1))


def moe_layer(h, eidx, gates, tile_counts, layer, w_up, b_up, w_down, b_down, ln_g, ln_b):
    tables = _seg_tables(tile_counts)
    xs = moe_dispatch(h, eidx, tables)
    ys = moe_experts(xs, tables[4], layer, w_up, b_up, w_down, b_down)
    return moe_combine(ys, eidx, gates, h, tables, ln_g, ln_b)


def kernel(x, positions, attn_w_in, attn_w_out, ssm_w_in, ssm_conv_w, ssm_conv_b, ssm_dt_bias,
           ssm_a_log, ssm_d, ssm_norm_w, ssm_w_out, moe_w_router, moe_b_router, moe_w_up,
           moe_b_up, moe_w_down, moe_b_down, ln_mix_g, ln_mix_b, ln_ffn_g, ln_ffn_b):
    bsz, seq, _ = x.shape
    xf = x.reshape(bsz * seq, D_MODEL)

    w_qkv = attn_w_in[0].astype(_BF16)
    outs = [dilated_group_attention(qkv_proj_group(xf, w_qkv, positions, g), g, bsz, seq)
            for g in range(N_GROUPS)]
    h, eidx, gates, tile_counts = attn_out_norm_route(
        [o for o, _ in outs], [l for _, l in outs], xf, attn_w_out[0].astype(_BF16),
        ln_mix_g[0], ln_mix_b[0], moe_w_router[0], moe_b_router[0])
    xf = moe_layer(h, eidx, gates, tile_counts, 0, moe_w_up, moe_b_up, moe_w_down, moe_b_down,
                   ln_ffn_g[0], ln_ffn_b[0])

    gz, xs, bc, dt = ssm_in_proj(xf, ssm_w_in[0], ssm_conv_w[0], ssm_conv_b[0], ssm_dt_bias[0], seq)
    yn = ssd_mixer(gz, xs, bc, dt, ssm_a_log[0], ssm_d[0], ssm_norm_w[0], bsz, seq)
    h, eidx, gates, tile_counts = ssm_out_norm_route(
        yn, xf, ssm_w_out[0].astype(_BF16), ln_mix_g[1], ln_mix_b[1], moe_w_router[1], moe_b_router[1])
    xf = moe_layer(h, eidx, gates, tile_counts, 1, moe_w_up, moe_b_up, moe_w_down, moe_b_down,
                   ln_ffn_g[1], ln_ffn_b[1])
    return xf.reshape(bsz, seq, D_MODEL)
```

```python
import functools
import math

import jax
import jax.numpy as jnp
from jax import lax
from jax.experimental import pallas as pl
from jax.experimental.pallas import tpu as pltpu

D_MODEL = 1024
DEPTH = 2
DIL = (1, 4, 16)
N_GROUPS = 3
SUB_WIN = 128
HEAD_DIM = 128
HEADS = 8
ROT_DIM = 32
ROT_HALF = 16
ROPE_THETA = 500000.0
ATTN_PROJ = N_GROUPS * 3 * D_MODEL
SSM_INNER = 2048
SSM_P = 64
SSM_HEADS = 32
SSM_GROUPS = 4
SSM_GHEADS = 8
SSM_N = 128
SSM_CONV = 4
CHUNK = 128
SSM_BC = SSM_GROUPS * SSM_N
SSM_XBC = SSM_INNER + 2 * SSM_BC
N_EXPERTS = 32
TOP_K = 4
D_FF = 1024
SWIGLU_LIMIT = 7.0
SWIGLU_ALPHA = 1.702
DEEPNORM_ALPHA = (2 * DEPTH) ** 0.25
LN_EPS = 1e-5
RMS_EPS = 1e-5

LANES = 128
SUBLANES = 8
VMEM_LIMIT_BYTES = 56 * 1024 * 1024

NEG_BIG = -1e30

_F32 = jnp.float32
_BF16 = jnp.bfloat16


def _cparams(*sem):
    return pltpu.CompilerParams(dimension_semantics=sem, vmem_limit_bytes=VMEM_LIMIT_BYTES)


ROPE_PACK = LANES // ROT_DIM


def _rope_table_kernel(pos_ref, invf_ref, c_ref, sa_ref, sb_ref):
    pos = pos_ref[...].astype(_F32)
    lane = lax.broadcasted_iota(jnp.int32, (pos.shape[0], LANES), 1)
    p = jnp.zeros((pos.shape[0], LANES), _F32)
    group = lax.shift_right_logical(lane, ROT_DIM.bit_length() - 1)
    for q in range(ROPE_PACK):
        p = jnp.where(group == q, pos[:, q:q + 1], p)
    ang = p * invf_ref[...]
    cos = jnp.cos(ang)
    sin = jnp.sin(ang)
    for q in range(ROPE_PACK):
        shift = (LANES - q * ROT_DIM) % LANES
        cq = pltpu.roll(cos, shift, 1) if shift else cos
        sq = pltpu.roll(sin, shift, 1) if shift else sin
        c_ref[q] = jnp.where(lane < ROT_DIM, cq, 1.0)
        sa_ref[q] = jnp.where((lane >= ROT_HALF) & (lane < ROT_DIM), sq, 0.0)
        sb_ref[q] = jnp.where(lane < ROT_HALF, -sq, 0.0)


def rope_tables(positions):
    t = positions.size
    rows = t // ROPE_PACK
    tm = min(1024, rows)
    pos = positions.reshape(ROPE_PACK, rows).T
    j = jnp.arange(LANES)
    invf = (ROPE_THETA ** (-(2.0 * (j % ROT_HALF)).astype(_F32) / ROT_DIM)).astype(_F32).reshape(1, LANES)
    out = jax.ShapeDtypeStruct((ROPE_PACK, rows, LANES), _F32)
    tabs = pl.pallas_call(
        _rope_table_kernel,
        grid=(rows // tm,),
        in_specs=[pl.BlockSpec((tm, ROPE_PACK), lambda i: (i, 0)),
                  pl.BlockSpec((1, LANES), lambda i: (0, 0))],
        out_specs=[pl.BlockSpec((ROPE_PACK, tm, LANES), lambda i: (0, i, 0))] * 3,
        out_shape=[out, out, out],
        compiler_params=_cparams("parallel"),
        name="rope_tables",
    )(pos, invf)
    return [x.reshape(t, LANES) for x in tabs]


QKV_TM = 512
ATTN_ROWS = 2048
HEADS_PER_STEP = 2


def _qkv_proj_kernel(x_ref, w_ref, c_ref, sa_ref, sb_ref, o_ref, *, d, ns, tt):
    tm = x_ref.shape[0]
    xb = x_ref[...].astype(_BF16)
    if d > 1:
        dst = lax.broadcasted_iota(jnp.int32, (tm, tm), 0)
        src = lax.broadcasted_iota(jnp.int32, (tm, tm), 1)
        lt, ld = tt.bit_length() - 1, d.bit_length() - 1
        t_ = dst & (tt - 1)
        r_ = lax.shift_right_logical(dst, lt) & (d - 1)
        s_ = lax.shift_right_logical(dst, lt + ld)
        perm = (src == ((s_ * tt + t_) * d + r_)).astype(_BF16)
        xb = jnp.dot(perm, xb, preferred_element_type=_F32).astype(_BF16)
    c = c_ref[...]
    sa = sa_ref[...]
    sb = sb_ref[...]
    for comp in range(3):
        acc = jnp.dot(xb, w_ref[:, comp * D_MODEL:(comp + 1) * D_MODEL], preferred_element_type=_F32)
        for h in range(HEADS):
            sl = slice(h * HEAD_DIM, (h + 1) * HEAD_DIM)
            t = acc[:, sl]
            if comp < 2:
                t = t * c + pltpu.roll(t, ROT_HALF, 1) * sa + pltpu.roll(t, HEAD_DIM - ROT_HALF, 1) * sb
            o_ref[:, :, :, comp * D_MODEL + h * HEAD_DIM:comp * D_MODEL + (h + 1) * HEAD_DIM] = (
                t.reshape(ns, d, tt, HEAD_DIM).astype(o_ref.dtype))


def qkv_proj_group(x, w_bf16, positions, g):
    t, k = x.shape
    d = DIL[g]
    span = SUB_WIN * d
    tm = QKV_TM
    ns = max(1, tm // span)
    tps = max(1, span // tm)
    tt = tm // (ns * d)
    rope = rope_tables(positions.reshape(-1, ns, tt, d).swapaxes(2, 3).reshape(-1))
    tab_spec = pl.BlockSpec((tm, LANES), lambda i: (i, 0))
    return pl.pallas_call(
        functools.partial(_qkv_proj_kernel, d=d, ns=ns, tt=tt),
        grid=(t // tm,),
        in_specs=[pl.BlockSpec((tm, k), lambda i: (i, 0)),
                  pl.BlockSpec((k, 3 * D_MODEL), lambda i: (0, g)),
                  tab_spec, tab_spec, tab_spec],
        out_specs=pl.BlockSpec((ns, d, tt, 3 * D_MODEL), lambda i: (i // tps, 0, i % tps, 0)),
        out_shape=jax.ShapeDtypeStruct((t // span, d, SUB_WIN, 3 * D_MODEL), _BF16),
        compiler_params=_cparams("parallel"), name=f"qkv_proj_g{g}",
    )(x, w_bf16, *rope)


def _dil_attn_kernel(q_ref, k_ref, v_ref, kp_ref, vp_ref, o_ref, lse_ref, *, d, ns):
    j = pl.program_id(1)
    hp = pl.program_id(2)
    nb = ns * d
    row = lax.broadcasted_iota(jnp.int32, (1, SUB_WIN, SUB_WIN), 1)
    col = lax.broadcasted_iota(jnp.int32, (1, SUB_WIN, SUB_WIN), 2)
    cur_ok = col <= row
    blk = lax.broadcasted_iota(jnp.int32, (nb, 1, 1), 0)
    prev_ok = (col >= row) & ((blk >= d) | (j > 0))
    lane = lax.broadcasted_iota(jnp.int32, (SUB_WIN, LANES), 1)
    scale = 1.0 / math.sqrt(HEAD_DIM)

    def block_rows(b):
        s, r = divmod(b, d)
        start = s * SUB_WIN * d + r
        return pl.ds(start, SUB_WIN, stride=d) if d > 1 else pl.ds(start, SUB_WIN)

    @pl.when(hp == 0)
    def _():
        lse_ref[...] = jnp.zeros_like(lse_ref)

    lses = []
    for hh in range(HEADS_PER_STEP):
        sl = slice(hh * HEAD_DIM, (hh + 1) * HEAD_DIM)
        flat = lambda a: a.reshape(nb, SUB_WIN, HEAD_DIM)
        q, kc, vc = flat(q_ref[:, :, :, sl]), flat(k_ref[:, :, :, sl]), flat(v_ref[:, :, :, sl])
        if ns > 1:
            kp = flat(jnp.concatenate([kp_ref[:, :, :, sl], k_ref[:ns - 1, :, :, sl]], axis=0))
            vp = flat(jnp.concatenate([vp_ref[:, :, :, sl], v_ref[:ns - 1, :, :, sl]], axis=0))
        else:
            kp, vp = flat(kp_ref[:, :, :, sl]), flat(vp_ref[:, :, :, sl])
        sc = jnp.einsum('bqd,bkd->bqk', q, kc, preferred_element_type=_F32) * scale
        sp = jnp.einsum('bqd,bkd->bqk', q, kp, preferred_element_type=_F32) * scale
        sc = jnp.where(cur_ok, sc, NEG_BIG)
        sp = jnp.where(prev_ok, sp, NEG_BIG)
        m = jnp.maximum(jnp.max(sc, axis=2, keepdims=True), jnp.max(sp, axis=2, keepdims=True))
        pc = jnp.exp(sc - m)
        pp = jnp.exp(sp - m)
        l = jnp.sum(pc, axis=2, keepdims=True) + jnp.sum(pp, axis=2, keepdims=True)
        acc = jnp.einsum('bqk,bkd->bqd', pc.astype(_BF16), vc, preferred_element_type=_F32)
        acc += jnp.einsum('bqk,bkd->bqd', pp.astype(_BF16), vp, preferred_element_type=_F32)
        o = acc / l
        for b in range(nb):
            o_ref[hh, block_rows(b), :] = o[b]
        lses.append(m + jnp.log(l))
    for b in range(nb):
        cur = lse_ref[block_rows(b), :]
        for hh in range(HEADS_PER_STEP):
            cur = jnp.where(lane == hp * HEADS_PER_STEP + hh, lses[hh][b], cur)
        lse_ref[block_rows(b), :] = cur


def dilated_group_attention(qkv, g, bsz, seq):
    d = DIL[g]
    ns = ATTN_ROWS // (SUB_WIN * d)
    nsteps = seq // ATTN_ROWS
    hw = HEADS_PER_STEP * HEAD_DIM
    ncol = D_MODEL // hw
    cur = lambda comp: pl.BlockSpec((ns, d, SUB_WIN, hw), lambda b, j, hp: (b * nsteps + j, 0, 0, comp * ncol + hp))
    prev = lambda comp: pl.BlockSpec(
        (1, d, SUB_WIN, hw), lambda b, j, hp: (jnp.maximum((b * nsteps + j) * ns - 1, 0), 0, 0, comp * ncol + hp))
    return pl.pallas_call(
        functools.partial(_dil_attn_kernel, d=d, ns=ns),
        grid=(bsz, nsteps, HEADS // HEADS_PER_STEP),
        in_specs=[cur(0), cur(1), cur(2), prev(1), prev(2)],
        out_specs=[pl.BlockSpec((HEADS_PER_STEP, ATTN_ROWS, HEAD_DIM), lambda b, j, hp: (hp, b * nsteps + j, 0)),
                   pl.BlockSpec((ATTN_ROWS, LANES), lambda b, j, hp: (b * nsteps + j, 0))],
        out_shape=[jax.ShapeDtypeStruct((HEADS, bsz * seq, HEAD_DIM), _F32),
                   jax.ShapeDtypeStruct((bsz * seq, LANES), _F32)],
        compiler_params=_cparams("parallel", "parallel", "arbitrary"),
        name=f"dilated_attn_g{g}",
    )(qkv, qkv, qkv, qkv, qkv)


SSM_IN_TM = 512
HALO = SUBLANES
ZX_COLS = SSM_INNER + SSM_XBC


def _softplus(x):
    return jnp.maximum(x, 0.0) + jnp.log1p(jnp.exp(-jnp.abs(x)))


def _ssm_in_kernel(x_ref, w_ref, wdt_ref, cw_ref, cb_ref, dtb_ref, gz_ref, xs_ref, bc_ref, dt_ref,
                   carry_ref, *, tiles_per_seq):
    tm = x_ref.shape[0]
    cw = D_MODEL
    xb = x_ref[...].astype(_BF16)

    @pl.when(pl.program_id(0) % tiles_per_seq == 0)
    def _():
        carry_ref[...] = jnp.zeros_like(carry_ref)

    for i in range(SSM_INNER // cw):
        z = jnp.dot(xb, w_ref[:, i * cw:(i + 1) * cw], preferred_element_type=_F32)
        gz_ref[:, i * cw:(i + 1) * cw] = z * jax.nn.sigmoid(z)
    for c in range(SSM_XBC // cw):
        cols = slice(c * cw, (c + 1) * cw)
        acc = jnp.dot(xb, w_ref[:, SSM_INNER + c * cw:SSM_INNER + (c + 1) * cw], preferred_element_type=_F32)
        ext = jnp.concatenate([carry_ref[c], acc], axis=0)
        carry_ref[c] = acc[tm - HALO:, :]
        conv = cb_ref[:, cols] + cw_ref[SSM_CONV - 1:SSM_CONV, cols] * acc
        for k in range(1, SSM_CONV):
            conv = conv + cw_ref[SSM_CONV - 1 - k:SSM_CONV - k, cols] * pltpu.roll(ext, k, 0)[HALO:, :]
        act = conv * jax.nn.sigmoid(conv)
        if (c + 1) * cw <= SSM_INNER:
            xs_ref[:, cols] = act
        else:
            bc_ref[...] = act.astype(_BF16)
    dt_raw = jnp.dot(xb, wdt_ref[...], preferred_element_type=_F32)
    dt_ref[...] = _softplus(dt_raw + dtb_ref[...])


def ssm_in_proj(x, w_in, conv_w, conv_b, dt_bias, seq):
    t, k = x.shape
    tm = SSM_IN_TM
    assert 2 * SSM_BC == D_MODEL
    pad = LANES - SSM_HEADS
    w_main = w_in[:, :ZX_COLS].astype(_BF16)
    w_dt = jnp.pad(w_in[:, ZX_COLS:], ((0, 0), (0, pad))).astype(_BF16)
    dtb = jnp.pad(dt_bias, (0, pad)).reshape(1, LANES)
    cst = lambda shape: pl.BlockSpec(shape, lambda i: (0,) * len(shape))
    once = lambda shape: pl.BlockSpec(shape, lambda i: (0,) * len(shape), pipeline_mode=pl.Buffered(1))
    row = lambda w: pl.BlockSpec((tm, w), lambda i: (i, 0))
    return pl.pallas_call(
        functools.partial(_ssm_in_kernel, tiles_per_seq=seq // tm),
        grid=(t // tm,),
        in_specs=[row(k), once((k, ZX_COLS)), once((k, LANES)),
                  cst((SSM_CONV, SSM_XBC)), cst((1, SSM_XBC)), cst((1, LANES))],
        out_specs=[row(SSM_INNER), row(SSM_INNER), row(2 * SSM_BC), row(LANES)],
        out_shape=[jax.ShapeDtypeStruct((t, SSM_INNER), _F32), jax.ShapeDtypeStruct((t, SSM_INNER), _F32),
                   jax.ShapeDtypeStruct((t, 2 * SSM_BC), _BF16), jax.ShapeDtypeStruct((t, LANES), _F32)],
        scratch_shapes=[pltpu.VMEM((SSM_XBC // D_MODEL, HALO, D_MODEL), _F32)],
        compiler_params=_cparams("arbitrary"), name="ssm_in_proj",
    )(x, w_main, w_dt, conv_w, conv_b.reshape(1, -1), dtb)


def _ssd_kernel(gz_ref, xs_ref, bc_ref, dt_ref, a_ref, dsk_ref, nw_ref, o_ref, state_ref, y_ref):
    c = pl.program_id(1)
    q = CHUNK

    @pl.when(c == 0)
    def _():
        state_ref[...] = jnp.zeros_like(state_ref)

    dt = dt_ref[...]
    da = dt * a_ref[...]
    r_io = lax.broadcasted_iota(jnp.int32, (q, q), 0)
    c_io = lax.broadcasted_iota(jnp.int32, (q, q), 1)
    causal = r_io >= c_io
    tri = causal.astype(_F32)
    a_cum = jnp.dot(tri, da, preferred_element_type=_F32, precision=lax.Precision.HIGHEST)
    a_cum_t = a_cum.T
    dt_t = dt.T
    a_last = a_cum[q - 1:q, :]
    w_state = jnp.exp(a_last - a_cum) * dt
    e_acum = jnp.exp(a_cum)
    chunk_decay = jnp.exp(a_last)
    lane = lax.broadcasted_iota(jnp.int32, (q, LANES), 1)
    low = lane < SSM_P
    lane1 = lax.broadcasted_iota(jnp.int32, (1, LANES), 1)
    low1 = lane1 < SSM_P
    tn = (((0,), (0,)), ((), ()))
    nt = (((1,), (1,)), ((), ()))

    for g in range(SSM_GROUPS):
        bg = bc_ref[:, g * SSM_N:(g + 1) * SSM_N]
        cg = bc_ref[:, SSM_BC + g * SSM_N:SSM_BC + (g + 1) * SSM_N]
        cbm = lax.dot_general(cg, bg, nt, preferred_element_type=_F32)
        st = state_ref[g]
        y_off = jnp.dot(cg, st.astype(_BF16), preferred_element_type=_F32)
        xw_parts, dec_parts = [], []
        for j in range(SSM_GHEADS // 2):
            h0 = g * SSM_GHEADS + 2 * j
            col = (g * SSM_GHEADS // 2 + j) * LANES
            xp = xs_ref[:, col:col + LANES]
            y_diag = jnp.zeros((q, LANES), _F32)
            for half, hh in enumerate((h0, h0 + 1)):
                diff = a_cum[:, hh:hh + 1] - a_cum_t[hh:hh + 1, :]
                lmat = jnp.exp(jnp.where(causal, diff, NEG_BIG))
                mm = (cbm * lmat * dt_t[hh:hh + 1, :]).astype(_BF16)
                keep = low if half == 0 else jnp.logical_not(low)
                xh = jnp.where(keep, xp, 0.0).astype(_BF16)
                y_diag = y_diag + jnp.dot(mm, xh, preferred_element_type=_F32)
            e_pair = jnp.where(low, e_acum[:, h0:h0 + 1], e_acum[:, h0 + 1:h0 + 2])
            w_pair = jnp.where(low, w_state[:, h0:h0 + 1], w_state[:, h0 + 1:h0 + 2])
            y_ref[:, col:col + LANES] = (y_diag + y_off[:, j * LANES:(j + 1) * LANES] * e_pair
                                         + xp * dsk_ref[:, col:col + LANES])
            xw_parts.append((xp * w_pair).astype(_BF16))
            dec_parts.append(jnp.where(low1, chunk_decay[:, h0:h0 + 1], chunk_decay[:, h0 + 1:h0 + 2]))
        xw = jnp.concatenate(xw_parts, axis=1)
        dec = jnp.concatenate(dec_parts, axis=1)
        state_ref[g] = st * dec + lax.dot_general(bg, xw, tn, preferred_element_type=_F32)

    yg = y_ref[...] * gz_ref[...]
    gw = SSM_INNER // SSM_GROUPS
    for g in range(SSM_GROUPS):
        v = yg[:, g * gw:(g + 1) * gw]
        ms = jnp.mean(v * v, axis=-1, keepdims=True)
        o_ref[:, g * gw:(g + 1) * gw] = (v * lax.rsqrt(ms + RMS_EPS)
                                         * nw_ref[:, g * gw:(g + 1) * gw]).astype(o_ref.dtype)


def ssd_mixer(gz, xs, bc, dt, a_log, d_skip, norm_w, bsz, seq):
    nc = seq // CHUNK
    a = jnp.pad(-jnp.exp(a_log.astype(_F32)), (0, LANES - SSM_HEADS)).reshape(1, LANES)
    dsk = jnp.repeat(d_skip.astype(_F32), SSM_P).reshape(1, SSM_INNER)
    cst = lambda shape: pl.BlockSpec(shape, lambda b, c: (0,) * len(shape))
    row = lambda w: pl.BlockSpec((CHUNK, w), lambda b, c: (b * nc + c, 0))
    return pl.pallas_call(
        _ssd_kernel,
        grid=(bsz, nc),
        in_specs=[row(SSM_INNER), row(SSM_INNER), row(2 * SSM_BC), row(LANES),
                  cst((1, LANES)), cst((1, SSM_INNER)), cst((1, SSM_INNER))],
        out_specs=row(SSM_INNER),
        out_shape=jax.ShapeDtypeStruct((bsz * seq, SSM_INNER), _BF16),
        scratch_shapes=[pltpu.VMEM((SSM_GROUPS, SSM_N, SSM_GHEADS * SSM_P), _F32),
                        pltpu.VMEM((CHUNK, SSM_INNER), _F32)],
        compiler_params=_cparams("parallel", "arbitrary"), name="ssd_mixer",
    )(gz, xs, bc, dt, a, dsk, norm_w.reshape(1, -1))


ROUTE_SUB = 256
ROUTE_TM_ATTN = 512
ROUTE_TM_SSM = 1024


def _layer_norm(y, g, b):
    mu = jnp.mean(y, axis=-1, keepdims=True)
    yc = y - mu
    var = jnp.mean(yc * yc, axis=-1, keepdims=True)
    return yc * lax.rsqrt(var + LN_EPS) * g + b


def _route(h1s, wrt_ref, br_ref, eidx_ref, gate_ref, tcnt_ref):
    sub = ROUTE_SUB
    n = len(h1s)
    nt = (((1,), (1,)), ((), ()))
    wrt = wrt_ref[...]
    vals = [lax.dot_general(wrt, h, nt, preferred_element_type=_F32, precision=lax.Precision.HIGHEST)
            + br_ref[...] for h in h1s]
    eio = lax.broadcasted_iota(jnp.int32, (N_EXPERTS, sub), 0)
    tops, ids, hots = ([[] for _ in range(n)] for _ in range(3))
    for _ in range(TOP_K):
        for a in range(n):
            mk = jnp.max(vals[a], axis=0, keepdims=True)
            ik = jnp.min(jnp.where(vals[a] == mk, eio, N_EXPERTS), axis=0, keepdims=True)
            hot = eio == ik
            vals[a] = jnp.where(hot, -jnp.inf, vals[a])
            tops[a].append(mk)
            ids[a].append(ik)
            hots[a].append(hot)
    col = lax.broadcasted_iota(jnp.int32, tcnt_ref.shape, 1)
    tcnt = tcnt_ref[...]
    for a in range(n):
        sl = slice(a * sub, (a + 1) * sub)
        exps = [jnp.exp(t - tops[a][0]) for t in tops[a]]
        den = exps[0] + exps[1] + exps[2] + exps[3]
        for k in range(TOP_K):
            eidx_ref[k:k + 1, sl] = ids[a][k]
            gate_ref[k:k + 1, sl] = exps[k] / den
        hot_all = hots[a][0] | hots[a][1] | hots[a][2] | hots[a][3]
        counts = jnp.sum(hot_all.astype(_F32), axis=1, keepdims=True)
        tcnt = jnp.where(col == pl.program_id(0) * n + a, counts, tcnt)
    tcnt_ref[...] = tcnt


def _norm_and_route(mixes, x_ref, g_ref, b_ref, wrt_ref, br_ref, h_ref, eidx_ref, gate_ref, tcnt_ref):
    sub = ROUTE_SUB

    @pl.when(pl.program_id(0) == 0)
    def _():
        tcnt_ref[...] = jnp.zeros_like(tcnt_ref)

    h1s = [_layer_norm(DEEPNORM_ALPHA * x_ref[a * sub:(a + 1) * sub, :] + mix, g_ref[...], b_ref[...])
           for a, mix in enumerate(mixes)]
    for a, h1 in enumerate(h1s):
        h_ref[a * sub:(a + 1) * sub, :] = h1
    _route(h1s, wrt_ref, br_ref, eidx_ref, gate_ref, tcnt_ref)


def _attn_out_kernel(o0_ref, o1_ref, o2_ref, l0_ref, l1_ref, l2_ref, x_ref, wo_ref, g_ref, b_ref,
                     wrt_ref, br_ref, h_ref, eidx_ref, gate_ref, tcnt_ref, ob_ref):
    sub = ROUTE_SUB
    l0, l1, l2 = l0_ref[...], l1_ref[...], l2_ref[...]
    m = jnp.maximum(jnp.maximum(l0, l1), l2)
    e0, e1, e2 = jnp.exp(l0 - m), jnp.exp(l1 - m), jnp.exp(l2 - m)
    inv = 1.0 / (e0 + e1 + e2)
    w0, w1, w2 = e0 * inv, e1 * inv, e2 * inv
    for h in range(HEADS):
        o = w0[:, h:h + 1] * o0_ref[h] + w1[:, h:h + 1] * o1_ref[h] + w2[:, h:h + 1] * o2_ref[h]
        ob_ref[:, h * HEAD_DIM:(h + 1) * HEAD_DIM] = o.astype(_BF16)
    mixes = [jnp.dot(ob_ref[a * sub:(a + 1) * sub, :], wo_ref[...], preferred_element_type=_F32)
             for a in range(x_ref.shape[0] // sub)]
    _norm_and_route(mixes, x_ref, g_ref, b_ref, wrt_ref, br_ref, h_ref, eidx_ref, gate_ref, tcnt_ref)


def _ssm_out_kernel(y_ref, x_ref, wo_ref, g_ref, b_ref, wrt_ref, br_ref, h_ref, eidx_ref, gate_ref, tcnt_ref):
    sub = ROUTE_SUB
    mixes = [jnp.dot(y_ref[a * sub:(a + 1) * sub, :], wo_ref[...], preferred_element_type=_F32)
             for a in range(x_ref.shape[0] // sub)]
    _norm_and_route(mixes, x_ref, g_ref, b_ref, wrt_ref, br_ref, h_ref, eidx_ref, gate_ref, tcnt_ref)


def _route_out_specs(t, tm):
    nsub = t // ROUTE_SUB
    specs = [pl.BlockSpec((tm, D_MODEL), lambda i: (i, 0)),
             pl.BlockSpec((TOP_K, tm), lambda i: (0, i)),
             pl.BlockSpec((TOP_K, tm), lambda i: (0, i)),
             pl.BlockSpec((N_EXPERTS, nsub), lambda i: (0, 0))]
    shapes = [jax.ShapeDtypeStruct((t, D_MODEL), _F32),
              jax.ShapeDtypeStruct((TOP_K, t), jnp.int32),
              jax.ShapeDtypeStruct((TOP_K, t), _F32),
              jax.ShapeDtypeStruct((N_EXPERTS, nsub), _F32)]
    return specs, shapes


def _const_spec(shape):
    return pl.BlockSpec(shape, lambda i: (0,) * len(shape))


def attn_out_norm_route(os_, lses, x, wo_bf16, ln_g, ln_b, w_router, b_router):
    t = x.shape[0]
    tm = ROUTE_TM_ATTN
    row = lambda w: pl.BlockSpec((tm, w), lambda i: (i, 0))
    out_specs, out_shapes = _route_out_specs(t, tm)
    return pl.pallas_call(
        _attn_out_kernel,
        grid=(t // tm,),
        in_specs=[pl.BlockSpec((HEADS, tm, HEAD_DIM), lambda i: (0, i, 0))] * 3 + [row(LANES)] * 3 + [row(D_MODEL),
                  _const_spec((D_MODEL, D_MODEL)), _const_spec((1, D_MODEL)), _const_spec((1, D_MODEL)),
                  _const_spec((N_EXPERTS, D_MODEL)), _const_spec((N_EXPERTS, 1))],
        out_specs=out_specs, out_shape=out_shapes,
        scratch_shapes=[pltpu.VMEM((tm, D_MODEL), _BF16)],
        compiler_params=_cparams("arbitrary"), name="attn_out_norm_route",
    )(*os_, *lses, x, wo_bf16, ln_g.reshape(1, -1), ln_b.reshape(1, -1),
      w_router.T, b_router.reshape(-1, 1))


def ssm_out_norm_route(yn, x, wo_bf16, ln_g, ln_b, w_router, b_router):
    t = x.shape[0]
    tm = ROUTE_TM_SSM
    row = lambda w: pl.BlockSpec((tm, w), lambda i: (i, 0))
    out_specs, out_shapes = _route_out_specs(t, tm)
    return pl.pallas_call(
        _ssm_out_kernel,
        grid=(t // tm,),
        in_specs=[row(SSM_INNER), row(D_MODEL),
                  _const_spec((SSM_INNER, D_MODEL)), _const_spec((1, D_MODEL)), _const_spec((1, D_MODEL)),
                  _const_spec((N_EXPERTS, D_MODEL)), _const_spec((N_EXPERTS, 1))],
        out_specs=out_specs, out_shape=out_shapes,
        compiler_params=_cparams("arbitrary"), name="ssm_out_norm_route",
    )(yn, x, wo_bf16, ln_g.reshape(1, -1), ln_b.reshape(1, -1), w_router.T, b_router.reshape(-1, 1))


MOE_BM = 1024
MOE_SUB_BM = 256
MOE_SUB = ROUTE_SUB
SEG_ALIGN = SUBLANES
LOCAL_ROWS = MOE_SUB * TOP_K + N_EXPERTS * SUBLANES
SEG_UNIT = 16
SEG_REST = (8,)
TAIL_SIZES = tuple(SEG_ALIGN << k for k in reversed(range((MOE_BM // SEG_ALIGN).bit_length() - 1)))
PAIR_COLS = D_MODEL // 2
_U32 = jnp.uint32


def _pack_pairs(x):
    lo = lax.shift_right_logical(lax.bitcast_convert_type(x[:, :PAIR_COLS], _U32), jnp.uint32(16))
    hi = lax.bitcast_convert_type(x[:, PAIR_COLS:], _U32) & jnp.uint32(0xFFFF0000)
    return hi | lo


def _unpack_pairs(u):
    lo = lax.bitcast_convert_type(lax.shift_left(u, jnp.uint32(16)), _F32).astype(_BF16)
    hi = lax.bitcast_convert_type(u & jnp.uint32(0xFFFF0000), _F32).astype(_BF16)
    return lo, hi


def _local_positions(eidx):
    sub = eidx.shape[1]
    eio = lax.broadcasted_iota(jnp.int32, (N_EXPERTS, sub), 0)
    hots = [eio == eidx[k:k + 1, :] for k in range(TOP_K)]
    hot_all = (hots[0] | hots[1] | hots[2] | hots[3]).astype(_BF16)
    r_io = lax.broadcasted_iota(jnp.int32, (sub, sub), 0)
    c_io = lax.broadcasted_iota(jnp.int32, (sub, sub), 1)
    upper = (r_io < c_io).astype(_BF16)
    prefix = jnp.dot(hot_all, upper, preferred_element_type=_F32)
    counts = jnp.sum(hot_all.astype(_F32), axis=1, keepdims=True).astype(jnp.int32)
    padded = ((counts + (SEG_ALIGN - 1)) & (-SEG_ALIGN)).astype(_F32)
    e_r = lax.broadcasted_iota(jnp.int32, (N_EXPERTS, N_EXPERTS), 0)
    e_c = lax.broadcasted_iota(jnp.int32, (N_EXPERTS, N_EXPERTS), 1)
    below = (e_c < e_r).astype(_BF16)
    seg_start = jnp.dot(below, jnp.broadcast_to(padded, (N_EXPERTS, sub)).astype(_BF16),
                        preferred_element_type=_F32)
    base = seg_start + prefix
    return [jnp.sum(jnp.where(h, base, 0.0), axis=0, keepdims=True) for h in hots]


def _segment_copies(cnt_ref, lo_ref, go_ref, i, copy_fn):
    for e in range(N_EXPERTS):
        n = cnt_ref[i * N_EXPERTS + e]
        lo = lo_ref[i * N_EXPERTS + e]
        go = go_ref[i * N_EXPERTS + e]

        def unit(u, c, lo=lo, go=go):
            copy_fn(pl.multiple_of(lo + u * SEG_UNIT, SEG_ALIGN), pl.multiple_of(go + u * SEG_UNIT, SEG_ALIGN),
                    SEG_UNIT)
            return c

        lax.fori_loop(0, n // SEG_UNIT, unit, 0)
        for sz in SEG_REST:
            done = n & -(2 * sz)

            @pl.when((n & sz) != 0)
            def _(done=done, sz=sz, lo=lo, go=go):
                copy_fn(pl.multiple_of(lo + done, SEG_ALIGN), pl.multiple_of(go + done, SEG_ALIGN), sz)


def _dispatch_kernel(cnt_ref, lo_ref, go_ref, units_ref, tail_ref, eidx_ref, h_ref, xs_ref, lpos_ref,
                     buf_ref, sem):
    i = pl.program_id(0)
    slot = i % 2
    mine, other = buf_ref.at[slot], buf_ref.at[1 - slot]
    lpos = _local_positions(eidx_ref[...])
    for k in range(TOP_K):
        lpos_ref[k:k + 1, :] = lpos[k]
    p_io = lax.broadcasted_iota(jnp.int32, (LOCAL_ROWS, MOE_SUB), 0).astype(_F32)
    sel = (p_io == lpos[0]) | (p_io == lpos[1]) | (p_io == lpos[2]) | (p_io == lpos[3])
    mine[...] = _pack_pairs(jnp.dot(sel.astype(_BF16), h_ref[...].astype(_BF16), preferred_element_type=_F32))

    def copy(lo, go, sz):
        pltpu.make_async_copy(mine.at[pl.ds(lo, sz)], xs_ref.at[pl.ds(go, sz)], sem.at[slot]).start()

    _segment_copies(cnt_ref, lo_ref, go_ref, i, copy)

    def wait_units(buf, s, n_units):
        def body(j, c):
            pltpu.make_async_copy(buf.at[pl.ds(0, SEG_ALIGN)], xs_ref.at[pl.ds(0, SEG_ALIGN)], sem.at[s]).wait()
            return c
        lax.fori_loop(0, n_units, body, 0)

    @pl.when(i > 0)
    def _():
        wait_units(other, 1 - slot, units_ref[jnp.maximum(i - 1, 0)])

    @pl.when(i == pl.num_programs(0) - 1)
    def _():
        wait_units(mine, slot, units_ref[i])
        mine[0:MOE_BM, :] = jnp.zeros((MOE_BM, PAIR_COLS), _U32)
        start, n, n_blocks = tail_ref[0], tail_ref[1], tail_ref[2]
        done = jnp.int32(0)
        for sz in TAIL_SIZES:
            @pl.when((n & sz) != 0)
            def _(done=done, sz=sz):
                pltpu.make_async_copy(mine.at[pl.ds(0, sz)],
                                      xs_ref.at[pl.ds(pl.multiple_of(start + done, SEG_ALIGN), sz)],
                                      sem.at[slot]).start()
            done = done + (n & sz)

        def zero_block(j, c):
            row0 = pl.multiple_of(start + n + j * MOE_BM, MOE_BM)
            pltpu.make_async_copy(mine.at[pl.ds(0, MOE_BM)], xs_ref.at[pl.ds(row0, MOE_BM)], sem.at[slot]).start()
            return c

        lax.fori_loop(0, n_blocks, zero_block, 0)
        wait_units(mine, slot, n // SEG_ALIGN + n_blocks * (MOE_BM // SEG_ALIGN))


def _seg_tables(tile_counts):
    cnt = tile_counts.T.astype(jnp.int32)
    pc = (cnt + (SEG_ALIGN - 1)) & (-SEG_ALIGN)
    lo = jnp.cumsum(pc, axis=1) - pc
    tot = jnp.sum(pc, axis=0)
    offs = jnp.concatenate([jnp.zeros((1,), jnp.int32), jnp.cumsum(tot)])
    go = offs[None, :-1] + jnp.cumsum(pc, axis=0) - pc
    units = jnp.sum(pc, axis=1) // SEG_ALIGN
    return pc.reshape(-1), lo.reshape(-1), go.reshape(-1), units.astype(jnp.int32), offs.astype(jnp.int32)


def _max_rows(t):
    n = t * TOP_K + (t // MOE_SUB) * N_EXPERTS * (SEG_ALIGN - 1)
    return -(-n // MOE_BM) * MOE_BM


def moe_dispatch(h, eidx, tables):
    t = h.shape[0]
    pc, lo, go, units, offs = tables
    used = offs[-1]
    partial = (-used) % MOE_BM
    tail = jnp.stack([used, partial, (_max_rows(t) - used - partial) // MOE_BM]).astype(jnp.int32)
    grid_spec = pltpu.PrefetchScalarGridSpec(
        num_scalar_prefetch=5, grid=(t // MOE_SUB,),
        in_specs=[pl.BlockSpec((TOP_K, MOE_SUB), lambda i, *_: (0, i)),
                  pl.BlockSpec((MOE_SUB, D_MODEL), lambda i, *_: (i, 0))],
        out_specs=[pl.BlockSpec(memory_space=pl.ANY), pl.BlockSpec((TOP_K, MOE_SUB), lambda i, *_: (0, i))],
        scratch_shapes=[pltpu.VMEM((2, LOCAL_ROWS, PAIR_COLS), _U32), pltpu.SemaphoreType.DMA((2,))])
    return pl.pallas_call(
        _dispatch_kernel, grid_spec=grid_spec,
        out_shape=[jax.ShapeDtypeStruct((_max_rows(t), PAIR_COLS), _U32),
                   jax.ShapeDtypeStruct((TOP_K, t), _F32)],
        compiler_params=_cparams("arbitrary"), name="moe_dispatch",
    )(pc, lo, go, units, tail, eidx, h)


def _combine_kernel(cnt_ref, lo_ref, go_ref, units_ref, lpos_ref, gate_ref, ys_ref, h_ref, g_ref, b_ref,
                    o_ref, buf_ref, sem):
    i = pl.program_id(0)
    slot = i % 2

    def fetch(step, s):
        dst = buf_ref.at[s]
        dst[MOE_SUB * TOP_K:, :] = jnp.zeros((LOCAL_ROWS - MOE_SUB * TOP_K, PAIR_COLS), _U32)

        def copy(lo, go, sz):
            pltpu.make_async_copy(ys_ref.at[pl.ds(go, sz)], dst.at[pl.ds(lo, sz)], sem.at[s]).start()

        _segment_copies(cnt_ref, lo_ref, go_ref, step, copy)

    @pl.when(i == 0)
    def _():
        fetch(i, slot)

    @pl.when(i + 1 < pl.num_programs(0))
    def _():
        fetch(i + 1, 1 - slot)

    mine = buf_ref.at[slot]
    lpos = lpos_ref[...]
    gates = gate_ref[...]
    p_io = lax.broadcasted_iota(jnp.int32, (LOCAL_ROWS, MOE_SUB), 0).astype(_F32)
    w = jnp.zeros((LOCAL_ROWS, MOE_SUB), _F32)
    for k in range(TOP_K):
        w = w + jnp.where(p_io == lpos[k:k + 1, :], gates[k:k + 1, :], 0.0)
    wb = w.astype(_BF16)

    def wait_unit(j, c):
        pltpu.make_async_copy(ys_ref.at[pl.ds(0, SEG_ALIGN)], mine.at[pl.ds(0, SEG_ALIGN)], sem.at[slot]).wait()
        return c

    lax.fori_loop(0, units_ref[i], wait_unit, 0)
    tn = (((0,), (0,)), ((), ()))
    ffn = jnp.concatenate([lax.dot_general(wb, yb, tn, preferred_element_type=_F32)
                           for yb in _unpack_pairs(mine[...])], axis=1)
    o_ref[...] = _layer_norm(DEEPNORM_ALPHA * h_ref[...] + ffn, g_ref[...], b_ref[...])


def moe_combine(ys, lpos, gates, h, tables, ln_g, ln_b):
    t = h.shape[0]
    pc, lo, go, units, _ = tables
    row = pl.BlockSpec((MOE_SUB, D_MODEL), lambda i, *_: (i, 0))
    kt = pl.BlockSpec((TOP_K, MOE_SUB), lambda i, *_: (0, i))
    cst = lambda shape: pl.BlockSpec(shape, lambda i, *_: (0,) * len(shape))
    grid_spec = pltpu.PrefetchScalarGridSpec(
        num_scalar_prefetch=4, grid=(t // MOE_SUB,),
        in_specs=[kt, kt, pl.BlockSpec(memory_space=pl.ANY), row, cst((1, D_MODEL)), cst((1, D_MODEL))],
        out_specs=row,
        scratch_shapes=[pltpu.VMEM((2, LOCAL_ROWS, PAIR_COLS), _U32), pltpu.SemaphoreType.DMA((2,))])
    return pl.pallas_call(
        _combine_kernel, grid_spec=grid_spec,
        out_shape=jax.ShapeDtypeStruct((t, D_MODEL), _F32),
        compiler_params=_cparams("arbitrary"), name="moe_combine",
    )(pc, lo, go, units, lpos, gates, ys, h, ln_g.reshape(1, -1), ln_b.reshape(1, -1))


def _expert_kernel(blk_ref, e_ref, lo_ref, hi_ref, xs_ref, wu_ref, bu_ref, wd_ref, bd_ref,
                   ys_ref, wub_ref, wdb_ref):
    i = pl.program_id(0)
    e = e_ref[i]
    e_prev = e_ref[jnp.maximum(i - 1, 0)]

    @pl.when((i == 0) | (e != e_prev))
    def _():
        wub_ref[...] = wu_ref[...].astype(_BF16)
        wdb_ref[...] = wd_ref[...].astype(_BF16)

    lo = lo_ref[i]
    hi = hi_ref[i]

    @pl.when((lo == 0) | (hi < 0))
    def _():
        ys_ref[...] = jnp.zeros_like(ys_ref)

    for sb in range(xs_ref.shape[0] // MOE_SUB_BM):
        r0 = sb * MOE_SUB_BM
        rows = slice(r0, r0 + MOE_SUB_BM)

        @pl.when((hi > lo) & (hi > r0) & (lo < r0 + MOE_SUB_BM))
        def _(r0=r0, rows=rows):
            x_lo, x_hi = _unpack_pairs(xs_ref[rows, :])
            h = (jnp.dot(x_lo, wub_ref[:PAIR_COLS, :], preferred_element_type=_F32)
                 + jnp.dot(x_hi, wub_ref[PAIR_COLS:, :], preferred_element_type=_F32) + bu_ref[...])
            gate = jnp.minimum(h[:, :D_FF], SWIGLU_LIMIT)
            up = jnp.clip(h[:, D_FF:], -SWIGLU_LIMIT, SWIGLU_LIMIT)
            glu = gate * jax.nn.sigmoid(SWIGLU_ALPHA * gate)
            act = ((up + 1.0) * glu).astype(_BF16)
            y = jnp.dot(act, wdb_ref[...], preferred_element_type=_F32) + bd_ref[...]
            packed = _pack_pairs(y.astype(_BF16).astype(_F32))
            row = r0 + lax.broadcasted_iota(jnp.int32, packed.shape, 0)
            ys_ref[rows, :] = jnp.where((row >= lo) & (row < hi), packed, ys_ref[rows, :])


def _expert_schedule(offs, n_rows, bm):
    nblk = n_rows // bm
    n_items = nblk + N_EXPERTS - 1
    used = offs[-1]
    bnd = offs[1:-1]
    pos = jnp.arange(N_EXPERTS - 1, dtype=jnp.int32) + jnp.minimum(bnd // bm + 1, nblk)
    i = jnp.arange(n_items, dtype=jnp.int32)[:, None]
    at = pos[None, :] == i
    is_bnd = jnp.any(at, axis=1)
    bnd_val = jnp.sum(jnp.where(at, bnd[None, :], 0), axis=1)
    n_before = jnp.sum((pos[None, :] < i).astype(jnp.int32), axis=1)
    cuts = jnp.where(is_bnd, bnd_val, (i[:, 0] - n_before) * bm)
    ends = jnp.concatenate([cuts[1:], jnp.array([n_rows], jnp.int32)])
    last_blk = jnp.maximum(used - 1, 0) // bm
    past = jnp.logical_not(is_bnd) & (cuts // bm > last_blk)
    raw_blk = cuts // bm
    cuts = jnp.minimum(cuts, used)
    ends = jnp.minimum(ends, used)
    blk = jnp.minimum(cuts // bm, last_blk)
    e = jnp.minimum(jnp.sum((bnd[None, :] <= cuts[:, None]).astype(jnp.int32), axis=1), N_EXPERTS - 1)
    lo = jnp.where(past, 0, cuts - blk * bm)
    hi = jnp.where(past, -1, jnp.minimum(ends - blk * bm, bm))
    return lax.cummax(jnp.where(past, raw_blk, blk)), e, lo, hi


def moe_experts(xs, offs, layer, w_up, b_up, w_down, b_down):
    n_rows = xs.shape[0]
    bm = MOE_BM
    blk, e, lo, hi = _expert_schedule(offs, n_rows, bm)
    n_items = n_rows // bm + N_EXPERTS - 1
    wmap = lambda i, b, e, lo, hi: (layer, e[i], 0, 0)
    grid_spec = pltpu.PrefetchScalarGridSpec(
        num_scalar_prefetch=4, grid=(n_items,),
        in_specs=[pl.BlockSpec((bm, PAIR_COLS), lambda i, b, e, lo, hi: (b[i], 0)),
                  pl.BlockSpec((None, None, D_MODEL, 2 * D_FF), wmap),
                  pl.BlockSpec((None, None, 1, 2 * D_FF), wmap),
                  pl.BlockSpec((None, None, D_FF, D_MODEL), wmap),
                  pl.BlockSpec((None, None, 1, D_MODEL), wmap)],
        out_specs=pl.BlockSpec((bm, PAIR_COLS), lambda i, b, e, lo, hi: (b[i], 0)),
        scratch_shapes=[pltpu.VMEM((D_MODEL, 2 * D_FF), _BF16), pltpu.VMEM((D_FF, D_MODEL), _BF16)])
    nl = w_up.shape[0]
    return pl.pallas_call(
        _expert_kernel, grid_spec=grid_spec,
        out_shape=jax.ShapeDtypeStruct((n_rows, PAIR_COLS), _U32),
        compiler_params=_cparams("arbitrary"), name="moe_experts",
    )(blk, e, lo, hi, xs, w_up, b_up.reshape(nl, N_EXPERTS, 1, -1), w_down, b_down.reshape(nl, N_EXPERTS, 1, -1))


def moe_layer(h, eidx, gates, tile_counts, layer, w_up, b_up, w_down, b_down, ln_g, ln_b):
    tables = _seg_tables(tile_counts)
    xs, lpos = moe_dispatch(h, eidx, tables)
    ys = moe_experts(xs, tables[4], layer, w_up, b_up, w_down, b_down)
    return moe_combine(ys, lpos, gates, h, tables, ln_g, ln_b)


def kernel(x, positions, attn_w_in, attn_w_out, ssm_w_in, ssm_conv_w, ssm_conv_b, ssm_dt_bias,
           ssm_a_log, ssm_d, ssm_norm_w, ssm_w_out, moe_w_router, moe_b_router, moe_w_up,
           moe_b_up, moe_w_down, moe_b_down, ln_mix_g, ln_mix_b, ln_ffn_g, ln_ffn_b):
    bsz, seq, _ = x.shape
    xf = x.reshape(bsz * seq, D_MODEL)

    w_qkv = attn_w_in[0].astype(_BF16)
    outs = [dilated_group_attention(qkv_proj_group(xf, w_qkv, positions, g), g, bsz, seq)
            for g in range(N_GROUPS)]
    h, eidx, gates, tile_counts = attn_out_norm_route(
        [o for o, _ in outs], [l for _, l in outs], xf, attn_w_out[0].astype(_BF16),
        ln_mix_g[0], ln_mix_b[0], moe_w_router[0], moe_b_router[0])
    xf = moe_layer(h, eidx, gates, tile_counts, 0, moe_w_up, moe_b_up, moe_w_down, moe_b_down,
                   ln_ffn_g[0], ln_ffn_b[0])

    gz, xs, bc, dt = ssm_in_proj(xf, ssm_w_in[0], ssm_conv_w[0], ssm_conv_b[0], ssm_dt_bias[0], seq)
    yn = ssd_mixer(gz, xs, bc, dt, ssm_a_log[0], ssm_d[0], ssm_norm_w[0], bsz, seq)
    h, eidx, gates, tile_counts = ssm_out_norm_route(
        yn, xf, ssm_w_out[0].astype(_BF16), ln_mix_g[1], ln_mix_b[1], moe_w_router[1], moe_b_router[1])
    xf = moe_layer(h, eidx, gates, tile_counts, 1, moe_w_up, moe_b_up, moe_w_down, moe_b_down,
                   ln_ffn_g[1], ln_ffn_b[1])
    return xf.reshape(bsz, seq, D_MODEL)
```

```python
import functools
import math

import jax
import jax.numpy as jnp
from jax import lax
from jax.experimental import pallas as pl
from jax.experimental.pallas import tpu as pltpu

D_MODEL = 1024
DEPTH = 2
DIL = (1, 4, 16)
N_GROUPS = 3
SUB_WIN = 128
HEAD_DIM = 128
HEADS = 8
ROT_DIM = 32
ROT_HALF = 16
ROPE_THETA = 500000.0
ATTN_PROJ = N_GROUPS * 3 * D_MODEL
SSM_INNER = 2048
SSM_P = 64
SSM_HEADS = 32
SSM_GROUPS = 4
SSM_GHEADS = 8
SSM_N = 128
SSM_CONV = 4
CHUNK = 128
SSM_BC = SSM_GROUPS * SSM_N
SSM_XBC = SSM_INNER + 2 * SSM_BC
N_EXPERTS = 32
TOP_K = 4
D_FF = 1024
SWIGLU_LIMIT = 7.0
SWIGLU_ALPHA = 1.702
DEEPNORM_ALPHA = (2 * DEPTH) ** 0.25
LN_EPS = 1e-5
RMS_EPS = 1e-5

LANES = 128
SUBLANES = 8
VMEM_LIMIT_BYTES = 56 * 1024 * 1024

NEG_BIG = -1e30

_F32 = jnp.float32
_BF16 = jnp.bfloat16


def _cparams(*sem):
    return pltpu.CompilerParams(dimension_semantics=sem, vmem_limit_bytes=VMEM_LIMIT_BYTES)


def _per_head_lanes(m, lanes_per_head, width):
    head_of_lane = lax.broadcasted_iota(jnp.int32, (LANES, width), 1) // lanes_per_head
    spread = (head_of_lane == lax.broadcasted_iota(jnp.int32, (LANES, width), 0)).astype(_BF16)
    hi = m.astype(_BF16)
    lo = (m - hi.astype(_F32)).astype(_BF16)
    return jnp.dot(hi, spread, preferred_element_type=_F32) + jnp.dot(lo, spread, preferred_element_type=_F32)


ROPE_PACK = LANES // ROT_DIM


def _rope_table_kernel(pos_ref, invf_ref, c_ref, sa_ref, sb_ref):
    pos = pos_ref[...].astype(_F32)
    lane = lax.broadcasted_iota(jnp.int32, (pos.shape[0], LANES), 1)
    p = jnp.zeros((pos.shape[0], LANES), _F32)
    group = lax.shift_right_logical(lane, ROT_DIM.bit_length() - 1)
    for q in range(ROPE_PACK):
        p = jnp.where(group == q, pos[:, q:q + 1], p)
    ang = p * invf_ref[...]
    cos = jnp.cos(ang)
    sin = jnp.sin(ang)
    for q in range(ROPE_PACK):
        shift = (LANES - q * ROT_DIM) % LANES
        cq = pltpu.roll(cos, shift, 1) if shift else cos
        sq = pltpu.roll(sin, shift, 1) if shift else sin
        c_ref[q] = jnp.where(lane < ROT_DIM, cq, 1.0)
        sa_ref[q] = jnp.where((lane >= ROT_HALF) & (lane < ROT_DIM), sq, 0.0)
        sb_ref[q] = jnp.where(lane < ROT_HALF, -sq, 0.0)


def rope_tables(positions):
    t = positions.size
    rows = t // ROPE_PACK
    tm = min(1024, rows)
    pos = positions.reshape(ROPE_PACK, rows).T
    j = jnp.arange(LANES)
    invf = (ROPE_THETA ** (-(2.0 * (j % ROT_HALF)).astype(_F32) / ROT_DIM)).astype(_F32).reshape(1, LANES)
    out = jax.ShapeDtypeStruct((ROPE_PACK, rows, LANES), _F32)
    tabs = pl.pallas_call(
        _rope_table_kernel,
        grid=(rows // tm,),
        in_specs=[pl.BlockSpec((tm, ROPE_PACK), lambda i: (i, 0)),
                  pl.BlockSpec((1, LANES), lambda i: (0, 0))],
        out_specs=[pl.BlockSpec((ROPE_PACK, tm, LANES), lambda i: (0, i, 0))] * 3,
        out_shape=[out, out, out],
        compiler_params=_cparams("parallel"),
        name="rope_tables",
    )(pos, invf)
    return [x.reshape(t, LANES) for x in tabs]


QKV_TM = 512
ATTN_ROWS = 2048
HEADS_PER_STEP = 2


def _qkv_proj_kernel(x_ref, w_ref, c_ref, sa_ref, sb_ref, o_ref, *, d, ns, tt):
    tm = x_ref.shape[0]
    xb = x_ref[...].astype(_BF16)
    if d > 1:
        dst = lax.broadcasted_iota(jnp.int32, (tm, tm), 0)
        src = lax.broadcasted_iota(jnp.int32, (tm, tm), 1)
        lt, ld = tt.bit_length() - 1, d.bit_length() - 1
        t_ = dst & (tt - 1)
        r_ = lax.shift_right_logical(dst, lt) & (d - 1)
        s_ = lax.shift_right_logical(dst, lt + ld)
        perm = (src == ((s_ * tt + t_) * d + r_)).astype(_BF16)
        xb = jnp.dot(perm, xb, preferred_element_type=_F32).astype(_BF16)
    c = c_ref[...]
    sa = sa_ref[...]
    sb = sb_ref[...]
    for comp in range(3):
        acc = jnp.dot(xb, w_ref[:, comp * D_MODEL:(comp + 1) * D_MODEL], preferred_element_type=_F32)
        for h in range(HEADS):
            sl = slice(h * HEAD_DIM, (h + 1) * HEAD_DIM)
            t = acc[:, sl]
            if comp < 2:
                t = t * c + pltpu.roll(t, ROT_HALF, 1) * sa + pltpu.roll(t, HEAD_DIM - ROT_HALF, 1) * sb
            o_ref[:, :, :, comp * D_MODEL + h * HEAD_DIM:comp * D_MODEL + (h + 1) * HEAD_DIM] = (
                t.reshape(ns, d, tt, HEAD_DIM).astype(o_ref.dtype))


def qkv_proj_group(x, w_bf16, positions, g):
    t, k = x.shape
    d = DIL[g]
    span = SUB_WIN * d
    tm = QKV_TM
    ns = max(1, tm // span)
    tps = max(1, span // tm)
    tt = tm // (ns * d)
    rope = rope_tables(positions.reshape(-1, ns, tt, d).swapaxes(2, 3).reshape(-1))
    tab_spec = pl.BlockSpec((tm, LANES), lambda i: (i, 0))
    return pl.pallas_call(
        functools.partial(_qkv_proj_kernel, d=d, ns=ns, tt=tt),
        grid=(t // tm,),
        in_specs=[pl.BlockSpec((tm, k), lambda i: (i, 0)),
                  pl.BlockSpec((k, 3 * D_MODEL), lambda i: (0, g)),
                  tab_spec, tab_spec, tab_spec],
        out_specs=pl.BlockSpec((ns, d, tt, 3 * D_MODEL), lambda i: (i // tps, 0, i % tps, 0)),
        out_shape=jax.ShapeDtypeStruct((t // span, d, SUB_WIN, 3 * D_MODEL), _BF16),
        compiler_params=_cparams("parallel"), name=f"qkv_proj_g{g}",
    )(x, w_bf16, *rope)


def _dil_attn_kernel(q_ref, k_ref, v_ref, kp_ref, vp_ref, o_ref, lse_ref, *, d, ns):
    j = pl.program_id(1)
    hp = pl.program_id(2)
    nb = ns * d
    row = lax.broadcasted_iota(jnp.int32, (1, SUB_WIN, SUB_WIN), 1)
    col = lax.broadcasted_iota(jnp.int32, (1, SUB_WIN, SUB_WIN), 2)
    cur_ok = col <= row
    blk = lax.broadcasted_iota(jnp.int32, (nb, 1, 1), 0)
    prev_ok = (col >= row) & ((blk >= d) | (j > 0))
    lane = lax.broadcasted_iota(jnp.int32, (SUB_WIN, LANES), 1)
    scale = 1.0 / math.sqrt(HEAD_DIM)

    def block_rows(b):
        s, r = divmod(b, d)
        start = s * SUB_WIN * d + r
        return pl.ds(start, SUB_WIN, stride=d) if d > 1 else pl.ds(start, SUB_WIN)

    @pl.when(hp == 0)
    def _():
        lse_ref[...] = jnp.zeros_like(lse_ref)

    lses = []
    for hh in range(HEADS_PER_STEP):
        sl = slice(hh * HEAD_DIM, (hh + 1) * HEAD_DIM)
        flat = lambda a: a.reshape(nb, SUB_WIN, HEAD_DIM)
        q, kc, vc = flat(q_ref[:, :, :, sl]), flat(k_ref[:, :, :, sl]), flat(v_ref[:, :, :, sl])
        if ns > 1:
            kp = flat(jnp.concatenate([kp_ref[:, :, :, sl], k_ref[:ns - 1, :, :, sl]], axis=0))
            vp = flat(jnp.concatenate([vp_ref[:, :, :, sl], v_ref[:ns - 1, :, :, sl]], axis=0))
        else:
            kp, vp = flat(kp_ref[:, :, :, sl]), flat(vp_ref[:, :, :, sl])
        sc = jnp.einsum('bqd,bkd->bqk', q, kc, preferred_element_type=_F32) * scale
        sp = jnp.einsum('bqd,bkd->bqk', q, kp, preferred_element_type=_F32) * scale
        sc = jnp.where(cur_ok, sc, NEG_BIG)
        sp = jnp.where(prev_ok, sp, NEG_BIG)
        m = jnp.maximum(jnp.max(sc, axis=2, keepdims=True), jnp.max(sp, axis=2, keepdims=True))
        pc = jnp.exp(sc - m)
        pp = jnp.exp(sp - m)
        l = jnp.sum(pc, axis=2, keepdims=True) + jnp.sum(pp, axis=2, keepdims=True)
        acc = jnp.einsum('bqk,bkd->bqd', pc.astype(_BF16), vc, preferred_element_type=_F32)
        acc += jnp.einsum('bqk,bkd->bqd', pp.astype(_BF16), vp, preferred_element_type=_F32)
        o = acc / l
        for b in range(nb):
            o_ref[hh, block_rows(b), :] = o[b]
        lses.append(m + jnp.log(l))
    for b in range(nb):
        cur = lse_ref[block_rows(b), :]
        for hh in range(HEADS_PER_STEP):
            cur = jnp.where(lane == hp * HEADS_PER_STEP + hh, lses[hh][b], cur)
        lse_ref[block_rows(b), :] = cur


def dilated_group_attention(qkv, g, bsz, seq):
    d = DIL[g]
    ns = ATTN_ROWS // (SUB_WIN * d)
    nsteps = seq // ATTN_ROWS
    hw = HEADS_PER_STEP * HEAD_DIM
    ncol = D_MODEL // hw
    cur = lambda comp: pl.BlockSpec((ns, d, SUB_WIN, hw), lambda b, j, hp: (b * nsteps + j, 0, 0, comp * ncol + hp))
    prev = lambda comp: pl.BlockSpec(
        (1, d, SUB_WIN, hw), lambda b, j, hp: (jnp.maximum((b * nsteps + j) * ns - 1, 0), 0, 0, comp * ncol + hp))
    return pl.pallas_call(
        functools.partial(_dil_attn_kernel, d=d, ns=ns),
        grid=(bsz, nsteps, HEADS // HEADS_PER_STEP),
        in_specs=[cur(0), cur(1), cur(2), prev(1), prev(2)],
        out_specs=[pl.BlockSpec((HEADS_PER_STEP, ATTN_ROWS, HEAD_DIM), lambda b, j, hp: (hp, b * nsteps + j, 0)),
                   pl.BlockSpec((ATTN_ROWS, LANES), lambda b, j, hp: (b * nsteps + j, 0))],
        out_shape=[jax.ShapeDtypeStruct((HEADS, bsz * seq, HEAD_DIM), _F32),
                   jax.ShapeDtypeStruct((bsz * seq, LANES), _F32)],
        compiler_params=_cparams("parallel", "parallel", "arbitrary"),
        name=f"dilated_attn_g{g}",
    )(qkv, qkv, qkv, qkv, qkv)


SSM_IN_TM = 512
HALO = SUBLANES
ZX_COLS = SSM_INNER + SSM_XBC


def _softplus(x):
    return jnp.maximum(x, 0.0) + jnp.log1p(jnp.exp(-jnp.abs(x)))


def _ssm_in_kernel(x_ref, w_ref, wdt_ref, cw_ref, cb_ref, dtb_ref, gz_ref, xs_ref, bc_ref, dt_ref,
                   carry_ref, *, tiles_per_seq):
    tm = x_ref.shape[0]
    cw = D_MODEL
    xb = x_ref[...].astype(_BF16)

    @pl.when(pl.program_id(0) % tiles_per_seq == 0)
    def _():
        carry_ref[...] = jnp.zeros_like(carry_ref)

    for i in range(SSM_INNER // cw):
        z = jnp.dot(xb, w_ref[:, i * cw:(i + 1) * cw], preferred_element_type=_F32)
        gz_ref[:, i * cw:(i + 1) * cw] = z * jax.nn.sigmoid(z)
    for c in range(SSM_XBC // cw):
        cols = slice(c * cw, (c + 1) * cw)
        acc = jnp.dot(xb, w_ref[:, SSM_INNER + c * cw:SSM_INNER + (c + 1) * cw], preferred_element_type=_F32)
        ext = jnp.concatenate([carry_ref[c], acc], axis=0)
        carry_ref[c] = acc[tm - HALO:, :]
        conv = cb_ref[:, cols] + cw_ref[SSM_CONV - 1:SSM_CONV, cols] * acc
        for k in range(1, SSM_CONV):
            conv = conv + cw_ref[SSM_CONV - 1 - k:SSM_CONV - k, cols] * pltpu.roll(ext, k, 0)[HALO:, :]
        act = conv * jax.nn.sigmoid(conv)
        if (c + 1) * cw <= SSM_INNER:
            xs_ref[:, cols] = act
        else:
            bc_ref[...] = act.astype(_BF16)
    dt_raw = jnp.dot(xb, wdt_ref[...], preferred_element_type=_F32)
    dt_ref[...] = _softplus(dt_raw + dtb_ref[...])


def ssm_in_proj(x, w_in, conv_w, conv_b, dt_bias, seq):
    t, k = x.shape
    tm = SSM_IN_TM
    assert 2 * SSM_BC == D_MODEL
    pad = LANES - SSM_HEADS
    w_main = w_in[:, :ZX_COLS].astype(_BF16)
    w_dt = jnp.pad(w_in[:, ZX_COLS:], ((0, 0), (0, pad))).astype(_BF16)
    dtb = jnp.pad(dt_bias, (0, pad)).reshape(1, LANES)
    cst = lambda shape: pl.BlockSpec(shape, lambda i: (0,) * len(shape))
    once = lambda shape: pl.BlockSpec(shape, lambda i: (0,) * len(shape), pipeline_mode=pl.Buffered(1))
    row = lambda w: pl.BlockSpec((tm, w), lambda i: (i, 0))
    return pl.pallas_call(
        functools.partial(_ssm_in_kernel, tiles_per_seq=seq // tm),
        grid=(t // tm,),
        in_specs=[row(k), once((k, ZX_COLS)), once((k, LANES)),
                  cst((SSM_CONV, SSM_XBC)), cst((1, SSM_XBC)), cst((1, LANES))],
        out_specs=[row(SSM_INNER), row(SSM_INNER), row(2 * SSM_BC), row(LANES)],
        out_shape=[jax.ShapeDtypeStruct((t, SSM_INNER), _F32), jax.ShapeDtypeStruct((t, SSM_INNER), _F32),
                   jax.ShapeDtypeStruct((t, 2 * SSM_BC), _BF16), jax.ShapeDtypeStruct((t, LANES), _F32)],
        scratch_shapes=[pltpu.VMEM((SSM_XBC // D_MODEL, HALO, D_MODEL), _F32)],
        compiler_params=_cparams("arbitrary"), name="ssm_in_proj",
    )(x, w_main, w_dt, conv_w, conv_b.reshape(1, -1), dtb)


def _ssd_kernel(gz_ref, xs_ref, bc_ref, dt_ref, a_ref, dsk_ref, nw_ref, o_ref, state_ref, y_ref):
    c = pl.program_id(1)
    q = CHUNK

    @pl.when(c == 0)
    def _():
        state_ref[...] = jnp.zeros_like(state_ref)

    dt = dt_ref[...]
    da = dt * a_ref[...]
    r_io = lax.broadcasted_iota(jnp.int32, (q, q), 0)
    c_io = lax.broadcasted_iota(jnp.int32, (q, q), 1)
    causal = r_io >= c_io
    tri = causal.astype(_F32)
    a_cum = jnp.dot(tri, da, preferred_element_type=_F32, precision=lax.Precision.HIGHEST)
    a_cum_t = a_cum.T
    dt_t = dt.T
    a_last = a_cum[q - 1:q, :]
    w_lanes = _per_head_lanes(jnp.exp(a_last - a_cum) * dt, SSM_P, SSM_INNER)
    e_lanes = _per_head_lanes(jnp.exp(a_cum), SSM_P, SSM_INNER)
    chunk_decay = jnp.exp(a_last)
    lane = lax.broadcasted_iota(jnp.int32, (q, LANES), 1)
    low = lane < SSM_P
    lane1 = lax.broadcasted_iota(jnp.int32, (1, LANES), 1)
    low1 = lane1 < SSM_P
    tn = (((0,), (0,)), ((), ()))
    nt = (((1,), (1,)), ((), ()))

    for g in range(SSM_GROUPS):
        bg = bc_ref[:, g * SSM_N:(g + 1) * SSM_N]
        cg = bc_ref[:, SSM_BC + g * SSM_N:SSM_BC + (g + 1) * SSM_N]
        cbm = lax.dot_general(cg, bg, nt, preferred_element_type=_F32)
        st = state_ref[g]
        y_off = jnp.dot(cg, st.astype(_BF16), preferred_element_type=_F32)
        xw_parts, dec_parts = [], []
        for j in range(SSM_GHEADS // 2):
            h0 = g * SSM_GHEADS + 2 * j
            col = (g * SSM_GHEADS // 2 + j) * LANES
            xp = xs_ref[:, col:col + LANES]
            y_diag = jnp.zeros((q, LANES), _F32)
            for half, hh in enumerate((h0, h0 + 1)):
                diff = a_cum[:, hh:hh + 1] - a_cum_t[hh:hh + 1, :]
                lmat = jnp.exp(jnp.where(causal, diff, NEG_BIG))
                mm = (cbm * lmat * dt_t[hh:hh + 1, :]).astype(_BF16)
                keep = low if half == 0 else jnp.logical_not(low)
                xh = jnp.where(keep, xp, 0.0).astype(_BF16)
                y_diag = y_diag + jnp.dot(mm, xh, preferred_element_type=_F32)
            y_ref[:, col:col + LANES] = (y_diag + y_off[:, j * LANES:(j + 1) * LANES] * e_lanes[:, col:col + LANES]
                                         + xp * dsk_ref[:, col:col + LANES])
            xw_parts.append((xp * w_lanes[:, col:col + LANES]).astype(_BF16))
            dec_parts.append(jnp.where(low1, chunk_decay[:, h0:h0 + 1], chunk_decay[:, h0 + 1:h0 + 2]))
        xw = jnp.concatenate(xw_parts, axis=1)
        dec = jnp.concatenate(dec_parts, axis=1)
        state_ref[g] = st * dec + lax.dot_general(bg, xw, tn, preferred_element_type=_F32)

    yg = y_ref[...] * gz_ref[...]
    gw = SSM_INNER // SSM_GROUPS
    for g in range(SSM_GROUPS):
        v = yg[:, g * gw:(g + 1) * gw]
        ms = jnp.mean(v * v, axis=-1, keepdims=True)
        o_ref[:, g * gw:(g + 1) * gw] = (v * lax.rsqrt(ms + RMS_EPS)
                                         * nw_ref[:, g * gw:(g + 1) * gw]).astype(o_ref.dtype)


def ssd_mixer(gz, xs, bc, dt, a_log, d_skip, norm_w, bsz, seq):
    nc = seq // CHUNK
    a = jnp.pad(-jnp.exp(a_log.astype(_F32)), (0, LANES - SSM_HEADS)).reshape(1, LANES)
    dsk = jnp.repeat(d_skip.astype(_F32), SSM_P).reshape(1, SSM_INNER)
    cst = lambda shape: pl.BlockSpec(shape, lambda b, c: (0,) * len(shape))
    row = lambda w: pl.BlockSpec((CHUNK, w), lambda b, c: (b * nc + c, 0))
    return pl.pallas_call(
        _ssd_kernel,
        grid=(bsz, nc),
        in_specs=[row(SSM_INNER), row(SSM_INNER), row(2 * SSM_BC), row(LANES),
                  cst((1, LANES)), cst((1, SSM_INNER)), cst((1, SSM_INNER))],
        out_specs=row(SSM_INNER),
        out_shape=jax.ShapeDtypeStruct((bsz * seq, SSM_INNER), _BF16),
        scratch_shapes=[pltpu.VMEM((SSM_GROUPS, SSM_N, SSM_GHEADS * SSM_P), _F32),
                        pltpu.VMEM((CHUNK, SSM_INNER), _F32)],
        compiler_params=_cparams("parallel", "arbitrary"), name="ssd_mixer",
    )(gz, xs, bc, dt, a, dsk, norm_w.reshape(1, -1))


ROUTE_SUB = 256
ROUTE_TM_ATTN = 512
ROUTE_TM_SSM = 1024


def _layer_norm(y, g, b):
    mu = jnp.mean(y, axis=-1, keepdims=True)
    yc = y - mu
    var = jnp.mean(yc * yc, axis=-1, keepdims=True)
    return yc * lax.rsqrt(var + LN_EPS) * g + b


def _route(h1s, wrt_ref, br_ref, eidx_ref, gate_ref, tcnt_ref):
    sub = ROUTE_SUB
    n = len(h1s)
    nt = (((1,), (1,)), ((), ()))
    wrt = wrt_ref[...]
    vals = [lax.dot_general(wrt, h, nt, preferred_element_type=_F32, precision=lax.Precision.HIGHEST)
            + br_ref[...] for h in h1s]
    eio = lax.broadcasted_iota(jnp.int32, (N_EXPERTS, sub), 0)
    tops, ids, hots = ([[] for _ in range(n)] for _ in range(3))
    for _ in range(TOP_K):
        for a in range(n):
            mk = jnp.max(vals[a], axis=0, keepdims=True)
            ik = jnp.min(jnp.where(vals[a] == mk, eio, N_EXPERTS), axis=0, keepdims=True)
            hot = eio == ik
            vals[a] = jnp.where(hot, -jnp.inf, vals[a])
            tops[a].append(mk)
            ids[a].append(ik)
            hots[a].append(hot)
    col = lax.broadcasted_iota(jnp.int32, tcnt_ref.shape, 1)
    tcnt = tcnt_ref[...]
    for a in range(n):
        sl = slice(a * sub, (a + 1) * sub)
        exps = [jnp.exp(t - tops[a][0]) for t in tops[a]]
        den = exps[0] + exps[1] + exps[2] + exps[3]
        for k in range(TOP_K):
            eidx_ref[k:k + 1, sl] = ids[a][k]
            gate_ref[k:k + 1, sl] = exps[k] / den
        hot_all = hots[a][0] | hots[a][1] | hots[a][2] | hots[a][3]
        counts = jnp.sum(hot_all.astype(_F32), axis=1, keepdims=True)
        tcnt = jnp.where(col == pl.program_id(0) * n + a, counts, tcnt)
    tcnt_ref[...] = tcnt


def _norm_and_route(mixes, x_ref, g_ref, b_ref, wrt_ref, br_ref, h_ref, eidx_ref, gate_ref, tcnt_ref):
    sub = ROUTE_SUB

    @pl.when(pl.program_id(0) == 0)
    def _():
        tcnt_ref[...] = jnp.zeros_like(tcnt_ref)

    h1s = [_layer_norm(DEEPNORM_ALPHA * x_ref[a * sub:(a + 1) * sub, :] + mix, g_ref[...], b_ref[...])
           for a, mix in enumerate(mixes)]
    for a, h1 in enumerate(h1s):
        h_ref[a * sub:(a + 1) * sub, :] = h1
    _route(h1s, wrt_ref, br_ref, eidx_ref, gate_ref, tcnt_ref)


def _attn_out_kernel(o0_ref, o1_ref, o2_ref, l0_ref, l1_ref, l2_ref, x_ref, wo_ref, g_ref, b_ref,
                     wrt_ref, br_ref, h_ref, eidx_ref, gate_ref, tcnt_ref, ob_ref):
    sub = ROUTE_SUB
    l0, l1, l2 = l0_ref[...], l1_ref[...], l2_ref[...]
    m = jnp.maximum(jnp.maximum(l0, l1), l2)
    e0, e1, e2 = jnp.exp(l0 - m), jnp.exp(l1 - m), jnp.exp(l2 - m)
    inv = 1.0 / (e0 + e1 + e2)
    w0, w1, w2 = e0 * inv, e1 * inv, e2 * inv
    for h in range(HEADS):
        o = w0[:, h:h + 1] * o0_ref[h] + w1[:, h:h + 1] * o1_ref[h] + w2[:, h:h + 1] * o2_ref[h]
        ob_ref[:, h * HEAD_DIM:(h + 1) * HEAD_DIM] = o.astype(_BF16)
    mixes = [jnp.dot(ob_ref[a * sub:(a + 1) * sub, :], wo_ref[...], preferred_element_type=_F32)
             for a in range(x_ref.shape[0] // sub)]
    _norm_and_route(mixes, x_ref, g_ref, b_ref, wrt_ref, br_ref, h_ref, eidx_ref, gate_ref, tcnt_ref)


def _ssm_out_kernel(y_ref, x_ref, wo_ref, g_ref, b_ref, wrt_ref, br_ref, h_ref, eidx_ref, gate_ref, tcnt_ref):
    sub = ROUTE_SUB
    mixes = [jnp.dot(y_ref[a * sub:(a + 1) * sub, :], wo_ref[...], preferred_element_type=_F32)
             for a in range(x_ref.shape[0] // sub)]
    _norm_and_route(mixes, x_ref, g_ref, b_ref, wrt_ref, br_ref, h_ref, eidx_ref, gate_ref, tcnt_ref)


def _route_out_specs(t, tm):
    nsub = t // ROUTE_SUB
    specs = [pl.BlockSpec((tm, D_MODEL), lambda i: (i, 0)),
             pl.BlockSpec((TOP_K, tm), lambda i: (0, i)),
             pl.BlockSpec((TOP_K, tm), lambda i: (0, i)),
             pl.BlockSpec((N_EXPERTS, nsub), lambda i: (0, 0))]
    shapes = [jax.ShapeDtypeStruct((t, D_MODEL), _F32),
              jax.ShapeDtypeStruct((TOP_K, t), jnp.int32),
              jax.ShapeDtypeStruct((TOP_K, t), _F32),
              jax.ShapeDtypeStruct((N_EXPERTS, nsub), _F32)]
    return specs, shapes


def _const_spec(shape):
    return pl.BlockSpec(shape, lambda i: (0,) * len(shape))


def attn_out_norm_route(os_, lses, x, wo_bf16, ln_g, ln_b, w_router, b_router):
    t = x.shape[0]
    tm = ROUTE_TM_ATTN
    row = lambda w: pl.BlockSpec((tm, w), lambda i: (i, 0))
    out_specs, out_shapes = _route_out_specs(t, tm)
    return pl.pallas_call(
        _attn_out_kernel,
        grid=(t // tm,),
        in_specs=[pl.BlockSpec((HEADS, tm, HEAD_DIM), lambda i: (0, i, 0))] * 3 + [row(LANES)] * 3 + [row(D_MODEL),
                  _const_spec((D_MODEL, D_MODEL)), _const_spec((1, D_MODEL)), _const_spec((1, D_MODEL)),
                  _const_spec((N_EXPERTS, D_MODEL)), _const_spec((N_EXPERTS, 1))],
        out_specs=out_specs, out_shape=out_shapes,
        scratch_shapes=[pltpu.VMEM((tm, D_MODEL), _BF16)],
        compiler_params=_cparams("arbitrary"), name="attn_out_norm_route",
    )(*os_, *lses, x, wo_bf16, ln_g.reshape(1, -1), ln_b.reshape(1, -1),
      w_router.T, b_router.reshape(-1, 1))


def ssm_out_norm_route(yn, x, wo_bf16, ln_g, ln_b, w_router, b_router):
    t = x.shape[0]
    tm = ROUTE_TM_SSM
    row = lambda w: pl.BlockSpec((tm, w), lambda i: (i, 0))
    out_specs, out_shapes = _route_out_specs(t, tm)
    return pl.pallas_call(
        _ssm_out_kernel,
        grid=(t // tm,),
        in_specs=[row(SSM_INNER), row(D_MODEL),
                  _const_spec((SSM_INNER, D_MODEL)), _const_spec((1, D_MODEL)), _const_spec((1, D_MODEL)),
                  _const_spec((N_EXPERTS, D_MODEL)), _const_spec((N_EXPERTS, 1))],
        out_specs=out_specs, out_shape=out_shapes,
        compiler_params=_cparams("arbitrary"), name="ssm_out_norm_route",
    )(yn, x, wo_bf16, ln_g.reshape(1, -1), ln_b.reshape(1, -1), w_router.T, b_router.reshape(-1, 1))


MOE_BM = 1024
MOE_SUB_BM = 256
MOE_SUB = ROUTE_SUB
SEG_ALIGN = SUBLANES
LOCAL_ROWS = MOE_SUB * TOP_K + N_EXPERTS * SUBLANES
SEG_UNIT = 16
SEG_REST = (8,)
TAIL_SIZES = tuple(SEG_ALIGN << k for k in reversed(range((MOE_BM // SEG_ALIGN).bit_length() - 1)))
PAIR_COLS = D_MODEL // 2
_U32 = jnp.uint32


def _pack_pairs(x):
    lo = lax.shift_right_logical(lax.bitcast_convert_type(x[:, :PAIR_COLS], _U32), jnp.uint32(16))
    hi = lax.bitcast_convert_type(x[:, PAIR_COLS:], _U32) & jnp.uint32(0xFFFF0000)
    return hi | lo


def _unpack_pairs(u):
    lo = lax.bitcast_convert_type(lax.shift_left(u, jnp.uint32(16)), _F32).astype(_BF16)
    hi = lax.bitcast_convert_type(u & jnp.uint32(0xFFFF0000), _F32).astype(_BF16)
    return lo, hi


def _local_positions(eidx):
    sub = eidx.shape[1]
    eio = lax.broadcasted_iota(jnp.int32, (N_EXPERTS, sub), 0)
    hots = [eio == eidx[k:k + 1, :] for k in range(TOP_K)]
    hot_all = (hots[0] | hots[1] | hots[2] | hots[3]).astype(_BF16)
    r_io = lax.broadcasted_iota(jnp.int32, (sub, sub), 0)
    c_io = lax.broadcasted_iota(jnp.int32, (sub, sub), 1)
    upper = (r_io < c_io).astype(_BF16)
    prefix = jnp.dot(hot_all, upper, preferred_element_type=_F32)
    counts = jnp.sum(hot_all.astype(_F32), axis=1, keepdims=True).astype(jnp.int32)
    padded = ((counts + (SEG_ALIGN - 1)) & (-SEG_ALIGN)).astype(_F32)
    e_r = lax.broadcasted_iota(jnp.int32, (N_EXPERTS, N_EXPERTS), 0)
    e_c = lax.broadcasted_iota(jnp.int32, (N_EXPERTS, N_EXPERTS), 1)
    below = (e_c < e_r).astype(_BF16)
    seg_start = jnp.dot(below, jnp.broadcast_to(padded, (N_EXPERTS, sub)).astype(_BF16),
                        preferred_element_type=_F32)
    base = seg_start + prefix
    return [jnp.sum(jnp.where(h, base, 0.0), axis=0, keepdims=True) for h in hots]


def _segment_copies(cnt_ref, lo_ref, go_ref, i, copy_fn):
    for e in range(N_EXPERTS):
        n = cnt_ref[i * N_EXPERTS + e]
        lo = lo_ref[i * N_EXPERTS + e]
        go = go_ref[i * N_EXPERTS + e]

        def unit(u, c, lo=lo, go=go):
            copy_fn(pl.multiple_of(lo + u * SEG_UNIT, SEG_ALIGN), pl.multiple_of(go + u * SEG_UNIT, SEG_ALIGN),
                    SEG_UNIT)
            return c

        lax.fori_loop(0, n // SEG_UNIT, unit, 0)
        for sz in SEG_REST:
            done = n & -(2 * sz)

            @pl.when((n & sz) != 0)
            def _(done=done, sz=sz, lo=lo, go=go):
                copy_fn(pl.multiple_of(lo + done, SEG_ALIGN), pl.multiple_of(go + done, SEG_ALIGN), sz)


def _dispatch_kernel(cnt_ref, lo_ref, go_ref, units_ref, tail_ref, eidx_ref, h_ref, xs_ref, lpos_ref,
                     buf_ref, sem):
    i = pl.program_id(0)
    slot = i % 2
    mine, other = buf_ref.at[slot], buf_ref.at[1 - slot]
    lpos = _local_positions(eidx_ref[...])
    for k in range(TOP_K):
        lpos_ref[k:k + 1, :] = lpos[k]
    p_io = lax.broadcasted_iota(jnp.int32, (LOCAL_ROWS, MOE_SUB), 0).astype(_F32)
    sel = (p_io == lpos[0]) | (p_io == lpos[1]) | (p_io == lpos[2]) | (p_io == lpos[3])
    mine[...] = _pack_pairs(jnp.dot(sel.astype(_BF16), h_ref[...].astype(_BF16), preferred_element_type=_F32))

    def copy(lo, go, sz):
        pltpu.make_async_copy(mine.at[pl.ds(lo, sz)], xs_ref.at[pl.ds(go, sz)], sem.at[slot]).start()

    _segment_copies(cnt_ref, lo_ref, go_ref, i, copy)

    def wait_units(buf, s, n_units):
        def body(j, c):
            pltpu.make_async_copy(buf.at[pl.ds(0, SEG_ALIGN)], xs_ref.at[pl.ds(0, SEG_ALIGN)], sem.at[s]).wait()
            return c
        lax.fori_loop(0, n_units, body, 0)

    @pl.when(i > 0)
    def _():
        wait_units(other, 1 - slot, units_ref[jnp.maximum(i - 1, 0)])

    @pl.when(i == pl.num_programs(0) - 1)
    def _():
        wait_units(mine, slot, units_ref[i])
        mine[0:MOE_BM, :] = jnp.zeros((MOE_BM, PAIR_COLS), _U32)
        start, n, n_blocks = tail_ref[0], tail_ref[1], tail_ref[2]
        done = jnp.int32(0)
        for sz in TAIL_SIZES:
            @pl.when((n & sz) != 0)
            def _(done=done, sz=sz):
                pltpu.make_async_copy(mine.at[pl.ds(0, sz)],
                                      xs_ref.at[pl.ds(pl.multiple_of(start + done, SEG_ALIGN), sz)],
                                      sem.at[slot]).start()
            done = done + (n & sz)

        def zero_block(j, c):
            row0 = pl.multiple_of(start + n + j * MOE_BM, MOE_BM)
            pltpu.make_async_copy(mine.at[pl.ds(0, MOE_BM)], xs_ref.at[pl.ds(row0, MOE_BM)], sem.at[slot]).start()
            return c

        lax.fori_loop(0, n_blocks, zero_block, 0)
        wait_units(mine, slot, n // SEG_ALIGN + n_blocks * (MOE_BM // SEG_ALIGN))


def _seg_tables(tile_counts):
    cnt = tile_counts.T.astype(jnp.int32)
    pc = (cnt + (SEG_ALIGN - 1)) & (-SEG_ALIGN)
    lo = jnp.cumsum(pc, axis=1) - pc
    tot = jnp.sum(pc, axis=0)
    offs = jnp.concatenate([jnp.zeros((1,), jnp.int32), jnp.cumsum(tot)])
    go = offs[None, :-1] + jnp.cumsum(pc, axis=0) - pc
    units = jnp.sum(pc, axis=1) // SEG_ALIGN
    return pc.reshape(-1), lo.reshape(-1), go.reshape(-1), units.astype(jnp.int32), offs.astype(jnp.int32)


def _max_rows(t):
    n = t * TOP_K + (t // MOE_SUB) * N_EXPERTS * (SEG_ALIGN - 1)
    return -(-n // MOE_BM) * MOE_BM


def moe_dispatch(h, eidx, tables):
    t = h.shape[0]
    pc, lo, go, units, offs = tables
    used = offs[-1]
    partial = (-used) % MOE_BM
    tail = jnp.stack([used, partial, (_max_rows(t) - used - partial) // MOE_BM]).astype(jnp.int32)
    grid_spec = pltpu.PrefetchScalarGridSpec(
        num_scalar_prefetch=5, grid=(t // MOE_SUB,),
        in_specs=[pl.BlockSpec((TOP_K, MOE_SUB), lambda i, *_: (0, i)),
                  pl.BlockSpec((MOE_SUB, D_MODEL), lambda i, *_: (i, 0))],
        out_specs=[pl.BlockSpec(memory_space=pl.ANY), pl.BlockSpec((TOP_K, MOE_SUB), lambda i, *_: (0, i))],
        scratch_shapes=[pltpu.VMEM((2, LOCAL_ROWS, PAIR_COLS), _U32), pltpu.SemaphoreType.DMA((2,))])
    return pl.pallas_call(
        _dispatch_kernel, grid_spec=grid_spec,
        out_shape=[jax.ShapeDtypeStruct((_max_rows(t), PAIR_COLS), _U32),
                   jax.ShapeDtypeStruct((TOP_K, t), _F32)],
        compiler_params=_cparams("arbitrary"), name="moe_dispatch",
    )(pc, lo, go, units, tail, eidx, h)


def _combine_kernel(cnt_ref, lo_ref, go_ref, units_ref, lpos_ref, gate_ref, ys_ref, h_ref, g_ref, b_ref,
                    o_ref, buf_ref, sem):
    i = pl.program_id(0)
    slot = i % 2

    def fetch(step, s):
        dst = buf_ref.at[s]
        dst[MOE_SUB * TOP_K:, :] = jnp.zeros((LOCAL_ROWS - MOE_SUB * TOP_K, PAIR_COLS), _U32)

        def copy(lo, go, sz):
            pltpu.make_async_copy(ys_ref.at[pl.ds(go, sz)], dst.at[pl.ds(lo, sz)], sem.at[s]).start()

        _segment_copies(cnt_ref, lo_ref, go_ref, step, copy)

    @pl.when(i == 0)
    def _():
        fetch(i, slot)

    @pl.when(i + 1 < pl.num_programs(0))
    def _():
        fetch(i + 1, 1 - slot)

    mine = buf_ref.at[slot]
    lpos = lpos_ref[...]
    gates = gate_ref[...]
    p_io = lax.broadcasted_iota(jnp.int32, (LOCAL_ROWS, MOE_SUB), 0).astype(_F32)
    w = jnp.zeros((LOCAL_ROWS, MOE_SUB), _F32)
    for k in range(TOP_K):
        w = w + jnp.where(p_io == lpos[k:k + 1, :], gates[k:k + 1, :], 0.0)
    wb = w.astype(_BF16)

    def wait_unit(j, c):
        pltpu.make_async_copy(ys_ref.at[pl.ds(0, SEG_ALIGN)], mine.at[pl.ds(0, SEG_ALIGN)], sem.at[slot]).wait()
        return c

    lax.fori_loop(0, units_ref[i], wait_unit, 0)
    tn = (((0,), (0,)), ((), ()))
    ffn = jnp.concatenate([lax.dot_general(wb, yb, tn, preferred_element_type=_F32)
                           for yb in _unpack_pairs(mine[...])], axis=1)
    o_ref[...] = _layer_norm(DEEPNORM_ALPHA * h_ref[...] + ffn, g_ref[...], b_ref[...])


def moe_combine(ys, lpos, gates, h, tables, ln_g, ln_b):
    t = h.shape[0]
    pc, lo, go, units, _ = tables
    row = pl.BlockSpec((MOE_SUB, D_MODEL), lambda i, *_: (i, 0))
    kt = pl.BlockSpec((TOP_K, MOE_SUB), lambda i, *_: (0, i))
    cst = lambda shape: pl.BlockSpec(shape, lambda i, *_: (0,) * len(shape))
    grid_spec = pltpu.PrefetchScalarGridSpec(
        num_scalar_prefetch=4, grid=(t // MOE_SUB,),
        in_specs=[kt, kt, pl.BlockSpec(memory_space=pl.ANY), row, cst((1, D_MODEL)), cst((1, D_MODEL))],
        out_specs=row,
        scratch_shapes=[pltpu.VMEM((2, LOCAL_ROWS, PAIR_COLS), _U32), pltpu.SemaphoreType.DMA((2,))])
    return pl.pallas_call(
        _combine_kernel, grid_spec=grid_spec,
        out_shape=jax.ShapeDtypeStruct((t, D_MODEL), _F32),
        compiler_params=_cparams("arbitrary"), name="moe_combine",
    )(pc, lo, go, units, lpos, gates, ys, h, ln_g.reshape(1, -1), ln_b.reshape(1, -1))


def _expert_kernel(blk_ref, e_ref, lo_ref, hi_ref, xs_ref, wu_ref, bu_ref, wd_ref, bd_ref,
                   ys_ref, wub_ref, wdb_ref):
    i = pl.program_id(0)
    e = e_ref[i]
    e_prev = e_ref[jnp.maximum(i - 1, 0)]

    @pl.when((i == 0) | (e != e_prev))
    def _():
        wub_ref[...] = wu_ref[...].astype(_BF16)
        wdb_ref[...] = wd_ref[...].astype(_BF16)

    lo = lo_ref[i]
    hi = hi_ref[i]

    @pl.when((lo == 0) | (hi < 0))
    def _():
        ys_ref[...] = jnp.zeros_like(ys_ref)

    for sb in range(xs_ref.shape[0] // MOE_SUB_BM):
        r0 = sb * MOE_SUB_BM
        rows = slice(r0, r0 + MOE_SUB_BM)

        @pl.when((hi > lo) & (hi > r0) & (lo < r0 + MOE_SUB_BM))
        def _(r0=r0, rows=rows):
            x_lo, x_hi = _unpack_pairs(xs_ref[rows, :])
            h = (jnp.dot(x_lo, wub_ref[:PAIR_COLS, :], preferred_element_type=_F32)
                 + jnp.dot(x_hi, wub_ref[PAIR_COLS:, :], preferred_element_type=_F32) + bu_ref[...])
            gate = jnp.minimum(h[:, :D_FF], SWIGLU_LIMIT)
            up = jnp.clip(h[:, D_FF:], -SWIGLU_LIMIT, SWIGLU_LIMIT)
            glu = gate * jax.nn.sigmoid(SWIGLU_ALPHA * gate)
            act = ((up + 1.0) * glu).astype(_BF16)
            y = jnp.dot(act, wdb_ref[...], preferred_element_type=_F32) + bd_ref[...]
            packed = _pack_pairs(y.astype(_BF16).astype(_F32))
            row = r0 + lax.broadcasted_iota(jnp.int32, packed.shape, 0)
            ys_ref[rows, :] = jnp.where((row >= lo) & (row < hi), packed, ys_ref[rows, :])


def _expert_schedule(offs, n_rows, bm):
    nblk = n_rows // bm
    n_items = nblk + N_EXPERTS - 1
    used = offs[-1]
    bnd = offs[1:-1]
    pos = jnp.arange(N_EXPERTS - 1, dtype=jnp.int32) + jnp.minimum(bnd // bm + 1, nblk)
    i = jnp.arange(n_items, dtype=jnp.int32)[:, None]
    at = pos[None, :] == i
    is_bnd = jnp.any(at, axis=1)
    bnd_val = jnp.sum(jnp.where(at, bnd[None, :], 0), axis=1)
    n_before = jnp.sum((pos[None, :] < i).astype(jnp.int32), axis=1)
    cuts = jnp.where(is_bnd, bnd_val, (i[:, 0] - n_before) * bm)
    ends = jnp.concatenate([cuts[1:], jnp.array([n_rows], jnp.int32)])
    last_blk = jnp.maximum(used - 1, 0) // bm
    past = jnp.logical_not(is_bnd) & (cuts // bm > last_blk)
    raw_blk = cuts // bm
    cuts = jnp.minimum(cuts, used)
    ends = jnp.minimum(ends, used)
    blk = jnp.minimum(cuts // bm, last_blk)
    e = jnp.minimum(jnp.sum((bnd[None, :] <= cuts[:, None]).astype(jnp.int32), axis=1), N_EXPERTS - 1)
    lo = jnp.where(past, 0, cuts - blk * bm)
    hi = jnp.where(past, -1, jnp.minimum(ends - blk * bm, bm))
    return lax.cummax(jnp.where(past, raw_blk, blk)), e, lo, hi


def moe_experts(xs, offs, layer, w_up, b_up, w_down, b_down):
    n_rows = xs.shape[0]
    bm = MOE_BM
    blk, e, lo, hi = _expert_schedule(offs, n_rows, bm)
    n_items = n_rows // bm + N_EXPERTS - 1
    wmap = lambda i, b, e, lo, hi: (layer, e[i], 0, 0)
    grid_spec = pltpu.PrefetchScalarGridSpec(
        num_scalar_prefetch=4, grid=(n_items,),
        in_specs=[pl.BlockSpec((bm, PAIR_COLS), lambda i, b, e, lo, hi: (b[i], 0)),
                  pl.BlockSpec((None, None, D_MODEL, 2 * D_FF), wmap),
                  pl.BlockSpec((None, None, 1, 2 * D_FF), wmap),
                  pl.BlockSpec((None, None, D_FF, D_MODEL), wmap),
                  pl.BlockSpec((None, None, 1, D_MODEL), wmap)],
        out_specs=pl.BlockSpec((bm, PAIR_COLS), lambda i, b, e, lo, hi: (b[i], 0)),
        scratch_shapes=[pltpu.VMEM((D_MODEL, 2 * D_FF), _BF16), pltpu.VMEM((D_FF, D_MODEL), _BF16)])
    nl = w_up.shape[0]
    return pl.pallas_call(
        _expert_kernel, grid_spec=grid_spec,
        out_shape=jax.ShapeDtypeStruct((n_rows, PAIR_COLS), _U32),
        compiler_params=_cparams("arbitrary"), name="moe_experts",
    )(blk, e, lo, hi, xs, w_up, b_up.reshape(nl, N_EXPERTS, 1, -1), w_down, b_down.reshape(nl, N_EXPERTS, 1, -1))


def moe_layer(h, eidx, gates, tile_counts, layer, w_up, b_up, w_down, b_down, ln_g, ln_b):
    tables = _seg_tables(tile_counts)
    xs, lpos = moe_dispatch(h, eidx, tables)
    ys = moe_experts(xs, tables[4], layer, w_up, b_up, w_down, b_down)
    return moe_combine(ys, lpos, gates, h, tables, ln_g, ln_b)


def kernel(x, positions, attn_w_in, attn_w_out, ssm_w_in, ssm_conv_w, ssm_conv_b, ssm_dt_bias,
           ssm_a_log, ssm_d, ssm_norm_w, ssm_w_out, moe_w_router, moe_b_router, moe_w_up,
           moe_b_up, moe_w_down, moe_b_down, ln_mix_g, ln_mix_b, ln_ffn_g, ln_ffn_b):
    bsz, seq, _ = x.shape
    xf = x.reshape(bsz * seq, D_MODEL)

    w_qkv = attn_w_in[0].astype(_BF16)
    outs = [dilated_group_attention(qkv_proj_group(xf, w_qkv, positions, g), g, bsz, seq)
            for g in range(N_GROUPS)]
    h, eidx, gates, tile_counts = attn_out_norm_route(
        [o for o, _ in outs], [l for _, l in outs], xf, attn_w_out[0].astype(_BF16),
        ln_mix_g[0], ln_mix_b[0], moe_w_router[0], moe_b_router[0])
    xf = moe_layer(h, eidx, gates, tile_counts, 0, moe_w_up, moe_b_up, moe_w_down, moe_b_down,
                   ln_ffn_g[0], ln_ffn_b[0])

    gz, xs, bc, dt = ssm_in_proj(xf, ssm_w_in[0], ssm_conv_w[0], ssm_conv_b[0], ssm_dt_bias[0], seq)
    yn = ssd_mixer(gz, xs, bc, dt, ssm_a_log[0], ssm_d[0], ssm_norm_w[0], bsz, seq)
    h, eidx, gates, tile_counts = ssm_out_norm_route(
        yn, xf, ssm_w_out[0].astype(_BF16), ln_mix_g[1], ln_mix_b[1], moe_w_router[1], moe_b_router[1])
    xf = moe_layer(h, eidx, gates, tile_counts, 1, moe_w_up, moe_b_up, moe_w_down, moe_b_down,
                   ln_ffn_g[1], ln_ffn_b[1])
    return xf.reshape(bsz, seq, D_MODEL)
```

```python
import functools
import math

import jax
import jax.numpy as jnp
from jax import lax
from jax.experimental import pallas as pl
from jax.experimental.pallas import tpu as pltpu

D_MODEL = 1024
DEPTH = 2
DIL = (1, 4, 16)
N_GROUPS = 3
SUB_WIN = 128
HEAD_DIM = 128
HEADS = 8
ROT_DIM = 32
ROT_HALF = 16
ROPE_THETA = 500000.0
ATTN_PROJ = N_GROUPS * 3 * D_MODEL
SSM_INNER = 2048
SSM_P = 64
SSM_HEADS = 32
SSM_GROUPS = 4
SSM_GHEADS = 8
SSM_N = 128
SSM_CONV = 4
CHUNK = 128
SSM_BC = SSM_GROUPS * SSM_N
SSM_XBC = SSM_INNER + 2 * SSM_BC
N_EXPERTS = 32
TOP_K = 4
D_FF = 1024
SWIGLU_LIMIT = 7.0
SWIGLU_ALPHA = 1.702
DEEPNORM_ALPHA = (2 * DEPTH) ** 0.25
LN_EPS = 1e-5
RMS_EPS = 1e-5

LANES = 128
SUBLANES = 8
VMEM_LIMIT_BYTES = 56 * 1024 * 1024

NEG_BIG = -1e30

_F32 = jnp.float32
_BF16 = jnp.bfloat16


def _cparams(*sem):
    return pltpu.CompilerParams(dimension_semantics=sem, vmem_limit_bytes=VMEM_LIMIT_BYTES)


def _per_head_lanes(m, lanes_per_head, width):
    head_of_lane = lax.broadcasted_iota(jnp.int32, (LANES, width), 1) // lanes_per_head
    spread = (head_of_lane == lax.broadcasted_iota(jnp.int32, (LANES, width), 0)).astype(_BF16)
    hi = m.astype(_BF16)
    lo = (m - hi.astype(_F32)).astype(_BF16)
    return jnp.dot(hi, spread, preferred_element_type=_F32) + jnp.dot(lo, spread, preferred_element_type=_F32)


ROPE_PACK = LANES // ROT_DIM


def _rope_table_kernel(pos_ref, invf_ref, c_ref, sa_ref, sb_ref):
    pos = pos_ref[...].astype(_F32)
    lane = lax.broadcasted_iota(jnp.int32, (pos.shape[0], LANES), 1)
    p = jnp.zeros((pos.shape[0], LANES), _F32)
    group = lax.shift_right_logical(lane, ROT_DIM.bit_length() - 1)
    for q in range(ROPE_PACK):
        p = jnp.where(group == q, pos[:, q:q + 1], p)
    ang = p * invf_ref[...]
    cos = jnp.cos(ang)
    sin = jnp.sin(ang)
    for q in range(ROPE_PACK):
        shift = (LANES - q * ROT_DIM) % LANES
        cq = pltpu.roll(cos, shift, 1) if shift else cos
        sq = pltpu.roll(sin, shift, 1) if shift else sin
        c_ref[q] = jnp.where(lane < ROT_DIM, cq, 1.0)
        sa_ref[q] = jnp.where((lane >= ROT_HALF) & (lane < ROT_DIM), sq, 0.0)
        sb_ref[q] = jnp.where(lane < ROT_HALF, -sq, 0.0)


def rope_tables(positions):
    t = positions.size
    rows = t // ROPE_PACK
    tm = min(1024, rows)
    pos = positions.reshape(ROPE_PACK, rows).T
    j = jnp.arange(LANES)
    invf = (ROPE_THETA ** (-(2.0 * (j % ROT_HALF)).astype(_F32) / ROT_DIM)).astype(_F32).reshape(1, LANES)
    out = jax.ShapeDtypeStruct((ROPE_PACK, rows, LANES), _F32)
    tabs = pl.pallas_call(
        _rope_table_kernel,
        grid=(rows // tm,),
        in_specs=[pl.BlockSpec((tm, ROPE_PACK), lambda i: (i, 0)),
                  pl.BlockSpec((1, LANES), lambda i: (0, 0))],
        out_specs=[pl.BlockSpec((ROPE_PACK, tm, LANES), lambda i: (0, i, 0))] * 3,
        out_shape=[out, out, out],
        compiler_params=_cparams("parallel"),
        name="rope_tables",
    )(pos, invf)
    return [x.reshape(t, LANES) for x in tabs]


QKV_TM = 512
ATTN_ROWS = 2048
HEADS_PER_STEP = 4


def _qkv_proj_kernel(x_ref, w_ref, c_ref, sa_ref, sb_ref, o_ref, *, d, ns, tt):
    tm = x_ref.shape[0]
    xb = x_ref[...].astype(_BF16)
    if d > 1:
        dst = lax.broadcasted_iota(jnp.int32, (tm, tm), 0)
        src = lax.broadcasted_iota(jnp.int32, (tm, tm), 1)
        lt, ld = tt.bit_length() - 1, d.bit_length() - 1
        t_ = dst & (tt - 1)
        r_ = lax.shift_right_logical(dst, lt) & (d - 1)
        s_ = lax.shift_right_logical(dst, lt + ld)
        perm = (src == ((s_ * tt + t_) * d + r_)).astype(_BF16)
        xb = jnp.dot(perm, xb, preferred_element_type=_F32).astype(_BF16)
    c = c_ref[...]
    sa = sa_ref[...]
    sb = sb_ref[...]
    for comp in range(3):
        acc = jnp.dot(xb, w_ref[:, comp * D_MODEL:(comp + 1) * D_MODEL], preferred_element_type=_F32)
        for h in range(HEADS):
            sl = slice(h * HEAD_DIM, (h + 1) * HEAD_DIM)
            t = acc[:, sl]
            if comp < 2:
                t = t * c + pltpu.roll(t, ROT_HALF, 1) * sa + pltpu.roll(t, HEAD_DIM - ROT_HALF, 1) * sb
            o_ref[:, :, :, comp * D_MODEL + h * HEAD_DIM:comp * D_MODEL + (h + 1) * HEAD_DIM] = (
                t.reshape(ns, d, tt, HEAD_DIM).astype(o_ref.dtype))


def qkv_proj_group(x, w_bf16, positions, g):
    t, k = x.shape
    d = DIL[g]
    span = SUB_WIN * d
    tm = QKV_TM
    ns = max(1, tm // span)
    tps = max(1, span // tm)
    tt = tm // (ns * d)
    rope = rope_tables(positions.reshape(-1, ns, tt, d).swapaxes(2, 3).reshape(-1))
    tab_spec = pl.BlockSpec((tm, LANES), lambda i: (i, 0))
    return pl.pallas_call(
        functools.partial(_qkv_proj_kernel, d=d, ns=ns, tt=tt),
        grid=(t // tm,),
        in_specs=[pl.BlockSpec((tm, k), lambda i: (i, 0)),
                  pl.BlockSpec((k, 3 * D_MODEL), lambda i: (0, g)),
                  tab_spec, tab_spec, tab_spec],
        out_specs=pl.BlockSpec((ns, d, tt, 3 * D_MODEL), lambda i: (i // tps, 0, i % tps, 0)),
        out_shape=jax.ShapeDtypeStruct((t // span, d, SUB_WIN, 3 * D_MODEL), _BF16),
        compiler_params=_cparams("parallel"), name=f"qkv_proj_g{g}",
    )(x, w_bf16, *rope)


def _dil_attn_kernel(q_ref, k_ref, v_ref, kp_ref, vp_ref, o_ref, lse_ref, *, d, ns):
    j = pl.program_id(1)
    hp = pl.program_id(2)
    nb = ns * d
    row = lax.broadcasted_iota(jnp.int32, (1, SUB_WIN, SUB_WIN), 1)
    col = lax.broadcasted_iota(jnp.int32, (1, SUB_WIN, SUB_WIN), 2)
    cur_ok = col <= row
    blk = lax.broadcasted_iota(jnp.int32, (nb, 1, 1), 0)
    prev_ok = (col >= row) & ((blk >= d) | (j > 0))
    lane = lax.broadcasted_iota(jnp.int32, (SUB_WIN, LANES), 1)
    scale = 1.0 / math.sqrt(HEAD_DIM)

    def block_rows(b):
        s, r = divmod(b, d)
        start = s * SUB_WIN * d + r
        return pl.ds(start, SUB_WIN, stride=d) if d > 1 else pl.ds(start, SUB_WIN)

    @pl.when(hp == 0)
    def _():
        lse_ref[...] = jnp.zeros_like(lse_ref)

    lses = []
    for hh in range(HEADS_PER_STEP):
        sl = slice(hh * HEAD_DIM, (hh + 1) * HEAD_DIM)
        flat = lambda a: a.reshape(nb, SUB_WIN, HEAD_DIM)
        q, kc, vc = flat(q_ref[:, :, :, sl]), flat(k_ref[:, :, :, sl]), flat(v_ref[:, :, :, sl])
        if ns > 1:
            kp = flat(jnp.concatenate([kp_ref[:, :, :, sl], k_ref[:ns - 1, :, :, sl]], axis=0))
            vp = flat(jnp.concatenate([vp_ref[:, :, :, sl], v_ref[:ns - 1, :, :, sl]], axis=0))
        else:
            kp, vp = flat(kp_ref[:, :, :, sl]), flat(vp_ref[:, :, :, sl])
        sc = jnp.einsum('bqd,bkd->bqk', q, kc, preferred_element_type=_F32) * scale
        sp = jnp.einsum('bqd,bkd->bqk', q, kp, preferred_element_type=_F32) * scale
        sc = jnp.where(cur_ok, sc, NEG_BIG)
        sp = jnp.where(prev_ok, sp, NEG_BIG)
        m = jnp.maximum(jnp.max(sc, axis=2, keepdims=True), jnp.max(sp, axis=2, keepdims=True))
        pc = jnp.exp(sc - m)
        pp = jnp.exp(sp - m)
        l = jnp.sum(pc, axis=2, keepdims=True) + jnp.sum(pp, axis=2, keepdims=True)
        acc = jnp.einsum('bqk,bkd->bqd', pc.astype(_BF16), vc, preferred_element_type=_F32)
        acc += jnp.einsum('bqk,bkd->bqd', pp.astype(_BF16), vp, preferred_element_type=_F32)
        o = acc / l
        for b in range(nb):
            o_ref[hh, block_rows(b), :] = o[b]
        lses.append(m + jnp.log(l))
    for b in range(nb):
        cur = lse_ref[block_rows(b), :]
        for hh in range(HEADS_PER_STEP):
            cur = jnp.where(lane == hp * HEADS_PER_STEP + hh, lses[hh][b], cur)
        lse_ref[block_rows(b), :] = cur


def dilated_group_attention(qkv, g, bsz, seq):
    d = DIL[g]
    ns = ATTN_ROWS // (SUB_WIN * d)
    nsteps = seq // ATTN_ROWS
    hw = HEADS_PER_STEP * HEAD_DIM
    ncol = D_MODEL // hw
    cur = lambda comp: pl.BlockSpec((ns, d, SUB_WIN, hw), lambda b, j, hp: (b * nsteps + j, 0, 0, comp * ncol + hp))
    prev = lambda comp: pl.BlockSpec(
        (1, d, SUB_WIN, hw), lambda b, j, hp: (jnp.maximum((b * nsteps + j) * ns - 1, 0), 0, 0, comp * ncol + hp))
    return pl.pallas_call(
        functools.partial(_dil_attn_kernel, d=d, ns=ns),
        grid=(bsz, nsteps, HEADS // HEADS_PER_STEP),
        in_specs=[cur(0), cur(1), cur(2), prev(1), prev(2)],
        out_specs=[pl.BlockSpec((HEADS_PER_STEP, ATTN_ROWS, HEAD_DIM), lambda b, j, hp: (hp, b * nsteps + j, 0)),
                   pl.BlockSpec((ATTN_ROWS, LANES), lambda b, j, hp: (b * nsteps + j, 0))],
        out_shape=[jax.ShapeDtypeStruct((HEADS, bsz * seq, HEAD_DIM), _F32),
                   jax.ShapeDtypeStruct((bsz * seq, LANES), _F32)],
        compiler_params=_cparams("parallel", "parallel", "arbitrary"),
        name=f"dilated_attn_g{g}",
    )(qkv, qkv, qkv, qkv, qkv)


SSM_IN_TM = 512
HALO = SUBLANES
ZX_COLS = SSM_INNER + SSM_XBC


def _softplus(x):
    return jnp.maximum(x, 0.0) + jnp.log1p(jnp.exp(-jnp.abs(x)))


def _ssm_in_kernel(x_ref, w_ref, wdt_ref, cw_ref, cb_ref, dtb_ref, gz_ref, xs_ref, bc_ref, dt_ref,
                   carry_ref, *, tiles_per_seq):
    tm = x_ref.shape[0]
    cw = D_MODEL
    xb = x_ref[...].astype(_BF16)

    @pl.when(pl.program_id(0) % tiles_per_seq == 0)
    def _():
        carry_ref[...] = jnp.zeros_like(carry_ref)

    for i in range(SSM_INNER // cw):
        z = jnp.dot(xb, w_ref[:, i * cw:(i + 1) * cw], preferred_element_type=_F32)
        gz_ref[:, i * cw:(i + 1) * cw] = z * jax.nn.sigmoid(z)
    for c in range(SSM_XBC // cw):
        cols = slice(c * cw, (c + 1) * cw)
        acc = jnp.dot(xb, w_ref[:, SSM_INNER + c * cw:SSM_INNER + (c + 1) * cw], preferred_element_type=_F32)
        ext = jnp.concatenate([carry_ref[c], acc], axis=0)
        carry_ref[c] = acc[tm - HALO:, :]
        conv = cb_ref[:, cols] + cw_ref[SSM_CONV - 1:SSM_CONV, cols] * acc
        for k in range(1, SSM_CONV):
            conv = conv + cw_ref[SSM_CONV - 1 - k:SSM_CONV - k, cols] * pltpu.roll(ext, k, 0)[HALO:, :]
        act = conv * jax.nn.sigmoid(conv)
        if (c + 1) * cw <= SSM_INNER:
            xs_ref[:, cols] = act
        else:
            bc_ref[...] = act.astype(_BF16)
    dt_raw = jnp.dot(xb, wdt_ref[...], preferred_element_type=_F32)
    dt_ref[...] = _softplus(dt_raw + dtb_ref[...])


def ssm_in_proj(x, w_in, conv_w, conv_b, dt_bias, seq):
    t, k = x.shape
    tm = SSM_IN_TM
    assert 2 * SSM_BC == D_MODEL
    pad = LANES - SSM_HEADS
    w_main = w_in[:, :ZX_COLS].astype(_BF16)
    w_dt = jnp.pad(w_in[:, ZX_COLS:], ((0, 0), (0, pad))).astype(_BF16)
    dtb = jnp.pad(dt_bias, (0, pad)).reshape(1, LANES)
    cst = lambda shape: pl.BlockSpec(shape, lambda i: (0,) * len(shape))
    once = lambda shape: pl.BlockSpec(shape, lambda i: (0,) * len(shape), pipeline_mode=pl.Buffered(1))
    row = lambda w: pl.BlockSpec((tm, w), lambda i: (i, 0))
    return pl.pallas_call(
        functools.partial(_ssm_in_kernel, tiles_per_seq=seq // tm),
        grid=(t // tm,),
        in_specs=[row(k), once((k, ZX_COLS)), once((k, LANES)),
                  cst((SSM_CONV, SSM_XBC)), cst((1, SSM_XBC)), cst((1, LANES))],
        out_specs=[row(SSM_INNER), row(SSM_INNER), row(2 * SSM_BC), row(LANES)],
        out_shape=[jax.ShapeDtypeStruct((t, SSM_INNER), _F32), jax.ShapeDtypeStruct((t, SSM_INNER), _F32),
                   jax.ShapeDtypeStruct((t, 2 * SSM_BC), _BF16), jax.ShapeDtypeStruct((t, LANES), _F32)],
        scratch_shapes=[pltpu.VMEM((SSM_XBC // D_MODEL, HALO, D_MODEL), _F32)],
        compiler_params=_cparams("arbitrary"), name="ssm_in_proj",
    )(x, w_main, w_dt, conv_w, conv_b.reshape(1, -1), dtb)


def _ssd_kernel(gz_ref, xs_ref, bc_ref, dt_ref, a_ref, dsk_ref, nw_ref, o_ref, state_ref, y_ref):
    c = pl.program_id(1)
    q = CHUNK

    @pl.when(c == 0)
    def _():
        state_ref[...] = jnp.zeros_like(state_ref)

    dt = dt_ref[...]
    da = dt * a_ref[...]
    r_io = lax.broadcasted_iota(jnp.int32, (q, q), 0)
    c_io = lax.broadcasted_iota(jnp.int32, (q, q), 1)
    causal = r_io >= c_io
    tri = causal.astype(_F32)
    a_cum = jnp.dot(tri, da, preferred_element_type=_F32, precision=lax.Precision.HIGHEST)
    a_cum_t = a_cum.T
    dt_t = dt.T
    a_last = a_cum[q - 1:q, :]
    w_lanes = _per_head_lanes(jnp.exp(a_last - a_cum) * dt, SSM_P, SSM_INNER)
    e_lanes = _per_head_lanes(jnp.exp(a_cum), SSM_P, SSM_INNER)
    chunk_decay = jnp.exp(a_last)
    lane = lax.broadcasted_iota(jnp.int32, (q, LANES), 1)
    low = lane < SSM_P
    lane1 = lax.broadcasted_iota(jnp.int32, (1, LANES), 1)
    low1 = lane1 < SSM_P
    tn = (((0,), (0,)), ((), ()))
    nt = (((1,), (1,)), ((), ()))

    for g in range(SSM_GROUPS):
        bg = bc_ref[:, g * SSM_N:(g + 1) * SSM_N]
        cg = bc_ref[:, SSM_BC + g * SSM_N:SSM_BC + (g + 1) * SSM_N]
        cbm = lax.dot_general(cg, bg, nt, preferred_element_type=_F32)
        st = state_ref[g]
        y_off = jnp.dot(cg, st.astype(_BF16), preferred_element_type=_F32)
        xw_parts, dec_parts = [], []
        for j in range(SSM_GHEADS // 2):
            h0 = g * SSM_GHEADS + 2 * j
            col = (g * SSM_GHEADS // 2 + j) * LANES
            xp = xs_ref[:, col:col + LANES]
            y_diag = jnp.zeros((q, LANES), _F32)
            for half, hh in enumerate((h0, h0 + 1)):
                diff = a_cum[:, hh:hh + 1] - a_cum_t[hh:hh + 1, :]
                lmat = jnp.exp(jnp.where(causal, diff, NEG_BIG))
                mm = (cbm * lmat * dt_t[hh:hh + 1, :]).astype(_BF16)
                keep = low if half == 0 else jnp.logical_not(low)
                xh = jnp.where(keep, xp, 0.0).astype(_BF16)
                y_diag = y_diag + jnp.dot(mm, xh, preferred_element_type=_F32)
            y_ref[:, col:col + LANES] = (y_diag + y_off[:, j * LANES:(j + 1) * LANES] * e_lanes[:, col:col + LANES]
                                         + xp * dsk_ref[:, col:col + LANES])
            xw_parts.append((xp * w_lanes[:, col:col + LANES]).astype(_BF16))
            dec_parts.append(jnp.where(low1, chunk_decay[:, h0:h0 + 1], chunk_decay[:, h0 + 1:h0 + 2]))
        xw = jnp.concatenate(xw_parts, axis=1)
        dec = jnp.concatenate(dec_parts, axis=1)
        state_ref[g] = st * dec + lax.dot_general(bg, xw, tn, preferred_element_type=_F32)

    yg = y_ref[...] * gz_ref[...]
    gw = SSM_INNER // SSM_GROUPS
    for g in range(SSM_GROUPS):
        v = yg[:, g * gw:(g + 1) * gw]
        ms = jnp.mean(v * v, axis=-1, keepdims=True)
        o_ref[:, g * gw:(g + 1) * gw] = (v * lax.rsqrt(ms + RMS_EPS)
                                         * nw_ref[:, g * gw:(g + 1) * gw]).astype(o_ref.dtype)


def ssd_mixer(gz, xs, bc, dt, a_log, d_skip, norm_w, bsz, seq):
    nc = seq // CHUNK
    a = jnp.pad(-jnp.exp(a_log.astype(_F32)), (0, LANES - SSM_HEADS)).reshape(1, LANES)
    dsk = jnp.repeat(d_skip.astype(_F32), SSM_P).reshape(1, SSM_INNER)
    cst = lambda shape: pl.BlockSpec(shape, lambda b, c: (0,) * len(shape))
    row = lambda w: pl.BlockSpec((CHUNK, w), lambda b, c: (b * nc + c, 0))
    return pl.pallas_call(
        _ssd_kernel,
        grid=(bsz, nc),
        in_specs=[row(SSM_INNER), row(SSM_INNER), row(2 * SSM_BC), row(LANES),
                  cst((1, LANES)), cst((1, SSM_INNER)), cst((1, SSM_INNER))],
        out_specs=row(SSM_INNER),
        out_shape=jax.ShapeDtypeStruct((bsz * seq, SSM_INNER), _BF16),
        scratch_shapes=[pltpu.VMEM((SSM_GROUPS, SSM_N, SSM_GHEADS * SSM_P), _F32),
                        pltpu.VMEM((CHUNK, SSM_INNER), _F32)],
        compiler_params=_cparams("parallel", "arbitrary"), name="ssd_mixer",
    )(gz, xs, bc, dt, a, dsk, norm_w.reshape(1, -1))


ROUTE_SUB = 256
ROUTE_TM_ATTN = 1024
ROUTE_TM_SSM = 1024


def _layer_norm(y, g, b):
    mu = jnp.mean(y, axis=-1, keepdims=True)
    yc = y - mu
    var = jnp.mean(yc * yc, axis=-1, keepdims=True)
    return yc * lax.rsqrt(var + LN_EPS) * g + b


def _route(h1s, wrt_ref, br_ref, eidx_ref, gate_ref, tcnt_ref):
    sub = ROUTE_SUB
    n = len(h1s)
    nt = (((1,), (1,)), ((), ()))
    wrt = wrt_ref[...]
    vals = [lax.dot_general(wrt, h, nt, preferred_element_type=_F32, precision=lax.Precision.HIGHEST)
            + br_ref[...] for h in h1s]
    eio = lax.broadcasted_iota(jnp.int32, (N_EXPERTS, sub), 0)
    tops, ids, hots = ([[] for _ in range(n)] for _ in range(3))
    for _ in range(TOP_K):
        for a in range(n):
            mk = jnp.max(vals[a], axis=0, keepdims=True)
            ik = jnp.min(jnp.where(vals[a] == mk, eio, N_EXPERTS), axis=0, keepdims=True)
            hot = eio == ik
            vals[a] = jnp.where(hot, -jnp.inf, vals[a])
            tops[a].append(mk)
            ids[a].append(ik)
            hots[a].append(hot)
    col = lax.broadcasted_iota(jnp.int32, tcnt_ref.shape, 1)
    tcnt = tcnt_ref[...]
    for a in range(n):
        sl = slice(a * sub, (a + 1) * sub)
        exps = [jnp.exp(t - tops[a][0]) for t in tops[a]]
        den = exps[0] + exps[1] + exps[2] + exps[3]
        for k in range(TOP_K):
            eidx_ref[k:k + 1, sl] = ids[a][k]
            gate_ref[k:k + 1, sl] = exps[k] / den
        hot_all = hots[a][0] | hots[a][1] | hots[a][2] | hots[a][3]
        counts = jnp.sum(hot_all.astype(_F32), axis=1, keepdims=True)
        tcnt = jnp.where(col == pl.program_id(0) * n + a, counts, tcnt)
    tcnt_ref[...] = tcnt


def _norm_and_route(mixes, x_ref, g_ref, b_ref, wrt_ref, br_ref, h_ref, eidx_ref, gate_ref, tcnt_ref):
    sub = ROUTE_SUB

    @pl.when(pl.program_id(0) == 0)
    def _():
        tcnt_ref[...] = jnp.zeros_like(tcnt_ref)

    h1s = [_layer_norm(DEEPNORM_ALPHA * x_ref[a * sub:(a + 1) * sub, :] + mix, g_ref[...], b_ref[...])
           for a, mix in enumerate(mixes)]
    for a, h1 in enumerate(h1s):
        h_ref[a * sub:(a + 1) * sub, :] = h1
    _route(h1s, wrt_ref, br_ref, eidx_ref, gate_ref, tcnt_ref)


def _attn_out_kernel(o0_ref, o1_ref, o2_ref, l0_ref, l1_ref, l2_ref, x_ref, wo_ref, g_ref, b_ref,
                     wrt_ref, br_ref, h_ref, eidx_ref, gate_ref, tcnt_ref, ob_ref):
    sub = ROUTE_SUB
    l0, l1, l2 = l0_ref[...], l1_ref[...], l2_ref[...]
    m = jnp.maximum(jnp.maximum(l0, l1), l2)
    e0, e1, e2 = jnp.exp(l0 - m), jnp.exp(l1 - m), jnp.exp(l2 - m)
    inv = 1.0 / (e0 + e1 + e2)
    w0, w1, w2 = e0 * inv, e1 * inv, e2 * inv
    for h in range(HEADS):
        o = w0[:, h:h + 1] * o0_ref[h] + w1[:, h:h + 1] * o1_ref[h] + w2[:, h:h + 1] * o2_ref[h]
        ob_ref[:, h * HEAD_DIM:(h + 1) * HEAD_DIM] = o.astype(_BF16)
    mixes = [jnp.dot(ob_ref[a * sub:(a + 1) * sub, :], wo_ref[...], preferred_element_type=_F32)
             for a in range(x_ref.shape[0] // sub)]
    _norm_and_route(mixes, x_ref, g_ref, b_ref, wrt_ref, br_ref, h_ref, eidx_ref, gate_ref, tcnt_ref)


def _ssm_out_kernel(y_ref, x_ref, wo_ref, g_ref, b_ref, wrt_ref, br_ref, h_ref, eidx_ref, gate_ref, tcnt_ref):
    sub = ROUTE_SUB
    mixes = [jnp.dot(y_ref[a * sub:(a + 1) * sub, :], wo_ref[...], preferred_element_type=_F32)
             for a in range(x_ref.shape[0] // sub)]
    _norm_and_route(mixes, x_ref, g_ref, b_ref, wrt_ref, br_ref, h_ref, eidx_ref, gate_ref, tcnt_ref)


def _route_out_specs(t, tm):
    nsub = t // ROUTE_SUB
    specs = [pl.BlockSpec((tm, D_MODEL), lambda i: (i, 0)),
             pl.BlockSpec((TOP_K, tm), lambda i: (0, i)),
             pl.BlockSpec((TOP_K, tm), lambda i: (0, i)),
             pl.BlockSpec((N_EXPERTS, nsub), lambda i: (0, 0))]
    shapes = [jax.ShapeDtypeStruct((t, D_MODEL), _F32),
              jax.ShapeDtypeStruct((TOP_K, t), jnp.int32),
              jax.ShapeDtypeStruct((TOP_K, t), _F32),
              jax.ShapeDtypeStruct((N_EXPERTS, nsub), _F32)]
    return specs, shapes


def _const_spec(shape):
    return pl.BlockSpec(shape, lambda i: (0,) * len(shape))


def attn_out_norm_route(os_, lses, x, wo_bf16, ln_g, ln_b, w_router, b_router):
    t = x.shape[0]
    tm = ROUTE_TM_ATTN
    row = lambda w: pl.BlockSpec((tm, w), lambda i: (i, 0))
    out_specs, out_shapes = _route_out_specs(t, tm)
    return pl.pallas_call(
        _attn_out_kernel,
        grid=(t // tm,),
        in_specs=[pl.BlockSpec((HEADS, tm, HEAD_DIM), lambda i: (0, i, 0))] * 3 + [row(LANES)] * 3 + [row(D_MODEL),
                  _const_spec((D_MODEL, D_MODEL)), _const_spec((1, D_MODEL)), _const_spec((1, D_MODEL)),
                  _const_spec((N_EXPERTS, D_MODEL)), _const_spec((N_EXPERTS, 1))],
        out_specs=out_specs, out_shape=out_shapes,
        scratch_shapes=[pltpu.VMEM((tm, D_MODEL), _BF16)],
        compiler_params=_cparams("arbitrary"), name="attn_out_norm_route",
    )(*os_, *lses, x, wo_bf16, ln_g.reshape(1, -1), ln_b.reshape(1, -1),
      w_router.T, b_router.reshape(-1, 1))


def ssm_out_norm_route(yn, x, wo_bf16, ln_g, ln_b, w_router, b_router):
    t = x.shape[0]
    tm = ROUTE_TM_SSM
    row = lambda w: pl.BlockSpec((tm, w), lambda i: (i, 0))
    out_specs, out_shapes = _route_out_specs(t, tm)
    return pl.pallas_call(
        _ssm_out_kernel,
        grid=(t // tm,),
        in_specs=[row(SSM_INNER), row(D_MODEL),
                  _const_spec((SSM_INNER, D_MODEL)), _const_spec((1, D_MODEL)), _const_spec((1, D_MODEL)),
                  _const_spec((N_EXPERTS, D_MODEL)), _const_spec((N_EXPERTS, 1))],
        out_specs=out_specs, out_shape=out_shapes,
        compiler_params=_cparams("arbitrary"), name="ssm_out_norm_route",
    )(yn, x, wo_bf16, ln_g.reshape(1, -1), ln_b.reshape(1, -1), w_router.T, b_router.reshape(-1, 1))


MOE_BM = 1024
MOE_SUB_BM = 256
MOE_SUB = ROUTE_SUB
SEG_ALIGN = SUBLANES
LOCAL_ROWS = MOE_SUB * TOP_K + N_EXPERTS * SUBLANES
SEG_UNIT = 16
SEG_REST = (8,)
TAIL_SIZES = tuple(SEG_ALIGN << k for k in reversed(range((MOE_BM // SEG_ALIGN).bit_length() - 1)))
PAIR_COLS = D_MODEL // 2
_U32 = jnp.uint32


def _pack_pairs(x):
    lo = lax.shift_right_logical(lax.bitcast_convert_type(x[:, :PAIR_COLS], _U32), jnp.uint32(16))
    hi = lax.bitcast_convert_type(x[:, PAIR_COLS:], _U32) & jnp.uint32(0xFFFF0000)
    return hi | lo


def _unpack_pairs(u):
    lo = lax.bitcast_convert_type(lax.shift_left(u, jnp.uint32(16)), _F32).astype(_BF16)
    hi = lax.bitcast_convert_type(u & jnp.uint32(0xFFFF0000), _F32).astype(_BF16)
    return lo, hi


def _local_positions(eidx):
    sub = eidx.shape[1]
    eio = lax.broadcasted_iota(jnp.int32, (N_EXPERTS, sub), 0)
    hots = [eio == eidx[k:k + 1, :] for k in range(TOP_K)]
    hot_all = (hots[0] | hots[1] | hots[2] | hots[3]).astype(_BF16)
    r_io = lax.broadcasted_iota(jnp.int32, (sub, sub), 0)
    c_io = lax.broadcasted_iota(jnp.int32, (sub, sub), 1)
    upper = (r_io < c_io).astype(_BF16)
    prefix = jnp.dot(hot_all, upper, preferred_element_type=_F32)
    counts = jnp.sum(hot_all.astype(_F32), axis=1, keepdims=True).astype(jnp.int32)
    padded = ((counts + (SEG_ALIGN - 1)) & (-SEG_ALIGN)).astype(_F32)
    e_r = lax.broadcasted_iota(jnp.int32, (N_EXPERTS, N_EXPERTS), 0)
    e_c = lax.broadcasted_iota(jnp.int32, (N_EXPERTS, N_EXPERTS), 1)
    below = (e_c < e_r).astype(_BF16)
    seg_start = jnp.dot(below, jnp.broadcast_to(padded, (N_EXPERTS, sub)).astype(_BF16),
                        preferred_element_type=_F32)
    base = seg_start + prefix
    return [jnp.sum(jnp.where(h, base, 0.0), axis=0, keepdims=True) for h in hots]


def _segment_copies(cnt_ref, lo_ref, go_ref, i, copy_fn):
    for e in range(N_EXPERTS):
        n = cnt_ref[i * N_EXPERTS + e]
        lo = lo_ref[i * N_EXPERTS + e]
        go = go_ref[i * N_EXPERTS + e]

        def unit(u, c, lo=lo, go=go):
            copy_fn(pl.multiple_of(lo + u * SEG_UNIT, SEG_ALIGN), pl.multiple_of(go + u * SEG_UNIT, SEG_ALIGN),
                    SEG_UNIT)
            return c

        lax.fori_loop(0, n // SEG_UNIT, unit, 0)
        for sz in SEG_REST:
            done = n & -(2 * sz)

            @pl.when((n & sz) != 0)
            def _(done=done, sz=sz, lo=lo, go=go):
                copy_fn(pl.multiple_of(lo + done, SEG_ALIGN), pl.multiple_of(go + done, SEG_ALIGN), sz)


def _dispatch_kernel(cnt_ref, lo_ref, go_ref, units_ref, tail_ref, eidx_ref, h_ref, xs_ref, lpos_ref,
                     buf_ref, sem):
    i = pl.program_id(0)
    slot = i % 2
    mine, other = buf_ref.at[slot], buf_ref.at[1 - slot]
    lpos = _local_positions(eidx_ref[...])
    for k in range(TOP_K):
        lpos_ref[k:k + 1, :] = lpos[k]
    p_io = lax.broadcasted_iota(jnp.int32, (LOCAL_ROWS, MOE_SUB), 0).astype(_F32)
    sel = (p_io == lpos[0]) | (p_io == lpos[1]) | (p_io == lpos[2]) | (p_io == lpos[3])
    mine[...] = _pack_pairs(jnp.dot(sel.astype(_BF16), h_ref[...].astype(_BF16), preferred_element_type=_F32))

    def copy(lo, go, sz):
        pltpu.make_async_copy(mine.at[pl.ds(lo, sz)], xs_ref.at[pl.ds(go, sz)], sem.at[slot]).start()

    _segment_copies(cnt_ref, lo_ref, go_ref, i, copy)

    def wait_units(buf, s, n_units):
        def body(j, c):
            pltpu.make_async_copy(buf.at[pl.ds(0, SEG_ALIGN)], xs_ref.at[pl.ds(0, SEG_ALIGN)], sem.at[s]).wait()
            return c
        lax.fori_loop(0, n_units, body, 0)

    @pl.when(i > 0)
    def _():
        wait_units(other, 1 - slot, units_ref[jnp.maximum(i - 1, 0)])

    @pl.when(i == pl.num_programs(0) - 1)
    def _():
        wait_units(mine, slot, units_ref[i])
        mine[0:MOE_BM, :] = jnp.zeros((MOE_BM, PAIR_COLS), _U32)
        start, n, n_blocks = tail_ref[0], tail_ref[1], tail_ref[2]
        done = jnp.int32(0)
        for sz in TAIL_SIZES:
            @pl.when((n & sz) != 0)
            def _(done=done, sz=sz):
                pltpu.make_async_copy(mine.at[pl.ds(0, sz)],
                                      xs_ref.at[pl.ds(pl.multiple_of(start + done, SEG_ALIGN), sz)],
                                      sem.at[slot]).start()
            done = done + (n & sz)

        def zero_block(j, c):
            row0 = pl.multiple_of(start + n + j * MOE_BM, MOE_BM)
            pltpu.make_async_copy(mine.at[pl.ds(0, MOE_BM)], xs_ref.at[pl.ds(row0, MOE_BM)], sem.at[slot]).start()
            return c

        lax.fori_loop(0, n_blocks, zero_block, 0)
        wait_units(mine, slot, n // SEG_ALIGN + n_blocks * (MOE_BM // SEG_ALIGN))


def _seg_tables(tile_counts):
    cnt = tile_counts.T.astype(jnp.int32)
    pc = (cnt + (SEG_ALIGN - 1)) & (-SEG_ALIGN)
    lo = jnp.cumsum(pc, axis=1) - pc
    tot = jnp.sum(pc, axis=0)
    offs = jnp.concatenate([jnp.zeros((1,), jnp.int32), jnp.cumsum(tot)])
    go = offs[None, :-1] + jnp.cumsum(pc, axis=0) - pc
    units = jnp.sum(pc, axis=1) // SEG_ALIGN
    return pc.reshape(-1), lo.reshape(-1), go.reshape(-1), units.astype(jnp.int32), offs.astype(jnp.int32)


def _max_rows(t):
    n = t * TOP_K + (t // MOE_SUB) * N_EXPERTS * (SEG_ALIGN - 1)
    return -(-n // MOE_BM) * MOE_BM


def moe_dispatch(h, eidx, tables):
    t = h.shape[0]
    pc, lo, go, units, offs = tables
    used = offs[-1]
    partial = (-used) % MOE_BM
    tail = jnp.stack([used, partial, (_max_rows(t) - used - partial) // MOE_BM]).astype(jnp.int32)
    grid_spec = pltpu.PrefetchScalarGridSpec(
        num_scalar_prefetch=5, grid=(t // MOE_SUB,),
        in_specs=[pl.BlockSpec((TOP_K, MOE_SUB), lambda i, *_: (0, i)),
                  pl.BlockSpec((MOE_SUB, D_MODEL), lambda i, *_: (i, 0))],
        out_specs=[pl.BlockSpec(memory_space=pl.ANY), pl.BlockSpec((TOP_K, MOE_SUB), lambda i, *_: (0, i))],
        scratch_shapes=[pltpu.VMEM((2, LOCAL_ROWS, PAIR_COLS), _U32), pltpu.SemaphoreType.DMA((2,))])
    return pl.pallas_call(
        _dispatch_kernel, grid_spec=grid_spec,
        out_shape=[jax.ShapeDtypeStruct((_max_rows(t), PAIR_COLS), _U32),
                   jax.ShapeDtypeStruct((TOP_K, t), _F32)],
        compiler_params=_cparams("arbitrary"), name="moe_dispatch",
    )(pc, lo, go, units, tail, eidx, h)


def _combine_kernel(cnt_ref, lo_ref, go_ref, units_ref, lpos_ref, gate_ref, ys_ref, h_ref, g_ref, b_ref,
                    o_ref, buf_ref, sem):
    i = pl.program_id(0)
    slot = i % 2

    def fetch(step, s):
        dst = buf_ref.at[s]
        dst[MOE_SUB * TOP_K:, :] = jnp.zeros((LOCAL_ROWS - MOE_SUB * TOP_K, PAIR_COLS), _U32)

        def copy(lo, go, sz):
            pltpu.make_async_copy(ys_ref.at[pl.ds(go, sz)], dst.at[pl.ds(lo, sz)], sem.at[s]).start()

        _segment_copies(cnt_ref, lo_ref, go_ref, step, copy)

    @pl.when(i == 0)
    def _():
        fetch(i, slot)

    @pl.when(i + 1 < pl.num_programs(0))
    def _():
        fetch(i + 1, 1 - slot)

    mine = buf_ref.at[slot]
    lpos = lpos_ref[...]
    gates = gate_ref[...]
    p_io = lax.broadcasted_iota(jnp.int32, (LOCAL_ROWS, MOE_SUB), 0).astype(_F32)
    w = jnp.zeros((LOCAL_ROWS, MOE_SUB), _F32)
    for k in range(TOP_K):
        w = w + jnp.where(p_io == lpos[k:k + 1, :], gates[k:k + 1, :], 0.0)
    wb = w.astype(_BF16)

    def wait_unit(j, c):
        pltpu.make_async_copy(ys_ref.at[pl.ds(0, SEG_ALIGN)], mine.at[pl.ds(0, SEG_ALIGN)], sem.at[slot]).wait()
        return c

    lax.fori_loop(0, units_ref[i], wait_unit, 0)
    tn = (((0,), (0,)), ((), ()))
    ffn = jnp.concatenate([lax.dot_general(wb, yb, tn, preferred_element_type=_F32)
                           for yb in _unpack_pairs(mine[...])], axis=1)
    o_ref[...] = _layer_norm(DEEPNORM_ALPHA * h_ref[...] + ffn, g_ref[...], b_ref[...])


def moe_combine(ys, lpos, gates, h, tables, ln_g, ln_b):
    t = h.shape[0]
    pc, lo, go, units, _ = tables
    row = pl.BlockSpec((MOE_SUB, D_MODEL), lambda i, *_: (i, 0))
    kt = pl.BlockSpec((TOP_K, MOE_SUB), lambda i, *_: (0, i))
    cst = lambda shape: pl.BlockSpec(shape, lambda i, *_: (0,) * len(shape))
    grid_spec = pltpu.PrefetchScalarGridSpec(
        num_scalar_prefetch=4, grid=(t // MOE_SUB,),
        in_specs=[kt, kt, pl.BlockSpec(memory_space=pl.ANY), row, cst((1, D_MODEL)), cst((1, D_MODEL))],
        out_specs=row,
        scratch_shapes=[pltpu.VMEM((2, LOCAL_ROWS, PAIR_COLS), _U32), pltpu.SemaphoreType.DMA((2,))])
    return pl.pallas_call(
        _combine_kernel, grid_spec=grid_spec,
        out_shape=jax.ShapeDtypeStruct((t, D_MODEL), _F32),
        compiler_params=_cparams("arbitrary"), name="moe_combine",
    )(pc, lo, go, units, lpos, gates, ys, h, ln_g.reshape(1, -1), ln_b.reshape(1, -1))


def _expert_kernel(blk_ref, e_ref, lo_ref, hi_ref, xs_ref, wu_ref, bu_ref, wd_ref, bd_ref,
                   ys_ref, wub_ref, wdb_ref):
    i = pl.program_id(0)
    e = e_ref[i]
    e_prev = e_ref[jnp.maximum(i - 1, 0)]

    @pl.when((i == 0) | (e != e_prev))
    def _():
        wub_ref[...] = wu_ref[...].astype(_BF16)
        wdb_ref[...] = wd_ref[...].astype(_BF16)

    lo = lo_ref[i]
    hi = hi_ref[i]

    @pl.when((lo == 0) | (hi < 0))
    def _():
        ys_ref[...] = jnp.zeros_like(ys_ref)

    for sb in range(xs_ref.shape[0] // MOE_SUB_BM):
        r0 = sb * MOE_SUB_BM
        rows = slice(r0, r0 + MOE_SUB_BM)

        @pl.when((hi > lo) & (hi > r0) & (lo < r0 + MOE_SUB_BM))
        def _(r0=r0, rows=rows):
            x_lo, x_hi = _unpack_pairs(xs_ref[rows, :])
            h = (jnp.dot(x_lo, wub_ref[:PAIR_COLS, :], preferred_element_type=_F32)
                 + jnp.dot(x_hi, wub_ref[PAIR_COLS:, :], preferred_element_type=_F32) + bu_ref[...])
            gate = jnp.minimum(h[:, :D_FF], SWIGLU_LIMIT)
            up = jnp.clip(h[:, D_FF:], -SWIGLU_LIMIT, SWIGLU_LIMIT)
            glu = gate * jax.nn.sigmoid(SWIGLU_ALPHA * gate)
            act = ((up + 1.0) * glu).astype(_BF16)
            y = jnp.dot(act, wdb_ref[...], preferred_element_type=_F32) + bd_ref[...]
            packed = _pack_pairs(y.astype(_BF16).astype(_F32))
            row = r0 + lax.broadcasted_iota(jnp.int32, packed.shape, 0)
            ys_ref[rows, :] = jnp.where((row >= lo) & (row < hi), packed, ys_ref[rows, :])


def _expert_schedule(offs, n_rows, bm):
    nblk = n_rows // bm
    n_items = nblk + N_EXPERTS - 1
    used = offs[-1]
    bnd = offs[1:-1]
    pos = jnp.arange(N_EXPERTS - 1, dtype=jnp.int32) + jnp.minimum(bnd // bm + 1, nblk)
    i = jnp.arange(n_items, dtype=jnp.int32)[:, None]
    at = pos[None, :] == i
    is_bnd = jnp.any(at, axis=1)
    bnd_val = jnp.sum(jnp.where(at, bnd[None, :], 0), axis=1)
    n_before = jnp.sum((pos[None, :] < i).astype(jnp.int32), axis=1)
    cuts = jnp.where(is_bnd, bnd_val, (i[:, 0] - n_before) * bm)
    ends = jnp.concatenate([cuts[1:], jnp.array([n_rows], jnp.int32)])
    last_blk = jnp.maximum(used - 1, 0) // bm
    past = jnp.logical_not(is_bnd) & (cuts // bm > last_blk)
    raw_blk = cuts // bm
    cuts = jnp.minimum(cuts, used)
    ends = jnp.minimum(ends, used)
    blk = jnp.minimum(cuts // bm, last_blk)
    e = jnp.minimum(jnp.sum((bnd[None, :] <= cuts[:, None]).astype(jnp.int32), axis=1), N_EXPERTS - 1)
    lo = jnp.where(past, 0, cuts - blk * bm)
    hi = jnp.where(past, -1, jnp.minimum(ends - blk * bm, bm))
    return lax.cummax(jnp.where(past, raw_blk, blk)), e, lo, hi


def moe_experts(xs, offs, layer, w_up, b_up, w_down, b_down):
    n_rows = xs.shape[0]
    bm = MOE_BM
    blk, e, lo, hi = _expert_schedule(offs, n_rows, bm)
    n_items = n_rows // bm + N_EXPERTS - 1
    wmap = lambda i, b, e, lo, hi: (layer, e[i], 0, 0)
    grid_spec = pltpu.PrefetchScalarGridSpec(
        num_scalar_prefetch=4, grid=(n_items,),
        in_specs=[pl.BlockSpec((bm, PAIR_COLS), lambda i, b, e, lo, hi: (b[i], 0)),
                  pl.BlockSpec((None, None, D_MODEL, 2 * D_FF), wmap),
                  pl.BlockSpec((None, None, 1, 2 * D_FF), wmap),
                  pl.BlockSpec((None, None, D_FF, D_MODEL), wmap),
                  pl.BlockSpec((None, None, 1, D_MODEL), wmap)],
        out_specs=pl.BlockSpec((bm, PAIR_COLS), lambda i, b, e, lo, hi: (b[i], 0)),
        scratch_shapes=[pltpu.VMEM((D_MODEL, 2 * D_FF), _BF16), pltpu.VMEM((D_FF, D_MODEL), _BF16)])
    nl = w_up.shape[0]
    return pl.pallas_call(
        _expert_kernel, grid_spec=grid_spec,
        out_shape=jax.ShapeDtypeStruct((n_rows, PAIR_COLS), _U32),
        compiler_params=_cparams("arbitrary"), name="moe_experts",
    )(blk, e, lo, hi, xs, w_up, b_up.reshape(nl, N_EXPERTS, 1, -1), w_down, b_down.reshape(nl, N_EXPERTS, 1, -1))


def moe_layer(h, eidx, gates, tile_counts, layer, w_up, b_up, w_down, b_down, ln_g, ln_b):
    tables = _seg_tables(tile_counts)
    xs, lpos = moe_dispatch(h, eidx, tables)
    ys = moe_experts(xs, tables[4], layer, w_up, b_up, w_down, b_down)
    return moe_combine(ys, lpos, gates, h, tables, ln_g, ln_b)


def kernel(x, positions, attn_w_in, attn_w_out, ssm_w_in, ssm_conv_w, ssm_conv_b, ssm_dt_bias,
           ssm_a_log, ssm_d, ssm_norm_w, ssm_w_out, moe_w_router, moe_b_router, moe_w_up,
           moe_b_up, moe_w_down, moe_b_down, ln_mix_g, ln_mix_b, ln_ffn_g, ln_ffn_b):
    bsz, seq, _ = x.shape
    xf = x.reshape(bsz * seq, D_MODEL)

    w_qkv = attn_w_in[0].astype(_BF16)
    outs = [dilated_group_attention(qkv_proj_group(xf, w_qkv, positions, g), g, bsz, seq)
            for g in range(N_GROUPS)]
    h, eidx, gates, tile_counts = attn_out_norm_route(
        [o for o, _ in outs], [l for _, l in outs], xf, attn_w_out[0].astype(_BF16),
        ln_mix_g[0], ln_mix_b[0], moe_w_router[0], moe_b_router[0])
    xf = moe_layer(h, eidx, gates, tile_counts, 0, moe_w_up, moe_b_up, moe_w_down, moe_b_down,
                   ln_ffn_g[0], ln_ffn_b[0])

    gz, xs, bc, dt = ssm_in_proj(xf, ssm_w_in[0], ssm_conv_w[0], ssm_conv_b[0], ssm_dt_bias[0], seq)
    yn = ssd_mixer(gz, xs, bc, dt, ssm_a_log[0], ssm_d[0], ssm_norm_w[0], bsz, seq)
    h, eidx, gates, tile_counts = ssm_out_norm_route(
        yn, xf, ssm_w_out[0].astype(_BF16), ln_mix_g[1], ln_mix_b[1], moe_w_router[1], moe_b_router[1])
    xf = moe_layer(h, eidx, gates, tile_counts, 1, moe_w_up, moe_b_up, moe_w_down, moe_b_down,
                   ln_ffn_g[1], ln_ffn_b[1])
    return xf.reshape(bsz, seq, D_MODEL)
```

```python
import functools
import math

import jax
import jax.numpy as jnp
from jax import lax
from jax.experimental import pallas as pl
from jax.experimental.pallas import tpu as pltpu

D_MODEL = 1024
DEPTH = 2
DIL = (1, 4, 16)
N_GROUPS = 3
SUB_WIN = 128
HEAD_DIM = 128
HEADS = 8
ROT_DIM = 32
ROT_HALF = 16
ROPE_THETA = 500000.0
SSM_INNER = 2048
SSM_P = 64
SSM_HEADS = 32
SSM_GROUPS = 4
SSM_GHEADS = 8
SSM_N = 128
SSM_CONV = 4
CHUNK = 128
SSM_BC = SSM_GROUPS * SSM_N
SSM_XBC = SSM_INNER + 2 * SSM_BC
N_EXPERTS = 32
TOP_K = 4
D_FF = 1024
SWIGLU_LIMIT = 7.0
SWIGLU_ALPHA = 1.702
DEEPNORM_ALPHA = (2 * DEPTH) ** 0.25
LN_EPS = 1e-5
RMS_EPS = 1e-5

LANES = 128
SUBLANES = 8
VMEM_LIMIT_BYTES = 56 * 1024 * 1024

NEG_BIG = -1e30

_F32 = jnp.float32
_BF16 = jnp.bfloat16


def _cparams(*sem):
    return pltpu.CompilerParams(dimension_semantics=sem, vmem_limit_bytes=VMEM_LIMIT_BYTES)


def _per_head_lanes(m, lanes_per_head, width):
    head_of_lane = lax.broadcasted_iota(jnp.int32, (LANES, width), 1) // lanes_per_head
    spread = (head_of_lane == lax.broadcasted_iota(jnp.int32, (LANES, width), 0)).astype(_BF16)
    hi = m.astype(_BF16)
    lo = (m - hi.astype(_F32)).astype(_BF16)
    return jnp.dot(hi, spread, preferred_element_type=_F32) + jnp.dot(lo, spread, preferred_element_type=_F32)


ROPE_PACK = LANES // ROT_DIM


def _rope_table_kernel(pos_ref, invf_ref, c_ref, sa_ref, sb_ref):
    pos = pos_ref[...].astype(_F32)
    lane = lax.broadcasted_iota(jnp.int32, (pos.shape[0], LANES), 1)
    p = jnp.zeros((pos.shape[0], LANES), _F32)
    group = lax.shift_right_logical(lane, ROT_DIM.bit_length() - 1)
    for q in range(ROPE_PACK):
        p = jnp.where(group == q, pos[:, q:q + 1], p)
    ang = p * invf_ref[...]
    cos = jnp.cos(ang)
    sin = jnp.sin(ang)
    for q in range(ROPE_PACK):
        shift = (LANES - q * ROT_DIM) % LANES
        cq = pltpu.roll(cos, shift, 1) if shift else cos
        sq = pltpu.roll(sin, shift, 1) if shift else sin
        c_ref[q] = jnp.where(lane < ROT_DIM, cq, 1.0)
        sa_ref[q] = jnp.where((lane >= ROT_HALF) & (lane < ROT_DIM), sq, 0.0)
        sb_ref[q] = jnp.where(lane < ROT_HALF, -sq, 0.0)


def rope_tables(positions):
    t = positions.size
    rows = t // ROPE_PACK
    tm = min(1024, rows)
    pos = positions.reshape(ROPE_PACK, rows).T
    j = jnp.arange(LANES)
    invf = (ROPE_THETA ** (-(2.0 * (j % ROT_HALF)).astype(_F32) / ROT_DIM)).astype(_F32).reshape(1, LANES)
    out = jax.ShapeDtypeStruct((ROPE_PACK, rows, LANES), _F32)
    tabs = pl.pallas_call(
        _rope_table_kernel,
        grid=(rows // tm,),
        in_specs=[pl.BlockSpec((tm, ROPE_PACK), lambda i: (i, 0)),
                  pl.BlockSpec((1, LANES), lambda i: (0, 0))],
        out_specs=[pl.BlockSpec((ROPE_PACK, tm, LANES), lambda i: (0, i, 0))] * 3,
        out_shape=[out, out, out],
        compiler_params=_cparams("parallel"),
        name="rope_tables",
    )(pos, invf)
    return [x.reshape(t, LANES) for x in tabs]


QKV_TM = 512
ATTN_ROWS = 2048
HEADS_PER_STEP = 4


def _qkv_proj_kernel(x_ref, w_ref, c_ref, sa_ref, sb_ref, o_ref, *, d, ns, tt):
    tm = x_ref.shape[0]
    xb = x_ref[...].astype(_BF16)
    if d > 1:
        dst = lax.broadcasted_iota(jnp.int32, (tm, tm), 0)
        src = lax.broadcasted_iota(jnp.int32, (tm, tm), 1)
        lt, ld = tt.bit_length() - 1, d.bit_length() - 1
        t_ = dst & (tt - 1)
        r_ = lax.shift_right_logical(dst, lt) & (d - 1)
        s_ = lax.shift_right_logical(dst, lt + ld)
        perm = (src == ((s_ * tt + t_) * d + r_)).astype(_BF16)
        xb = jnp.dot(perm, xb, preferred_element_type=_F32).astype(_BF16)
    c = c_ref[...]
    sa = sa_ref[...]
    sb = sb_ref[...]
    for comp in range(3):
        acc = jnp.dot(xb, w_ref[:, comp * D_MODEL:(comp + 1) * D_MODEL], preferred_element_type=_F32)
        for h in range(HEADS):
            sl = slice(h * HEAD_DIM, (h + 1) * HEAD_DIM)
            t = acc[:, sl]
            if comp < 2:
                t = t * c + pltpu.roll(t, ROT_HALF, 1) * sa + pltpu.roll(t, HEAD_DIM - ROT_HALF, 1) * sb
            o_ref[:, :, :, comp * D_MODEL + h * HEAD_DIM:comp * D_MODEL + (h + 1) * HEAD_DIM] = (
                t.reshape(ns, d, tt, HEAD_DIM).astype(o_ref.dtype))


def qkv_proj_group(x, w_bf16, positions, g):
    t, k = x.shape
    d = DIL[g]
    span = SUB_WIN * d
    tm = QKV_TM
    ns = max(1, tm // span)
    tps = max(1, span // tm)
    tt = tm // (ns * d)
    rope = rope_tables(positions.reshape(-1, ns, tt, d).swapaxes(2, 3).reshape(-1))
    tab_spec = pl.BlockSpec((tm, LANES), lambda i: (i, 0))
    return pl.pallas_call(
        functools.partial(_qkv_proj_kernel, d=d, ns=ns, tt=tt),
        grid=(t // tm,),
        in_specs=[pl.BlockSpec((tm, k), lambda i: (i, 0)),
                  pl.BlockSpec((k, 3 * D_MODEL), lambda i: (0, g)),
                  tab_spec, tab_spec, tab_spec],
        out_specs=pl.BlockSpec((ns, d, tt, 3 * D_MODEL), lambda i: (i // tps, 0, i % tps, 0)),
        out_shape=jax.ShapeDtypeStruct((t // span, d, SUB_WIN, 3 * D_MODEL), _BF16),
        compiler_params=_cparams("parallel"), name=f"qkv_proj_g{g}",
    )(x, w_bf16, *rope)


def _dil_attn_kernel(q_ref, k_ref, v_ref, kp_ref, vp_ref, o_ref, lse_ref, *, d, ns):
    j = pl.program_id(1)
    hp = pl.program_id(2)
    nb = ns * d
    row = lax.broadcasted_iota(jnp.int32, (1, SUB_WIN, SUB_WIN), 1)
    col = lax.broadcasted_iota(jnp.int32, (1, SUB_WIN, SUB_WIN), 2)
    cur_ok = col <= row
    blk = lax.broadcasted_iota(jnp.int32, (nb, 1, 1), 0)
    prev_ok = (col >= row) & ((blk >= d) | (j > 0))
    lane = lax.broadcasted_iota(jnp.int32, (SUB_WIN, LANES), 1)
    scale = 1.0 / math.sqrt(HEAD_DIM)

    def block_rows(b):
        s, r = divmod(b, d)
        start = s * SUB_WIN * d + r
        return pl.ds(start, SUB_WIN, stride=d) if d > 1 else pl.ds(start, SUB_WIN)

    @pl.when(hp == 0)
    def _():
        lse_ref[...] = jnp.zeros_like(lse_ref)

    lses = []
    for hh in range(HEADS_PER_STEP):
        sl = slice(hh * HEAD_DIM, (hh + 1) * HEAD_DIM)
        flat = lambda a: a.reshape(nb, SUB_WIN, HEAD_DIM)
        q, kc, vc = flat(q_ref[:, :, :, sl]), flat(k_ref[:, :, :, sl]), flat(v_ref[:, :, :, sl])
        if ns > 1:
            kp = flat(jnp.concatenate([kp_ref[:, :, :, sl], k_ref[:ns - 1, :, :, sl]], axis=0))
            vp = flat(jnp.concatenate([vp_ref[:, :, :, sl], v_ref[:ns - 1, :, :, sl]], axis=0))
        else:
            kp, vp = flat(kp_ref[:, :, :, sl]), flat(vp_ref[:, :, :, sl])
        sc = jnp.einsum('bqd,bkd->bqk', q, kc, preferred_element_type=_F32) * scale
        sp = jnp.einsum('bqd,bkd->bqk', q, kp, preferred_element_type=_F32) * scale
        sc = jnp.where(cur_ok, sc, NEG_BIG)
        sp = jnp.where(prev_ok, sp, NEG_BIG)
        m = jnp.maximum(jnp.max(sc, axis=2, keepdims=True), jnp.max(sp, axis=2, keepdims=True))
        pc = jnp.exp(sc - m)
        pp = jnp.exp(sp - m)
        l = jnp.sum(pc, axis=2, keepdims=True) + jnp.sum(pp, axis=2, keepdims=True)
        acc = jnp.einsum('bqk,bkd->bqd', pc.astype(_BF16), vc, preferred_element_type=_F32)
        acc += jnp.einsum('bqk,bkd->bqd', pp.astype(_BF16), vp, preferred_element_type=_F32)
        o = acc / l
        for b in range(nb):
            o_ref[hh, block_rows(b), :] = o[b]
        lses.append(m + jnp.log(l))
    for b in range(nb):
        cur = lse_ref[block_rows(b), :]
        for hh in range(HEADS_PER_STEP):
            cur = jnp.where(lane == hp * HEADS_PER_STEP + hh, lses[hh][b], cur)
        lse_ref[block_rows(b), :] = cur


def dilated_group_attention(qkv, g, bsz, seq):
    d = DIL[g]
    ns = ATTN_ROWS // (SUB_WIN * d)
    nsteps = seq // ATTN_ROWS
    hw = HEADS_PER_STEP * HEAD_DIM
    ncol = D_MODEL // hw
    cur = lambda comp: pl.BlockSpec((ns, d, SUB_WIN, hw), lambda b, j, hp: (b * nsteps + j, 0, 0, comp * ncol + hp))
    prev = lambda comp: pl.BlockSpec(
        (1, d, SUB_WIN, hw), lambda b, j, hp: (jnp.maximum((b * nsteps + j) * ns - 1, 0), 0, 0, comp * ncol + hp))
    return pl.pallas_call(
        functools.partial(_dil_attn_kernel, d=d, ns=ns),
        grid=(bsz, nsteps, HEADS // HEADS_PER_STEP),
        in_specs=[cur(0), cur(1), cur(2), prev(1), prev(2)],
        out_specs=[pl.BlockSpec((HEADS_PER_STEP, ATTN_ROWS, HEAD_DIM), lambda b, j, hp: (hp, b * nsteps + j, 0)),
                   pl.BlockSpec((ATTN_ROWS, LANES), lambda b, j, hp: (b * nsteps + j, 0))],
        out_shape=[jax.ShapeDtypeStruct((HEADS, bsz * seq, HEAD_DIM), _F32),
                   jax.ShapeDtypeStruct((bsz * seq, LANES), _F32)],
        compiler_params=_cparams("parallel", "parallel", "arbitrary"),
        name=f"dilated_attn_g{g}",
    )(qkv, qkv, qkv, qkv, qkv)


SSM_IN_TM = 512
HALO = SUBLANES
ZX_COLS = SSM_INNER + SSM_XBC


def _softplus(x):
    return jnp.maximum(x, 0.0) + jnp.log1p(jnp.exp(-jnp.abs(x)))


def _ssm_in_kernel(x_ref, w_ref, wdt_ref, cw_ref, cb_ref, dtb_ref, gz_ref, xs_ref, bc_ref, dt_ref,
                   carry_ref, *, tiles_per_seq):
    tm = x_ref.shape[0]
    cw = D_MODEL
    xb = x_ref[...].astype(_BF16)

    @pl.when(pl.program_id(0) % tiles_per_seq == 0)
    def _():
        carry_ref[...] = jnp.zeros_like(carry_ref)

    for i in range(SSM_INNER // cw):
        z = jnp.dot(xb, w_ref[:, i * cw:(i + 1) * cw], preferred_element_type=_F32)
        gz_ref[:, i * cw:(i + 1) * cw] = z * jax.nn.sigmoid(z)
    for c in range(SSM_XBC // cw):
        cols = slice(c * cw, (c + 1) * cw)
        acc = jnp.dot(xb, w_ref[:, SSM_INNER + c * cw:SSM_INNER + (c + 1) * cw], preferred_element_type=_F32)
        ext = jnp.concatenate([carry_ref[c], acc], axis=0)
        carry_ref[c] = acc[tm - HALO:, :]
        conv = cb_ref[:, cols] + cw_ref[SSM_CONV - 1:SSM_CONV, cols] * acc
        for k in range(1, SSM_CONV):
            conv = conv + cw_ref[SSM_CONV - 1 - k:SSM_CONV - k, cols] * pltpu.roll(ext, k, 0)[HALO:, :]
        act = conv * jax.nn.sigmoid(conv)
        if (c + 1) * cw <= SSM_INNER:
            xs_ref[:, cols] = act
        else:
            bc_ref[...] = act.astype(_BF16)
    dt_raw = jnp.dot(xb, wdt_ref[...], preferred_element_type=_F32)
    dt_ref[...] = _softplus(dt_raw + dtb_ref[...])


def ssm_in_proj(x, w_in, conv_w, conv_b, dt_bias, seq):
    t, k = x.shape
    tm = SSM_IN_TM
    assert 2 * SSM_BC == D_MODEL
    pad = LANES - SSM_HEADS
    w_main = w_in[:, :ZX_COLS].astype(_BF16)
    w_dt = jnp.pad(w_in[:, ZX_COLS:], ((0, 0), (0, pad))).astype(_BF16)
    dtb = jnp.pad(dt_bias, (0, pad)).reshape(1, LANES)
    cst = lambda shape: pl.BlockSpec(shape, lambda i: (0,) * len(shape))
    once = lambda shape: pl.BlockSpec(shape, lambda i: (0,) * len(shape), pipeline_mode=pl.Buffered(1))
    row = lambda w: pl.BlockSpec((tm, w), lambda i: (i, 0))
    return pl.pallas_call(
        functools.partial(_ssm_in_kernel, tiles_per_seq=seq // tm),
        grid=(t // tm,),
        in_specs=[row(k), once((k, ZX_COLS)), once((k, LANES)),
                  cst((SSM_CONV, SSM_XBC)), cst((1, SSM_XBC)), cst((1, LANES))],
        out_specs=[row(SSM_INNER), row(SSM_INNER), row(2 * SSM_BC), row(LANES)],
        out_shape=[jax.ShapeDtypeStruct((t, SSM_INNER), _F32), jax.ShapeDtypeStruct((t, SSM_INNER), _F32),
                   jax.ShapeDtypeStruct((t, 2 * SSM_BC), _BF16), jax.ShapeDtypeStruct((t, LANES), _F32)],
        scratch_shapes=[pltpu.VMEM((SSM_XBC // D_MODEL, HALO, D_MODEL), _F32)],
        compiler_params=_cparams("arbitrary"), name="ssm_in_proj",
    )(x, w_main, w_dt, conv_w, conv_b.reshape(1, -1), dtb)


def _ssd_kernel(gz_ref, xs_ref, bc_ref, dt_ref, a_ref, dsk_ref, nw_ref, o_ref, state_ref, y_ref):
    c = pl.program_id(1)
    q = CHUNK

    @pl.when(c == 0)
    def _():
        state_ref[...] = jnp.zeros_like(state_ref)

    dt = dt_ref[...]
    da = dt * a_ref[...]
    r_io = lax.broadcasted_iota(jnp.int32, (q, q), 0)
    c_io = lax.broadcasted_iota(jnp.int32, (q, q), 1)
    causal = r_io >= c_io
    tri = causal.astype(_F32)
    a_cum = jnp.dot(tri, da, preferred_element_type=_F32, precision=lax.Precision.HIGHEST)
    a_cum_t = a_cum.T
    dt_t = dt.T
    a_last = a_cum[q - 1:q, :]
    w_lanes = _per_head_lanes(jnp.exp(a_last - a_cum) * dt, SSM_P, SSM_INNER)
    e_lanes = _per_head_lanes(jnp.exp(a_cum), SSM_P, SSM_INNER)
    chunk_decay = jnp.exp(a_last)
    lane = lax.broadcasted_iota(jnp.int32, (q, LANES), 1)
    low = lane < SSM_P
    lane1 = lax.broadcasted_iota(jnp.int32, (1, LANES), 1)
    low1 = lane1 < SSM_P
    tn = (((0,), (0,)), ((), ()))
    nt = (((1,), (1,)), ((), ()))

    for g in range(SSM_GROUPS):
        bg = bc_ref[:, g * SSM_N:(g + 1) * SSM_N]
        cg = bc_ref[:, SSM_BC + g * SSM_N:SSM_BC + (g + 1) * SSM_N]
        cbm = lax.dot_general(cg, bg, nt, preferred_element_type=_F32)
        st = state_ref[g]
        y_off = jnp.dot(cg, st.astype(_BF16), preferred_element_type=_F32)
        xw_parts, dec_parts = [], []
        for j in range(SSM_GHEADS // 2):
            h0 = g * SSM_GHEADS + 2 * j
            col = (g * SSM_GHEADS // 2 + j) * LANES
            xp = xs_ref[:, col:col + LANES]
            y_diag = jnp.zeros((q, LANES), _F32)
            for half, hh in enumerate((h0, h0 + 1)):
                diff = a_cum[:, hh:hh + 1] - a_cum_t[hh:hh + 1, :]
                lmat = jnp.exp(jnp.where(causal, diff, NEG_BIG))
                mm = (cbm * lmat * dt_t[hh:hh + 1, :]).astype(_BF16)
                keep = low if half == 0 else jnp.logical_not(low)
                xh = jnp.where(keep, xp, 0.0).astype(_BF16)
                y_diag = y_diag + jnp.dot(mm, xh, preferred_element_type=_F32)
            y_ref[:, col:col + LANES] = (y_diag + y_off[:, j * LANES:(j + 1) * LANES] * e_lanes[:, col:col + LANES]
                                         + xp * dsk_ref[:, col:col + LANES])
            xw_parts.append((xp * w_lanes[:, col:col + LANES]).astype(_BF16))
            dec_parts.append(jnp.where(low1, chunk_decay[:, h0:h0 + 1], chunk_decay[:, h0 + 1:h0 + 2]))
        xw = jnp.concatenate(xw_parts, axis=1)
        dec = jnp.concatenate(dec_parts, axis=1)
        state_ref[g] = st * dec + lax.dot_general(bg, xw, tn, preferred_element_type=_F32)

    yg = y_ref[...] * gz_ref[...]
    gw = SSM_INNER // SSM_GROUPS
    for g in range(SSM_GROUPS):
        v = yg[:, g * gw:(g + 1) * gw]
        ms = jnp.mean(v * v, axis=-1, keepdims=True)
        o_ref[:, g * gw:(g + 1) * gw] = (v * lax.rsqrt(ms + RMS_EPS)
                                         * nw_ref[:, g * gw:(g + 1) * gw]).astype(o_ref.dtype)


def ssd_mixer(gz, xs, bc, dt, a_log, d_skip, norm_w, bsz, seq):
    nc = seq // CHUNK
    a = jnp.pad(-jnp.exp(a_log.astype(_F32)), (0, LANES - SSM_HEADS)).reshape(1, LANES)
    dsk = jnp.repeat(d_skip.astype(_F32), SSM_P).reshape(1, SSM_INNER)
    cst = lambda shape: pl.BlockSpec(shape, lambda b, c: (0,) * len(shape))
    row = lambda w: pl.BlockSpec((CHUNK, w), lambda b, c: (b * nc + c, 0))
    return pl.pallas_call(
        _ssd_kernel,
        grid=(bsz, nc),
        in_specs=[row(SSM_INNER), row(SSM_INNER), row(2 * SSM_BC), row(LANES),
                  cst((1, LANES)), cst((1, SSM_INNER)), cst((1, SSM_INNER))],
        out_specs=row(SSM_INNER),
        out_shape=jax.ShapeDtypeStruct((bsz * seq, SSM_INNER), _BF16),
        scratch_shapes=[pltpu.VMEM((SSM_GROUPS, SSM_N, SSM_GHEADS * SSM_P), _F32),
                        pltpu.VMEM((CHUNK, SSM_INNER), _F32)],
        compiler_params=_cparams("parallel", "arbitrary"), name="ssd_mixer",
    )(gz, xs, bc, dt, a, dsk, norm_w.reshape(1, -1))


ROUTE_SUB = 256
ROUTE_TM_ATTN = 1024
ROUTE_TM_SSM = 1024


def _layer_norm(y, g, b):
    mu = jnp.mean(y, axis=-1, keepdims=True)
    yc = y - mu
    var = jnp.mean(yc * yc, axis=-1, keepdims=True)
    return yc * lax.rsqrt(var + LN_EPS) * g + b


def _route(h1s, wrt_ref, br_ref, eidx_ref, gate_ref, tcnt_ref):
    sub = ROUTE_SUB
    n = len(h1s)
    nt = (((1,), (1,)), ((), ()))
    wrt = wrt_ref[...]
    vals = [lax.dot_general(wrt, h, nt, preferred_element_type=_F32, precision=lax.Precision.HIGHEST)
            + br_ref[...] for h in h1s]
    eio = lax.broadcasted_iota(jnp.int32, (N_EXPERTS, sub), 0)
    tops, ids, hots = ([[] for _ in range(n)] for _ in range(3))
    for _ in range(TOP_K):
        for a in range(n):
            mk = jnp.max(vals[a], axis=0, keepdims=True)
            ik = jnp.min(jnp.where(vals[a] == mk, eio, N_EXPERTS), axis=0, keepdims=True)
            hot = eio == ik
            vals[a] = jnp.where(hot, -jnp.inf, vals[a])
            tops[a].append(mk)
            ids[a].append(ik)
            hots[a].append(hot)
    col = lax.broadcasted_iota(jnp.int32, tcnt_ref.shape, 1)
    tcnt = tcnt_ref[...]
    for a in range(n):
        sl = slice(a * sub, (a + 1) * sub)
        exps = [jnp.exp(t - tops[a][0]) for t in tops[a]]
        den = exps[0] + exps[1] + exps[2] + exps[3]
        for k in range(TOP_K):
            eidx_ref[k:k + 1, sl] = ids[a][k]
            gate_ref[k:k + 1, sl] = exps[k] / den
        hot_all = hots[a][0] | hots[a][1] | hots[a][2] | hots[a][3]
        counts = jnp.sum(hot_all.astype(_F32), axis=1, keepdims=True)
        tcnt = jnp.where(col == pl.program_id(0) * n + a, counts, tcnt)
    tcnt_ref[...] = tcnt


def _norm_and_route(mixes, x_ref, g_ref, b_ref, wrt_ref, br_ref, h_ref, eidx_ref, gate_ref, tcnt_ref):
    sub = ROUTE_SUB

    @pl.when(pl.program_id(0) == 0)
    def _():
        tcnt_ref[...] = jnp.zeros_like(tcnt_ref)

    h1s = [_layer_norm(DEEPNORM_ALPHA * x_ref[a * sub:(a + 1) * sub, :] + mix, g_ref[...], b_ref[...])
           for a, mix in enumerate(mixes)]
    for a, h1 in enumerate(h1s):
        h_ref[a * sub:(a + 1) * sub, :] = h1
    _route(h1s, wrt_ref, br_ref, eidx_ref, gate_ref, tcnt_ref)


def _attn_out_kernel(o0_ref, o1_ref, o2_ref, l0_ref, l1_ref, l2_ref, x_ref, wo_ref, g_ref, b_ref,
                     wrt_ref, br_ref, h_ref, eidx_ref, gate_ref, tcnt_ref, ob_ref):
    sub = ROUTE_SUB
    l0, l1, l2 = l0_ref[...], l1_ref[...], l2_ref[...]
    m = jnp.maximum(jnp.maximum(l0, l1), l2)
    e0, e1, e2 = jnp.exp(l0 - m), jnp.exp(l1 - m), jnp.exp(l2 - m)
    inv = 1.0 / (e0 + e1 + e2)
    w0, w1, w2 = e0 * inv, e1 * inv, e2 * inv
    for h in range(HEADS):
        o = w0[:, h:h + 1] * o0_ref[h] + w1[:, h:h + 1] * o1_ref[h] + w2[:, h:h + 1] * o2_ref[h]
        ob_ref[:, h * HEAD_DIM:(h + 1) * HEAD_DIM] = o.astype(_BF16)
    mixes = [jnp.dot(ob_ref[a * sub:(a + 1) * sub, :], wo_ref[...], preferred_element_type=_F32)
             for a in range(x_ref.shape[0] // sub)]
    _norm_and_route(mixes, x_ref, g_ref, b_ref, wrt_ref, br_ref, h_ref, eidx_ref, gate_ref, tcnt_ref)


def _ssm_out_kernel(y_ref, x_ref, wo_ref, g_ref, b_ref, wrt_ref, br_ref, h_ref, eidx_ref, gate_ref, tcnt_ref):
    sub = ROUTE_SUB
    mixes = [jnp.dot(y_ref[a * sub:(a + 1) * sub, :], wo_ref[...], preferred_element_type=_F32)
             for a in range(x_ref.shape[0] // sub)]
    _norm_and_route(mixes, x_ref, g_ref, b_ref, wrt_ref, br_ref, h_ref, eidx_ref, gate_ref, tcnt_ref)


def _route_out_specs(t, tm):
    nsub = t // ROUTE_SUB
    specs = [pl.BlockSpec((tm, D_MODEL), lambda i: (i, 0)),
             pl.BlockSpec((TOP_K, tm), lambda i: (0, i)),
             pl.BlockSpec((TOP_K, tm), lambda i: (0, i)),
             pl.BlockSpec((N_EXPERTS, nsub), lambda i: (0, 0))]
    shapes = [jax.ShapeDtypeStruct((t, D_MODEL), _F32),
              jax.ShapeDtypeStruct((TOP_K, t), jnp.int32),
              jax.ShapeDtypeStruct((TOP_K, t), _F32),
              jax.ShapeDtypeStruct((N_EXPERTS, nsub), _F32)]
    return specs, shapes


def _const_spec(shape):
    return pl.BlockSpec(shape, lambda i: (0,) * len(shape))


def attn_out_norm_route(os_, lses, x, wo_bf16, ln_g, ln_b, w_router, b_router):
    t = x.shape[0]
    tm = ROUTE_TM_ATTN
    row = lambda w: pl.BlockSpec((tm, w), lambda i: (i, 0))
    out_specs, out_shapes = _route_out_specs(t, tm)
    return pl.pallas_call(
        _attn_out_kernel,
        grid=(t // tm,),
        in_specs=[pl.BlockSpec((HEADS, tm, HEAD_DIM), lambda i: (0, i, 0))] * 3 + [row(LANES)] * 3 + [row(D_MODEL),
                  _const_spec((D_MODEL, D_MODEL)), _const_spec((1, D_MODEL)), _const_spec((1, D_MODEL)),
                  _const_spec((N_EXPERTS, D_MODEL)), _const_spec((N_EXPERTS, 1))],
        out_specs=out_specs, out_shape=out_shapes,
        scratch_shapes=[pltpu.VMEM((tm, D_MODEL), _BF16)],
        compiler_params=_cparams("arbitrary"), name="attn_out_norm_route",
    )(*os_, *lses, x, wo_bf16, ln_g.reshape(1, -1), ln_b.reshape(1, -1),
      w_router.T, b_router.reshape(-1, 1))


def ssm_out_norm_route(yn, x, wo_bf16, ln_g, ln_b, w_router, b_router):
    t = x.shape[0]
    tm = ROUTE_TM_SSM
    row = lambda w: pl.BlockSpec((tm, w), lambda i: (i, 0))
    out_specs, out_shapes = _route_out_specs(t, tm)
    return pl.pallas_call(
        _ssm_out_kernel,
        grid=(t // tm,),
        in_specs=[row(SSM_INNER), row(D_MODEL),
                  _const_spec((SSM_INNER, D_MODEL)), _const_spec((1, D_MODEL)), _const_spec((1, D_MODEL)),
                  _const_spec((N_EXPERTS, D_MODEL)), _const_spec((N_EXPERTS, 1))],
        out_specs=out_specs, out_shape=out_shapes,
        compiler_params=_cparams("arbitrary"), name="ssm_out_norm_route",
    )(yn, x, wo_bf16, ln_g.reshape(1, -1), ln_b.reshape(1, -1), w_router.T, b_router.reshape(-1, 1))


MOE_BM = 1024
MOE_SUB_BM = 256
MOE_SUB = ROUTE_SUB
SEG_ALIGN = SUBLANES
LOCAL_ROWS = MOE_SUB * TOP_K + N_EXPERTS * SUBLANES
SEG_UNIT = 16
SEG_REST = (8,)
TAIL_SIZES = tuple(SEG_ALIGN << k for k in reversed(range((MOE_BM // SEG_ALIGN).bit_length() - 1)))
PAIR_COLS = D_MODEL // 2
_U32 = jnp.uint32


def _pack_pairs(x):
    lo = lax.shift_right_logical(lax.bitcast_convert_type(x[:, :PAIR_COLS], _U32), jnp.uint32(16))
    hi = lax.bitcast_convert_type(x[:, PAIR_COLS:], _U32) & jnp.uint32(0xFFFF0000)
    return hi | lo


def _unpack_pairs(u):
    lo = lax.bitcast_convert_type(lax.shift_left(u, jnp.uint32(16)), _F32).astype(_BF16)
    hi = lax.bitcast_convert_type(u & jnp.uint32(0xFFFF0000), _F32).astype(_BF16)
    return lo, hi


def _local_positions(eidx):
    sub = eidx.shape[1]
    eio = lax.broadcasted_iota(jnp.int32, (N_EXPERTS, sub), 0)
    hots = [eio == eidx[k:k + 1, :] for k in range(TOP_K)]
    hot_all = (hots[0] | hots[1] | hots[2] | hots[3]).astype(_BF16)
    r_io = lax.broadcasted_iota(jnp.int32, (sub, sub), 0)
    c_io = lax.broadcasted_iota(jnp.int32, (sub, sub), 1)
    upper = (r_io < c_io).astype(_BF16)
    prefix = jnp.dot(hot_all, upper, preferred_element_type=_F32)
    counts = jnp.sum(hot_all.astype(_F32), axis=1, keepdims=True).astype(jnp.int32)
    padded = ((counts + (SEG_ALIGN - 1)) & (-SEG_ALIGN)).astype(_F32)
    e_r = lax.broadcasted_iota(jnp.int32, (N_EXPERTS, N_EXPERTS), 0)
    e_c = lax.broadcasted_iota(jnp.int32, (N_EXPERTS, N_EXPERTS), 1)
    below = (e_c < e_r).astype(_BF16)
    seg_start = jnp.dot(below, jnp.broadcast_to(padded, (N_EXPERTS, sub)).astype(_BF16),
                        preferred_element_type=_F32)
    base = seg_start + prefix
    return [jnp.sum(jnp.where(h, base, 0.0), axis=0, keepdims=True) for h in hots]


def _segment_copies(cnt_ref, lo_ref, go_ref, i, copy_fn):
    for e in range(N_EXPERTS):
        n = cnt_ref[i * N_EXPERTS + e]
        lo = lo_ref[i * N_EXPERTS + e]
        go = go_ref[i * N_EXPERTS + e]

        def unit(u, c, lo=lo, go=go):
            copy_fn(pl.multiple_of(lo + u * SEG_UNIT, SEG_ALIGN), pl.multiple_of(go + u * SEG_UNIT, SEG_ALIGN),
                    SEG_UNIT)
            return c

        lax.fori_loop(0, n // SEG_UNIT, unit, 0)
        for sz in SEG_REST:
            done = n & -(2 * sz)

            @pl.when((n & sz) != 0)
            def _(done=done, sz=sz, lo=lo, go=go):
                copy_fn(pl.multiple_of(lo + done, SEG_ALIGN), pl.multiple_of(go + done, SEG_ALIGN), sz)


def _dispatch_kernel(cnt_ref, lo_ref, go_ref, units_ref, tail_ref, eidx_ref, h_ref, xs_ref, lpos_ref,
                     buf_ref, sem):
    i = pl.program_id(0)
    slot = i % 2
    mine, other = buf_ref.at[slot], buf_ref.at[1 - slot]
    lpos = _local_positions(eidx_ref[...])
    for k in range(TOP_K):
        lpos_ref[k:k + 1, :] = lpos[k]
    p_io = lax.broadcasted_iota(jnp.int32, (LOCAL_ROWS, MOE_SUB), 0).astype(_F32)
    sel = (p_io == lpos[0]) | (p_io == lpos[1]) | (p_io == lpos[2]) | (p_io == lpos[3])
    mine[...] = _pack_pairs(jnp.dot(sel.astype(_BF16), h_ref[...].astype(_BF16), preferred_element_type=_F32))

    def copy(lo, go, sz):
        pltpu.make_async_copy(mine.at[pl.ds(lo, sz)], xs_ref.at[pl.ds(go, sz)], sem.at[slot]).start()

    _segment_copies(cnt_ref, lo_ref, go_ref, i, copy)

    def wait_units(buf, s, n_units):
        def body(j, c):
            pltpu.make_async_copy(buf.at[pl.ds(0, SEG_ALIGN)], xs_ref.at[pl.ds(0, SEG_ALIGN)], sem.at[s]).wait()
            return c
        lax.fori_loop(0, n_units, body, 0)

    @pl.when(i > 0)
    def _():
        wait_units(other, 1 - slot, units_ref[jnp.maximum(i - 1, 0)])

    @pl.when(i == pl.num_programs(0) - 1)
    def _():
        wait_units(mine, slot, units_ref[i])
        mine[0:MOE_BM, :] = jnp.zeros((MOE_BM, PAIR_COLS), _U32)
        start, n, n_blocks = tail_ref[0], tail_ref[1], tail_ref[2]
        done = jnp.int32(0)
        for sz in TAIL_SIZES:
            @pl.when((n & sz) != 0)
            def _(done=done, sz=sz):
                pltpu.make_async_copy(mine.at[pl.ds(0, sz)],
                                      xs_ref.at[pl.ds(pl.multiple_of(start + done, SEG_ALIGN), sz)],
                                      sem.at[slot]).start()
            done = done + (n & sz)

        def zero_block(j, c):
            row0 = pl.multiple_of(start + n + j * MOE_BM, MOE_BM)
            pltpu.make_async_copy(mine.at[pl.ds(0, MOE_BM)], xs_ref.at[pl.ds(row0, MOE_BM)], sem.at[slot]).start()
            return c

        lax.fori_loop(0, n_blocks, zero_block, 0)
        wait_units(mine, slot, n // SEG_ALIGN + n_blocks * (MOE_BM // SEG_ALIGN))


def _seg_tables(tile_counts):
    cnt = tile_counts.T.astype(jnp.int32)
    pc = (cnt + (SEG_ALIGN - 1)) & (-SEG_ALIGN)
    lo = jnp.cumsum(pc, axis=1) - pc
    tot = jnp.sum(pc, axis=0)
    offs = jnp.concatenate([jnp.zeros((1,), jnp.int32), jnp.cumsum(tot)])
    go = offs[None, :-1] + jnp.cumsum(pc, axis=0) - pc
    units = jnp.sum(pc, axis=1) // SEG_ALIGN
    return pc.reshape(-1), lo.reshape(-1), go.reshape(-1), units.astype(jnp.int32), offs.astype(jnp.int32)


def _max_rows(t):
    n = t * TOP_K + (t // MOE_SUB) * N_EXPERTS * (SEG_ALIGN - 1)
    return -(-n // MOE_BM) * MOE_BM


def moe_dispatch(h, eidx, tables):
    t = h.shape[0]
    pc, lo, go, units, offs = tables
    used = offs[-1]
    partial = (-used) % MOE_BM
    tail = jnp.stack([used, partial, (_max_rows(t) - used - partial) // MOE_BM]).astype(jnp.int32)
    grid_spec = pltpu.PrefetchScalarGridSpec(
        num_scalar_prefetch=5, grid=(t // MOE_SUB,),
        in_specs=[pl.BlockSpec((TOP_K, MOE_SUB), lambda i, *_: (0, i)),
                  pl.BlockSpec((MOE_SUB, D_MODEL), lambda i, *_: (i, 0))],
        out_specs=[pl.BlockSpec(memory_space=pl.ANY), pl.BlockSpec((TOP_K, MOE_SUB), lambda i, *_: (0, i))],
        scratch_shapes=[pltpu.VMEM((2, LOCAL_ROWS, PAIR_COLS), _U32), pltpu.SemaphoreType.DMA((2,))])
    return pl.pallas_call(
        _dispatch_kernel, grid_spec=grid_spec,
        out_shape=[jax.ShapeDtypeStruct((_max_rows(t), PAIR_COLS), _U32),
                   jax.ShapeDtypeStruct((TOP_K, t), _F32)],
        compiler_params=_cparams("arbitrary"), name="moe_dispatch",
    )(pc, lo, go, units, tail, eidx, h)


def _combine_kernel(cnt_ref, lo_ref, go_ref, units_ref, lpos_ref, gate_ref, ys_ref, h_ref, g_ref, b_ref,
                    o_ref, buf_ref, sem):
    i = pl.program_id(0)
    slot = i % 2

    def fetch(step, s):
        dst = buf_ref.at[s]
        dst[MOE_SUB * TOP_K:, :] = jnp.zeros((LOCAL_ROWS - MOE_SUB * TOP_K, PAIR_COLS), _U32)

        def copy(lo, go, sz):
            pltpu.make_async_copy(ys_ref.at[pl.ds(go, sz)], dst.at[pl.ds(lo, sz)], sem.at[s]).start()

        _segment_copies(cnt_ref, lo_ref, go_ref, step, copy)

    @pl.when(i == 0)
    def _():
        fetch(i, slot)

    @pl.when(i + 1 < pl.num_programs(0))
    def _():
        fetch(i + 1, 1 - slot)

    mine = buf_ref.at[slot]
    lpos = lpos_ref[...]
    gates = gate_ref[...]
    p_io = lax.broadcasted_iota(jnp.int32, (LOCAL_ROWS, MOE_SUB), 0).astype(_F32)
    w = jnp.zeros((LOCAL_ROWS, MOE_SUB), _F32)
    for k in range(TOP_K):
        w = w + jnp.where(p_io == lpos[k:k + 1, :], gates[k:k + 1, :], 0.0)
    wb = w.astype(_BF16)

    def wait_unit(j, c):
        pltpu.make_async_copy(ys_ref.at[pl.ds(0, SEG_ALIGN)], mine.at[pl.ds(0, SEG_ALIGN)], sem.at[slot]).wait()
        return c

    lax.fori_loop(0, units_ref[i], wait_unit, 0)
    tn = (((0,), (0,)), ((), ()))
    ffn = jnp.concatenate([lax.dot_general(wb, yb, tn, preferred_element_type=_F32)
                           for yb in _unpack_pairs(mine[...])], axis=1)
    o_ref[...] = _layer_norm(DEEPNORM_ALPHA * h_ref[...] + ffn, g_ref[...], b_ref[...])


def moe_combine(ys, lpos, gates, h, tables, ln_g, ln_b):
    t = h.shape[0]
    pc, lo, go, units, _ = tables
    row = pl.BlockSpec((MOE_SUB, D_MODEL), lambda i, *_: (i, 0))
    kt = pl.BlockSpec((TOP_K, MOE_SUB), lambda i, *_: (0, i))
    cst = lambda shape: pl.BlockSpec(shape, lambda i, *_: (0,) * len(shape))
    grid_spec = pltpu.PrefetchScalarGridSpec(
        num_scalar_prefetch=4, grid=(t // MOE_SUB,),
        in_specs=[kt, kt, pl.BlockSpec(memory_space=pl.ANY), row, cst((1, D_MODEL)), cst((1, D_MODEL))],
        out_specs=row,
        scratch_shapes=[pltpu.VMEM((2, LOCAL_ROWS, PAIR_COLS), _U32), pltpu.SemaphoreType.DMA((2,))])
    return pl.pallas_call(
        _combine_kernel, grid_spec=grid_spec,
        out_shape=jax.ShapeDtypeStruct((t, D_MODEL), _F32),
        compiler_params=_cparams("arbitrary"), name="moe_combine",
    )(pc, lo, go, units, lpos, gates, ys, h, ln_g.reshape(1, -1), ln_b.reshape(1, -1))


def _expert_kernel(blk_ref, e_ref, lo_ref, hi_ref, xs_ref, wu_ref, bu_ref, wd_ref, bd_ref,
                   ys_ref, wub_ref, wdb_ref):
    i = pl.program_id(0)
    e = e_ref[i]
    e_prev = e_ref[jnp.maximum(i - 1, 0)]

    @pl.when((i == 0) | (e != e_prev))
    def _():
        wub_ref[...] = wu_ref[...].astype(_BF16)
        wdb_ref[...] = wd_ref[...].astype(_BF16)

    lo = lo_ref[i]
    hi = hi_ref[i]

    @pl.when((lo == 0) | (hi < 0))
    def _():
        ys_ref[...] = jnp.zeros_like(ys_ref)

    for sb in range(xs_ref.shape[0] // MOE_SUB_BM):
        r0 = sb * MOE_SUB_BM
        rows = slice(r0, r0 + MOE_SUB_BM)

        @pl.when((hi > lo) & (hi > r0) & (lo < r0 + MOE_SUB_BM))
        def _(r0=r0, rows=rows):
            x_lo, x_hi = _unpack_pairs(xs_ref[rows, :])
            h = (jnp.dot(x_lo, wub_ref[:PAIR_COLS, :], preferred_element_type=_F32)
                 + jnp.dot(x_hi, wub_ref[PAIR_COLS:, :], preferred_element_type=_F32) + bu_ref[...])
            gate = jnp.minimum(h[:, :D_FF], SWIGLU_LIMIT)
            up = jnp.clip(h[:, D_FF:], -SWIGLU_LIMIT, SWIGLU_LIMIT)
            glu = gate * jax.nn.sigmoid(SWIGLU_ALPHA * gate)
            act = ((up + 1.0) * glu).astype(_BF16)
            y = jnp.dot(act, wdb_ref[...], preferred_element_type=_F32) + bd_ref[...]
            packed = _pack_pairs(y.astype(_BF16).astype(_F32))
            row = r0 + lax.broadcasted_iota(jnp.int32, packed.shape, 0)
            ys_ref[rows, :] = jnp.where((row >= lo) & (row < hi), packed, ys_ref[rows, :])


def _expert_schedule(offs, n_rows, bm):
    nblk = n_rows // bm
    n_items = nblk + N_EXPERTS - 1
    used = offs[-1]
    bnd = offs[1:-1]
    pos = jnp.arange(N_EXPERTS - 1, dtype=jnp.int32) + jnp.minimum(bnd // bm + 1, nblk)
    i = jnp.arange(n_items, dtype=jnp.int32)[:, None]
    at = pos[None, :] == i
    is_bnd = jnp.any(at, axis=1)
    bnd_val = jnp.sum(jnp.where(at, bnd[None, :], 0), axis=1)
    n_before = jnp.sum((pos[None, :] < i).astype(jnp.int32), axis=1)
    cuts = jnp.where(is_bnd, bnd_val, (i[:, 0] - n_before) * bm)
    ends = jnp.concatenate([cuts[1:], jnp.array([n_rows], jnp.int32)])
    last_blk = jnp.maximum(used - 1, 0) // bm
    past = jnp.logical_not(is_bnd) & (cuts // bm > last_blk)
    raw_blk = cuts // bm
    cuts = jnp.minimum(cuts, used)
    ends = jnp.minimum(ends, used)
    blk = jnp.minimum(cuts // bm, last_blk)
    e = jnp.minimum(jnp.sum((bnd[None, :] <= cuts[:, None]).astype(jnp.int32), axis=1), N_EXPERTS - 1)
    lo = jnp.where(past, 0, cuts - blk * bm)
    hi = jnp.where(past, -1, jnp.minimum(ends - blk * bm, bm))
    return lax.cummax(jnp.where(past, raw_blk, blk)), e, lo, hi


def moe_experts(xs, offs, layer, w_up, b_up, w_down, b_down):
    n_rows = xs.shape[0]
    bm = MOE_BM
    blk, e, lo, hi = _expert_schedule(offs, n_rows, bm)
    n_items = n_rows // bm + N_EXPERTS - 1
    wmap = lambda i, b, e, lo, hi: (layer, e[i], 0, 0)
    grid_spec = pltpu.PrefetchScalarGridSpec(
        num_scalar_prefetch=4, grid=(n_items,),
        in_specs=[pl.BlockSpec((bm, PAIR_COLS), lambda i, b, e, lo, hi: (b[i], 0)),
                  pl.BlockSpec((None, None, D_MODEL, 2 * D_FF), wmap),
                  pl.BlockSpec((None, None, 1, 2 * D_FF), wmap),
                  pl.BlockSpec((None, None, D_FF, D_MODEL), wmap),
                  pl.BlockSpec((None, None, 1, D_MODEL), wmap)],
        out_specs=pl.BlockSpec((bm, PAIR_COLS), lambda i, b, e, lo, hi: (b[i], 0)),
        scratch_shapes=[pltpu.VMEM((D_MODEL, 2 * D_FF), _BF16), pltpu.VMEM((D_FF, D_MODEL), _BF16)])
    nl = w_up.shape[0]
    return pl.pallas_call(
        _expert_kernel, grid_spec=grid_spec,
        out_shape=jax.ShapeDtypeStruct((n_rows, PAIR_COLS), _U32),
        compiler_params=_cparams("arbitrary"), name="moe_experts",
    )(blk, e, lo, hi, xs, w_up, b_up.reshape(nl, N_EXPERTS, 1, -1), w_down, b_down.reshape(nl, N_EXPERTS, 1, -1))


def moe_layer(h, eidx, gates, tile_counts, layer, w_up, b_up, w_down, b_down, ln_g, ln_b):
    tables = _seg_tables(tile_counts)
    xs, lpos = moe_dispatch(h, eidx, tables)
    ys = moe_experts(xs, tables[4], layer, w_up, b_up, w_down, b_down)
    return moe_combine(ys, lpos, gates, h, tables, ln_g, ln_b)


def kernel(x, positions, attn_w_in, attn_w_out, ssm_w_in, ssm_conv_w, ssm_conv_b, ssm_dt_bias,
           ssm_a_log, ssm_d, ssm_norm_w, ssm_w_out, moe_w_router, moe_b_router, moe_w_up,
           moe_b_up, moe_w_down, moe_b_down, ln_mix_g, ln_mix_b, ln_ffn_g, ln_ffn_b):
    bsz, seq, _ = x.shape
    xf = x.reshape(bsz * seq, D_MODEL)

    w_qkv = attn_w_in[0].astype(_BF16)
    outs = [dilated_group_attention(qkv_proj_group(xf, w_qkv, positions, g), g, bsz, seq)
            for g in range(N_GROUPS)]
    h, eidx, gates, tile_counts = attn_out_norm_route(
        [o for o, _ in outs], [l for _, l in outs], xf, attn_w_out[0].astype(_BF16),
        ln_mix_g[0], ln_mix_b[0], moe_w_router[0], moe_b_router[0])
    xf = moe_layer(h, eidx, gates, tile_counts, 0, moe_w_up, moe_b_up, moe_w_down, moe_b_down,
                   ln_ffn_g[0], ln_ffn_b[0])

    gz, xs, bc, dt = ssm_in_proj(xf, ssm_w_in[0], ssm_conv_w[0], ssm_conv_b[0], ssm_dt_bias[0], seq)
    yn = ssd_mixer(gz, xs, bc, dt, ssm_a_log[0], ssm_d[0], ssm_norm_w[0], bsz, seq)
    h, eidx, gates, tile_counts = ssm_out_norm_route(
        yn, xf, ssm_w_out[0].astype(_BF16), ln_mix_g[1], ln_mix_b[1], moe_w_router[1], moe_b_router[1])
    xf = moe_layer(h, eidx, gates, tile_counts, 1, moe_w_up, moe_b_up, moe_w_down, moe_b_down,
                   ln_ffn_g[1], ln_ffn_b[1])
    return xf.reshape(bsz, seq, D_MODEL)
```

```python
import functools
import math

import jax
import jax.numpy as jnp
from jax import lax
from jax.experimental import pallas as pl
from jax.experimental.pallas import tpu as pltpu

D_MODEL = 1024
DEPTH = 2
DIL = (1, 4, 16)
N_GROUPS = 3
SUB_WIN = 128
HEAD_DIM = 128
HEADS = 8
ROT_DIM = 32
ROT_HALF = 16
ROPE_THETA = 500000.0
SSM_INNER = 2048
SSM_P = 64
SSM_HEADS = 32
SSM_GROUPS = 4
SSM_GHEADS = 8
SSM_N = 128
SSM_CONV = 4
CHUNK = 128
SSM_BC = SSM_GROUPS * SSM_N
SSM_XBC = SSM_INNER + 2 * SSM_BC
N_EXPERTS = 32
TOP_K = 4
D_FF = 1024
SWIGLU_LIMIT = 7.0
SWIGLU_ALPHA = 1.702
DEEPNORM_ALPHA = (2 * DEPTH) ** 0.25
LN_EPS = 1e-5
RMS_EPS = 1e-5

LANES = 128
SUBLANES = 8
VMEM_LIMIT_BYTES = 56 * 1024 * 1024

NEG_BIG = -1e30

_F32 = jnp.float32
_BF16 = jnp.bfloat16


def _cparams(*sem):
    return pltpu.CompilerParams(dimension_semantics=sem, vmem_limit_bytes=VMEM_LIMIT_BYTES)


def _per_head_lanes(m, lanes_per_head, width):
    head_of_lane = lax.broadcasted_iota(jnp.int32, (LANES, width), 1) // lanes_per_head
    spread = (head_of_lane == lax.broadcasted_iota(jnp.int32, (LANES, width), 0)).astype(_BF16)
    hi = m.astype(_BF16)
    lo = (m - hi.astype(_F32)).astype(_BF16)
    return jnp.dot(hi, spread, preferred_element_type=_F32) + jnp.dot(lo, spread, preferred_element_type=_F32)


ROPE_PACK = LANES // ROT_DIM


def _rope_table_kernel(pos_ref, invf_ref, c_ref, sa_ref, sb_ref):
    pos = pos_ref[...].astype(_F32)
    lane = lax.broadcasted_iota(jnp.int32, (pos.shape[0], LANES), 1)
    p = jnp.zeros((pos.shape[0], LANES), _F32)
    group = lax.shift_right_logical(lane, ROT_DIM.bit_length() - 1)
    for q in range(ROPE_PACK):
        p = jnp.where(group == q, pos[:, q:q + 1], p)
    ang = p * invf_ref[...]
    cos = jnp.cos(ang)
    sin = jnp.sin(ang)
    for q in range(ROPE_PACK):
        shift = (LANES - q * ROT_DIM) % LANES
        cq = pltpu.roll(cos, shift, 1) if shift else cos
        sq = pltpu.roll(sin, shift, 1) if shift else sin
        c_ref[q] = jnp.where(lane < ROT_DIM, cq, 1.0)
        sa_ref[q] = jnp.where((lane >= ROT_HALF) & (lane < ROT_DIM), sq, 0.0)
        sb_ref[q] = jnp.where(lane < ROT_HALF, -sq, 0.0)


def rope_tables(positions):
    t = positions.size
    rows = t // ROPE_PACK
    tm = min(1024, rows)
    pos = positions.reshape(ROPE_PACK, rows).T
    j = jnp.arange(LANES)
    invf = (ROPE_THETA ** (-(2.0 * (j % ROT_HALF)).astype(_F32) / ROT_DIM)).astype(_F32).reshape(1, LANES)
    out = jax.ShapeDtypeStruct((ROPE_PACK, rows, LANES), _F32)
    tabs = pl.pallas_call(
        _rope_table_kernel,
        grid=(rows // tm,),
        in_specs=[pl.BlockSpec((tm, ROPE_PACK), lambda i: (i, 0)),
                  pl.BlockSpec((1, LANES), lambda i: (0, 0))],
        out_specs=[pl.BlockSpec((ROPE_PACK, tm, LANES), lambda i: (0, i, 0))] * 3,
        out_shape=[out, out, out],
        compiler_params=_cparams("parallel"),
        name="rope_tables",
    )(pos, invf)
    return [x.reshape(t, LANES) for x in tabs]


QKV_TM = 512
ATTN_ROWS = 2048
HEADS_PER_STEP = 4


def _qkv_proj_kernel(x_ref, w_ref, c_ref, sa_ref, sb_ref, o_ref, *, d, ns, tt):
    tm = x_ref.shape[0]
    xb = x_ref[...].astype(_BF16)
    if d > 1:
        dst = lax.broadcasted_iota(jnp.int32, (tm, tm), 0)
        src = lax.broadcasted_iota(jnp.int32, (tm, tm), 1)
        lt, ld = tt.bit_length() - 1, d.bit_length() - 1
        t_ = dst & (tt - 1)
        r_ = lax.shift_right_logical(dst, lt) & (d - 1)
        s_ = lax.shift_right_logical(dst, lt + ld)
        perm = (src == ((s_ * tt + t_) * d + r_)).astype(_BF16)
        xb = jnp.dot(perm, xb, preferred_element_type=_F32).astype(_BF16)
    c = c_ref[...]
    sa = sa_ref[...]
    sb = sb_ref[...]
    for comp in range(3):
        acc = jnp.dot(xb, w_ref[:, comp * D_MODEL:(comp + 1) * D_MODEL], preferred_element_type=_F32)
        for h in range(HEADS):
            sl = slice(h * HEAD_DIM, (h + 1) * HEAD_DIM)
            t = acc[:, sl]
            if comp < 2:
                t = t * c + pltpu.roll(t, ROT_HALF, 1) * sa + pltpu.roll(t, HEAD_DIM - ROT_HALF, 1) * sb
            o_ref[:, :, :, comp * D_MODEL + h * HEAD_DIM:comp * D_MODEL + (h + 1) * HEAD_DIM] = (
                t.reshape(ns, d, tt, HEAD_DIM).astype(o_ref.dtype))


def qkv_proj_group(x, w_bf16, positions, g):
    t, k = x.shape
    d = DIL[g]
    span = SUB_WIN * d
    tm = QKV_TM
    ns = max(1, tm // span)
    tps = max(1, span // tm)
    tt = tm // (ns * d)
    rope = rope_tables(positions.reshape(-1, ns, tt, d).swapaxes(2, 3).reshape(-1))
    tab_spec = pl.BlockSpec((tm, LANES), lambda i: (i, 0))
    return pl.pallas_call(
        functools.partial(_qkv_proj_kernel, d=d, ns=ns, tt=tt),
        grid=(t // tm,),
        in_specs=[pl.BlockSpec((tm, k), lambda i: (i, 0)),
                  pl.BlockSpec((k, 3 * D_MODEL), lambda i: (0, g)),
                  tab_spec, tab_spec, tab_spec],
        out_specs=pl.BlockSpec((ns, d, tt, 3 * D_MODEL), lambda i: (i // tps, 0, i % tps, 0)),
        out_shape=jax.ShapeDtypeStruct((t // span, d, SUB_WIN, 3 * D_MODEL), _BF16),
        compiler_params=_cparams("parallel"), name=f"qkv_proj_g{g}",
    )(x, w_bf16, *rope)


def _dil_attn_kernel(q_ref, k_ref, v_ref, kp_ref, vp_ref, o_ref, lse_ref, *, d, ns):
    j = pl.program_id(1)
    hp = pl.program_id(2)
    nb = ns * d
    row = lax.broadcasted_iota(jnp.int32, (1, SUB_WIN, 2 * SUB_WIN), 1)
    col = lax.broadcasted_iota(jnp.int32, (1, SUB_WIN, 2 * SUB_WIN), 2)
    blk = lax.broadcasted_iota(jnp.int32, (nb, 1, 1), 0)
    has_prev = (blk >= d) | (j > 0)
    valid = ((col < SUB_WIN) & (col >= row) & has_prev) | ((col >= SUB_WIN) & (col - SUB_WIN <= row))
    lane = lax.broadcasted_iota(jnp.int32, (SUB_WIN, LANES), 1)
    scale = 1.0 / math.sqrt(HEAD_DIM)

    def block_rows(b):
        s, r = divmod(b, d)
        start = s * SUB_WIN * d + r
        return pl.ds(start, SUB_WIN, stride=d) if d > 1 else pl.ds(start, SUB_WIN)

    @pl.when(hp == 0)
    def _():
        lse_ref[...] = jnp.zeros_like(lse_ref)

    lses = []
    for hh in range(HEADS_PER_STEP):
        sl = slice(hh * HEAD_DIM, (hh + 1) * HEAD_DIM)
        flat = lambda a: a.reshape(nb, SUB_WIN, HEAD_DIM)
        q, kc, vc = flat(q_ref[:, :, :, sl]), flat(k_ref[:, :, :, sl]), flat(v_ref[:, :, :, sl])
        if ns > 1:
            kp = flat(jnp.concatenate([kp_ref[:, :, :, sl], k_ref[:ns - 1, :, :, sl]], axis=0))
            vp = flat(jnp.concatenate([vp_ref[:, :, :, sl], v_ref[:ns - 1, :, :, sl]], axis=0))
        else:
            kp, vp = flat(kp_ref[:, :, :, sl]), flat(vp_ref[:, :, :, sl])
        keys = jnp.concatenate([kp, kc], axis=1)
        vals = jnp.concatenate([vp, vc], axis=1)
        s = jnp.einsum('bqd,bkd->bqk', q, keys, preferred_element_type=_F32) * scale
        s = jnp.where(valid, s, NEG_BIG)
        m = jnp.max(s, axis=2, keepdims=True)
        p = jnp.exp(s - m)
        l = jnp.sum(p, axis=2, keepdims=True)
        o = jnp.einsum('bqk,bkd->bqd', p.astype(_BF16), vals, preferred_element_type=_F32) / l
        for b in range(nb):
            o_ref[hh, block_rows(b), :] = o[b]
        lses.append(m + jnp.log(l))
    for b in range(nb):
        cur = lse_ref[block_rows(b), :]
        for hh in range(HEADS_PER_STEP):
            cur = jnp.where(lane == hp * HEADS_PER_STEP + hh, lses[hh][b], cur)
        lse_ref[block_rows(b), :] = cur


def dilated_group_attention(qkv, g, bsz, seq):
    d = DIL[g]
    ns = ATTN_ROWS // (SUB_WIN * d)
    nsteps = seq // ATTN_ROWS
    hw = HEADS_PER_STEP * HEAD_DIM
    ncol = D_MODEL // hw
    cur = lambda comp: pl.BlockSpec((ns, d, SUB_WIN, hw), lambda b, j, hp: (b * nsteps + j, 0, 0, comp * ncol + hp))
    prev = lambda comp: pl.BlockSpec(
        (1, d, SUB_WIN, hw), lambda b, j, hp: (jnp.maximum((b * nsteps + j) * ns - 1, 0), 0, 0, comp * ncol + hp))
    return pl.pallas_call(
        functools.partial(_dil_attn_kernel, d=d, ns=ns),
        grid=(bsz, nsteps, HEADS // HEADS_PER_STEP),
        in_specs=[cur(0), cur(1), cur(2), prev(1), prev(2)],
        out_specs=[pl.BlockSpec((HEADS_PER_STEP, ATTN_ROWS, HEAD_DIM), lambda b, j, hp: (hp, b * nsteps + j, 0)),
                   pl.BlockSpec((ATTN_ROWS, LANES), lambda b, j, hp: (b * nsteps + j, 0))],
        out_shape=[jax.ShapeDtypeStruct((HEADS, bsz * seq, HEAD_DIM), _F32),
                   jax.ShapeDtypeStruct((bsz * seq, LANES), _F32)],
        compiler_params=_cparams("parallel", "parallel", "arbitrary"),
        name=f"dilated_attn_g{g}",
    )(qkv, qkv, qkv, qkv, qkv)


SSM_IN_TM = 512
HALO = SUBLANES
ZX_COLS = SSM_INNER + SSM_XBC


def _softplus(x):
    return jnp.maximum(x, 0.0) + jnp.log1p(jnp.exp(-jnp.abs(x)))


def _ssm_in_kernel(x_ref, w_ref, wdt_ref, cw_ref, cb_ref, dtb_ref, gz_ref, xs_ref, bc_ref, dt_ref,
                   carry_ref, *, tiles_per_seq):
    tm = x_ref.shape[0]
    cw = D_MODEL
    xb = x_ref[...].astype(_BF16)

    @pl.when(pl.program_id(0) % tiles_per_seq == 0)
    def _():
        carry_ref[...] = jnp.zeros_like(carry_ref)

    for i in range(SSM_INNER // cw):
        z = jnp.dot(xb, w_ref[:, i * cw:(i + 1) * cw], preferred_element_type=_F32)
        gz_ref[:, i * cw:(i + 1) * cw] = z * jax.nn.sigmoid(z)
    for c in range(SSM_XBC // cw):
        cols = slice(c * cw, (c + 1) * cw)
        acc = jnp.dot(xb, w_ref[:, SSM_INNER + c * cw:SSM_INNER + (c + 1) * cw], preferred_element_type=_F32)
        ext = jnp.concatenate([carry_ref[c], acc], axis=0)
        carry_ref[c] = acc[tm - HALO:, :]
        conv = cb_ref[:, cols] + cw_ref[SSM_CONV - 1:SSM_CONV, cols] * acc
        for k in range(1, SSM_CONV):
            conv = conv + cw_ref[SSM_CONV - 1 - k:SSM_CONV - k, cols] * pltpu.roll(ext, k, 0)[HALO:, :]
        act = conv * jax.nn.sigmoid(conv)
        if (c + 1) * cw <= SSM_INNER:
            xs_ref[:, cols] = act
        else:
            bc_ref[...] = act.astype(_BF16)
    dt_raw = jnp.dot(xb, wdt_ref[...], preferred_element_type=_F32)
    dt_ref[...] = _softplus(dt_raw + dtb_ref[...])


def ssm_in_proj(x, w_in, conv_w, conv_b, dt_bias, seq):
    t, k = x.shape
    tm = SSM_IN_TM
    assert 2 * SSM_BC == D_MODEL
    pad = LANES - SSM_HEADS
    w_main = w_in[:, :ZX_COLS].astype(_BF16)
    w_dt = jnp.pad(w_in[:, ZX_COLS:], ((0, 0), (0, pad))).astype(_BF16)
    dtb = jnp.pad(dt_bias, (0, pad)).reshape(1, LANES)
    cst = lambda shape: pl.BlockSpec(shape, lambda i: (0,) * len(shape))
    once = lambda shape: pl.BlockSpec(shape, lambda i: (0,) * len(shape), pipeline_mode=pl.Buffered(1))
    row = lambda w: pl.BlockSpec((tm, w), lambda i: (i, 0))
    return pl.pallas_call(
        functools.partial(_ssm_in_kernel, tiles_per_seq=seq // tm),
        grid=(t // tm,),
        in_specs=[row(k), once((k, ZX_COLS)), once((k, LANES)),
                  cst((SSM_CONV, SSM_XBC)), cst((1, SSM_XBC)), cst((1, LANES))],
        out_specs=[row(SSM_INNER), row(SSM_INNER), row(2 * SSM_BC), row(LANES)],
        out_shape=[jax.ShapeDtypeStruct((t, SSM_INNER), _F32), jax.ShapeDtypeStruct((t, SSM_INNER), _F32),
                   jax.ShapeDtypeStruct((t, 2 * SSM_BC), _BF16), jax.ShapeDtypeStruct((t, LANES), _F32)],
        scratch_shapes=[pltpu.VMEM((SSM_XBC // D_MODEL, HALO, D_MODEL), _F32)],
        compiler_params=_cparams("arbitrary"), name="ssm_in_proj",
    )(x, w_main, w_dt, conv_w, conv_b.reshape(1, -1), dtb)


def _ssd_kernel(gz_ref, xs_ref, bc_ref, dt_ref, a_ref, dsk_ref, nw_ref, o_ref, state_ref, y_ref):
    c = pl.program_id(1)
    q = CHUNK

    @pl.when(c == 0)
    def _():
        state_ref[...] = jnp.zeros_like(state_ref)

    dt = dt_ref[...]
    da = dt * a_ref[...]
    r_io = lax.broadcasted_iota(jnp.int32, (q, q), 0)
    c_io = lax.broadcasted_iota(jnp.int32, (q, q), 1)
    causal = r_io >= c_io
    tri = causal.astype(_F32)
    a_cum = jnp.dot(tri, da, preferred_element_type=_F32, precision=lax.Precision.HIGHEST)
    a_cum_t = a_cum.T
    dt_t = dt.T
    a_last = a_cum[q - 1:q, :]
    w_lanes = _per_head_lanes(jnp.exp(a_last - a_cum) * dt, SSM_P, SSM_INNER)
    e_lanes = _per_head_lanes(jnp.exp(a_cum), SSM_P, SSM_INNER)
    chunk_decay = jnp.exp(a_last)
    lane = lax.broadcasted_iota(jnp.int32, (q, LANES), 1)
    low = lane < SSM_P
    lane1 = lax.broadcasted_iota(jnp.int32, (1, LANES), 1)
    low1 = lane1 < SSM_P
    tn = (((0,), (0,)), ((), ()))
    nt = (((1,), (1,)), ((), ()))

    for g in range(SSM_GROUPS):
        bg = bc_ref[:, g * SSM_N:(g + 1) * SSM_N]
        cg = bc_ref[:, SSM_BC + g * SSM_N:SSM_BC + (g + 1) * SSM_N]
        cbm = lax.dot_general(cg, bg, nt, preferred_element_type=_F32)
        st = state_ref[g]
        y_off = jnp.dot(cg, st.astype(_BF16), preferred_element_type=_F32)
        xw_parts, dec_parts = [], []
        for j in range(SSM_GHEADS // 2):
            h0 = g * SSM_GHEADS + 2 * j
            col = (g * SSM_GHEADS // 2 + j) * LANES
            xp = xs_ref[:, col:col + LANES]
            y_diag = jnp.zeros((q, LANES), _F32)
            for half, hh in enumerate((h0, h0 + 1)):
                diff = a_cum[:, hh:hh + 1] - a_cum_t[hh:hh + 1, :]
                lmat = jnp.exp(jnp.where(causal, diff, NEG_BIG))
                mm = (cbm * lmat * dt_t[hh:hh + 1, :]).astype(_BF16)
                keep = low if half == 0 else jnp.logical_not(low)
                xh = jnp.where(keep, xp, 0.0).astype(_BF16)
                y_diag = y_diag + jnp.dot(mm, xh, preferred_element_type=_F32)
            y_ref[:, col:col + LANES] = (y_diag + y_off[:, j * LANES:(j + 1) * LANES] * e_lanes[:, col:col + LANES]
                                         + xp * dsk_ref[:, col:col + LANES])
            xw_parts.append((xp * w_lanes[:, col:col + LANES]).astype(_BF16))
            dec_parts.append(jnp.where(low1, chunk_decay[:, h0:h0 + 1], chunk_decay[:, h0 + 1:h0 + 2]))
        xw = jnp.concatenate(xw_parts, axis=1)
        dec = jnp.concatenate(dec_parts, axis=1)
        state_ref[g] = st * dec + lax.dot_general(bg, xw, tn, preferred_element_type=_F32)

    yg = y_ref[...] * gz_ref[...]
    gw = SSM_INNER // SSM_GROUPS
    for g in range(SSM_GROUPS):
        v = yg[:, g * gw:(g + 1) * gw]
        ms = jnp.mean(v * v, axis=-1, keepdims=True)
        o_ref[:, g * gw:(g + 1) * gw] = (v * lax.rsqrt(ms + RMS_EPS)
                                         * nw_ref[:, g * gw:(g + 1) * gw]).astype(o_ref.dtype)


def ssd_mixer(gz, xs, bc, dt, a_log, d_skip, norm_w, bsz, seq):
    nc = seq // CHUNK
    a = jnp.pad(-jnp.exp(a_log.astype(_F32)), (0, LANES - SSM_HEADS)).reshape(1, LANES)
    dsk = jnp.repeat(d_skip.astype(_F32), SSM_P).reshape(1, SSM_INNER)
    cst = lambda shape: pl.BlockSpec(shape, lambda b, c: (0,) * len(shape))
    row = lambda w: pl.BlockSpec((CHUNK, w), lambda b, c: (b * nc + c, 0))
    return pl.pallas_call(
        _ssd_kernel,
        grid=(bsz, nc),
        in_specs=[row(SSM_INNER), row(SSM_INNER), row(2 * SSM_BC), row(LANES),
                  cst((1, LANES)), cst((1, SSM_INNER)), cst((1, SSM_INNER))],
        out_specs=row(SSM_INNER),
        out_shape=jax.ShapeDtypeStruct((bsz * seq, SSM_INNER), _BF16),
        scratch_shapes=[pltpu.VMEM((SSM_GROUPS, SSM_N, SSM_GHEADS * SSM_P), _F32),
                        pltpu.VMEM((CHUNK, SSM_INNER), _F32)],
        compiler_params=_cparams("parallel", "arbitrary"), name="ssd_mixer",
    )(gz, xs, bc, dt, a, dsk, norm_w.reshape(1, -1))


ROUTE_SUB = 256
ROUTE_TM_ATTN = 1024
ROUTE_TM_SSM = 1024


def _layer_norm(y, g, b):
    mu = jnp.mean(y, axis=-1, keepdims=True)
    yc = y - mu
    var = jnp.mean(yc * yc, axis=-1, keepdims=True)
    return yc * lax.rsqrt(var + LN_EPS) * g + b


def _route(h1s, wrt_ref, br_ref, eidx_ref, gate_ref, tcnt_ref):
    sub = ROUTE_SUB
    n = len(h1s)
    nt = (((1,), (1,)), ((), ()))
    wrt = wrt_ref[...]
    vals = [lax.dot_general(wrt, h, nt, preferred_element_type=_F32, precision=lax.Precision.HIGHEST)
            + br_ref[...] for h in h1s]
    eio = lax.broadcasted_iota(jnp.int32, (N_EXPERTS, sub), 0)
    tops, ids, hots = ([[] for _ in range(n)] for _ in range(3))
    for _ in range(TOP_K):
        for a in range(n):
            mk = jnp.max(vals[a], axis=0, keepdims=True)
            ik = jnp.min(jnp.where(vals[a] == mk, eio, N_EXPERTS), axis=0, keepdims=True)
            hot = eio == ik
            vals[a] = jnp.where(hot, -jnp.inf, vals[a])
            tops[a].append(mk)
            ids[a].append(ik)
            hots[a].append(hot)
    col = lax.broadcasted_iota(jnp.int32, tcnt_ref.shape, 1)
    tcnt = tcnt_ref[...]
    for a in range(n):
        sl = slice(a * sub, (a + 1) * sub)
        exps = [jnp.exp(t - tops[a][0]) for t in tops[a]]
        den = exps[0] + exps[1] + exps[2] + exps[3]
        for k in range(TOP_K):
            eidx_ref[k:k + 1, sl] = ids[a][k]
            gate_ref[k:k + 1, sl] = exps[k] / den
        hot_all = hots[a][0] | hots[a][1] | hots[a][2] | hots[a][3]
        counts = jnp.sum(hot_all.astype(_F32), axis=1, keepdims=True)
        tcnt = jnp.where(col == pl.program_id(0) * n + a, counts, tcnt)
    tcnt_ref[...] = tcnt


def _norm_and_route(mixes, x_ref, g_ref, b_ref, wrt_ref, br_ref, h_ref, eidx_ref, gate_ref, tcnt_ref):
    sub = ROUTE_SUB

    @pl.when(pl.program_id(0) == 0)
    def _():
        tcnt_ref[...] = jnp.zeros_like(tcnt_ref)

    h1s = [_layer_norm(DEEPNORM_ALPHA * x_ref[a * sub:(a + 1) * sub, :] + mix, g_ref[...], b_ref[...])
           for a, mix in enumerate(mixes)]
    for a, h1 in enumerate(h1s):
        h_ref[a * sub:(a + 1) * sub, :] = h1
    _route(h1s, wrt_ref, br_ref, eidx_ref, gate_ref, tcnt_ref)


def _attn_out_kernel(o0_ref, o1_ref, o2_ref, l0_ref, l1_ref, l2_ref, x_ref, wo_ref, g_ref, b_ref,
                     wrt_ref, br_ref, h_ref, eidx_ref, gate_ref, tcnt_ref, ob_ref):
    sub = ROUTE_SUB
    l0, l1, l2 = l0_ref[...], l1_ref[...], l2_ref[...]
    m = jnp.maximum(jnp.maximum(l0, l1), l2)
    e0, e1, e2 = jnp.exp(l0 - m), jnp.exp(l1 - m), jnp.exp(l2 - m)
    inv = 1.0 / (e0 + e1 + e2)
    w0, w1, w2 = e0 * inv, e1 * inv, e2 * inv
    for h in range(HEADS):
        o = w0[:, h:h + 1] * o0_ref[h] + w1[:, h:h + 1] * o1_ref[h] + w2[:, h:h + 1] * o2_ref[h]
        ob_ref[:, h * HEAD_DIM:(h + 1) * HEAD_DIM] = o.astype(_BF16)
    mixes = [jnp.dot(ob_ref[a * sub:(a + 1) * sub, :], wo_ref[...], preferred_element_type=_F32)
             for a in range(x_ref.shape[0] // sub)]
    _norm_and_route(mixes, x_ref, g_ref, b_ref, wrt_ref, br_ref, h_ref, eidx_ref, gate_ref, tcnt_ref)


def _ssm_out_kernel(y_ref, x_ref, wo_ref, g_ref, b_ref, wrt_ref, br_ref, h_ref, eidx_ref, gate_ref, tcnt_ref):
    sub = ROUTE_SUB
    mixes = [jnp.dot(y_ref[a * sub:(a + 1) * sub, :], wo_ref[...], preferred_element_type=_F32)
             for a in range(x_ref.shape[0] // sub)]
    _norm_and_route(mixes, x_ref, g_ref, b_ref, wrt_ref, br_ref, h_ref, eidx_ref, gate_ref, tcnt_ref)


def _route_out_specs(t, tm):
    nsub = t // ROUTE_SUB
    specs = [pl.BlockSpec((tm, D_MODEL), lambda i: (i, 0)),
             pl.BlockSpec((TOP_K, tm), lambda i: (0, i)),
             pl.BlockSpec((TOP_K, tm), lambda i: (0, i)),
             pl.BlockSpec((N_EXPERTS, nsub), lambda i: (0, 0))]
    shapes = [jax.ShapeDtypeStruct((t, D_MODEL), _F32),
              jax.ShapeDtypeStruct((TOP_K, t), jnp.int32),
              jax.ShapeDtypeStruct((TOP_K, t), _F32),
              jax.ShapeDtypeStruct((N_EXPERTS, nsub), _F32)]
    return specs, shapes


def _const_spec(shape):
    return pl.BlockSpec(shape, lambda i: (0,) * len(shape))


def attn_out_norm_route(os_, lses, x, wo_bf16, ln_g, ln_b, w_router, b_router):
    t = x.shape[0]
    tm = ROUTE_TM_ATTN
    row = lambda w: pl.BlockSpec((tm, w), lambda i: (i, 0))
    out_specs, out_shapes = _route_out_specs(t, tm)
    return pl.pallas_call(
        _attn_out_kernel,
        grid=(t // tm,),
        in_specs=[pl.BlockSpec((HEADS, tm, HEAD_DIM), lambda i: (0, i, 0))] * 3 + [row(LANES)] * 3 + [row(D_MODEL),
                  _const_spec((D_MODEL, D_MODEL)), _const_spec((1, D_MODEL)), _const_spec((1, D_MODEL)),
                  _const_spec((N_EXPERTS, D_MODEL)), _const_spec((N_EXPERTS, 1))],
        out_specs=out_specs, out_shape=out_shapes,
        scratch_shapes=[pltpu.VMEM((tm, D_MODEL), _BF16)],
        compiler_params=_cparams("arbitrary"), name="attn_out_norm_route",
    )(*os_, *lses, x, wo_bf16, ln_g.reshape(1, -1), ln_b.reshape(1, -1),
      w_router.T, b_router.reshape(-1, 1))


def ssm_out_norm_route(yn, x, wo_bf16, ln_g, ln_b, w_router, b_router):
    t = x.shape[0]
    tm = ROUTE_TM_SSM
    row = lambda w: pl.BlockSpec((tm, w), lambda i: (i, 0))
    out_specs, out_shapes = _route_out_specs(t, tm)
    return pl.pallas_call(
        _ssm_out_kernel,
        grid=(t // tm,),
        in_specs=[row(SSM_INNER), row(D_MODEL),
                  _const_spec((SSM_INNER, D_MODEL)), _const_spec((1, D_MODEL)), _const_spec((1, D_MODEL)),
                  _const_spec((N_EXPERTS, D_MODEL)), _const_spec((N_EXPERTS, 1))],
        out_specs=out_specs, out_shape=out_shapes,
        compiler_params=_cparams("arbitrary"), name="ssm_out_norm_route",
    )(yn, x, wo_bf16, ln_g.reshape(1, -1), ln_b.reshape(1, -1), w_router.T, b_router.reshape(-1, 1))


MOE_BM = 1024
MOE_SUB_BM = 256
MOE_SUB = ROUTE_SUB
SEG_ALIGN = SUBLANES
LOCAL_ROWS = MOE_SUB * TOP_K + N_EXPERTS * SUBLANES
SEG_UNIT = 16
SEG_REST = (8,)
TAIL_SIZES = tuple(SEG_ALIGN << k for k in reversed(range((MOE_BM // SEG_ALIGN).bit_length() - 1)))
PAIR_COLS = D_MODEL // 2
_U32 = jnp.uint32


def _pack_pairs(x):
    lo = lax.shift_right_logical(lax.bitcast_convert_type(x[:, :PAIR_COLS], _U32), jnp.uint32(16))
    hi = lax.bitcast_convert_type(x[:, PAIR_COLS:], _U32) & jnp.uint32(0xFFFF0000)
    return hi | lo


def _unpack_pairs(u):
    lo = lax.bitcast_convert_type(lax.shift_left(u, jnp.uint32(16)), _F32).astype(_BF16)
    hi = lax.bitcast_convert_type(u & jnp.uint32(0xFFFF0000), _F32).astype(_BF16)
    return lo, hi


def _local_positions(eidx):
    sub = eidx.shape[1]
    eio = lax.broadcasted_iota(jnp.int32, (N_EXPERTS, sub), 0)
    hots = [eio == eidx[k:k + 1, :] for k in range(TOP_K)]
    hot_all = (hots[0] | hots[1] | hots[2] | hots[3]).astype(_BF16)
    r_io = lax.broadcasted_iota(jnp.int32, (sub, sub), 0)
    c_io = lax.broadcasted_iota(jnp.int32, (sub, sub), 1)
    upper = (r_io < c_io).astype(_BF16)
    prefix = jnp.dot(hot_all, upper, preferred_element_type=_F32)
    counts = jnp.sum(hot_all.astype(_F32), axis=1, keepdims=True).astype(jnp.int32)
    padded = ((counts + (SEG_ALIGN - 1)) & (-SEG_ALIGN)).astype(_F32)
    e_r = lax.broadcasted_iota(jnp.int32, (N_EXPERTS, N_EXPERTS), 0)
    e_c = lax.broadcasted_iota(jnp.int32, (N_EXPERTS, N_EXPERTS), 1)
    below = (e_c < e_r).astype(_BF16)
    seg_start = jnp.dot(below, jnp.broadcast_to(padded, (N_EXPERTS, sub)).astype(_BF16),
                        preferred_element_type=_F32)
    base = seg_start + prefix
    return [jnp.sum(jnp.where(h, base, 0.0), axis=0, keepdims=True) for h in hots]


def _segment_copies(cnt_ref, lo_ref, go_ref, i, copy_fn):
    for e in range(N_EXPERTS):
        n = cnt_ref[i * N_EXPERTS + e]
        lo = lo_ref[i * N_EXPERTS + e]
        go = go_ref[i * N_EXPERTS + e]

        def unit(u, c, lo=lo, go=go):
            copy_fn(pl.multiple_of(lo + u * SEG_UNIT, SEG_ALIGN), pl.multiple_of(go + u * SEG_UNIT, SEG_ALIGN),
                    SEG_UNIT)
            return c

        lax.fori_loop(0, n // SEG_UNIT, unit, 0)
        for sz in SEG_REST:
            done = n & -(2 * sz)

            @pl.when((n & sz) != 0)
            def _(done=done, sz=sz, lo=lo, go=go):
                copy_fn(pl.multiple_of(lo + done, SEG_ALIGN), pl.multiple_of(go + done, SEG_ALIGN), sz)


def _dispatch_kernel(cnt_ref, lo_ref, go_ref, units_ref, tail_ref, eidx_ref, h_ref, xs_ref, lpos_ref,
                     buf_ref, sem):
    i = pl.program_id(0)
    slot = i % 2
    mine, other = buf_ref.at[slot], buf_ref.at[1 - slot]
    lpos = _local_positions(eidx_ref[...])
    for k in range(TOP_K):
        lpos_ref[k:k + 1, :] = lpos[k]
    p_io = lax.broadcasted_iota(jnp.int32, (LOCAL_ROWS, MOE_SUB), 0).astype(_F32)
    sel = (p_io == lpos[0]) | (p_io == lpos[1]) | (p_io == lpos[2]) | (p_io == lpos[3])
    mine[...] = _pack_pairs(jnp.dot(sel.astype(_BF16), h_ref[...].astype(_BF16), preferred_element_type=_F32))

    def copy(lo, go, sz):
        pltpu.make_async_copy(mine.at[pl.ds(lo, sz)], xs_ref.at[pl.ds(go, sz)], sem.at[slot]).start()

    _segment_copies(cnt_ref, lo_ref, go_ref, i, copy)

    def wait_units(buf, s, n_units):
        def body(j, c):
            pltpu.make_async_copy(buf.at[pl.ds(0, SEG_ALIGN)], xs_ref.at[pl.ds(0, SEG_ALIGN)], sem.at[s]).wait()
            return c
        lax.fori_loop(0, n_units, body, 0)

    @pl.when(i > 0)
    def _():
        wait_units(other, 1 - slot, units_ref[jnp.maximum(i - 1, 0)])

    @pl.when(i == pl.num_programs(0) - 1)
    def _():
        wait_units(mine, slot, units_ref[i])
        mine[0:MOE_BM, :] = jnp.zeros((MOE_BM, PAIR_COLS), _U32)
        start, n, n_blocks = tail_ref[0], tail_ref[1], tail_ref[2]
        done = jnp.int32(0)
        for sz in TAIL_SIZES:
            @pl.when((n & sz) != 0)
            def _(done=done, sz=sz):
                pltpu.make_async_copy(mine.at[pl.ds(0, sz)],
                                      xs_ref.at[pl.ds(pl.multiple_of(start + done, SEG_ALIGN), sz)],
                                      sem.at[slot]).start()
            done = done + (n & sz)

        def zero_block(j, c):
            row0 = pl.multiple_of(start + n + j * MOE_BM, MOE_BM)
            pltpu.make_async_copy(mine.at[pl.ds(0, MOE_BM)], xs_ref.at[pl.ds(row0, MOE_BM)], sem.at[slot]).start()
            return c

        lax.fori_loop(0, n_blocks, zero_block, 0)
        wait_units(mine, slot, n // SEG_ALIGN + n_blocks * (MOE_BM // SEG_ALIGN))


def _seg_tables(tile_counts):
    cnt = tile_counts.T.astype(jnp.int32)
    pc = (cnt + (SEG_ALIGN - 1)) & (-SEG_ALIGN)
    lo = jnp.cumsum(pc, axis=1) - pc
    tot = jnp.sum(pc, axis=0)
    offs = jnp.concatenate([jnp.zeros((1,), jnp.int32), jnp.cumsum(tot)])
    go = offs[None, :-1] + jnp.cumsum(pc, axis=0) - pc
    units = jnp.sum(pc, axis=1) // SEG_ALIGN
    return pc.reshape(-1), lo.reshape(-1), go.reshape(-1), units.astype(jnp.int32), offs.astype(jnp.int32)


def _max_rows(t):
    n = t * TOP_K + (t // MOE_SUB) * N_EXPERTS * (SEG_ALIGN - 1)
    return -(-n // MOE_BM) * MOE_BM


def moe_dispatch(h, eidx, tables):
    t = h.shape[0]
    pc, lo, go, units, offs = tables
    used = offs[-1]
    partial = (-used) % MOE_BM
    tail = jnp.stack([used, partial, (_max_rows(t) - used - partial) // MOE_BM]).astype(jnp.int32)
    grid_spec = pltpu.PrefetchScalarGridSpec(
        num_scalar_prefetch=5, grid=(t // MOE_SUB,),
        in_specs=[pl.BlockSpec((TOP_K, MOE_SUB), lambda i, *_: (0, i)),
                  pl.BlockSpec((MOE_SUB, D_MODEL), lambda i, *_: (i, 0))],
        out_specs=[pl.BlockSpec(memory_space=pl.ANY), pl.BlockSpec((TOP_K, MOE_SUB), lambda i, *_: (0, i))],
        scratch_shapes=[pltpu.VMEM((2, LOCAL_ROWS, PAIR_COLS), _U32), pltpu.SemaphoreType.DMA((2,))])
    return pl.pallas_call(
        _dispatch_kernel, grid_spec=grid_spec,
        out_shape=[jax.ShapeDtypeStruct((_max_rows(t), PAIR_COLS), _U32),
                   jax.ShapeDtypeStruct((TOP_K, t), _F32)],
        compiler_params=_cparams("arbitrary"), name="moe_dispatch",
    )(pc, lo, go, units, tail, eidx, h)


def _combine_kernel(cnt_ref, lo_ref, go_ref, units_ref, lpos_ref, gate_ref, ys_ref, h_ref, g_ref, b_ref,
                    o_ref, buf_ref, sem):
    i = pl.program_id(0)
    slot = i % 2

    def fetch(step, s):
        dst = buf_ref.at[s]
        dst[MOE_SUB * TOP_K:, :] = jnp.zeros((LOCAL_ROWS - MOE_SUB * TOP_K, PAIR_COLS), _U32)

        def copy(lo, go, sz):
            pltpu.make_async_copy(ys_ref.at[pl.ds(go, sz)], dst.at[pl.ds(lo, sz)], sem.at[s]).start()

        _segment_copies(cnt_ref, lo_ref, go_ref, step, copy)

    @pl.when(i == 0)
    def _():
        fetch(i, slot)

    @pl.when(i + 1 < pl.num_programs(0))
    def _():
        fetch(i + 1, 1 - slot)

    mine = buf_ref.at[slot]
    lpos = lpos_ref[...]
    gates = gate_ref[...]
    p_io = lax.broadcasted_iota(jnp.int32, (LOCAL_ROWS, MOE_SUB), 0).astype(_F32)
    w = jnp.zeros((LOCAL_ROWS, MOE_SUB), _F32)
    for k in range(TOP_K):
        w = w + jnp.where(p_io == lpos[k:k + 1, :], gates[k:k + 1, :], 0.0)
    wb = w.astype(_BF16)

    def wait_unit(j, c):
        pltpu.make_async_copy(ys_ref.at[pl.ds(0, SEG_ALIGN)], mine.at[pl.ds(0, SEG_ALIGN)], sem.at[slot]).wait()
        return c

    lax.fori_loop(0, units_ref[i], wait_unit, 0)
    tn = (((0,), (0,)), ((), ()))
    ffn = jnp.concatenate([lax.dot_general(wb, yb, tn, preferred_element_type=_F32)
                           for yb in _unpack_pairs(mine[...])], axis=1)
    o_ref[...] = _layer_norm(DEEPNORM_ALPHA * h_ref[...] + ffn, g_ref[...], b_ref[...])


def moe_combine(ys, lpos, gates, h, tables, ln_g, ln_b):
    t = h.shape[0]
    pc, lo, go, units, _ = tables
    row = pl.BlockSpec((MOE_SUB, D_MODEL), lambda i, *_: (i, 0))
    kt = pl.BlockSpec((TOP_K, MOE_SUB), lambda i, *_: (0, i))
    cst = lambda shape: pl.BlockSpec(shape, lambda i, *_: (0,) * len(shape))
    grid_spec = pltpu.PrefetchScalarGridSpec(
        num_scalar_prefetch=4, grid=(t // MOE_SUB,),
        in_specs=[kt, kt, pl.BlockSpec(memory_space=pl.ANY), row, cst((1, D_MODEL)), cst((1, D_MODEL))],
        out_specs=row,
        scratch_shapes=[pltpu.VMEM((2, LOCAL_ROWS, PAIR_COLS), _U32), pltpu.SemaphoreType.DMA((2,))])
    return pl.pallas_call(
        _combine_kernel, grid_spec=grid_spec,
        out_shape=jax.ShapeDtypeStruct((t, D_MODEL), _F32),
        compiler_params=_cparams("arbitrary"), name="moe_combine",
    )(pc, lo, go, units, lpos, gates, ys, h, ln_g.reshape(1, -1), ln_b.reshape(1, -1))


def _expert_kernel(blk_ref, e_ref, lo_ref, hi_ref, xs_ref, wu_ref, bu_ref, wd_ref, bd_ref,
                   ys_ref, wub_ref, wdb_ref):
    i = pl.program_id(0)
    e = e_ref[i]
    e_prev = e_ref[jnp.maximum(i - 1, 0)]

    @pl.when((i == 0) | (e != e_prev))
    def _():
        wub_ref[...] = wu_ref[...].astype(_BF16)
        wdb_ref[...] = wd_ref[...].astype(_BF16)

    lo = lo_ref[i]
    hi = hi_ref[i]

    @pl.when((lo == 0) | (hi < 0))
    def _():
        ys_ref[...] = jnp.zeros_like(ys_ref)

    for sb in range(xs_ref.shape[0] // MOE_SUB_BM):
        r0 = sb * MOE_SUB_BM
        rows = slice(r0, r0 + MOE_SUB_BM)

        @pl.when((hi > lo) & (hi > r0) & (lo < r0 + MOE_SUB_BM))
        def _(r0=r0, rows=rows):
            x_lo, x_hi = _unpack_pairs(xs_ref[rows, :])
            h = (jnp.dot(x_lo, wub_ref[:PAIR_COLS, :], preferred_element_type=_F32)
                 + jnp.dot(x_hi, wub_ref[PAIR_COLS:, :], preferred_element_type=_F32) + bu_ref[...])
            gate = jnp.minimum(h[:, :D_FF], SWIGLU_LIMIT)
            up = jnp.clip(h[:, D_FF:], -SWIGLU_LIMIT, SWIGLU_LIMIT)
            glu = gate * jax.nn.sigmoid(SWIGLU_ALPHA * gate)
            act = ((up + 1.0) * glu).astype(_BF16)
            y = jnp.dot(act, wdb_ref[...], preferred_element_type=_F32) + bd_ref[...]
            packed = _pack_pairs(y.astype(_BF16).astype(_F32))
            row = r0 + lax.broadcasted_iota(jnp.int32, packed.shape, 0)
            ys_ref[rows, :] = jnp.where((row >= lo) & (row < hi), packed, ys_ref[rows, :])


def _expert_schedule(offs, n_rows, bm):
    nblk = n_rows // bm
    n_items = nblk + N_EXPERTS - 1
    used = offs[-1]
    bnd = offs[1:-1]
    pos = jnp.arange(N_EXPERTS - 1, dtype=jnp.int32) + jnp.minimum(bnd // bm + 1, nblk)
    i = jnp.arange(n_items, dtype=jnp.int32)[:, None]
    at = pos[None, :] == i
    is_bnd = jnp.any(at, axis=1)
    bnd_val = jnp.sum(jnp.where(at, bnd[None, :], 0), axis=1)
    n_before = jnp.sum((pos[None, :] < i).astype(jnp.int32), axis=1)
    cuts = jnp.where(is_bnd, bnd_val, (i[:, 0] - n_before) * bm)
    ends = jnp.concatenate([cuts[1:], jnp.array([n_rows], jnp.int32)])
    last_blk = jnp.maximum(used - 1, 0) // bm
    past = jnp.logical_not(is_bnd) & (cuts // bm > last_blk)
    raw_blk = cuts // bm
    cuts = jnp.minimum(cuts, used)
    ends = jnp.minimum(ends, used)
    blk = jnp.minimum(cuts // bm, last_blk)
    e = jnp.minimum(jnp.sum((bnd[None, :] <= cuts[:, None]).astype(jnp.int32), axis=1), N_EXPERTS - 1)
    lo = jnp.where(past, 0, cuts - blk * bm)
    hi = jnp.where(past, -1, jnp.minimum(ends - blk * bm, bm))
    return lax.cummax(jnp.where(past, raw_blk, blk)), e, lo, hi


def moe_experts(xs, offs, layer, w_up, b_up, w_down, b_down):
    n_rows = xs.shape[0]
    bm = MOE_BM
    blk, e, lo, hi = _expert_schedule(offs, n_rows, bm)
    n_items = n_rows // bm + N_EXPERTS - 1
    wmap = lambda i, b, e, lo, hi: (layer, e[i], 0, 0)
    grid_spec = pltpu.PrefetchScalarGridSpec(
        num_scalar_prefetch=4, grid=(n_items,),
        in_specs=[pl.BlockSpec((bm, PAIR_COLS), lambda i, b, e, lo, hi: (b[i], 0)),
                  pl.BlockSpec((None, None, D_MODEL, 2 * D_FF), wmap),
                  pl.BlockSpec((None, None, 1, 2 * D_FF), wmap),
                  pl.BlockSpec((None, None, D_FF, D_MODEL), wmap),
                  pl.BlockSpec((None, None, 1, D_MODEL), wmap)],
        out_specs=pl.BlockSpec((bm, PAIR_COLS), lambda i, b, e, lo, hi: (b[i], 0)),
        scratch_shapes=[pltpu.VMEM((D_MODEL, 2 * D_FF), _BF16), pltpu.VMEM((D_FF, D_MODEL), _BF16)])
    nl = w_up.shape[0]
    return pl.pallas_call(
        _expert_kernel, grid_spec=grid_spec,
        out_shape=jax.ShapeDtypeStruct((n_rows, PAIR_COLS), _U32),
        compiler_params=_cparams("arbitrary"), name="moe_experts",
    )(blk, e, lo, hi, xs, w_up, b_up.reshape(nl, N_EXPERTS, 1, -1), w_down, b_down.reshape(nl, N_EXPERTS, 1, -1))


def moe_layer(h, eidx, gates, tile_counts, layer, w_up, b_up, w_down, b_down, ln_g, ln_b):
    tables = _seg_tables(tile_counts)
    xs, lpos = moe_dispatch(h, eidx, tables)
    ys = moe_experts(xs, tables[4], layer, w_up, b_up, w_down, b_down)
    return moe_combine(ys, lpos, gates, h, tables, ln_g, ln_b)


def kernel(x, positions, attn_w_in, attn_w_out, ssm_w_in, ssm_conv_w, ssm_conv_b, ssm_dt_bias,
           ssm_a_log, ssm_d, ssm_norm_w, ssm_w_out, moe_w_router, moe_b_router, moe_w_up,
           moe_b_up, moe_w_down, moe_b_down, ln_mix_g, ln_mix_b, ln_ffn_g, ln_ffn_b):
    bsz, seq, _ = x.shape
    xf = x.reshape(bsz * seq, D_MODEL)

    w_qkv = attn_w_in[0].astype(_BF16)
    outs = [dilated_group_attention(qkv_proj_group(xf, w_qkv, positions, g), g, bsz, seq)
            for g in range(N_GROUPS)]
    h, eidx, gates, tile_counts = attn_out_norm_route(
        [o for o, _ in outs], [l for _, l in outs], xf, attn_w_out[0].astype(_BF16),
        ln_mix_g[0], ln_mix_b[0], moe_w_router[0], moe_b_router[0])
    xf = moe_layer(h, eidx, gates, tile_counts, 0, moe_w_up, moe_b_up, moe_w_down, moe_b_down,
                   ln_ffn_g[0], ln_ffn_b[0])

    gz, xs, bc, dt = ssm_in_proj(xf, ssm_w_in[0], ssm_conv_w[0], ssm_conv_b[0], ssm_dt_bias[0], seq)
    yn = ssd_mixer(gz, xs, bc, dt, ssm_a_log[0], ssm_d[0], ssm_norm_w[0], bsz, seq)
    h, eidx, gates, tile_counts = ssm_out_norm_route(
        yn, xf, ssm_w_out[0].astype(_BF16), ln_mix_g[1], ln_mix_b[1], moe_w_router[1], moe_b_router[1])
    xf = moe_layer(h, eidx, gates, tile_counts, 1, moe_w_up, moe_b_up, moe_w_down, moe_b_down,
                   ln_ffn_g[1], ln_ffn_b[1])
    return xf.reshape(bsz, seq, D_MODEL)
```
